```python
import math
import jax
import jax.numpy as jnp
from jax import lax
import numpy as np

D_MODEL = 2048
BATCH = 4
SEQ = 4096
DEPTH = 4

HEAD_DIM = 128
FOX_HEADS = 4
DIFF_HEADS = 4
DIFF_HALF = HEAD_DIM // 2
NSA_HEADS = 8
NSA_KV_HEADS = 2
CMP_BLOCK = 32
CMP_STRIDE = 16
CMP_HIDDEN = 256
SEL_BLOCK = 64
SEL_TOPK = 16
WINDOW = 512
Q_BLOCK = 128
D_FF = ((8 * D_MODEL + 3 * 256 - 1) // (3 * 256)) * 256
FOX_WIDTH = FOX_HEADS * HEAD_DIM
DIFF_WIDTH = DIFF_HEADS * HEAD_DIM
NSA_WIDTH = NSA_HEADS * HEAD_DIM
NSA_KV_WIDTH = NSA_KV_HEADS * HEAD_DIM
MIX_WIDTH = FOX_WIDTH + DIFF_WIDTH + NSA_WIDTH
N_BRANCHES = 3
IN_SPLITS = (FOX_WIDTH, FOX_WIDTH, FOX_WIDTH, FOX_HEADS,
             DIFF_WIDTH, DIFF_WIDTH, DIFF_WIDTH,
             NSA_WIDTH, NSA_KV_WIDTH, NSA_KV_WIDTH, NSA_KV_WIDTH, NSA_KV_WIDTH,
             NSA_KV_WIDTH, NSA_KV_WIDTH, 3 * NSA_HEADS)
IN_COLS = sum(IN_SPLITS)
EPS = 1e-6
NEG_INF = -1e30
FORCE_SCORE = 1e4
FORGET_BIAS_CENTER = 3.0

kernel_name = "fox_diff_nsa_gated_hybrid"


def _rmsnorm(x, g):
    xf = x.astype(jnp.float32)
    y = xf * lax.rsqrt(jnp.mean(xf * xf, axis=-1, keepdims=True) + EPS)
    return (y * g.astype(jnp.float32)).astype(x.dtype)


def _alibi_slopes(n):
    return 2.0 ** (-8.0 * jnp.arange(1, n + 1, dtype=jnp.float32) / n)


def _split_cols(t, widths):
    out, start = [], 0
    for w in widths:
        out.append(t[..., start:start + w])
        start += w
    return out


def _heads(t, n):
    b, s, _ = t.shape
    return t.reshape(b, s, n, -1).transpose(0, 2, 1, 3)


def _merge_heads(t):
    b, n, s, d = t.shape
    return t.transpose(0, 2, 1, 3).reshape(b, s, n * d)


def _fox_attention(q, k, v, log_f):
    b, h, s, d = q.shape
    nb = s // Q_BLOCK
    scale = d ** -0.5
    cum = jnp.cumsum(log_f, axis=-1)
    kpos = jnp.arange(s)
    q_blk = q.reshape(b, h, nb, Q_BLOCK, d).transpose(2, 0, 1, 3, 4)
    c_blk = cum.reshape(b, h, nb, Q_BLOCK).transpose(2, 0, 1, 3)

    def block(args):
        i, qi, ci = args
        qpos = i * Q_BLOCK + jnp.arange(Q_BLOCK)
        sc = jnp.einsum('bhqd,bhkd->bhqk', qi, k).astype(jnp.float32) * scale
        sc = sc + ci[..., :, None] - cum[..., None, :]
        sc = jnp.where(kpos[None, :] <= qpos[:, None], sc, NEG_INF)
        p = jax.nn.softmax(sc, axis=-1).astype(v.dtype)
        return jnp.einsum('bhqk,bhkd->bhqd', p, v)

    o = lax.map(block, (jnp.arange(nb), q_blk, c_blk))
    return o.transpose(1, 2, 0, 3, 4).reshape(b, h, s, d)


def _diff_attention(q, k, v, lam, lam_init, slopes, subln):
    b, h, s, _, dh = q.shape
    d = v.shape[-1]
    nb = s // Q_BLOCK
    scale = dh ** -0.5
    kpos = jnp.arange(s)
    sl = slopes[None, :, None, None, None]
    q_blk = q.reshape(b, h, nb, Q_BLOCK, 2, dh).transpose(2, 0, 1, 3, 4, 5)

    def block(args):
        i, qi = args
        qpos = i * Q_BLOCK + jnp.arange(Q_BLOCK)
        dist = qpos[:, None] - kpos[None, :]
        sc = jnp.einsum('bhqcd,bhkcd->bhcqk', qi, k).astype(jnp.float32) * scale
        sc = jnp.where(dist >= 0, sc - sl * dist.astype(jnp.float32), NEG_INF)
        p = jax.nn.softmax(sc, axis=-1)
        a = p[:, :, 0] - lam * p[:, :, 1]
        return jnp.einsum('bhqk,bhkd->bhqd', a.astype(v.dtype), v)

    o = lax.map(block, (jnp.arange(nb), q_blk))
    o = o.transpose(1, 2, 0, 3, 4).reshape(b, h, s, d)
    return _rmsnorm(o, subln) * (1.0 - lam_init)


def _compress(kv, pos, w1, w2):
    s = kv.shape[2]
    nc = (s - CMP_BLOCK) // CMP_STRIDE + 1
    idx = jnp.arange(nc)[:, None] * CMP_STRIDE + jnp.arange(CMP_BLOCK)[None, :]
    blocks = kv[:, :, idx, :] + pos
    flat = blocks.reshape(blocks.shape[0], blocks.shape[1], nc, CMP_BLOCK * HEAD_DIM)
    return jax.nn.silu(flat @ w1) @ w2


def _nsa_attention(q, k_c, v_c, k_s, v_s, k_w, v_w, gates, slopes):
    b, h, s, d = q.shape
    g = k_s.shape[1]
    r = h // g
    nb = s // Q_BLOCK
    nc = k_c.shape[2]
    nsel = s // SEL_BLOCK
    topk = min(SEL_TOPK, nsel)
    scale = d ** -0.5
    cmp_start = jnp.arange(nc) * CMP_STRIDE
    cmp_end = cmp_start + CMP_BLOCK - 1
    sel_start = jnp.arange(nsel) * SEL_BLOCK
    overlap = ((cmp_start[:, None] < sel_start[None, :] + SEL_BLOCK)
               & (cmp_end[:, None] >= sel_start[None, :])).astype(jnp.float32)
    blk_id = jnp.arange(nsel)
    sl = slopes.reshape(g, r)[None, :, :, None, None]
    q_blk = q.reshape(b, g, r, nb, Q_BLOCK, d).transpose(3, 0, 1, 2, 4, 5)
    g_blk = gates.reshape(b, g, r, nb, Q_BLOCK, 3).transpose(3, 0, 1, 2, 4, 5)
    pad = jnp.zeros((b, g, WINDOW, d), k_w.dtype)
    kw_pad = jnp.concatenate([pad, k_w], axis=2)
    vw_pad = jnp.concatenate([pad, v_w], axis=2)
    bi = jnp.arange(b)[:, None, None, None]
    gidx = jnp.arange(g)[None, :, None, None]

    def block(args):
        i, qi, gate_i = args
        qpos = i * Q_BLOCK + jnp.arange(Q_BLOCK)
        dist_c = qpos[:, None] - cmp_end[None, :]
        valid_c = dist_c >= 0
        sc = jnp.einsum('bgrqd,bgcd->bgrqc', qi, k_c).astype(jnp.float32) * scale
        sc = jnp.where(valid_c, sc - sl * dist_c.astype(jnp.float32), NEG_INF)
        p_c = jax.nn.softmax(sc, axis=-1) * jnp.any(valid_c, axis=-1)[:, None]
        o_cmp = jnp.einsum('bgrqc,bgcd->bgrqd', p_c.astype(v_c.dtype), v_c)
        imp = jnp.einsum('bgrqc,cj->bgqj', p_c, overlap)
        cur = qpos // SEL_BLOCK
        valid_s = blk_id[None, :] <= cur[:, None]
        forced = ((blk_id[None, :] == 0) | (blk_id[None, :] == cur[:, None])
                  | (blk_id[None, :] == cur[:, None] - 1))
        imp = jnp.where(valid_s, jnp.where(forced, FORCE_SCORE, imp), -1.0)
        _, top = lax.top_k(imp, topk)
        tok = (top[..., None] * SEL_BLOCK + jnp.arange(SEL_BLOCK)).reshape(b, g, Q_BLOCK, topk * SEL_BLOCK)
        ks = k_s[bi, gidx, tok]
        vs = v_s[bi, gidx, tok]
        dist_s = (qpos[None, None, :, None] - tok)[:, :, None]
        ss = jnp.einsum('bgrqd,bgqnd->bgrqn', qi, ks).astype(jnp.float32) * scale
        ss = jnp.where(dist_s >= 0, ss - sl * dist_s.astype(jnp.float32), NEG_INF)
        o_sel = jnp.einsum('bgrqn,bgqnd->bgrqd', jax.nn.softmax(ss, axis=-1).astype(vs.dtype), vs)
        kw = lax.dynamic_slice_in_dim(kw_pad, i * Q_BLOCK, WINDOW + Q_BLOCK, axis=2)
        vw = lax.dynamic_slice_in_dim(vw_pad, i * Q_BLOCK, WINDOW + Q_BLOCK, axis=2)
        kpos = i * Q_BLOCK - WINDOW + jnp.arange(WINDOW + Q_BLOCK)
        dist_w = qpos[:, None] - kpos[None, :]
        valid_w = (dist_w >= 0) & (dist_w < WINDOW) & (kpos[None, :] >= 0)
        sw = jnp.einsum('bgrqd,bgkd->bgrqk', qi, kw).astype(jnp.float32) * scale
        sw = jnp.where(valid_w, sw - sl * dist_w.astype(jnp.float32), NEG_INF)
        o_win = jnp.einsum('bgrqk,bgkd->bgrqd', jax.nn.softmax(sw, axis=-1).astype(vw.dtype), vw)
        return gate_i[..., 0:1] * o_cmp + gate_i[..., 1:2] * o_sel + gate_i[..., 2:3] * o_win

    o = lax.map(block, (jnp.arange(nb), q_blk, g_blk))
    return o.transpose(1, 2, 3, 0, 4, 5).reshape(b, h, s, d)


def _hybrid_mixer(h, w_in, f_bias, lam_vec, subln, cmp_pos, cmp_w1, cmp_w2,
                  wb_fox, wb_diff, wb_nsa, w_gate, w_out, lam_init):
    b, s, _ = h.shape
    (fq, fk, fv, ff, dq, dk, dv, nq, nkc, nvc, nks, nvs, nkw, nvw, ng) = _split_cols(h @ w_in, IN_SPLITS)
    log_f = jax.nn.log_sigmoid(ff.astype(jnp.float32) + f_bias.astype(jnp.float32)).transpose(0, 2, 1)
    o_fox = _fox_attention(_heads(fq, FOX_HEADS), _heads(fk, FOX_HEADS), _heads(fv, FOX_HEADS), log_f)
    lv = lam_vec.astype(jnp.float32)
    lam = jnp.exp(jnp.sum(lv[0] * lv[1])) - jnp.exp(jnp.sum(lv[2] * lv[3])) + lam_init
    dq2 = dq.reshape(b, s, DIFF_HEADS, 2, DIFF_HALF).transpose(0, 2, 1, 3, 4)
    dk2 = dk.reshape(b, s, DIFF_HEADS, 2, DIFF_HALF).transpose(0, 2, 1, 3, 4)
    o_diff = _diff_attention(dq2, dk2, _heads(dv, DIFF_HEADS), lam, lam_init,
                             _alibi_slopes(DIFF_HEADS), subln)
    k_cmp = _compress(_heads(nkc, NSA_KV_HEADS), cmp_pos[0], cmp_w1[0], cmp_w2[0])
    v_cmp = _compress(_heads(nvc, NSA_KV_HEADS), cmp_pos[1], cmp_w1[1], cmp_w2[1])
    nsa_gates = jax.nn.sigmoid(ng).reshape(b, s, NSA_HEADS, 3).transpose(0, 2, 1, 3)
    o_nsa = _nsa_attention(_heads(nq, NSA_HEADS), k_cmp, v_cmp,
                           _heads(nks, NSA_KV_HEADS), _heads(nvs, NSA_KV_HEADS),
                           _heads(nkw, NSA_KV_HEADS), _heads(nvw, NSA_KV_HEADS),
                           nsa_gates, _alibi_slopes(NSA_HEADS))
    y_fox = _merge_heads(o_fox) @ wb_fox
    y_diff = _merge_heads(o_diff) @ wb_diff
    y_nsa = _merge_heads(o_nsa) @ wb_nsa
    gate = jax.nn.sigmoid(h @ w_gate).reshape(b, s, N_BRANCHES, D_MODEL)
    merged = gate[:, :, 0] * y_fox + gate[:, :, 1] * y_diff + gate[:, :, 2] * y_nsa
    return merged @ w_out


def _swiglu(h, w_up, w_down):
    gate, up = _split_cols(h @ w_up, (D_FF, D_FF))
    return (jax.nn.silu(gate) * up) @ w_down


def setup_inputs(seed: int = 0) -> dict:
    key = jax.random.key(seed)
    ks = jax.random.split(key, 16)
    L, D = DEPTH, D_MODEL

    def nrm(k, shape, scale):
        return jax.random.normal(k, shape, jnp.float32) * scale

    return {
        "x": nrm(ks[0], (BATCH, SEQ, D), 1.0),
        "w_in": nrm(ks[1], (L, D, IN_COLS), D ** -0.5),
        "fox_forget_bias": FORGET_BIAS_CENTER + nrm(ks[2], (L, FOX_HEADS), 0.5),
        "diff_lambda": nrm(ks[3], (L, 4, DIFF_HALF), 0.1),
        "diff_subln": 1.0 + nrm(ks[4], (L, HEAD_DIM), 0.05),
        "nsa_cmp_pos": nrm(ks[5], (L, 2, CMP_BLOCK, HEAD_DIM), 0.1),
        "nsa_cmp_w1": nrm(ks[6], (L, 2, CMP_BLOCK * HEAD_DIM, CMP_HIDDEN), (CMP_BLOCK * HEAD_DIM) ** -0.5),
        "nsa_cmp_w2": nrm(ks[7], (L, 2, CMP_HIDDEN, HEAD_DIM), CMP_HIDDEN ** -0.5),
        "w_branch_fox": nrm(ks[8], (L, FOX_WIDTH, D), FOX_WIDTH ** -0.5),
        "w_branch_diff": nrm(ks[9], (L, DIFF_WIDTH, D), DIFF_WIDTH ** -0.5),
        "w_branch_nsa": nrm(ks[10], (L, NSA_WIDTH, D), NSA_WIDTH ** -0.5),
        "w_gate": nrm(ks[11], (L, D, N_BRANCHES * D), D ** -0.5),
        "w_out": nrm(ks[12], (L, D, D), D ** -0.5),
        "norm_gains": 1.0 + nrm(ks[13], (L, 4, D), 0.05),
        "w_ffn_up": nrm(ks[14], (L, D, 2 * D_FF), D ** -0.5),
        "w_ffn_down": nrm(ks[15], (L, D_FF, D), D_FF ** -0.5),
    }


def reference(x, w_in, fox_forget_bias, diff_lambda, diff_subln, nsa_cmp_pos, nsa_cmp_w1, nsa_cmp_w2,
              w_branch_fox, w_branch_diff, w_branch_nsa, w_gate, w_out, norm_gains, w_ffn_up, w_ffn_down):
    for l in range(DEPTH):
        lam_init = 0.8 - 0.6 * math.exp(-0.3 * l)
        h = _rmsnorm(x, norm_gains[l, 0])
        y = _hybrid_mixer(h, w_in[l], fox_forget_bias[l], diff_lambda[l], diff_subln[l],
                          nsa_cmp_pos[l], nsa_cmp_w1[l], nsa_cmp_w2[l],
                          w_branch_fox[l], w_branch_diff[l], w_branch_nsa[l],
                          w_gate[l], w_out[l], lam_init)
        x = x + _rmsnorm(y, norm_gains[l, 1])
        h = _rmsnorm(x, norm_gains[l, 2])
        x = x + _rmsnorm(_swiglu(h, w_ffn_up[l], w_ffn_down[l]), norm_gains[l, 3])
    return x
```

```python
import functools
import math

import jax
import jax.numpy as jnp
from jax import lax
from jax.experimental import pallas as pl
from jax.experimental.pallas import tpu as pltpu

F32 = jnp.float32
BF16 = jnp.bfloat16

HEAD_DIM = 128
FOX_HEADS = 4
DIFF_HEADS = 4
DIFF_HALF = HEAD_DIM // 2
NSA_HEADS = 8
NSA_KV_HEADS = 2
NSA_REP = NSA_HEADS // NSA_KV_HEADS
CMP_BLOCK = 32
CMP_STRIDE = 16
CMP_HIDDEN = 256
SEL_BLOCK = 64
SEL_TOPK = 16
WINDOW = 512
N_BRANCHES = 3
EPS = 1e-6
NEG_INF = -1e30
FORCE_SCORE = 1e4
SEL_MASK = -32768.0
LANES = 128

SLAB_FQ, SLAB_FK, SLAB_FV = 0, 4, 8
SLAB_DQ, SLAB_DK, SLAB_DV = 12, 16, 20
SLAB_NQ = 24
SLAB_NKC, SLAB_NVC, SLAB_NKS, SLAB_NVS, SLAB_NKW, SLAB_NVW = 32, 34, 36, 38, 40, 42
N_SLABS = 44
MISC_FF = 0
MISC_NG = 4
AUG_HI = 64
AUG_LO = 65

VMEM_LIMIT = 56 * 1024 * 1024


def _pick(n, prefs):
    for p in prefs:
        if p <= n and n % p == 0:
            return p
    return n


def _params(sem):
    return pltpu.CompilerParams(dimension_semantics=sem, vmem_limit_bytes=VMEM_LIMIT)


def _rms(y, g):
    return y * lax.rsqrt(jnp.mean(y * y, axis=-1, keepdims=True) + EPS) * g


def _dot(a, b):
    return jnp.dot(a, b, preferred_element_type=F32)


def _dot_nt(a, b):
    return lax.dot_general(a, b, (((1,), (1,)), ((), ())), preferred_element_type=F32)


def _split3(x):
    hi = x.astype(BF16)
    r = x - hi.astype(F32)
    mid = r.astype(BF16)
    lo = (r - mid.astype(F32)).astype(BF16)
    return hi, mid, lo


def _norm_kernel(x_ref, g_ref, h_ref):
    h_ref[...] = _rms(x_ref[...], g_ref[...]).astype(BF16)


def _norm(x, g):
    m, d = x.shape
    tm = _pick(m, (512, 256, 128))
    return pl.pallas_call(
        _norm_kernel,
        grid=(m // tm,),
        in_specs=[pl.BlockSpec((tm, d), lambda i: (i, 0)), pl.BlockSpec((1, d), lambda i: (0, 0))],
        out_specs=pl.BlockSpec((tm, d), lambda i: (i, 0)),
        out_shape=jax.ShapeDtypeStruct((m, d), BF16),
        compiler_params=_params(("parallel",)),
        name="norm_in",
    )(x, g)


def _inproj_kernel(h_ref, w_ref, cs_ref, o_ref):
    acc = _dot(h_ref[...], w_ref[...]) * cs_ref[...]
    for s in range(o_ref.shape[0]):
        o_ref[s] = acc[:, s * LANES:(s + 1) * LANES].astype(BF16)


def _inproj(h, w_main, colscale):
    m, d = h.shape
    n = w_main.shape[1]
    tm = _pick(m, (1024, 512, 256))
    tn = 512
    return pl.pallas_call(
        _inproj_kernel,
        grid=(m // tm, n // tn),
        in_specs=[pl.BlockSpec((tm, d), lambda i, j: (i, 0)),
                  pl.BlockSpec((d, tn), lambda i, j: (0, j)),
                  pl.BlockSpec((1, tn), lambda i, j: (0, j))],
        out_specs=pl.BlockSpec((tn // LANES, tm, LANES), lambda i, j: (j, i, 0)),
        out_shape=jax.ShapeDtypeStruct((n // LANES, m, LANES), BF16),
        compiler_params=_params(("parallel", "arbitrary")),
        name="inproj",
    )(h, w_main, colscale)


def _misc_kernel(h_ref, w_ref, o_ref):
    o_ref[...] = _dot(h_ref[...], w_ref[...])


def _miscproj(h, w_misc):
    m, d = h.shape
    tm = _pick(m, (1024, 512, 256))
    return pl.pallas_call(
        _misc_kernel,
        grid=(m // tm,),
        in_specs=[pl.BlockSpec((tm, d), lambda i: (i, 0)), pl.BlockSpec((d, LANES), lambda i: (0, 0))],
        out_specs=pl.BlockSpec((tm, LANES), lambda i: (i, 0)),
        out_shape=jax.ShapeDtypeStruct((m, LANES), F32),
        compiler_params=_params(("parallel",)),
        name="miscproj",
    )(h, w_misc)


def _logf_kernel(misc_ref, bias_ref, o_ref, *, tc):
    s = misc_ref.shape[0]
    row = lax.broadcasted_iota(jnp.int32, (tc, tc), 0)
    col = lax.broadcasted_iota(jnp.int32, (tc, tc), 1)
    tri = jnp.where(col <= row, 1.0, 0.0).astype(BF16)
    lane = lax.broadcasted_iota(jnp.int32, (tc, LANES), 1)

    def chunk(c, carry):
        r0 = pl.multiple_of(c * tc, tc)
        z = misc_ref[pl.ds(r0, tc), :] + bias_ref[...]
        lf = jnp.minimum(z, 0.0) - jnp.log1p(jnp.exp(-jnp.abs(z)))
        hi, mid, lo = _split3(lf)
        cum = _dot(tri, hi) + _dot(tri, mid) + _dot(tri, lo) + carry
        for hd in range(FOX_HEADS):
            c = jnp.broadcast_to(cum[:, hd:hd + 1], (tc, LANES))
            c_hi = c.astype(BF16).astype(F32)
            c_mid = (c - c_hi).astype(BF16).astype(F32)
            aug = jnp.where(lane == 0, c_hi, jnp.where(lane == 1, c_mid,
                                                      jnp.where(lane == 2, c - c_hi - c_mid, 0.0)))
            o_ref[hd, pl.ds(r0, tc), :] = aug.astype(BF16)
        return cum[tc - 1:tc, :]

    lax.fori_loop(0, s // tc, chunk, jnp.zeros((1, LANES), F32))


def _logf(misc, bias_row, batch, seq):
    tc = _pick(seq, (256, 128))
    return pl.pallas_call(
        functools.partial(_logf_kernel, tc=tc),
        grid=(batch,),
        in_specs=[pl.BlockSpec((seq, LANES), lambda b: (b, 0)), pl.BlockSpec((1, LANES), lambda b: (0, 0))],
        out_specs=pl.BlockSpec((FOX_HEADS, seq, LANES), lambda b: (0, b, 0)),
        out_shape=jax.ShapeDtypeStruct((FOX_HEADS, batch * seq, LANES), BF16),
        compiler_params=_params(("parallel",)),
        name="fox_logf",
    )(misc, bias_row)


def _softmax_init(m_ref, l_ref, acc_ref):
    m_ref[...] = jnp.full(m_ref.shape, NEG_INF, F32)
    l_ref[...] = jnp.zeros(l_ref.shape, F32)
    acc_ref[...] = jnp.zeros(acc_ref.shape, F32)


def _softmax_step(s, v, m_ref, l_ref, acc_ref):
    m_prev = m_ref[...]
    m_new = jnp.maximum(m_prev, jnp.max(s, axis=-1, keepdims=True))
    alpha = jnp.exp(m_prev - m_new)
    p = jnp.exp(s - m_new)
    l_ref[...] = alpha * l_ref[...] + jnp.sum(p, axis=-1, keepdims=True)
    acc_ref[...] = alpha * acc_ref[...] + _dot(p.astype(BF16), v)
    m_ref[...] = m_new


def _causal_tiles(qa, k_ref, kx_ref, v_ref, qi, row_pos, t, m_ref, l_ref, acc_ref):
    def tile(j, masked):
        k0 = pl.multiple_of(j * t, t)
        ka = jnp.concatenate([k_ref[0, pl.ds(k0, t), :], kx_ref[pl.ds(k0, t), :]], axis=1)
        s = _dot_nt(qa, ka)
        if masked:
            kpos = k0 + lax.broadcasted_iota(jnp.int32, s.shape, 1)
            s = jnp.where(kpos <= row_pos, s, NEG_INF)
        _softmax_step(s, v_ref[0, pl.ds(k0, t), :], m_ref, l_ref, acc_ref)

    def full(j, c):
        tile(j, False)
        return c

    lax.fori_loop(0, qi, full, 0)
    tile(qi, True)


def _fox_kernel(q_ref, k_ref, v_ref, cx_ref, o_ref, m_ref, l_ref, acc_ref, *, t):
    qi = pl.program_id(2)
    lane = lax.broadcasted_iota(jnp.int32, (t, LANES), 1)
    qx = jnp.where(lane < 3, -1.0, 0.0).astype(BF16)
    qa = jnp.concatenate([q_ref[0], qx], axis=1)
    row_pos = qi * t + lax.broadcasted_iota(jnp.int32, (t, 1), 0)
    _softmax_init(m_ref, l_ref, acc_ref)
    _causal_tiles(qa, k_ref, cx_ref.at[0], v_ref, qi, row_pos, t, m_ref, l_ref, acc_ref)
    o_ref[...] = (acc_ref[...] / l_ref[...]).astype(BF16)


def _fox(slabs, cx, batch, seq):
    t = _pick(seq, (512, 256, 128))
    nq = seq // t
    m = batch * seq
    kv_spec = lambda base: pl.BlockSpec((1, seq, LANES), lambda b, h, i: (base + h, b, 0))
    return pl.pallas_call(
        functools.partial(_fox_kernel, t=t),
        grid=(batch, FOX_HEADS, nq),
        in_specs=[pl.BlockSpec((1, t, LANES), lambda b, h, i: (SLAB_FQ + h, b * nq + i, 0)),
                  kv_spec(SLAB_FK), kv_spec(SLAB_FV),
                  pl.BlockSpec((1, seq, LANES), lambda b, h, i: (h, b, 0))],
        out_specs=pl.BlockSpec((t, LANES), lambda b, h, i: (b * nq + i, h)),
        out_shape=jax.ShapeDtypeStruct((m, FOX_HEADS * HEAD_DIM), BF16),
        scratch_shapes=[pltpu.VMEM((t, 1), F32), pltpu.VMEM((t, 1), F32), pltpu.VMEM((t, LANES), F32)],
        compiler_params=_params(("parallel", "parallel", "arbitrary")),
        name="fox_attn",
    )(slabs, slabs, slabs, cx)


def _diff_kernel(q_ref, k_ref, v_ref, kx_ref, qx_ref, lam_ref, sub_ref, o_ref, m_ref, l_ref, acc_ref,
                 *, t, lam_init):
    qi = pl.program_id(2)
    q = q_ref[0].astype(F32)
    lane = lax.broadcasted_iota(jnp.int32, (t, LANES), 1)
    qx = jnp.broadcast_to(qx_ref[0], (t, LANES)).astype(BF16)
    qa = jnp.concatenate([
        jnp.concatenate([jnp.where(lane < DIFF_HALF, q, 0.0).astype(BF16), qx], axis=1),
        jnp.concatenate([jnp.where(lane >= DIFF_HALF, q, 0.0).astype(BF16), qx], axis=1)], axis=0)
    row = lax.broadcasted_iota(jnp.int32, (2 * t, 1), 0)
    row_pos = qi * t + jnp.where(row >= t, row - t, row)
    _softmax_init(m_ref, l_ref, acc_ref)
    _causal_tiles(qa, k_ref, kx_ref, v_ref, qi, row_pos, t, m_ref, l_ref, acc_ref)
    lv = lam_ref[...]
    lam = (jnp.exp(jnp.sum(lv[0:1] * lv[1:2], axis=-1, keepdims=True))
           - jnp.exp(jnp.sum(lv[2:3] * lv[3:4], axis=-1, keepdims=True)) + lam_init)
    o = acc_ref[...] / l_ref[...]
    o = o[:t] - lam * o[t:]
    o_ref[...] = (_rms(o, sub_ref[...]) * (1.0 - lam_init)).astype(BF16)


def _diff(slabs, kx, qx_diff, lam_vec, subln, batch, seq, lam_init):
    t = _pick(seq, (256, 128))
    nq = seq // t
    m = batch * seq
    kv_spec = lambda base: pl.BlockSpec((1, seq, LANES), lambda b, h, i: (base + h, b, 0))
    return pl.pallas_call(
        functools.partial(_diff_kernel, t=t, lam_init=lam_init),
        grid=(batch, DIFF_HEADS, nq),
        in_specs=[pl.BlockSpec((1, t, LANES), lambda b, h, i: (SLAB_DQ + h, b * nq + i, 0)),
                  kv_spec(SLAB_DK), kv_spec(SLAB_DV),
                  pl.BlockSpec((seq, LANES), lambda b, h, i: (0, 0)),
                  pl.BlockSpec((1, 1, LANES), lambda b, h, i: (h, 0, 0)),
                  pl.BlockSpec((4, DIFF_HALF), lambda b, h, i: (0, 0)),
                  pl.BlockSpec((1, LANES), lambda b, h, i: (0, 0))],
        out_specs=pl.BlockSpec((t, LANES), lambda b, h, i: (b * nq + i, h)),
        out_shape=jax.ShapeDtypeStruct((m, DIFF_HEADS * HEAD_DIM), BF16),
        scratch_shapes=[pltpu.VMEM((2 * t, 1), F32), pltpu.VMEM((2 * t, 1), F32),
                        pltpu.VMEM((2 * t, LANES), F32)],
        compiler_params=_params(("parallel", "parallel", "arbitrary")),
        name="diff_attn",
    )(slabs, slabs, slabs, kx, qx_diff, lam_vec, subln)


def _compress_kernel(y_ref, pos_ref, w1_ref, w2_ref, o_ref):
    half = y_ref.shape[3]
    y = y_ref[0, 0].astype(F32)
    top = (y + pos_ref[0, 0:1, :]).astype(BF16)
    bot = (y + pos_ref[0, 1:2, :]).astype(BF16)
    a = _dot(top, w1_ref[0, 0:half, :])
    b = _dot(bot, w1_ref[0, half:2 * half, :])
    nrow = a.shape[0]
    hid = a + pltpu.roll(b, nrow - 1, 0)
    hid = hid * jax.nn.sigmoid(hid)
    o_ref[0, 0] = _dot(hid.astype(BF16), w2_ref[0]).astype(BF16)


def _compress(ykv, pos2, w1, w2, batch):
    nrow, half = ykv.shape[2], ykv.shape[3]
    return pl.pallas_call(
        _compress_kernel,
        grid=(4, batch),
        in_specs=[pl.BlockSpec((1, 1, nrow, half), lambda s, b: (s, b, 0, 0)),
                  pl.BlockSpec((1, 2, half), lambda s, b: (s // 2, 0, 0)),
                  pl.BlockSpec((1, 2 * half, CMP_HIDDEN), lambda s, b: (s // 2, 0, 0)),
                  pl.BlockSpec((1, CMP_HIDDEN, HEAD_DIM), lambda s, b: (s // 2, 0, 0))],
        out_specs=pl.BlockSpec((1, 1, nrow, HEAD_DIM), lambda s, b: (s, b, 0, 0)),
        out_shape=jax.ShapeDtypeStruct((4, batch, nrow, HEAD_DIM), BF16),
        compiler_params=_params(("parallel", "parallel")),
        name="nsa_compress",
    )(ykv, pos2, w1, w2)


def _nsa_kernel(q_ref, kc_ref, vc_ref, ks_ref, vs_ref, kw_ref, vw_ref, kx_ref, cx_ref, ov_ref, qx_ref,
                misc_ref, o_ref, m_ref, l_ref, acc_ref, *, tq, tk):
    g = pl.program_id(1)
    qi = pl.program_id(2)
    mrows = NSA_REP * tq
    q0 = qi * tq
    q4 = q_ref[...].reshape(mrows, LANES)
    qx = jnp.concatenate([jnp.broadcast_to(qx_ref[0, r:r + 1, :], (tq, LANES)) for r in range(NSA_REP)], axis=0)
    row = lax.broadcasted_iota(jnp.int32, (mrows, 1), 0)
    row_pos = q0 + (row & (tq - 1))

    nc = kc_ref.shape[2]
    qa = jnp.concatenate([q4, qx.astype(BF16)], axis=1)
    kca = jnp.concatenate([kc_ref[0, 0], cx_ref[...]], axis=1)
    sc = _dot_nt(qa, kca)
    cend = lax.broadcasted_iota(jnp.int32, (mrows, nc), 1) * CMP_STRIDE + (CMP_BLOCK - 1)
    sc = jnp.where(cend <= row_pos, sc, NEG_INF)
    e = jnp.exp(sc - jnp.max(sc, axis=-1, keepdims=True))
    p = e / jnp.sum(e, axis=-1, keepdims=True)
    p = jnp.where(row_pos >= CMP_BLOCK - 1, p, 0.0)
    o_cmp = _dot(p.astype(BF16), vc_ref[0, 0])

    psum = p[0:tq]
    for r in range(1, NSA_REP):
        psum = psum + p[r * tq:(r + 1) * tq]
    p_hi = psum.astype(BF16)
    p_lo = (psum - p_hi.astype(F32)).astype(BF16)
    imp = _dot(p_hi, ov_ref[...]) + _dot(p_lo, ov_ref[...])
    qpos = q0 + lax.broadcasted_iota(jnp.int32, (tq, LANES), 0)
    blk = lax.broadcasted_iota(jnp.int32, (tq, LANES), 1)
    cur = jnp.right_shift(qpos, SEL_BLOCK.bit_length() - 1)
    forced = (blk == 0) | (blk == cur) | (blk == cur - 1)
    imp = jnp.where(blk <= cur, jnp.where(forced, FORCE_SCORE, imp), -1.0)
    nsel = kx_ref.shape[0] // SEL_BLOCK
    imp = jnp.where(blk < nsel, imp, -2.0)
    cand = imp.T[0:nsel]
    jidx = lax.broadcasted_iota(jnp.int32, (nsel, tq), 0)
    rank = jnp.zeros((nsel, tq), jnp.int32)
    for k in range(nsel):
        rk = cand[k:k + 1, :]
        rank = rank + ((rk > cand) | ((rk == cand) & (jidx > k))).astype(jnp.int32)
    bias_t = jnp.where(rank < min(SEL_TOPK, nsel), 0.0, SEL_MASK)
    bias_t = jnp.concatenate([bias_t, jnp.zeros((LANES - nsel, tq), F32)], axis=0)
    selbias = jnp.concatenate([bias_t.T] * NSA_REP, axis=0)
    lane4 = lax.broadcasted_iota(jnp.int32, (mrows, LANES), 1)
    qa_sel = jnp.concatenate([q4, jnp.where(lane4 < SEL_BLOCK, selbias, qx).astype(BF16)], axis=1)

    _softmax_init(m_ref, l_ref, acc_ref)

    def sel_tile(j, masked):
        k0 = pl.multiple_of(j * tk, tk)
        ka = jnp.concatenate([ks_ref[0, pl.ds(k0, tk), :], kx_ref[pl.ds(k0, tk), :]], axis=1)
        s = _dot_nt(qa_sel, ka)
        if masked:
            kpos = k0 + lax.broadcasted_iota(jnp.int32, s.shape, 1)
            s = jnp.where(kpos <= row_pos, s, NEG_INF)
        _softmax_step(s, vs_ref[0, pl.ds(k0, tk), :], m_ref, l_ref, acc_ref)

    def sel_full(j, c):
        sel_tile(j, False)
        return c

    n_full = q0 // tk
    lax.fori_loop(0, n_full, sel_full, 0)
    sel_tile(n_full, True)
    o_sel = acc_ref[...] / l_ref[...]

    span = WINDOW + tq
    w0 = pl.multiple_of(jnp.maximum(q0 - WINDOW, 0), tq)
    kwa = jnp.concatenate([kw_ref[0, pl.ds(w0, span), :], kx_ref[pl.ds(w0, span), :]], axis=1)
    sw = _dot_nt(qa, kwa)
    dist = row_pos - (w0 + lax.broadcasted_iota(jnp.int32, sw.shape, 1))
    sw = jnp.where((dist >= 0) & (dist < WINDOW), sw, NEG_INF)
    ew = jnp.exp(sw - jnp.max(sw, axis=-1, keepdims=True))
    o_win = _dot(ew.astype(BF16), vw_ref[0, pl.ds(w0, span), :]) / jnp.sum(ew, axis=-1, keepdims=True)

    gates = jax.nn.sigmoid(misc_ref[...])
    for r in range(NSA_REP):
        c0 = MISC_NG + 3 * (NSA_REP * g + r)
        sl = slice(r * tq, (r + 1) * tq)
        gsel = [jnp.sum(jnp.where(blk == c0 + i, gates, 0.0), axis=-1, keepdims=True) for i in range(3)]
        o_ref[:, r * LANES:(r + 1) * LANES] = (
            gsel[0] * o_cmp[sl] + gsel[1] * o_sel[sl] + gsel[2] * o_win[sl]).astype(BF16)


def _nsa(slabs, kvc, kx, cx_cmp, ov, qx_nsa, misc, batch, seq):
    tq = 128
    tk = _pick(seq, (512, 256, 128))
    nq = seq // tq
    m = batch * seq
    mrows = NSA_REP * tq
    nc = kvc.shape[2]
    kv_spec = lambda base: pl.BlockSpec((1, seq, LANES), lambda b, g, i: (base + g, b, 0))
    const = lambda shape: pl.BlockSpec(shape, lambda b, g, i: (0,) * len(shape))
    return pl.pallas_call(
        functools.partial(_nsa_kernel, tq=tq, tk=tk),
        grid=(batch, NSA_KV_HEADS, nq),
        in_specs=[pl.BlockSpec((NSA_REP, tq, LANES), lambda b, g, i: (SLAB_NQ // NSA_REP + g, b * nq + i, 0)),
                  pl.BlockSpec((1, 1, nc, LANES), lambda b, g, i: (g, b, 0, 0)),
                  pl.BlockSpec((1, 1, nc, LANES), lambda b, g, i: (2 + g, b, 0, 0)),
                  kv_spec(SLAB_NKS), kv_spec(SLAB_NVS), kv_spec(SLAB_NKW), kv_spec(SLAB_NVW),
                  const((seq, LANES)), const((nc, LANES)), const((nc, LANES)),
                  pl.BlockSpec((1, NSA_REP, LANES), lambda b, g, i: (g, 0, 0)),
                  pl.BlockSpec((tq, LANES), lambda b, g, i: (b * nq + i, 0))],
        out_specs=pl.BlockSpec((tq, NSA_REP * LANES), lambda b, g, i: (b * nq + i, g)),
        out_shape=jax.ShapeDtypeStruct((m, NSA_HEADS * HEAD_DIM), BF16),
        scratch_shapes=[pltpu.VMEM((mrows, 1), F32), pltpu.VMEM((mrows, 1), F32),
                        pltpu.VMEM((mrows, LANES), F32)],
        compiler_params=_params(("parallel", "parallel", "arbitrary")),
        name="nsa_attn",
    )(slabs, kvc, kvc, slabs, slabs, slabs, slabs, kx, cx_cmp, ov, qx_nsa, misc)


def _merge_kernel(h_ref, of_ref, od_ref, on_ref, wg0_ref, wg1_ref, wg2_ref, wf_ref, wd_ref, wn_ref, o_ref):
    h = h_ref[...]
    acc = jax.nn.sigmoid(_dot(h, wg0_ref[...])) * _dot(of_ref[...], wf_ref[...])
    acc = acc + jax.nn.sigmoid(_dot(h, wg1_ref[...])) * _dot(od_ref[...], wd_ref[...])
    acc = acc + jax.nn.sigmoid(_dot(h, wg2_ref[...])) * _dot(on_ref[...], wn_ref[...])
    o_ref[...] = acc.astype(BF16)


def _merge(h, o_fox, o_diff, o_nsa, w_gate, wb_fox, wb_diff, wb_nsa):
    m, d = h.shape
    tm = _pick(m, (1024, 512, 256))
    tn = _pick(d, (256, 128))
    nj = d // tn
    row = lambda width: pl.BlockSpec((tm, width), lambda i, j: (i, 0))
    gate = lambda t: pl.BlockSpec((d, tn), lambda i, j: (0, t * nj + j))
    col = lambda k: pl.BlockSpec((k, tn), lambda i, j: (0, j))
    return pl.pallas_call(
        _merge_kernel,
        grid=(m // tm, nj),
        in_specs=[row(d), row(o_fox.shape[1]), row(o_diff.shape[1]), row(o_nsa.shape[1]),
                  gate(0), gate(1), gate(2),
                  col(wb_fox.shape[0]), col(wb_diff.shape[0]), col(wb_nsa.shape[0])],
        out_specs=pl.BlockSpec((tm, tn), lambda i, j: (i, j)),
        out_shape=jax.ShapeDtypeStruct((m, d), BF16),
        compiler_params=_params(("parallel", "arbitrary")),
        name="gate_merge",
    )(h, o_fox, o_diff, o_nsa, w_gate, w_gate, w_gate, wb_fox, wb_diff, wb_nsa)


def _wout_kernel(a_ref, w_ref, x_ref, gp_ref, gn_ref, xo_ref, ho_ref):
    y = _dot(a_ref[...], w_ref[...])
    x_new = x_ref[...] + _rms(y, gp_ref[...])
    xo_ref[...] = x_new
    ho_ref[...] = _rms(x_new, gn_ref[...]).astype(BF16)


def _wout(a, w, x, g_post, g_next):
    m, d = x.shape
    tm = _pick(m, (256, 128))
    row = lambda: pl.BlockSpec((tm, d), lambda i: (i, 0))
    vec = lambda: pl.BlockSpec((1, d), lambda i: (0, 0))
    return pl.pallas_call(
        _wout_kernel,
        grid=(m // tm,),
        in_specs=[row(), pl.BlockSpec((d, d), lambda i: (0, 0)), row(), vec(), vec()],
        out_specs=[row(), row()],
        out_shape=[jax.ShapeDtypeStruct((m, d), F32), jax.ShapeDtypeStruct((m, d), BF16)],
        compiler_params=_params(("parallel",)),
        name="out_proj",
    )(a, w, x, g_post, g_next)


def _ffn_up_kernel(h_ref, wg_ref, wu_ref, o_ref):
    h = h_ref[...]
    gate = _dot(h, wg_ref[...])
    o_ref[...] = (gate * jax.nn.sigmoid(gate) * _dot(h, wu_ref[...])).astype(BF16)


def _ffn_up(h, w_up):
    m, d = h.shape
    dff = w_up.shape[1] // 2
    tm = _pick(m, (1024, 512, 256))
    tn = _pick(dff, (512, 256, 128))
    nj = dff // tn
    return pl.pallas_call(
        _ffn_up_kernel,
        grid=(m // tm, nj),
        in_specs=[pl.BlockSpec((tm, d), lambda i, j: (i, 0)),
                  pl.BlockSpec((d, tn), lambda i, j: (0, j)),
                  pl.BlockSpec((d, tn), lambda i, j: (0, nj + j))],
        out_specs=pl.BlockSpec((tm, tn), lambda i, j: (i, j)),
        out_shape=jax.ShapeDtypeStruct((m, dff), BF16),
        compiler_params=_params(("parallel", "arbitrary")),
        name="ffn_up",
    )(h, w_up, w_up)


def _ffn_down_kernel(a_ref, w_ref, x_ref, gp_ref, gn_ref, xo_ref, ho_ref, acc_ref):
    k = pl.program_id(1)

    @pl.when(k == 0)
    def _():
        acc_ref[...] = jnp.zeros(acc_ref.shape, F32)

    acc_ref[...] += _dot(a_ref[...], w_ref[...])

    @pl.when(k == pl.num_programs(1) - 1)
    def _():
        x_new = x_ref[...] + _rms(acc_ref[...], gp_ref[...])
        xo_ref[...] = x_new
        ho_ref[...] = _rms(x_new, gn_ref[...]).astype(BF16)


def _ffn_down(a, w, x, g_post, g_next):
    m, d = x.shape
    dff = a.shape[1]
    tm = _pick(m, (512, 256, 128))
    tk = _pick(dff, (1408, 512, 256, 128))
    row = lambda: pl.BlockSpec((tm, d), lambda i, k: (i, 0))
    vec = lambda: pl.BlockSpec((1, d), lambda i, k: (0, 0))
    return pl.pallas_call(
        _ffn_down_kernel,
        grid=(m // tm, dff // tk),
        in_specs=[pl.BlockSpec((tm, tk), lambda i, k: (i, k)),
                  pl.BlockSpec((tk, d), lambda i, k: (k, 0)), row(), vec(), vec()],
        out_specs=[row(), row()],
        out_shape=[jax.ShapeDtypeStruct((m, d), F32), jax.ShapeDtypeStruct((m, d), BF16)],
        scratch_shapes=[pltpu.VMEM((tm, d), F32)],
        compiler_params=_params(("parallel", "arbitrary")),
        name="ffn_down",
    )(a, w, x, g_post, g_next)


def _key_aug_table(seq):
    j = jnp.arange(seq)
    lane = jnp.arange(LANES)
    onehot = (lane[None, :] == (j // SEL_BLOCK)[:, None]) & (lane[None, :] < SEL_BLOCK)
    tab = jnp.where(lane[None, :] == AUG_HI, (j // LANES)[:, None],
                    jnp.where(lane[None, :] == AUG_LO, (j % LANES)[:, None], onehot.astype(jnp.int32)))
    return tab.astype(BF16)


def _cmp_aug_table(nrow):
    end = jnp.arange(nrow) * CMP_STRIDE + CMP_BLOCK - 1
    lane = jnp.arange(LANES)
    tab = jnp.where(lane[None, :] == AUG_HI, (end // LANES)[:, None],
                    jnp.where(lane[None, :] == AUG_LO, (end % LANES)[:, None], 0))
    return tab.astype(BF16)


def _overlap_table(nrow, seq):
    start = jnp.arange(nrow) * CMP_STRIDE
    sel = jnp.arange(LANES) * SEL_BLOCK
    ov = ((start[:, None] < sel[None, :] + SEL_BLOCK) & (start[:, None] + CMP_BLOCK - 1 >= sel[None, :])
          & (jnp.arange(LANES)[None, :] < seq // SEL_BLOCK)
          & (jnp.arange(nrow)[:, None] < (seq - CMP_BLOCK) // CMP_STRIDE + 1))
    return ov.astype(BF16)


def _query_aug_rows(n_heads):
    slopes = 2.0 ** (-8.0 * jnp.arange(1, n_heads + 1, dtype=F32) / n_heads)
    lane = jnp.arange(LANES)
    return jnp.where(lane[None, :] == AUG_HI, slopes[:, None] * LANES,
                     jnp.where(lane[None, :] == AUG_LO, slopes[:, None], 0.0)).astype(F32)


def kernel(x, w_in, fox_forget_bias, diff_lambda, diff_subln, nsa_cmp_pos, nsa_cmp_w1, nsa_cmp_w2,
           w_branch_fox, w_branch_diff, w_branch_nsa, w_gate, w_out, norm_gains, w_ffn_up, w_ffn_down):
    batch, seq, d = x.shape
    depth = w_in.shape[0]
    m = batch * seq
    fw, dw, nw, kvw = FOX_HEADS * HEAD_DIM, DIFF_HEADS * HEAD_DIM, NSA_HEADS * HEAD_DIM, NSA_KV_HEADS * HEAD_DIM
    ff0 = 3 * fw
    dq0 = ff0 + FOX_HEADS
    ng0 = dq0 + 3 * dw + nw + 6 * kvw
    n_main = ng0 - FOX_HEADS
    assert n_main == N_SLABS * LANES and w_in.shape[2] == ng0 + 3 * NSA_HEADS

    w_main = jnp.concatenate([w_in[:, :, :ff0], w_in[:, :, dq0:ng0]], axis=2).astype(BF16)
    w_misc = jnp.concatenate([w_in[:, :, ff0:dq0], w_in[:, :, ng0:],
                              jnp.zeros((depth, d, LANES - FOX_HEADS - 3 * NSA_HEADS), F32)], axis=2).astype(BF16)
    colscale = jnp.ones((n_main,), F32)
    colscale = colscale.at[SLAB_FQ * LANES:SLAB_FK * LANES].set(HEAD_DIM ** -0.5)
    colscale = colscale.at[SLAB_DQ * LANES:SLAB_DK * LANES].set(DIFF_HALF ** -0.5)
    colscale = colscale.at[SLAB_NQ * LANES:SLAB_NKC * LANES].set(HEAD_DIM ** -0.5)
    colscale = colscale[None, :]
    fbias = jnp.pad(fox_forget_bias.astype(F32), ((0, 0), (0, LANES - FOX_HEADS)))[:, None, :]
    half = CMP_STRIDE * HEAD_DIM
    pos2 = nsa_cmp_pos.astype(F32).reshape(depth, 2, 2, half)
    w1 = nsa_cmp_w1.astype(BF16)
    w2 = nsa_cmp_w2.astype(BF16)
    wbf, wbd, wbn = w_branch_fox.astype(BF16), w_branch_diff.astype(BF16), w_branch_nsa.astype(BF16)
    wg, wo = w_gate.astype(BF16), w_out.astype(BF16)
    wup, wdn = w_ffn_up.astype(BF16), w_ffn_down.astype(BF16)
    gains = norm_gains.astype(F32)

    nrow = seq // CMP_STRIDE
    kx = _key_aug_table(seq)
    cx_cmp = _cmp_aug_table(nrow)
    ov = _overlap_table(nrow, seq)
    qx_diff = _query_aug_rows(DIFF_HEADS)[:, None, :]
    qx_nsa = _query_aug_rows(NSA_HEADS).reshape(NSA_KV_HEADS, NSA_REP, LANES)

    xf = x.reshape(m, d).astype(F32)
    h = _norm(xf, gains[0, 0][None, :])
    for l in range(depth):
        lam_init = 0.8 - 0.6 * math.exp(-0.3 * l)
        slabs = _inproj(h, w_main[l], colscale)
        misc = _miscproj(h, w_misc[l])
        cx_fox = _logf(misc, fbias[l], batch, seq)
        o_fox = _fox(slabs, cx_fox, batch, seq)
        o_diff = _diff(slabs, kx, qx_diff, diff_lambda[l].astype(F32), diff_subln[l].astype(F32)[None, :],
                       batch, seq, lam_init)
        ykv = slabs[SLAB_NKC:SLAB_NKC + 4].reshape(4, batch, nrow, half)
        kvc = _compress(ykv, pos2[l], w1[l], w2[l], batch)
        o_nsa = _nsa(slabs, kvc, kx, cx_cmp, ov, qx_nsa, misc, batch, seq)
        merged = _merge(h, o_fox, o_diff, o_nsa, wg[l], wbf[l], wbd[l], wbn[l])
        xf, h2 = _wout(merged, wo[l], xf, gains[l, 1][None, :], gains[l, 2][None, :])
        act = _ffn_up(h2, wup[l])
        g_next = gains[min(l + 1, depth - 1), 0][None, :]
        xf, h = _ffn_down(act, wdn[l], xf, gains[l, 3][None, :], g_next)
    return xf.reshape(batch, seq, d).astype(x.dtype)
```

```python
import functools
import math

import jax
import jax.numpy as jnp
from jax import lax
from jax.experimental import pallas as pl
from jax.experimental.pallas import tpu as pltpu

F32 = jnp.float32
BF16 = jnp.bfloat16

HEAD_DIM = 128
FOX_HEADS = 4
DIFF_HEADS = 4
DIFF_HALF = HEAD_DIM // 2
NSA_HEADS = 8
NSA_KV_HEADS = 2
NSA_REP = NSA_HEADS // NSA_KV_HEADS
CMP_BLOCK = 32
CMP_STRIDE = 16
CMP_HIDDEN = 256
SEL_BLOCK = 64
SEL_TOPK = 16
WINDOW = 512
N_BRANCHES = 3
EPS = 1e-6
NEG_INF = -1e30
FORCE_SCORE = 1e4
SEL_MASK = -32768.0
LOG2E = math.log2(math.e)
LANES = 128

SLAB_FQ, SLAB_FK, SLAB_FV = 0, 4, 8
SLAB_DQ, SLAB_DK, SLAB_DV = 12, 16, 20
SLAB_NQ = 24
SLAB_NKC, SLAB_NVC, SLAB_NKS, SLAB_NVS, SLAB_NKW, SLAB_NVW = 32, 34, 36, 38, 40, 42
N_SLABS = 44
MISC_FF = 0
MISC_NG = 4
AUG_HI = 64
AUG_LO = 67
AUG_TERMS = 3

VMEM_LIMIT = 56 * 1024 * 1024


def _pick(n, prefs):
    for p in prefs:
        if p <= n and n % p == 0:
            return p
    return n


def _params(sem):
    return pltpu.CompilerParams(dimension_semantics=sem, vmem_limit_bytes=VMEM_LIMIT)


def _rms(y, g):
    return y * lax.rsqrt(jnp.mean(y * y, axis=-1, keepdims=True) + EPS) * g


def _dot(a, b):
    return jnp.dot(a, b, preferred_element_type=F32)


def _dot_nt(a, b):
    return lax.dot_general(a, b, (((1,), (1,)), ((), ())), preferred_element_type=F32)


def _split3(x):
    hi = x.astype(BF16)
    r = x - hi.astype(F32)
    mid = r.astype(BF16)
    lo = (r - mid.astype(F32)).astype(BF16)
    return hi, mid, lo


def _norm_kernel(x_ref, g_ref, h_ref):
    h_ref[...] = _rms(x_ref[...], g_ref[...]).astype(BF16)


def _norm(x, g):
    m, d = x.shape
    tm = _pick(m, (512, 256, 128))
    return pl.pallas_call(
        _norm_kernel,
        grid=(m // tm,),
        in_specs=[pl.BlockSpec((tm, d), lambda i: (i, 0)), pl.BlockSpec((1, d), lambda i: (0, 0))],
        out_specs=pl.BlockSpec((tm, d), lambda i: (i, 0)),
        out_shape=jax.ShapeDtypeStruct((m, d), BF16),
        compiler_params=_params(("parallel",)),
        name="norm_in",
    )(x, g)


def _inproj_kernel(h_ref, w_ref, cs_ref, o_ref):
    acc = _dot(h_ref[...], w_ref[...]) * cs_ref[...]
    for s in range(o_ref.shape[0]):
        o_ref[s] = acc[:, s * LANES:(s + 1) * LANES].astype(BF16)


def _inproj(h, w_main, colscale):
    m, d = h.shape
    n = w_main.shape[1]
    tm = _pick(m, (1024, 512, 256))
    tn = 512
    return pl.pallas_call(
        _inproj_kernel,
        grid=(m // tm, n // tn),
        in_specs=[pl.BlockSpec((tm, d), lambda i, j: (i, 0)),
                  pl.BlockSpec((d, tn), lambda i, j: (0, j)),
                  pl.BlockSpec((1, tn), lambda i, j: (0, j))],
        out_specs=pl.BlockSpec((tn // LANES, tm, LANES), lambda i, j: (j, i, 0)),
        out_shape=jax.ShapeDtypeStruct((n // LANES, m, LANES), BF16),
        compiler_params=_params(("parallel", "arbitrary")),
        name="inproj",
    )(h, w_main, colscale)


def _misc_kernel(h_ref, w_ref, o_ref):
    o_ref[...] = _dot(h_ref[...], w_ref[...])


def _miscproj(h, w_misc):
    m, d = h.shape
    tm = _pick(m, (1024, 512, 256))
    return pl.pallas_call(
        _misc_kernel,
        grid=(m // tm,),
        in_specs=[pl.BlockSpec((tm, d), lambda i: (i, 0)), pl.BlockSpec((d, LANES), lambda i: (0, 0))],
        out_specs=pl.BlockSpec((tm, LANES), lambda i: (i, 0)),
        out_shape=jax.ShapeDtypeStruct((m, LANES), F32),
        compiler_params=_params(("parallel",)),
        name="miscproj",
    )(h, w_misc)


def _logf_kernel(misc_ref, bias_ref, o_ref, *, tc):
    s = misc_ref.shape[0]
    row = lax.broadcasted_iota(jnp.int32, (tc, tc), 0)
    col = lax.broadcasted_iota(jnp.int32, (tc, tc), 1)
    tri = jnp.where(col <= row, 1.0, 0.0).astype(BF16)
    lane = lax.broadcasted_iota(jnp.int32, (tc, LANES), 1)

    def chunk(c, carry):
        r0 = pl.multiple_of(c * tc, tc)
        z = misc_ref[pl.ds(r0, tc), :] + bias_ref[...]
        lf = jnp.minimum(z, 0.0) - jnp.log1p(jnp.exp(-jnp.abs(z)))
        hi, mid, lo = _split3(lf)
        cum = _dot(tri, hi) + _dot(tri, mid) + _dot(tri, lo) + carry
        for hd in range(FOX_HEADS):
            c2 = jnp.broadcast_to(cum[:, hd:hd + 1], (tc, LANES)) * LOG2E
            c_hi = c2.astype(BF16).astype(F32)
            c_mid = (c2 - c_hi).astype(BF16).astype(F32)
            aug = jnp.where(lane == 0, c_hi, jnp.where(lane == 1, c_mid,
                                                      jnp.where(lane == 2, c2 - c_hi - c_mid, 0.0)))
            o_ref[hd, pl.ds(r0, tc), :] = aug.astype(BF16)
        return cum[tc - 1:tc, :]

    lax.fori_loop(0, s // tc, chunk, jnp.zeros((1, LANES), F32))


def _logf(misc, bias_row, batch, seq):
    tc = _pick(seq, (256, 128))
    return pl.pallas_call(
        functools.partial(_logf_kernel, tc=tc),
        grid=(batch,),
        in_specs=[pl.BlockSpec((seq, LANES), lambda b: (b, 0)), pl.BlockSpec((1, LANES), lambda b: (0, 0))],
        out_specs=pl.BlockSpec((FOX_HEADS, seq, LANES), lambda b: (0, b, 0)),
        out_shape=jax.ShapeDtypeStruct((FOX_HEADS, batch * seq, LANES), BF16),
        compiler_params=_params(("parallel",)),
        name="fox_logf",
    )(misc, bias_row)


def _softmax_init(m_ref, l_ref, acc_ref):
    m_ref[...] = jnp.full(m_ref.shape, NEG_INF, F32)
    l_ref[...] = jnp.zeros(l_ref.shape, F32)
    acc_ref[...] = jnp.zeros(acc_ref.shape, F32)


def _softmax_step(s, vt, m_ref, l_ref, acc_ref):
    m_prev = m_ref[...]
    m_new = jnp.maximum(m_prev, jnp.max(s, axis=0, keepdims=True))
    alpha = jnp.exp2(m_prev - m_new)
    p = jnp.exp2(s - m_new)
    l_ref[...] = alpha * l_ref[...] + jnp.sum(p, axis=0, keepdims=True)
    acc_ref[...] = alpha * acc_ref[...] + _dot(vt, p.astype(BF16))
    m_ref[...] = m_new


def _flash_tiles(qa, k_ref, kx_ref, vt_ref, n_full, col_pos, tk, s_ref, m_ref, l_ref, acc_ref):
    def logits(j):
        k0 = pl.multiple_of(j * tk, tk)
        ka = jnp.concatenate([k_ref[0, pl.ds(k0, tk), :], kx_ref[pl.ds(k0, tk), :]], axis=1)
        return _dot_nt(ka, qa)

    s_ref[...] = logits(0)

    def full(j, c):
        k0 = pl.multiple_of(j * tk, tk)
        s_next = logits(j + 1)
        _softmax_step(s_ref[...], vt_ref[:, pl.ds(k0, tk)], m_ref, l_ref, acc_ref)
        s_ref[...] = s_next
        return c

    lax.fori_loop(0, n_full, full, 0)
    k0 = pl.multiple_of(n_full * tk, tk)
    s = s_ref[...]
    kpos = k0 + lax.broadcasted_iota(jnp.int32, s.shape, 0)
    s = jnp.where(kpos <= col_pos, s, NEG_INF)
    _softmax_step(s, vt_ref[:, pl.ds(k0, tk)], m_ref, l_ref, acc_ref)


def _transpose_into(vt_ref, v_ref, chunk):
    def body(c, carry):
        r0 = pl.multiple_of(c * chunk, chunk)
        vt_ref[:, pl.ds(r0, chunk)] = v_ref[0, pl.ds(r0, chunk), :].astype(F32).T.astype(BF16)
        return carry

    lax.fori_loop(0, v_ref.shape[1] // chunk, body, 0)


def _fox_kernel(q_ref, k_ref, v_ref, cx_ref, o_ref, vt_ref, s_ref, m_ref, l_ref, acc_ref, *, t):
    qi = pl.program_id(2)

    @pl.when(qi == 0)
    def _():
        _transpose_into(vt_ref, v_ref, t)

    lane = lax.broadcasted_iota(jnp.int32, (t, LANES), 1)
    qx = jnp.where(lane < 3, -1.0, 0.0).astype(BF16)
    qa = jnp.concatenate([q_ref[0], qx], axis=1)
    col_pos = qi * t + lax.broadcasted_iota(jnp.int32, (1, t), 1)
    _softmax_init(m_ref, l_ref, acc_ref)
    _flash_tiles(qa, k_ref, cx_ref.at[0], vt_ref, qi, col_pos, t, s_ref, m_ref, l_ref, acc_ref)
    o_ref[...] = (acc_ref[...] / l_ref[...]).T.astype(BF16)


def _fox(slabs, cx, batch, seq):
    t = _pick(seq, (512, 256, 128))
    nq = seq // t
    m = batch * seq
    kv_spec = lambda base: pl.BlockSpec((1, seq, LANES), lambda b, h, i: (base + h, b, 0))
    return pl.pallas_call(
        functools.partial(_fox_kernel, t=t),
        grid=(batch, FOX_HEADS, nq),
        in_specs=[pl.BlockSpec((1, t, LANES), lambda b, h, i: (SLAB_FQ + h, b * nq + i, 0)),
                  kv_spec(SLAB_FK), kv_spec(SLAB_FV),
                  pl.BlockSpec((1, seq, LANES), lambda b, h, i: (h, b, 0))],
        out_specs=pl.BlockSpec((t, LANES), lambda b, h, i: (b * nq + i, h)),
        out_shape=jax.ShapeDtypeStruct((m, FOX_HEADS * HEAD_DIM), BF16),
        scratch_shapes=[pltpu.VMEM((LANES, seq), BF16), pltpu.VMEM((t, t), F32),
                        pltpu.VMEM((1, t), F32), pltpu.VMEM((1, t), F32), pltpu.VMEM((LANES, t), F32)],
        compiler_params=_params(("parallel", "parallel", "arbitrary")),
        name="fox_attn",
    )(slabs, slabs, slabs, cx)


def _diff_kernel(q_ref, k_ref, v_ref, kx_ref, qx_ref, lam_ref, sub_ref, o_ref, vt_ref, s_ref, m_ref, l_ref,
                 acc_ref, *, t, lam_init):
    qi = pl.program_id(2)

    @pl.when(qi == 0)
    def _():
        _transpose_into(vt_ref, v_ref, t)

    q = q_ref[0].astype(F32)
    lane = lax.broadcasted_iota(jnp.int32, (t, LANES), 1)
    qx = jnp.broadcast_to(qx_ref[0], (t, LANES)).astype(BF16)
    qa = jnp.concatenate([
        jnp.concatenate([jnp.where(lane < DIFF_HALF, q, 0.0).astype(BF16), qx], axis=1),
        jnp.concatenate([jnp.where(lane >= DIFF_HALF, q, 0.0).astype(BF16), qx], axis=1)], axis=0)
    col = lax.broadcasted_iota(jnp.int32, (1, 2 * t), 1)
    col_pos = qi * t + jnp.where(col >= t, col - t, col)
    _softmax_init(m_ref, l_ref, acc_ref)
    _flash_tiles(qa, k_ref, kx_ref, vt_ref, qi, col_pos, t, s_ref, m_ref, l_ref, acc_ref)
    lv = lam_ref[...]
    lam = (jnp.exp(jnp.sum(lv[0:1] * lv[1:2], axis=-1, keepdims=True))
           - jnp.exp(jnp.sum(lv[2:3] * lv[3:4], axis=-1, keepdims=True)) + lam_init)
    o = acc_ref[...] / l_ref[...]
    o = (o[:, :t] - lam * o[:, t:]).T
    o_ref[...] = (_rms(o, sub_ref[...]) * (1.0 - lam_init)).astype(BF16)


def _diff(slabs, kx, qx_diff, lam_vec, subln, batch, seq, lam_init):
    t = _pick(seq, (512, 256, 128))
    nq = seq // t
    m = batch * seq
    kv_spec = lambda base: pl.BlockSpec((1, seq, LANES), lambda b, h, i: (base + h, b, 0))
    return pl.pallas_call(
        functools.partial(_diff_kernel, t=t, lam_init=lam_init),
        grid=(batch, DIFF_HEADS, nq),
        in_specs=[pl.BlockSpec((1, t, LANES), lambda b, h, i: (SLAB_DQ + h, b * nq + i, 0)),
                  kv_spec(SLAB_DK), kv_spec(SLAB_DV),
                  pl.BlockSpec((seq, LANES), lambda b, h, i: (0, 0)),
                  pl.BlockSpec((1, 1, LANES), lambda b, h, i: (h, 0, 0)),
                  pl.BlockSpec((4, DIFF_HALF), lambda b, h, i: (0, 0)),
                  pl.BlockSpec((1, LANES), lambda b, h, i: (0, 0))],
        out_specs=pl.BlockSpec((t, LANES), lambda b, h, i: (b * nq + i, h)),
        out_shape=jax.ShapeDtypeStruct((m, DIFF_HEADS * HEAD_DIM), BF16),
        scratch_shapes=[pltpu.VMEM((LANES, seq), BF16), pltpu.VMEM((t, 2 * t), F32),
                        pltpu.VMEM((1, 2 * t), F32), pltpu.VMEM((1, 2 * t), F32),
                        pltpu.VMEM((LANES, 2 * t), F32)],
        compiler_params=_params(("parallel", "parallel", "arbitrary")),
        name="diff_attn",
    )(slabs, slabs, slabs, kx, qx_diff, lam_vec, subln)


def _compress_kernel(y_ref, pos_ref, w1_ref, w2_ref, o_ref):
    half = y_ref.shape[3]
    y = y_ref[0, 0].astype(F32)
    top = (y + pos_ref[0, 0:1, :]).astype(BF16)
    bot = (y + pos_ref[0, 1:2, :]).astype(BF16)
    a = _dot(top, w1_ref[0, 0:half, :])
    b = _dot(bot, w1_ref[0, half:2 * half, :])
    nrow = a.shape[0]
    hid = a + pltpu.roll(b, nrow - 1, 0)
    hid = hid * jax.nn.sigmoid(hid)
    o_ref[0, 0] = _dot(hid.astype(BF16), w2_ref[0]).astype(BF16)


def _compress(ykv, pos2, w1, w2, batch):
    nrow, half = ykv.shape[2], ykv.shape[3]
    return pl.pallas_call(
        _compress_kernel,
        grid=(4, batch),
        in_specs=[pl.BlockSpec((1, 1, nrow, half), lambda s, b: (s, b, 0, 0)),
                  pl.BlockSpec((1, 2, half), lambda s, b: (s // 2, 0, 0)),
                  pl.BlockSpec((1, 2 * half, CMP_HIDDEN), lambda s, b: (s // 2, 0, 0)),
                  pl.BlockSpec((1, CMP_HIDDEN, HEAD_DIM), lambda s, b: (s // 2, 0, 0))],
        out_specs=pl.BlockSpec((1, 1, nrow, HEAD_DIM), lambda s, b: (s, b, 0, 0)),
        out_shape=jax.ShapeDtypeStruct((4, batch, nrow, HEAD_DIM), BF16),
        compiler_params=_params(("parallel", "parallel")),
        name="nsa_compress",
    )(ykv, pos2, w1, w2)


def _nsa_kernel(q_ref, kc_ref, vc_ref, ks_ref, vs_ref, kw_ref, vw_ref, kx_ref, cx_ref, ovt_ref, qx_ref,
                misc_ref, o_ref, vst_ref, vwt_ref, vct_ref, gt_ref, s_ref, m_ref, l_ref, acc_ref, *, tq, tk):
    g = pl.program_id(1)
    qi = pl.program_id(2)
    mcols = NSA_REP * tq
    q0 = qi * tq
    nc = kc_ref.shape[2]
    nsel = kx_ref.shape[0] // SEL_BLOCK

    @pl.when(qi == 0)
    def _():
        _transpose_into(vst_ref, vs_ref, tk)
        _transpose_into(vwt_ref, vw_ref, tk)
        vct_ref[...] = vc_ref[0, 0].astype(F32).T.astype(BF16)

    q4 = q_ref[...].reshape(mcols, LANES)
    qx = jnp.concatenate([jnp.broadcast_to(qx_ref[0, r:r + 1, :], (tq, LANES)) for r in range(NSA_REP)], axis=0)
    qa = jnp.concatenate([q4, qx.astype(BF16)], axis=1)
    col = lax.broadcasted_iota(jnp.int32, (1, mcols), 1)
    col_pos = q0 + (col & (tq - 1))

    kca = jnp.concatenate([kc_ref[0, 0], cx_ref[...]], axis=1)
    sc = _dot_nt(kca, qa)
    cend = lax.broadcasted_iota(jnp.int32, (nc, mcols), 0) * CMP_STRIDE + (CMP_BLOCK - 1)
    sc = jnp.where(cend <= col_pos, sc, NEG_INF)
    e = jnp.exp2(sc - jnp.max(sc, axis=0, keepdims=True))
    p = e / jnp.sum(e, axis=0, keepdims=True)
    p = jnp.where(col_pos >= CMP_BLOCK - 1, p, 0.0)
    o_cmp = _dot(vct_ref[...], p.astype(BF16))

    psum = p[:, 0:tq]
    for r in range(1, NSA_REP):
        psum = psum + p[:, r * tq:(r + 1) * tq]
    p_hi = psum.astype(BF16)
    p_lo = (psum - p_hi.astype(F32)).astype(BF16)
    imp = _dot(ovt_ref[...], p_hi) + _dot(ovt_ref[...], p_lo)
    qpos = q0 + lax.broadcasted_iota(jnp.int32, (LANES, tq), 1)
    blk = lax.broadcasted_iota(jnp.int32, (LANES, tq), 0)
    cur = jnp.right_shift(qpos, SEL_BLOCK.bit_length() - 1)
    forced = (blk == 0) | (blk == cur) | (blk == cur - 1)
    imp = jnp.where(blk <= cur, jnp.where(forced, FORCE_SCORE, imp), -1.0)
    cand = imp[0:nsel]
    jidx = blk[0:nsel]
    rank = jnp.zeros((nsel, tq), jnp.int32)
    for k in range(nsel):
        rk = cand[k:k + 1, :]
        rank = rank + ((rk > cand) | ((rk == cand) & (jidx > k))).astype(jnp.int32)
    bias_t = jnp.where(rank < min(SEL_TOPK, nsel), 0.0, SEL_MASK)
    if nsel < LANES:
        bias_t = jnp.concatenate([bias_t, jnp.zeros((LANES - nsel, tq), F32)], axis=0)
    selbias = jnp.concatenate([bias_t.T] * NSA_REP, axis=0)
    lane4 = lax.broadcasted_iota(jnp.int32, (mcols, LANES), 1)
    qa_sel = jnp.concatenate([q4, jnp.where(lane4 < SEL_BLOCK, selbias, qx).astype(BF16)], axis=1)

    _softmax_init(m_ref, l_ref, acc_ref)
    _flash_tiles(qa_sel, ks_ref, kx_ref, vst_ref, q0 // tk, col_pos, tk, s_ref, m_ref, l_ref, acc_ref)
    o_sel = acc_ref[...] / l_ref[...]

    span = WINDOW + tq
    w0 = pl.multiple_of(jnp.maximum(q0 - WINDOW, 0), tq)
    kwa = jnp.concatenate([kw_ref[0, pl.ds(w0, span), :], kx_ref[pl.ds(w0, span), :]], axis=1)
    sw = _dot_nt(kwa, qa)
    dist = col_pos - (w0 + lax.broadcasted_iota(jnp.int32, sw.shape, 0))
    sw = jnp.where((dist >= 0) & (dist < WINDOW), sw, NEG_INF)
    ew = jnp.exp2(sw - jnp.max(sw, axis=0, keepdims=True))
    o_win = _dot(vwt_ref[:, pl.ds(w0, span)], ew.astype(BF16)) / jnp.sum(ew, axis=0, keepdims=True)

    gt_ref[...] = jax.nn.sigmoid(misc_ref[...]).T
    for r in range(NSA_REP):
        c0 = MISC_NG + 3 * (NSA_REP * g + r)
        sl = slice(r * tq, (r + 1) * tq)
        out = (gt_ref[pl.ds(c0, 1), :] * o_cmp[:, sl] + gt_ref[pl.ds(c0 + 1, 1), :] * o_sel[:, sl]
               + gt_ref[pl.ds(c0 + 2, 1), :] * o_win[:, sl])
        o_ref[:, r * LANES:(r + 1) * LANES] = out.T.astype(BF16)


def _nsa(slabs, kvc, kx, cx_cmp, ovt, qx_nsa, misc, batch, seq):
    tq = 128
    tk = _pick(seq, (512, 256, 128))
    nq = seq // tq
    m = batch * seq
    mcols = NSA_REP * tq
    nc = kvc.shape[2]
    kv_spec = lambda base: pl.BlockSpec((1, seq, LANES), lambda b, g, i: (base + g, b, 0))
    const = lambda shape: pl.BlockSpec(shape, lambda b, g, i: (0,) * len(shape))
    return pl.pallas_call(
        functools.partial(_nsa_kernel, tq=tq, tk=tk),
        grid=(batch, NSA_KV_HEADS, nq),
        in_specs=[pl.BlockSpec((NSA_REP, tq, LANES), lambda b, g, i: (SLAB_NQ // NSA_REP + g, b * nq + i, 0)),
                  pl.BlockSpec((1, 1, nc, LANES), lambda b, g, i: (g, b, 0, 0)),
                  pl.BlockSpec((1, 1, nc, LANES), lambda b, g, i: (2 + g, b, 0, 0)),
                  kv_spec(SLAB_NKS), kv_spec(SLAB_NVS), kv_spec(SLAB_NKW), kv_spec(SLAB_NVW),
                  const((seq, LANES)), const((nc, LANES)), const((LANES, nc)),
                  pl.BlockSpec((1, NSA_REP, LANES), lambda b, g, i: (g, 0, 0)),
                  pl.BlockSpec((tq, LANES), lambda b, g, i: (b * nq + i, 0))],
        out_specs=pl.BlockSpec((tq, NSA_REP * LANES), lambda b, g, i: (b * nq + i, g)),
        out_shape=jax.ShapeDtypeStruct((m, NSA_HEADS * HEAD_DIM), BF16),
        scratch_shapes=[pltpu.VMEM((LANES, seq), BF16), pltpu.VMEM((LANES, seq), BF16),
                        pltpu.VMEM((LANES, nc), BF16), pltpu.VMEM((LANES, tq), F32),
                        pltpu.VMEM((tk, mcols), F32), pltpu.VMEM((1, mcols), F32), pltpu.VMEM((1, mcols), F32),
                        pltpu.VMEM((LANES, mcols), F32)],
        compiler_params=_params(("parallel", "parallel", "arbitrary")),
        name="nsa_attn",
    )(slabs, kvc, kvc, slabs, slabs, slabs, slabs, kx, cx_cmp, ovt, qx_nsa, misc)


def _merge_kernel(h_ref, of_ref, od_ref, on_ref, wg0_ref, wg1_ref, wg2_ref, wf_ref, wd_ref, wn_ref, o_ref):
    h = h_ref[...]
    acc = jax.nn.sigmoid(_dot(h, wg0_ref[...])) * _dot(of_ref[...], wf_ref[...])
    acc = acc + jax.nn.sigmoid(_dot(h, wg1_ref[...])) * _dot(od_ref[...], wd_ref[...])
    acc = acc + jax.nn.sigmoid(_dot(h, wg2_ref[...])) * _dot(on_ref[...], wn_ref[...])
    o_ref[...] = acc.astype(BF16)


def _merge(h, o_fox, o_diff, o_nsa, w_gate, wb_fox, wb_diff, wb_nsa):
    m, d = h.shape
    tm = _pick(m, (1024, 512, 256))
    tn = _pick(d, (256, 128))
    nj = d // tn
    row = lambda width: pl.BlockSpec((tm, width), lambda i, j: (i, 0))
    gate = lambda t: pl.BlockSpec((d, tn), lambda i, j: (0, t * nj + j))
    col = lambda k: pl.BlockSpec((k, tn), lambda i, j: (0, j))
    return pl.pallas_call(
        _merge_kernel,
        grid=(m // tm, nj),
        in_specs=[row(d), row(o_fox.shape[1]), row(o_diff.shape[1]), row(o_nsa.shape[1]),
                  gate(0), gate(1), gate(2),
                  col(wb_fox.shape[0]), col(wb_diff.shape[0]), col(wb_nsa.shape[0])],
        out_specs=pl.BlockSpec((tm, tn), lambda i, j: (i, j)),
        out_shape=jax.ShapeDtypeStruct((m, d), BF16),
        compiler_params=_params(("parallel", "arbitrary")),
        name="gate_merge",
    )(h, o_fox, o_diff, o_nsa, w_gate, w_gate, w_gate, wb_fox, wb_diff, wb_nsa)


def _wout_kernel(a_ref, w_ref, x_ref, gp_ref, gn_ref, xo_ref, ho_ref):
    y = _dot(a_ref[...], w_ref[...])
    x_new = x_ref[...] + _rms(y, gp_ref[...])
    xo_ref[...] = x_new
    ho_ref[...] = _rms(x_new, gn_ref[...]).astype(BF16)


def _wout(a, w, x, g_post, g_next):
    m, d = x.shape
    tm = _pick(m, (256, 128))
    row = lambda: pl.BlockSpec((tm, d), lambda i: (i, 0))
    vec = lambda: pl.BlockSpec((1, d), lambda i: (0, 0))
    return pl.pallas_call(
        _wout_kernel,
        grid=(m // tm,),
        in_specs=[row(), pl.BlockSpec((d, d), lambda i: (0, 0)), row(), vec(), vec()],
        out_specs=[row(), row()],
        out_shape=[jax.ShapeDtypeStruct((m, d), F32), jax.ShapeDtypeStruct((m, d), BF16)],
        compiler_params=_params(("parallel",)),
        name="out_proj",
    )(a, w, x, g_post, g_next)


def _ffn_up_kernel(h_ref, wg_ref, wu_ref, o_ref):
    h = h_ref[...]
    gate = _dot(h, wg_ref[...])
    o_ref[...] = (gate * jax.nn.sigmoid(gate) * _dot(h, wu_ref[...])).astype(BF16)


def _ffn_up(h, w_up):
    m, d = h.shape
    dff = w_up.shape[1] // 2
    tm = _pick(m, (1024, 512, 256))
    tn = _pick(dff, (512, 256, 128))
    nj = dff // tn
    return pl.pallas_call(
        _ffn_up_kernel,
        grid=(m // tm, nj),
        in_specs=[pl.BlockSpec((tm, d), lambda i, j: (i, 0)),
                  pl.BlockSpec((d, tn), lambda i, j: (0, j)),
                  pl.BlockSpec((d, tn), lambda i, j: (0, nj + j))],
        out_specs=pl.BlockSpec((tm, tn), lambda i, j: (i, j)),
        out_shape=jax.ShapeDtypeStruct((m, dff), BF16),
        compiler_params=_params(("parallel", "arbitrary")),
        name="ffn_up",
    )(h, w_up, w_up)


def _ffn_down_kernel(a_ref, w_ref, x_ref, gp_ref, gn_ref, xo_ref, ho_ref, acc_ref):
    k = pl.program_id(1)

    @pl.when(k == 0)
    def _():
        acc_ref[...] = jnp.zeros(acc_ref.shape, F32)

    acc_ref[...] += _dot(a_ref[...], w_ref[...])

    @pl.when(k == pl.num_programs(1) - 1)
    def _():
        x_new = x_ref[...] + _rms(acc_ref[...], gp_ref[...])
        xo_ref[...] = x_new
        ho_ref[...] = _rms(x_new, gn_ref[...]).astype(BF16)


def _ffn_down(a, w, x, g_post, g_next):
    m, d = x.shape
    dff = a.shape[1]
    tm = _pick(m, (512, 256, 128))
    tk = _pick(dff, (1408, 512, 256, 128))
    row = lambda: pl.BlockSpec((tm, d), lambda i, k: (i, 0))
    vec = lambda: pl.BlockSpec((1, d), lambda i, k: (0, 0))
    return pl.pallas_call(
        _ffn_down_kernel,
        grid=(m // tm, dff // tk),
        in_specs=[pl.BlockSpec((tm, tk), lambda i, k: (i, k)),
                  pl.BlockSpec((tk, d), lambda i, k: (k, 0)), row(), vec(), vec()],
        out_specs=[row(), row()],
        out_shape=[jax.ShapeDtypeStruct((m, d), F32), jax.ShapeDtypeStruct((m, d), BF16)],
        scratch_shapes=[pltpu.VMEM((tm, d), F32)],
        compiler_params=_params(("parallel", "arbitrary")),
        name="ffn_down",
    )(a, w, x, g_post, g_next)


def _pos_columns(pos):
    lane = jnp.arange(LANES)[None, :]
    hi = (lane >= AUG_HI) & (lane < AUG_HI + AUG_TERMS)
    lo = (lane >= AUG_LO) & (lane < AUG_LO + AUG_TERMS)
    return jnp.where(hi, (pos // LANES)[:, None], jnp.where(lo, (pos % LANES)[:, None], 0))


def _key_aug_table(seq):
    j = jnp.arange(seq)
    lane = jnp.arange(LANES)[None, :]
    onehot = (lane == (j // SEL_BLOCK)[:, None]) & (lane < SEL_BLOCK)
    return (_pos_columns(j) + onehot.astype(jnp.int32)).astype(BF16)


def _cmp_aug_table(nrow):
    return _pos_columns(jnp.arange(nrow) * CMP_STRIDE + CMP_BLOCK - 1).astype(BF16)


def _overlap_table_t(nrow, seq):
    start = jnp.arange(nrow)[None, :] * CMP_STRIDE
    blk = jnp.arange(LANES)[:, None]
    sel = blk * SEL_BLOCK
    ov = ((start < sel + SEL_BLOCK) & (start + CMP_BLOCK - 1 >= sel) & (blk < seq // SEL_BLOCK)
          & (jnp.arange(nrow)[None, :] < (seq - CMP_BLOCK) // CMP_STRIDE + 1))
    return ov.astype(BF16)


def _query_aug_rows(n_heads):
    slopes = 2.0 ** (-8.0 * jnp.arange(1, n_heads + 1, dtype=F32) / n_heads)
    terms = [t.astype(F32) for t in _split3(jnp.float32(LOG2E))]
    lane = jnp.arange(LANES)[None, :]
    out = jnp.zeros((n_heads, LANES), F32)
    for i, t in enumerate(terms):
        out = jnp.where(lane == AUG_HI + i, slopes[:, None] * t * LANES, out)
        out = jnp.where(lane == AUG_LO + i, slopes[:, None] * t, out)
    return out


def kernel(x, w_in, fox_forget_bias, diff_lambda, diff_subln, nsa_cmp_pos, nsa_cmp_w1, nsa_cmp_w2,
           w_branch_fox, w_branch_diff, w_branch_nsa, w_gate, w_out, norm_gains, w_ffn_up, w_ffn_down):
    batch, seq, d = x.shape
    depth = w_in.shape[0]
    m = batch * seq
    fw, dw, nw, kvw = FOX_HEADS * HEAD_DIM, DIFF_HEADS * HEAD_DIM, NSA_HEADS * HEAD_DIM, NSA_KV_HEADS * HEAD_DIM
    ff0 = 3 * fw
    dq0 = ff0 + FOX_HEADS
    ng0 = dq0 + 3 * dw + nw + 6 * kvw
    n_main = ng0 - FOX_HEADS
    assert n_main == N_SLABS * LANES and w_in.shape[2] == ng0 + 3 * NSA_HEADS

    w_main = jnp.concatenate([w_in[:, :, :ff0], w_in[:, :, dq0:ng0]], axis=2).astype(BF16)
    w_misc = jnp.concatenate([w_in[:, :, ff0:dq0], w_in[:, :, ng0:],
                              jnp.zeros((depth, d, LANES - FOX_HEADS - 3 * NSA_HEADS), F32)], axis=2).astype(BF16)
    colscale = jnp.ones((n_main,), F32)
    colscale = colscale.at[SLAB_FQ * LANES:SLAB_FK * LANES].set(HEAD_DIM ** -0.5 * LOG2E)
    colscale = colscale.at[SLAB_DQ * LANES:SLAB_DK * LANES].set(DIFF_HALF ** -0.5 * LOG2E)
    colscale = colscale.at[SLAB_NQ * LANES:SLAB_NKC * LANES].set(HEAD_DIM ** -0.5 * LOG2E)
    colscale = colscale[None, :]
    fbias = jnp.pad(fox_forget_bias.astype(F32), ((0, 0), (0, LANES - FOX_HEADS)))[:, None, :]
    half = CMP_STRIDE * HEAD_DIM
    pos2 = nsa_cmp_pos.astype(F32).reshape(depth, 2, 2, half)
    w1 = nsa_cmp_w1.astype(BF16)
    w2 = nsa_cmp_w2.astype(BF16)
    wbf, wbd, wbn = w_branch_fox.astype(BF16), w_branch_diff.astype(BF16), w_branch_nsa.astype(BF16)
    wg, wo = w_gate.astype(BF16), w_out.astype(BF16)
    wup, wdn = w_ffn_up.astype(BF16), w_ffn_down.astype(BF16)
    gains = norm_gains.astype(F32)

    nrow = seq // CMP_STRIDE
    kx = _key_aug_table(seq)
    cx_cmp = _cmp_aug_table(nrow)
    ovt = _overlap_table_t(nrow, seq)
    qx_diff = _query_aug_rows(DIFF_HEADS)[:, None, :]
    qx_nsa = _query_aug_rows(NSA_HEADS).reshape(NSA_KV_HEADS, NSA_REP, LANES)

    xf = x.reshape(m, d).astype(F32)
    h = _norm(xf, gains[0, 0][None, :])
    for l in range(depth):
        lam_init = 0.8 - 0.6 * math.exp(-0.3 * l)
        slabs = _inproj(h, w_main[l], colscale)
        misc = _miscproj(h, w_misc[l])
        cx_fox = _logf(misc, fbias[l], batch, seq)
        o_fox = _fox(slabs, cx_fox, batch, seq)
        o_diff = _diff(slabs, kx, qx_diff, diff_lambda[l].astype(F32), diff_subln[l].astype(F32)[None, :],
                       batch, seq, lam_init)
        ykv = slabs[SLAB_NKC:SLAB_NKC + 4].reshape(4, batch, nrow, half)
        kvc = _compress(ykv, pos2[l], w1[l], w2[l], batch)
        o_nsa = _nsa(slabs, kvc, kx, cx_cmp, ovt, qx_nsa, misc, batch, seq)
        merged = _merge(h, o_fox, o_diff, o_nsa, wg[l], wbf[l], wbd[l], wbn[l])
        xf, h2 = _wout(merged, wo[l], xf, gains[l, 1][None, :], gains[l, 2][None, :])
        act = _ffn_up(h2, wup[l])
        g_next = gains[min(l + 1, depth - 1), 0][None, :]
        xf, h = _ffn_down(act, wdn[l], xf, gains[l, 3][None, :], g_next)
    return xf.reshape(batch, seq, d).astype(x.dtype)
```

```python
import functools
import math

import jax
import jax.numpy as jnp
from jax import lax
from jax.experimental import pallas as pl
from jax.experimental.pallas import tpu as pltpu

F32 = jnp.float32
BF16 = jnp.bfloat16

HEAD_DIM = 128
FOX_HEADS = 4
DIFF_HEADS = 4
DIFF_HALF = HEAD_DIM // 2
NSA_HEADS = 8
NSA_KV_HEADS = 2
NSA_REP = NSA_HEADS // NSA_KV_HEADS
CMP_BLOCK = 32
CMP_STRIDE = 16
CMP_HIDDEN = 256
SEL_BLOCK = 64
SEL_TOPK = 16
WINDOW = 512
N_BRANCHES = 3
EPS = 1e-6
NEG_INF = -1e30
FORCE_SCORE = 1e4
SEL_MASK = -32768.0
LOG2E = math.log2(math.e)
LANES = 128

SLAB_FQ, SLAB_FK, SLAB_FV = 0, 4, 8
SLAB_DQ, SLAB_DK, SLAB_DV = 12, 16, 20
SLAB_NQ = 24
SLAB_NKC, SLAB_NVC, SLAB_NKS, SLAB_NVS, SLAB_NKW, SLAB_NVW = 32, 34, 36, 38, 40, 42
N_SLABS = 44
MISC_FF = 0
MISC_NG = 4
AUG_HI = 64
AUG_LO = 67
AUG_TERMS = 3
AUG_PAD = 70

VMEM_LIMIT = 56 * 1024 * 1024


def _pick(n, prefs):
    for p in prefs:
        if p <= n and n % p == 0:
            return p
    return n


def _params(sem):
    return pltpu.CompilerParams(dimension_semantics=sem, vmem_limit_bytes=VMEM_LIMIT)


def _rms(y, g):
    return y * lax.rsqrt(jnp.mean(y * y, axis=-1, keepdims=True) + EPS) * g


def _dot(a, b):
    return jnp.dot(a, b, preferred_element_type=F32)


def _dot_nt(a, b):
    return lax.dot_general(a, b, (((1,), (1,)), ((), ())), preferred_element_type=F32)


def _split3(x):
    hi = x.astype(BF16)
    r = x - hi.astype(F32)
    mid = r.astype(BF16)
    lo = (r - mid.astype(F32)).astype(BF16)
    return hi, mid, lo


def _norm_kernel(x_ref, g_ref, h_ref):
    h_ref[...] = _rms(x_ref[...], g_ref[...]).astype(BF16)


def _norm(x, g):
    m, d = x.shape
    tm = _pick(m, (512, 256, 128))
    return pl.pallas_call(
        _norm_kernel,
        grid=(m // tm,),
        in_specs=[pl.BlockSpec((tm, d), lambda i: (i, 0)), pl.BlockSpec((1, d), lambda i: (0, 0))],
        out_specs=pl.BlockSpec((tm, d), lambda i: (i, 0)),
        out_shape=jax.ShapeDtypeStruct((m, d), BF16),
        compiler_params=_params(("parallel",)),
        name="norm_in",
    )(x, g)


def _inproj_kernel(h_ref, w_ref, cs_ref, o_ref):
    acc = _dot(h_ref[...], w_ref[...]) * cs_ref[...]
    for s in range(o_ref.shape[0]):
        o_ref[s] = acc[:, s * LANES:(s + 1) * LANES].astype(BF16)


def _inproj(h, w_main, colscale, l):
    m, d = h.shape
    n = w_main.shape[2]
    tm = _pick(m, (1024, 512, 256))
    tn = 512
    return pl.pallas_call(
        _inproj_kernel,
        grid=(m // tm, n // tn),
        in_specs=[pl.BlockSpec((tm, d), lambda i, j: (i, 0)),
                  pl.BlockSpec((None, d, tn), lambda i, j: (l, 0, j)),
                  pl.BlockSpec((1, tn), lambda i, j: (0, j))],
        out_specs=pl.BlockSpec((tn // LANES, tm, LANES), lambda i, j: (j, i, 0)),
        out_shape=jax.ShapeDtypeStruct((n // LANES, m, LANES), BF16),
        compiler_params=_params(("parallel", "arbitrary")),
        name="inproj",
    )(h, w_main, colscale)


def _misc_kernel(h_ref, w_ref, o_ref):
    o_ref[...] = _dot(h_ref[...], w_ref[...])


def _miscproj(h, w_misc, l):
    m, d = h.shape
    tm = _pick(m, (1024, 512, 256))
    return pl.pallas_call(
        _misc_kernel,
        grid=(m // tm,),
        in_specs=[pl.BlockSpec((tm, d), lambda i: (i, 0)),
                  pl.BlockSpec((None, d, LANES), lambda i: (l, 0, 0))],
        out_specs=pl.BlockSpec((tm, LANES), lambda i: (i, 0)),
        out_shape=jax.ShapeDtypeStruct((m, LANES), F32),
        compiler_params=_params(("parallel",)),
        name="miscproj",
    )(h, w_misc)


def _logf_kernel(misc_ref, bias_ref, o_ref, *, tc):
    s = misc_ref.shape[0]
    row = lax.broadcasted_iota(jnp.int32, (tc, tc), 0)
    col = lax.broadcasted_iota(jnp.int32, (tc, tc), 1)
    tri = jnp.where(col <= row, 1.0, 0.0).astype(BF16)
    lane = lax.broadcasted_iota(jnp.int32, (tc, LANES), 1)

    def chunk(c, carry):
        r0 = pl.multiple_of(c * tc, tc)
        z = misc_ref[pl.ds(r0, tc), :] + bias_ref[...]
        lf = jnp.minimum(z, 0.0) - jnp.log1p(jnp.exp(-jnp.abs(z)))
        hi, mid, lo = _split3(lf)
        cum = _dot(tri, hi) + _dot(tri, mid) + _dot(tri, lo) + carry
        for hd in range(FOX_HEADS):
            c2 = jnp.broadcast_to(cum[:, hd:hd + 1], (tc, LANES)) * LOG2E
            c_hi = c2.astype(BF16).astype(F32)
            c_mid = (c2 - c_hi).astype(BF16).astype(F32)
            aug = jnp.where(lane == 0, c_hi, jnp.where(lane == 1, c_mid,
                                                      jnp.where(lane == 2, c2 - c_hi - c_mid, 0.0)))
            o_ref[hd, pl.ds(r0, tc), :] = aug.astype(BF16)
        return cum[tc - 1:tc, :]

    lax.fori_loop(0, s // tc, chunk, jnp.zeros((1, LANES), F32))


def _logf(misc, bias_row, batch, seq):
    tc = _pick(seq, (256, 128))
    return pl.pallas_call(
        functools.partial(_logf_kernel, tc=tc),
        grid=(batch,),
        in_specs=[pl.BlockSpec((seq, LANES), lambda b: (b, 0)), pl.BlockSpec((1, LANES), lambda b: (0, 0))],
        out_specs=pl.BlockSpec((FOX_HEADS, seq, LANES), lambda b: (0, b, 0)),
        out_shape=jax.ShapeDtypeStruct((FOX_HEADS, batch * seq, LANES), BF16),
        compiler_params=_params(("parallel",)),
        name="fox_logf",
    )(misc, bias_row)


def _softmax_init(m_ref, l_ref, acc_ref):
    m_ref[...] = jnp.full(m_ref.shape, NEG_INF, F32)
    l_ref[...] = jnp.zeros(l_ref.shape, F32)
    acc_ref[...] = jnp.zeros(acc_ref.shape, F32)


def _softmax_step(s, vt, m_ref, l_ref, acc_ref):
    m_prev = m_ref[...]
    m_new = jnp.maximum(m_prev, jnp.max(s, axis=0, keepdims=True))
    alpha = jnp.exp2(m_prev - m_new)
    p = jnp.exp2(s - m_new)
    l_ref[...] = alpha * l_ref[...] + jnp.sum(p, axis=0, keepdims=True)
    acc_ref[...] = alpha * acc_ref[...] + _dot(vt, p.astype(BF16))
    m_ref[...] = m_new


def _causal_bias(tk, mcols, tq, offset):
    rk = lax.broadcasted_iota(jnp.int32, (tk, mcols), 0)
    rq = lax.broadcasted_iota(jnp.int32, (tk, mcols), 1) & (tq - 1)
    return jnp.where(rk <= rq + offset, 0.0, NEG_INF)


def _flash_tiles(qa, ka_ref, vt_ref, n_full, diag_bias_ref, tk, s_ref, m_ref, l_ref, acc_ref):
    def logits(j):
        k0 = pl.multiple_of(j * tk, tk)
        return _dot_nt(ka_ref[pl.ds(k0, tk), :], qa)

    s_ref[...] = logits(0)

    def full(j, c):
        k0 = pl.multiple_of(j * tk, tk)
        s_next = logits(j + 1)
        _softmax_step(s_ref[...], vt_ref[:, pl.ds(k0, tk)], m_ref, l_ref, acc_ref)
        s_ref[...] = s_next
        return c

    lax.fori_loop(0, n_full, full, 0)
    k0 = pl.multiple_of(n_full * tk, tk)
    _softmax_step(s_ref[...] + diag_bias_ref[...], vt_ref[:, pl.ds(k0, tk)], m_ref, l_ref, acc_ref)


def _stage_kv(ka_ref, vt_ref, k_ref, kx_ref, v_ref, chunk, pad=0):
    def body(c, carry):
        r0 = pl.multiple_of(c * chunk, chunk)
        ka_ref[pl.ds(pad + r0, chunk), 0:LANES] = k_ref[0, pl.ds(r0, chunk), :]
        ka_ref[pl.ds(pad + r0, chunk), LANES:2 * LANES] = kx_ref[pl.ds(r0, chunk), :]
        vt_ref[:, pl.ds(pad + r0, chunk)] = v_ref[0, pl.ds(r0, chunk), :].astype(F32).T.astype(BF16)
        return carry

    lax.fori_loop(0, v_ref.shape[1] // chunk, body, 0)


def _fox_kernel(q_ref, k_ref, v_ref, cx_ref, o_ref, ka_ref, vt_ref, tri_ref, s_ref, m_ref, l_ref, acc_ref,
                *, t):
    qi = pl.program_id(2)

    @pl.when(qi == 0)
    def _():
        _stage_kv(ka_ref, vt_ref, k_ref, cx_ref.at[0], v_ref, t)
        tri_ref[...] = _causal_bias(t, t, t, 0)

    lane = lax.broadcasted_iota(jnp.int32, (t, LANES), 1)
    qx = jnp.where(lane < 3, -1.0, 0.0).astype(BF16)
    qa = jnp.concatenate([q_ref[0], qx], axis=1)
    _softmax_init(m_ref, l_ref, acc_ref)
    _flash_tiles(qa, ka_ref, vt_ref, qi, tri_ref, t, s_ref, m_ref, l_ref, acc_ref)
    o_ref[...] = (acc_ref[...] * (1.0 / l_ref[...])).T.astype(BF16)


def _fox(slabs, cx, batch, seq):
    t = _pick(seq, (512, 256, 128))
    nq = seq // t
    m = batch * seq
    kv_spec = lambda base: pl.BlockSpec((1, seq, LANES), lambda b, h, i: (base + h, b, 0))
    return pl.pallas_call(
        functools.partial(_fox_kernel, t=t),
        grid=(batch, FOX_HEADS, nq),
        in_specs=[pl.BlockSpec((1, t, LANES), lambda b, h, i: (SLAB_FQ + h, b * nq + i, 0)),
                  kv_spec(SLAB_FK), kv_spec(SLAB_FV),
                  pl.BlockSpec((1, seq, LANES), lambda b, h, i: (h, b, 0))],
        out_specs=pl.BlockSpec((t, LANES), lambda b, h, i: (b * nq + i, h)),
        out_shape=jax.ShapeDtypeStruct((m, FOX_HEADS * HEAD_DIM), BF16),
        scratch_shapes=[pltpu.VMEM((seq, 2 * LANES), BF16), pltpu.VMEM((LANES, seq), BF16),
                        pltpu.VMEM((t, t), F32), pltpu.VMEM((t, t), F32),
                        pltpu.VMEM((1, t), F32), pltpu.VMEM((1, t), F32), pltpu.VMEM((LANES, t), F32)],
        compiler_params=_params(("parallel", "parallel", "arbitrary")),
        name="fox_attn",
    )(slabs, slabs, slabs, cx)


def _diff_kernel(q_ref, k_ref, v_ref, kx_ref, qx_ref, lam_ref, sub_ref, o_ref, ka_ref, vt_ref, tri_ref, s_ref,
                 m_ref, l_ref, acc_ref, *, t, lam_init):
    qi = pl.program_id(2)

    @pl.when(qi == 0)
    def _():
        _stage_kv(ka_ref, vt_ref, k_ref, kx_ref, v_ref, t)
        tri_ref[...] = _causal_bias(t, 2 * t, t, 0)

    q = q_ref[0].astype(F32)
    lane = lax.broadcasted_iota(jnp.int32, (t, LANES), 1)
    qx = jnp.broadcast_to(qx_ref[0], (t, LANES)).astype(BF16)
    qa = jnp.concatenate([
        jnp.concatenate([jnp.where(lane < DIFF_HALF, q, 0.0).astype(BF16), qx], axis=1),
        jnp.concatenate([jnp.where(lane >= DIFF_HALF, q, 0.0).astype(BF16), qx], axis=1)], axis=0)
    _softmax_init(m_ref, l_ref, acc_ref)
    _flash_tiles(qa, ka_ref, vt_ref, qi, tri_ref, t, s_ref, m_ref, l_ref, acc_ref)
    lv = lam_ref[...]
    lam = (jnp.exp(jnp.sum(lv[0:1] * lv[1:2], axis=-1, keepdims=True))
           - jnp.exp(jnp.sum(lv[2:3] * lv[3:4], axis=-1, keepdims=True)) + lam_init)
    o = acc_ref[...] * (1.0 / l_ref[...])
    o = (o[:, :t] - lam * o[:, t:]).T
    o_ref[...] = (_rms(o, sub_ref[...]) * (1.0 - lam_init)).astype(BF16)


def _diff(slabs, kx, qx_diff, lam_vec, subln, batch, seq, lam_init):
    t = _pick(seq, (512, 256, 128))
    nq = seq // t
    m = batch * seq
    kv_spec = lambda base: pl.BlockSpec((1, seq, LANES), lambda b, h, i: (base + h, b, 0))
    return pl.pallas_call(
        functools.partial(_diff_kernel, t=t, lam_init=lam_init),
        grid=(batch, DIFF_HEADS, nq),
        in_specs=[pl.BlockSpec((1, t, LANES), lambda b, h, i: (SLAB_DQ + h, b * nq + i, 0)),
                  kv_spec(SLAB_DK), kv_spec(SLAB_DV),
                  pl.BlockSpec((seq, LANES), lambda b, h, i: (0, 0)),
                  pl.BlockSpec((1, 1, LANES), lambda b, h, i: (h, 0, 0)),
                  pl.BlockSpec((4, DIFF_HALF), lambda b, h, i: (0, 0)),
                  pl.BlockSpec((1, LANES), lambda b, h, i: (0, 0))],
        out_specs=pl.BlockSpec((t, LANES), lambda b, h, i: (b * nq + i, h)),
        out_shape=jax.ShapeDtypeStruct((m, DIFF_HEADS * HEAD_DIM), BF16),
        scratch_shapes=[pltpu.VMEM((seq, 2 * LANES), BF16), pltpu.VMEM((LANES, seq), BF16),
                        pltpu.VMEM((t, 2 * t), F32), pltpu.VMEM((t, 2 * t), F32),
                        pltpu.VMEM((1, 2 * t), F32), pltpu.VMEM((1, 2 * t), F32),
                        pltpu.VMEM((LANES, 2 * t), F32)],
        compiler_params=_params(("parallel", "parallel", "arbitrary")),
        name="diff_attn",
    )(slabs, slabs, slabs, kx, qx_diff, lam_vec, subln)


def _compress_kernel(y_ref, pos_ref, w1_ref, w2_ref, o_ref):
    half = y_ref.shape[3]
    y = y_ref[0, 0].astype(F32)
    top = (y + pos_ref[0, 0:1, :]).astype(BF16)
    bot = (y + pos_ref[0, 1:2, :]).astype(BF16)
    a = _dot(top, w1_ref[0, 0:half, :])
    b = _dot(bot, w1_ref[0, half:2 * half, :])
    nrow = a.shape[0]
    hid = a + pltpu.roll(b, nrow - 1, 0)
    hid = hid * jax.nn.sigmoid(hid)
    o_ref[0, 0] = _dot(hid.astype(BF16), w2_ref[0]).astype(BF16)


def _compress(ykv, pos2, w1, w2, batch, l):
    nrow, half = ykv.shape[2], ykv.shape[3]
    return pl.pallas_call(
        _compress_kernel,
        grid=(4, batch),
        in_specs=[pl.BlockSpec((1, 1, nrow, half), lambda s, b: (s, b, 0, 0)),
                  pl.BlockSpec((None, 1, 2, half), lambda s, b: (l, s // 2, 0, 0)),
                  pl.BlockSpec((None, 1, 2 * half, CMP_HIDDEN), lambda s, b: (l, s // 2, 0, 0)),
                  pl.BlockSpec((None, 1, CMP_HIDDEN, HEAD_DIM), lambda s, b: (l, s // 2, 0, 0))],
        out_specs=pl.BlockSpec((1, 1, nrow, HEAD_DIM), lambda s, b: (s, b, 0, 0)),
        out_shape=jax.ShapeDtypeStruct((4, batch, nrow, HEAD_DIM), BF16),
        compiler_params=_params(("parallel", "parallel")),
        name="nsa_compress",
    )(ykv, pos2, w1, w2)


def _topk_bias(cand, topk):
    nsel, tq = cand.shape
    sub = 8
    blocks = [cand[b * sub:(b + 1) * sub] for b in range(nsel // sub)]
    jidx = lax.broadcasted_iota(jnp.int32, (sub, tq), 0)
    ranks = [jnp.zeros((sub, tq), F32) for _ in blocks]
    for k in range(nsel):
        rk = cand[k:k + 1, :]
        for b, cb in enumerate(blocks):
            if b * sub > k:
                beats = rk >= cb
            elif b * sub + sub - 1 < k:
                beats = rk > cb
            else:
                beats = (rk > cb) | ((rk == cb) & (jidx + b * sub > k))
            ranks[b] = ranks[b] + jnp.where(beats, 1.0, 0.0)
    rank = jnp.concatenate(ranks, axis=0)
    return jnp.where(rank < topk, 0.0, SEL_MASK)


def _nsa_kernel(q_ref, kc_ref, vc_ref, ks_ref, vs_ref, kw_ref, vw_ref, kx_ref, cx_ref, ovt_ref, qx_ref,
                misc_ref, o_ref, ksa_ref, kwa_ref, vst_ref, vwt_ref, vct_ref, gt_ref, dbias_ref, wbias_ref,
                comb_ref, s_ref, m_ref, l_ref, acc_ref, *, tq, tk):
    g = pl.program_id(1)
    qi = pl.program_id(2)
    mcols = NSA_REP * tq
    q0 = qi * tq
    nc = kc_ref.shape[2]
    nsel = kx_ref.shape[0] // SEL_BLOCK

    span = WINDOW + tq

    @pl.when(qi == 0)
    def _():
        _stage_kv(ksa_ref, vst_ref, ks_ref, kx_ref, vs_ref, tk)
        _stage_kv(kwa_ref, vwt_ref, kw_ref, kx_ref, vw_ref, tk, pad=WINDOW)
        pad_lane = lax.broadcasted_iota(jnp.int32, (WINDOW, 2 * LANES), 1)
        kwa_ref[0:WINDOW, :] = jnp.where(pad_lane == LANES + AUG_PAD, SEL_MASK, 0.0).astype(BF16)
        vwt_ref[:, 0:WINDOW] = jnp.zeros((LANES, WINDOW), BF16)
        vct_ref[...] = vc_ref[0, 0].astype(F32).T.astype(BF16)
        for o in range(tk // tq):
            dbias_ref[o] = _causal_bias(tk, mcols, tq, o * tq)
        rk = lax.broadcasted_iota(jnp.int32, (span, mcols), 0)
        rq = lax.broadcasted_iota(jnp.int32, (span, mcols), 1) & (tq - 1)
        wbias_ref[...] = jnp.where((rk > rq) & (rk <= rq + WINDOW), 0.0, NEG_INF)

    q4 = q_ref[...].reshape(mcols, LANES)
    qx = jnp.concatenate([jnp.broadcast_to(qx_ref[0, r:r + 1, :], (tq, LANES)) for r in range(NSA_REP)], axis=0)
    qa = jnp.concatenate([q4, qx.astype(BF16)], axis=1)
    col = lax.broadcasted_iota(jnp.int32, (1, mcols), 1)
    col_pos = q0 + (col & (tq - 1))

    kca = jnp.concatenate([kc_ref[0, 0], cx_ref[...]], axis=1)
    sc = _dot_nt(kca, qa)
    cend = lax.broadcasted_iota(jnp.int32, (nc, mcols), 0) * CMP_STRIDE + (CMP_BLOCK - 1)
    sc = jnp.where(cend <= col_pos, sc, NEG_INF)
    e = jnp.exp2(sc - jnp.max(sc, axis=0, keepdims=True))
    inv = jnp.where(col_pos >= CMP_BLOCK - 1, 1.0 / jnp.sum(e, axis=0, keepdims=True), 0.0)
    p = e * inv
    o_cmp = _dot(vct_ref[...], p.astype(BF16))

    psum = p[:, 0:tq]
    for r in range(1, NSA_REP):
        psum = psum + p[:, r * tq:(r + 1) * tq]
    p_hi = psum.astype(BF16)
    p_lo = (psum - p_hi.astype(F32)).astype(BF16)
    imp = _dot(ovt_ref[...], p_hi) + _dot(ovt_ref[...], p_lo)
    qpos = q0 + lax.broadcasted_iota(jnp.int32, (LANES, tq), 1)
    blk = lax.broadcasted_iota(jnp.int32, (LANES, tq), 0)
    cur = jnp.right_shift(qpos, SEL_BLOCK.bit_length() - 1)
    forced = (blk == 0) | (blk == cur) | (blk == cur - 1)
    imp = jnp.where(blk <= cur, jnp.where(forced, FORCE_SCORE, imp), -1.0)
    bias_t = _topk_bias(imp[0:nsel], min(SEL_TOPK, nsel))
    if nsel < LANES:
        bias_t = jnp.concatenate([bias_t, jnp.zeros((LANES - nsel, tq), F32)], axis=0)
    selbias = jnp.concatenate([bias_t.T] * NSA_REP, axis=0)
    lane4 = lax.broadcasted_iota(jnp.int32, (mcols, LANES), 1)
    qa_sel = jnp.concatenate([q4, jnp.where(lane4 < SEL_BLOCK, selbias, qx).astype(BF16)], axis=1)

    w0 = pl.multiple_of(q0, tq)
    sw = _dot_nt(kwa_ref[pl.ds(w0, span), :], qa) + wbias_ref[...]
    ew = jnp.exp2(sw - jnp.max(sw, axis=0, keepdims=True))
    o_win = _dot(vwt_ref[:, pl.ds(w0, span)], ew.astype(BF16)) * (1.0 / jnp.sum(ew, axis=0, keepdims=True))

    gt_ref[...] = jax.nn.sigmoid(misc_ref[...]).T
    gate = lambda r, i: gt_ref[pl.ds(MISC_NG + 3 * (NSA_REP * g + r) + i, 1), :]
    for r in range(NSA_REP):
        sl = slice(r * tq, (r + 1) * tq)
        comb_ref[:, sl] = gate(r, 0) * o_cmp[:, sl] + gate(r, 2) * o_win[:, sl]

    _softmax_init(m_ref, l_ref, acc_ref)
    _flash_tiles(qa_sel, ksa_ref, vst_ref, q0 // tk, dbias_ref.at[qi & (tk // tq - 1)], tk, s_ref, m_ref, l_ref,
                 acc_ref)
    o_sel = acc_ref[...] * (1.0 / l_ref[...])
    for r in range(NSA_REP):
        sl = slice(r * tq, (r + 1) * tq)
        out = comb_ref[:, sl] + gate(r, 1) * o_sel[:, sl]
        o_ref[:, r * LANES:(r + 1) * LANES] = out.T.astype(BF16)


def _nsa(slabs, kvc, kx, cx_cmp, ovt, qx_nsa, misc, batch, seq):
    tq = _pick(seq, (256, 128))
    tk = _pick(seq, (512, 256, 128))
    assert tq & (tq - 1) == 0 and tk % tq == 0 and seq >= WINDOW + tq
    nq = seq // tq
    m = batch * seq
    mcols = NSA_REP * tq
    nc = kvc.shape[2]
    kv_spec = lambda base: pl.BlockSpec((1, seq, LANES), lambda b, g, i: (base + g, b, 0))
    const = lambda shape: pl.BlockSpec(shape, lambda b, g, i: (0,) * len(shape))
    return pl.pallas_call(
        functools.partial(_nsa_kernel, tq=tq, tk=tk),
        grid=(batch, NSA_KV_HEADS, nq),
        in_specs=[pl.BlockSpec((NSA_REP, tq, LANES), lambda b, g, i: (SLAB_NQ // NSA_REP + g, b * nq + i, 0)),
                  pl.BlockSpec((1, 1, nc, LANES), lambda b, g, i: (g, b, 0, 0)),
                  pl.BlockSpec((1, 1, nc, LANES), lambda b, g, i: (2 + g, b, 0, 0)),
                  kv_spec(SLAB_NKS), kv_spec(SLAB_NVS), kv_spec(SLAB_NKW), kv_spec(SLAB_NVW),
                  const((seq, LANES)), const((nc, LANES)), const((LANES, nc)),
                  pl.BlockSpec((1, NSA_REP, LANES), lambda b, g, i: (g, 0, 0)),
                  pl.BlockSpec((tq, LANES), lambda b, g, i: (b * nq + i, 0))],
        out_specs=pl.BlockSpec((tq, NSA_REP * LANES), lambda b, g, i: (b * nq + i, g)),
        out_shape=jax.ShapeDtypeStruct((m, NSA_HEADS * HEAD_DIM), BF16),
        scratch_shapes=[pltpu.VMEM((seq, 2 * LANES), BF16), pltpu.VMEM((WINDOW + seq, 2 * LANES), BF16),
                        pltpu.VMEM((LANES, seq), BF16), pltpu.VMEM((LANES, WINDOW + seq), BF16),
                        pltpu.VMEM((LANES, nc), BF16), pltpu.VMEM((LANES, tq), F32),
                        pltpu.VMEM((tk // tq, tk, mcols), F32), pltpu.VMEM((WINDOW + tq, mcols), F32),
                        pltpu.VMEM((LANES, mcols), F32), pltpu.VMEM((tk, mcols), F32), pltpu.VMEM((1, mcols), F32), pltpu.VMEM((1, mcols), F32),
                        pltpu.VMEM((LANES, mcols), F32)],
        compiler_params=_params(("parallel", "parallel", "arbitrary")),
        name="nsa_attn",
    )(slabs, kvc, kvc, slabs, slabs, slabs, slabs, kx, cx_cmp, ovt, qx_nsa, misc)


def _merge_kernel(h_ref, of_ref, od_ref, on_ref, wg0_ref, wg1_ref, wg2_ref, wf_ref, wd_ref, wn_ref, o_ref):
    h = h_ref[...]
    acc = jax.nn.sigmoid(_dot(h, wg0_ref[...])) * _dot(of_ref[...], wf_ref[...])
    acc = acc + jax.nn.sigmoid(_dot(h, wg1_ref[...])) * _dot(od_ref[...], wd_ref[...])
    acc = acc + jax.nn.sigmoid(_dot(h, wg2_ref[...])) * _dot(on_ref[...], wn_ref[...])
    o_ref[...] = acc.astype(BF16)


def _merge(h, o_fox, o_diff, o_nsa, w_gate, wb_fox, wb_diff, wb_nsa, l):
    m, d = h.shape
    tm = _pick(m, (1024, 512, 256))
    tn = _pick(d, (256, 128))
    nj = d // tn
    row = lambda width: pl.BlockSpec((tm, width), lambda i, j: (i, 0))
    gate = lambda t: pl.BlockSpec((None, d, tn), lambda i, j: (l, 0, t * nj + j))
    col = lambda k: pl.BlockSpec((None, k, tn), lambda i, j: (l, 0, j))
    return pl.pallas_call(
        _merge_kernel,
        grid=(m // tm, nj),
        in_specs=[row(d), row(o_fox.shape[1]), row(o_diff.shape[1]), row(o_nsa.shape[1]),
                  gate(0), gate(1), gate(2),
                  col(wb_fox.shape[1]), col(wb_diff.shape[1]), col(wb_nsa.shape[1])],
        out_specs=pl.BlockSpec((tm, tn), lambda i, j: (i, j)),
        out_shape=jax.ShapeDtypeStruct((m, d), BF16),
        compiler_params=_params(("parallel", "arbitrary")),
        name="gate_merge",
    )(h, o_fox, o_diff, o_nsa, w_gate, w_gate, w_gate, wb_fox, wb_diff, wb_nsa)


def _wout_kernel(a_ref, w_ref, x_ref, gp_ref, gn_ref, xo_ref, ho_ref):
    y = _dot(a_ref[...], w_ref[...])
    x_new = x_ref[...] + _rms(y, gp_ref[...])
    xo_ref[...] = x_new
    ho_ref[...] = _rms(x_new, gn_ref[...]).astype(BF16)


def _wout(a, w, x, g_post, g_next, l):
    m, d = x.shape
    tm = _pick(m, (256, 128))
    row = lambda: pl.BlockSpec((tm, d), lambda i: (i, 0))
    vec = lambda: pl.BlockSpec((1, d), lambda i: (0, 0))
    return pl.pallas_call(
        _wout_kernel,
        grid=(m // tm,),
        in_specs=[row(), pl.BlockSpec((None, d, d), lambda i: (l, 0, 0)), row(), vec(), vec()],
        out_specs=[row(), row()],
        out_shape=[jax.ShapeDtypeStruct((m, d), F32), jax.ShapeDtypeStruct((m, d), BF16)],
        compiler_params=_params(("parallel",)),
        name="out_proj",
    )(a, w, x, g_post, g_next)


def _ffn_up_kernel(h_ref, wg_ref, wu_ref, o_ref):
    h = h_ref[...]
    gate = _dot(h, wg_ref[...])
    o_ref[...] = (gate * jax.nn.sigmoid(gate) * _dot(h, wu_ref[...])).astype(BF16)


def _ffn_up(h, w_up, l):
    m, d = h.shape
    dff = w_up.shape[2] // 2
    tm = _pick(m, (1024, 512, 256))
    tn = _pick(dff, (512, 256, 128))
    nj = dff // tn
    return pl.pallas_call(
        _ffn_up_kernel,
        grid=(m // tm, nj),
        in_specs=[pl.BlockSpec((tm, d), lambda i, j: (i, 0)),
                  pl.BlockSpec((None, d, tn), lambda i, j: (l, 0, j)),
                  pl.BlockSpec((None, d, tn), lambda i, j: (l, 0, nj + j))],
        out_specs=pl.BlockSpec((tm, tn), lambda i, j: (i, j)),
        out_shape=jax.ShapeDtypeStruct((m, dff), BF16),
        compiler_params=_params(("parallel", "arbitrary")),
        name="ffn_up",
    )(h, w_up, w_up)


def _ffn_down_kernel(a_ref, w_ref, x_ref, gp_ref, gn_ref, xo_ref, ho_ref, acc_ref):
    k = pl.program_id(1)

    @pl.when(k == 0)
    def _():
        acc_ref[...] = jnp.zeros(acc_ref.shape, F32)

    acc_ref[...] += _dot(a_ref[...], w_ref[...])

    @pl.when(k == pl.num_programs(1) - 1)
    def _():
        x_new = x_ref[...] + _rms(acc_ref[...], gp_ref[...])
        xo_ref[...] = x_new
        ho_ref[...] = _rms(x_new, gn_ref[...]).astype(BF16)


def _ffn_down(a, w, x, g_post, g_next, l):
    m, d = x.shape
    dff = a.shape[1]
    tm = _pick(m, (512, 256, 128))
    tk = _pick(dff, (1408, 512, 256, 128))
    row = lambda: pl.BlockSpec((tm, d), lambda i, k: (i, 0))
    vec = lambda: pl.BlockSpec((1, d), lambda i, k: (0, 0))
    return pl.pallas_call(
        _ffn_down_kernel,
        grid=(m // tm, dff // tk),
        in_specs=[pl.BlockSpec((tm, tk), lambda i, k: (i, k)),
                  pl.BlockSpec((None, tk, d), lambda i, k: (l, k, 0)), row(), vec(), vec()],
        out_specs=[row(), row()],
        out_shape=[jax.ShapeDtypeStruct((m, d), F32), jax.ShapeDtypeStruct((m, d), BF16)],
        scratch_shapes=[pltpu.VMEM((tm, d), F32)],
        compiler_params=_params(("parallel", "arbitrary")),
        name="ffn_down",
    )(a, w, x, g_post, g_next)


def _pos_columns(pos):
    lane = jnp.arange(LANES)[None, :]
    hi = (lane >= AUG_HI) & (lane < AUG_HI + AUG_TERMS)
    lo = (lane >= AUG_LO) & (lane < AUG_LO + AUG_TERMS)
    return jnp.where(hi, (pos // LANES)[:, None], jnp.where(lo, (pos % LANES)[:, None], 0))


def _key_aug_table(seq):
    j = jnp.arange(seq)
    lane = jnp.arange(LANES)[None, :]
    onehot = (lane == (j // SEL_BLOCK)[:, None]) & (lane < SEL_BLOCK)
    return (_pos_columns(j) + onehot.astype(jnp.int32)).astype(BF16)


def _cmp_aug_table(nrow):
    return _pos_columns(jnp.arange(nrow) * CMP_STRIDE + CMP_BLOCK - 1).astype(BF16)


def _overlap_table_t(nrow, seq):
    start = jnp.arange(nrow)[None, :] * CMP_STRIDE
    blk = jnp.arange(LANES)[:, None]
    sel = blk * SEL_BLOCK
    ov = ((start < sel + SEL_BLOCK) & (start + CMP_BLOCK - 1 >= sel) & (blk < seq // SEL_BLOCK)
          & (jnp.arange(nrow)[None, :] < (seq - CMP_BLOCK) // CMP_STRIDE + 1))
    return ov.astype(BF16)


def _query_aug_rows(n_heads):
    slopes = 2.0 ** (-8.0 * jnp.arange(1, n_heads + 1, dtype=F32) / n_heads)
    terms = [t.astype(F32) for t in _split3(jnp.float32(LOG2E))]
    lane = jnp.arange(LANES)[None, :]
    out = jnp.where(lane == AUG_PAD, 1.0, jnp.zeros((n_heads, LANES), F32))
    for i, t in enumerate(terms):
        out = jnp.where(lane == AUG_HI + i, slopes[:, None] * t * LANES, out)
        out = jnp.where(lane == AUG_LO + i, slopes[:, None] * t, out)
    return out


def kernel(x, w_in, fox_forget_bias, diff_lambda, diff_subln, nsa_cmp_pos, nsa_cmp_w1, nsa_cmp_w2,
           w_branch_fox, w_branch_diff, w_branch_nsa, w_gate, w_out, norm_gains, w_ffn_up, w_ffn_down):
    batch, seq, d = x.shape
    depth = w_in.shape[0]
    m = batch * seq
    fw, dw, nw, kvw = FOX_HEADS * HEAD_DIM, DIFF_HEADS * HEAD_DIM, NSA_HEADS * HEAD_DIM, NSA_KV_HEADS * HEAD_DIM
    ff0 = 3 * fw
    dq0 = ff0 + FOX_HEADS
    ng0 = dq0 + 3 * dw + nw + 6 * kvw
    n_main = ng0 - FOX_HEADS
    assert n_main == N_SLABS * LANES and w_in.shape[2] == ng0 + 3 * NSA_HEADS

    w_main = jnp.concatenate([w_in[:, :, :ff0], w_in[:, :, dq0:ng0]], axis=2).astype(BF16)
    w_misc = jnp.concatenate([w_in[:, :, ff0:dq0], w_in[:, :, ng0:],
                              jnp.zeros((depth, d, LANES - FOX_HEADS - 3 * NSA_HEADS), F32)], axis=2).astype(BF16)
    colscale = jnp.ones((n_main,), F32)
    colscale = colscale.at[SLAB_FQ * LANES:SLAB_FK * LANES].set(HEAD_DIM ** -0.5 * LOG2E)
    colscale = colscale.at[SLAB_DQ * LANES:SLAB_DK * LANES].set(DIFF_HALF ** -0.5 * LOG2E)
    colscale = colscale.at[SLAB_NQ * LANES:SLAB_NKC * LANES].set(HEAD_DIM ** -0.5 * LOG2E)
    colscale = colscale[None, :]
    fbias = jnp.pad(fox_forget_bias.astype(F32), ((0, 0), (0, LANES - FOX_HEADS)))[:, None, :]
    half = CMP_STRIDE * HEAD_DIM
    pos2 = nsa_cmp_pos.astype(F32).reshape(depth, 2, 2, half)
    w1 = nsa_cmp_w1.astype(BF16)
    w2 = nsa_cmp_w2.astype(BF16)
    wbf, wbd, wbn = w_branch_fox.astype(BF16), w_branch_diff.astype(BF16), w_branch_nsa.astype(BF16)
    wg, wo = w_gate.astype(BF16), w_out.astype(BF16)
    wup, wdn = w_ffn_up.astype(BF16), w_ffn_down.astype(BF16)
    gains = norm_gains.astype(F32)

    nrow = seq // CMP_STRIDE
    kx = _key_aug_table(seq)
    cx_cmp = _cmp_aug_table(nrow)
    ovt = _overlap_table_t(nrow, seq)
    qx_diff = _query_aug_rows(DIFF_HEADS)[:, None, :]
    qx_nsa = _query_aug_rows(NSA_HEADS).reshape(NSA_KV_HEADS, NSA_REP, LANES)

    xf = x.reshape(m, d).astype(F32)
    h = _norm(xf, gains[0, 0][None, :])
    for l in range(depth):
        lam_init = 0.8 - 0.6 * math.exp(-0.3 * l)
        slabs = _inproj(h, w_main, colscale, l)
        misc = _miscproj(h, w_misc, l)
        cx_fox = _logf(misc, fbias[l], batch, seq)
        o_fox = _fox(slabs, cx_fox, batch, seq)
        o_diff = _diff(slabs, kx, qx_diff, diff_lambda[l].astype(F32), diff_subln[l].astype(F32)[None, :],
                       batch, seq, lam_init)
        ykv = slabs[SLAB_NKC:SLAB_NKC + 4].reshape(4, batch, nrow, half)
        kvc = _compress(ykv, pos2, w1, w2, batch, l)
        o_nsa = _nsa(slabs, kvc, kx, cx_cmp, ovt, qx_nsa, misc, batch, seq)
        merged = _merge(h, o_fox, o_diff, o_nsa, wg, wbf, wbd, wbn, l)
        xf, h2 = _wout(merged, wo, xf, gains[l, 1][None, :], gains[l, 2][None, :], l)
        act = _ffn_up(h2, wup, l)
        g_next = gains[min(l + 1, depth - 1), 0][None, :]
        xf, h = _ffn_down(act, wdn, xf, gains[l, 3][None, :], g_next, l)
    return xf.reshape(batch, seq, d).astype(x.dtype)
```

```python
import functools
import math

import jax
import jax.numpy as jnp
from jax import lax
from jax.experimental import pallas as pl
from jax.experimental.pallas import tpu as pltpu

F32 = jnp.float32
BF16 = jnp.bfloat16

HEAD_DIM = 128
FOX_HEADS = 4
DIFF_HEADS = 4
DIFF_HALF = HEAD_DIM // 2
NSA_HEADS = 8
NSA_KV_HEADS = 2
NSA_REP = NSA_HEADS // NSA_KV_HEADS
CMP_BLOCK = 32
CMP_STRIDE = 16
CMP_HIDDEN = 256
SEL_BLOCK = 64
SEL_TOPK = 16
WINDOW = 512
N_BRANCHES = 3
EPS = 1e-6
NEG_INF = -1e30
FORCE_SCORE = 1e4
SEL_MASK = -32768.0
LOG2E = math.log2(math.e)
LANES = 128

SLAB_FQ, SLAB_FK, SLAB_FV = 0, 4, 8
SLAB_DQ, SLAB_DK, SLAB_DV = 12, 16, 20
SLAB_NQ = 24
SLAB_NKC, SLAB_NVC, SLAB_NKS, SLAB_NVS, SLAB_NKW, SLAB_NVW = 32, 34, 36, 38, 40, 42
N_SLABS = 44
MISC_FF = 0
MISC_NG = 4
AUG_HI = 64
AUG_LO = 67
AUG_TERMS = 3
AUG_PAD = 70

VMEM_LIMIT = 56 * 1024 * 1024


def _pick(n, prefs):
    for p in prefs:
        if p <= n and n % p == 0:
            return p
    return n


def _params(sem):
    return pltpu.CompilerParams(dimension_semantics=sem, vmem_limit_bytes=VMEM_LIMIT)


def _rms(y, g):
    return y * lax.rsqrt(jnp.mean(y * y, axis=-1, keepdims=True) + EPS) * g


def _dot(a, b):
    return jnp.dot(a, b, preferred_element_type=F32)


def _dot_nt(a, b):
    return lax.dot_general(a, b, (((1,), (1,)), ((), ())), preferred_element_type=F32)


def _split3(x):
    hi = x.astype(BF16)
    r = x - hi.astype(F32)
    mid = r.astype(BF16)
    lo = (r - mid.astype(F32)).astype(BF16)
    return hi, mid, lo


def _norm_kernel(x_ref, g_ref, h_ref):
    h_ref[...] = _rms(x_ref[...], g_ref[...]).astype(BF16)


def _norm(x, g):
    m, d = x.shape
    tm = _pick(m, (512, 256, 128))
    return pl.pallas_call(
        _norm_kernel,
        grid=(m // tm,),
        in_specs=[pl.BlockSpec((tm, d), lambda i: (i, 0)), pl.BlockSpec((1, d), lambda i: (0, 0))],
        out_specs=pl.BlockSpec((tm, d), lambda i: (i, 0)),
        out_shape=jax.ShapeDtypeStruct((m, d), BF16),
        compiler_params=_params(("parallel",)),
        name="norm_in",
    )(x, g)


def _inproj_kernel(h_ref, w_ref, cs_ref, o_ref):
    acc = _dot(h_ref[...], w_ref[...]) * cs_ref[...]
    for s in range(o_ref.shape[0]):
        o_ref[s] = acc[:, s * LANES:(s + 1) * LANES].astype(BF16)


def _inproj(h, w_main, colscale, l):
    m, d = h.shape
    n = w_main.shape[2]
    tm = _pick(m, (1024, 512, 256))
    tn = 512
    return pl.pallas_call(
        _inproj_kernel,
        grid=(m // tm, n // tn),
        in_specs=[pl.BlockSpec((tm, d), lambda i, j: (i, 0)),
                  pl.BlockSpec((None, d, tn), lambda i, j: (l, 0, j)),
                  pl.BlockSpec((1, tn), lambda i, j: (0, j))],
        out_specs=pl.BlockSpec((tn // LANES, tm, LANES), lambda i, j: (j, i, 0)),
        out_shape=jax.ShapeDtypeStruct((n // LANES, m, LANES), BF16),
        compiler_params=_params(("parallel", "arbitrary")),
        name="inproj",
    )(h, w_main, colscale)


def _misc_kernel(h_ref, w_ref, o_ref):
    o_ref[...] = _dot(h_ref[...], w_ref[...])


def _miscproj(h, w_misc, l):
    m, d = h.shape
    tm = _pick(m, (1024, 512, 256))
    return pl.pallas_call(
        _misc_kernel,
        grid=(m // tm,),
        in_specs=[pl.BlockSpec((tm, d), lambda i: (i, 0)),
                  pl.BlockSpec((None, d, LANES), lambda i: (l, 0, 0))],
        out_specs=pl.BlockSpec((tm, LANES), lambda i: (i, 0)),
        out_shape=jax.ShapeDtypeStruct((m, LANES), F32),
        compiler_params=_params(("parallel",)),
        name="miscproj",
    )(h, w_misc)


def _logf_kernel(misc_ref, bias_ref, o_ref, *, tc):
    s = misc_ref.shape[0]
    row = lax.broadcasted_iota(jnp.int32, (tc, tc), 0)
    col = lax.broadcasted_iota(jnp.int32, (tc, tc), 1)
    tri = jnp.where(col <= row, 1.0, 0.0).astype(BF16)
    lane = lax.broadcasted_iota(jnp.int32, (tc, LANES), 1)

    def chunk(c, carry):
        r0 = pl.multiple_of(c * tc, tc)
        z = misc_ref[pl.ds(r0, tc), :] + bias_ref[...]
        lf = jnp.minimum(z, 0.0) - jnp.log1p(jnp.exp(-jnp.abs(z)))
        hi, mid, lo = _split3(lf)
        cum = _dot(tri, hi) + _dot(tri, mid) + _dot(tri, lo) + carry
        for hd in range(FOX_HEADS):
            c2 = jnp.broadcast_to(cum[:, hd:hd + 1], (tc, LANES)) * LOG2E
            c_hi = c2.astype(BF16).astype(F32)
            c_mid = (c2 - c_hi).astype(BF16).astype(F32)
            aug = jnp.where(lane == 0, c_hi, jnp.where(lane == 1, c_mid,
                                                      jnp.where(lane == 2, c2 - c_hi - c_mid, 0.0)))
            o_ref[hd, pl.ds(r0, tc), :] = aug.astype(BF16)
        return cum[tc - 1:tc, :]

    lax.fori_loop(0, s // tc, chunk, jnp.zeros((1, LANES), F32))


def _logf(misc, bias_row, batch, seq):
    tc = _pick(seq, (256, 128))
    return pl.pallas_call(
        functools.partial(_logf_kernel, tc=tc),
        grid=(batch,),
        in_specs=[pl.BlockSpec((seq, LANES), lambda b: (b, 0)), pl.BlockSpec((1, LANES), lambda b: (0, 0))],
        out_specs=pl.BlockSpec((FOX_HEADS, seq, LANES), lambda b: (0, b, 0)),
        out_shape=jax.ShapeDtypeStruct((FOX_HEADS, batch * seq, LANES), BF16),
        compiler_params=_params(("parallel",)),
        name="fox_logf",
    )(misc, bias_row)


def _softmax_init(m_ref, l_ref, acc_ref):
    m_ref[...] = jnp.full(m_ref.shape, NEG_INF, F32)
    l_ref[...] = jnp.zeros(l_ref.shape, F32)
    acc_ref[...] = jnp.zeros(acc_ref.shape, F32)


def _softmax_step(s, vt, m_ref, l_ref, acc_ref):
    m_prev = m_ref[...]
    m_new = jnp.maximum(m_prev, jnp.max(s, axis=0, keepdims=True))
    alpha = jnp.exp2(m_prev - m_new)
    p = jnp.exp2(s - m_new)
    l_ref[...] = alpha * l_ref[...] + jnp.sum(p, axis=0, keepdims=True)
    acc_ref[...] = alpha * acc_ref[...] + _dot(vt, p.astype(BF16))
    m_ref[...] = m_new


def _causal_bias(tk, mcols, tq, offset):
    rk = lax.broadcasted_iota(jnp.int32, (tk, mcols), 0)
    rq = lax.broadcasted_iota(jnp.int32, (tk, mcols), 1) & (tq - 1)
    return jnp.where(rk <= rq + offset, 0.0, NEG_INF)


def _flash_tiles(qa, ka_ref, vt_ref, n_full, diag_bias_ref, tk, s_ref, m_ref, l_ref, acc_ref):
    def logits(j):
        k0 = pl.multiple_of(j * tk, tk)
        return _dot_nt(ka_ref[pl.ds(k0, tk), :], qa)

    s_ref[...] = logits(0)

    def full(j, c):
        k0 = pl.multiple_of(j * tk, tk)
        s_next = logits(j + 1)
        _softmax_step(s_ref[...], vt_ref[:, pl.ds(k0, tk)], m_ref, l_ref, acc_ref)
        s_ref[...] = s_next
        return c

    lax.fori_loop(0, n_full, full, 0)
    k0 = pl.multiple_of(n_full * tk, tk)
    _softmax_step(s_ref[...] + diag_bias_ref[...], vt_ref[:, pl.ds(k0, tk)], m_ref, l_ref, acc_ref)


def _stage_kv(ka_ref, vt_ref, k_ref, kx_ref, v_ref, chunk, pad=0):
    def body(c, carry):
        r0 = pl.multiple_of(c * chunk, chunk)
        ka_ref[pl.ds(pad + r0, chunk), 0:LANES] = k_ref[0, pl.ds(r0, chunk), :]
        ka_ref[pl.ds(pad + r0, chunk), LANES:2 * LANES] = kx_ref[pl.ds(r0, chunk), :]
        vt_ref[:, pl.ds(pad + r0, chunk)] = v_ref[0, pl.ds(r0, chunk), :].astype(F32).T.astype(BF16)
        return carry

    lax.fori_loop(0, v_ref.shape[1] // chunk, body, 0)


def _fox_kernel(q_ref, k_ref, v_ref, cx_ref, o_ref, ka_ref, vt_ref, tri_ref, s_ref, m_ref, l_ref, acc_ref,
                *, t):
    _stage_kv(ka_ref, vt_ref, k_ref, cx_ref.at[0], v_ref, t)
    tri_ref[...] = _causal_bias(t, t, t, 0)
    lane = lax.broadcasted_iota(jnp.int32, (t, LANES), 1)
    qx = jnp.where(lane < 3, -1.0, 0.0).astype(BF16)

    def qstep(qi, c):
        q0 = pl.multiple_of(qi * t, t)
        qa = jnp.concatenate([q_ref[0, pl.ds(q0, t), :], qx], axis=1)
        _softmax_init(m_ref, l_ref, acc_ref)
        _flash_tiles(qa, ka_ref, vt_ref, qi, tri_ref, t, s_ref, m_ref, l_ref, acc_ref)
        o_ref[pl.ds(q0, t), :] = (acc_ref[...] * (1.0 / l_ref[...])).T.astype(BF16)
        return c

    lax.fori_loop(0, q_ref.shape[1] // t, qstep, 0)


def _fox(slabs, cx, batch, seq):
    t = _pick(seq, (512, 256, 128))
    m = batch * seq
    head = lambda base: pl.BlockSpec((1, seq, LANES), lambda b, h: (base + h, b, 0))
    return pl.pallas_call(
        functools.partial(_fox_kernel, t=t),
        grid=(batch, FOX_HEADS),
        in_specs=[head(SLAB_FQ), head(SLAB_FK), head(SLAB_FV), head(0)],
        out_specs=pl.BlockSpec((seq, LANES), lambda b, h: (b, h)),
        out_shape=jax.ShapeDtypeStruct((m, FOX_HEADS * HEAD_DIM), BF16),
        scratch_shapes=[pltpu.VMEM((seq, 2 * LANES), BF16), pltpu.VMEM((LANES, seq), BF16),
                        pltpu.VMEM((t, t), F32), pltpu.VMEM((t, t), F32),
                        pltpu.VMEM((1, t), F32), pltpu.VMEM((1, t), F32), pltpu.VMEM((LANES, t), F32)],
        compiler_params=_params(("parallel", "parallel")),
        name="fox_attn",
    )(slabs, slabs, slabs, cx)


def _diff_kernel(q_ref, k_ref, v_ref, kx_ref, qx_ref, lam_ref, sub_ref, o_ref, ka_ref, vt_ref, tri_ref, s_ref,
                 m_ref, l_ref, acc_ref, *, t, lam_init):
    _stage_kv(ka_ref, vt_ref, k_ref, kx_ref, v_ref, t)
    tri_ref[...] = _causal_bias(t, 2 * t, t, 0)
    lane = lax.broadcasted_iota(jnp.int32, (t, LANES), 1)
    qx = jnp.broadcast_to(qx_ref[0], (t, LANES)).astype(BF16)
    lv = lam_ref[...]
    lam = (jnp.exp(jnp.sum(lv[0:1] * lv[1:2], axis=-1, keepdims=True))
           - jnp.exp(jnp.sum(lv[2:3] * lv[3:4], axis=-1, keepdims=True)) + lam_init)

    def qstep(qi, c):
        q0 = pl.multiple_of(qi * t, t)
        q = q_ref[0, pl.ds(q0, t), :].astype(F32)
        qa = jnp.concatenate([
            jnp.concatenate([jnp.where(lane < DIFF_HALF, q, 0.0).astype(BF16), qx], axis=1),
            jnp.concatenate([jnp.where(lane >= DIFF_HALF, q, 0.0).astype(BF16), qx], axis=1)], axis=0)
        _softmax_init(m_ref, l_ref, acc_ref)
        _flash_tiles(qa, ka_ref, vt_ref, qi, tri_ref, t, s_ref, m_ref, l_ref, acc_ref)
        o = acc_ref[...] * (1.0 / l_ref[...])
        o = (o[:, :t] - lam * o[:, t:]).T
        o_ref[pl.ds(q0, t), :] = (_rms(o, sub_ref[...]) * (1.0 - lam_init)).astype(BF16)
        return c

    lax.fori_loop(0, q_ref.shape[1] // t, qstep, 0)


def _diff(slabs, kx, qx_diff, lam_vec, subln, batch, seq, lam_init):
    t = _pick(seq, (512, 256, 128))
    m = batch * seq
    head = lambda base: pl.BlockSpec((1, seq, LANES), lambda b, h: (base + h, b, 0))
    return pl.pallas_call(
        functools.partial(_diff_kernel, t=t, lam_init=lam_init),
        grid=(batch, DIFF_HEADS),
        in_specs=[head(SLAB_DQ), head(SLAB_DK), head(SLAB_DV),
                  pl.BlockSpec((seq, LANES), lambda b, h: (0, 0)),
                  pl.BlockSpec((1, 1, LANES), lambda b, h: (h, 0, 0)),
                  pl.BlockSpec((4, DIFF_HALF), lambda b, h: (0, 0)),
                  pl.BlockSpec((1, LANES), lambda b, h: (0, 0))],
        out_specs=pl.BlockSpec((seq, LANES), lambda b, h: (b, h)),
        out_shape=jax.ShapeDtypeStruct((m, DIFF_HEADS * HEAD_DIM), BF16),
        scratch_shapes=[pltpu.VMEM((seq, 2 * LANES), BF16), pltpu.VMEM((LANES, seq), BF16),
                        pltpu.VMEM((t, 2 * t), F32), pltpu.VMEM((t, 2 * t), F32),
                        pltpu.VMEM((1, 2 * t), F32), pltpu.VMEM((1, 2 * t), F32),
                        pltpu.VMEM((LANES, 2 * t), F32)],
        compiler_params=_params(("parallel", "parallel")),
        name="diff_attn",
    )(slabs, slabs, slabs, kx, qx_diff, lam_vec, subln)


def _compress_kernel(y_ref, pos_ref, w1_ref, w2_ref, o_ref):
    half = y_ref.shape[3]
    y = y_ref[0, 0].astype(F32)
    top = (y + pos_ref[0, 0:1, :]).astype(BF16)
    bot = (y + pos_ref[0, 1:2, :]).astype(BF16)
    a = _dot(top, w1_ref[0, 0:half, :])
    b = _dot(bot, w1_ref[0, half:2 * half, :])
    nrow = a.shape[0]
    hid = a + pltpu.roll(b, nrow - 1, 0)
    hid = hid * jax.nn.sigmoid(hid)
    o_ref[0, 0] = _dot(hid.astype(BF16), w2_ref[0]).astype(BF16)


def _compress(ykv, pos2, w1, w2, batch, l):
    nrow, half = ykv.shape[2], ykv.shape[3]
    return pl.pallas_call(
        _compress_kernel,
        grid=(4, batch),
        in_specs=[pl.BlockSpec((1, 1, nrow, half), lambda s, b: (s, b, 0, 0)),
                  pl.BlockSpec((None, 1, 2, half), lambda s, b: (l, s // 2, 0, 0)),
                  pl.BlockSpec((None, 1, 2 * half, CMP_HIDDEN), lambda s, b: (l, s // 2, 0, 0)),
                  pl.BlockSpec((None, 1, CMP_HIDDEN, HEAD_DIM), lambda s, b: (l, s // 2, 0, 0))],
        out_specs=pl.BlockSpec((1, 1, nrow, HEAD_DIM), lambda s, b: (s, b, 0, 0)),
        out_shape=jax.ShapeDtypeStruct((4, batch, nrow, HEAD_DIM), BF16),
        compiler_params=_params(("parallel", "parallel")),
        name="nsa_compress",
    )(ykv, pos2, w1, w2)


def _topk_bias(cand, topk):
    nsel, tq = cand.shape
    sub = 8
    blocks = [cand[b * sub:(b + 1) * sub] for b in range(nsel // sub)]
    jidx = lax.broadcasted_iota(jnp.int32, (sub, tq), 0)
    ranks = [jnp.zeros((sub, tq), F32) for _ in blocks]
    for k in range(nsel):
        rk = cand[k:k + 1, :]
        for b, cb in enumerate(blocks):
            if b * sub > k:
                beats = rk >= cb
            elif b * sub + sub - 1 < k:
                beats = rk > cb
            else:
                beats = (rk > cb) | ((rk == cb) & (jidx + b * sub > k))
            ranks[b] = ranks[b] + jnp.where(beats, 1.0, 0.0)
    rank = jnp.concatenate(ranks, axis=0)
    return jnp.where(rank < topk, 0.0, SEL_MASK)


def _nsa_kernel(q_ref, kc_ref, vc_ref, ks_ref, vs_ref, kw_ref, vw_ref, kx_ref, cx_ref, ovt_ref, qx_ref,
                misc_ref, o_ref, ksa_ref, kwa_ref, vst_ref, vwt_ref, vct_ref, gt_ref, dbias_ref, wbias_ref,
                comb_ref, s_ref, m_ref, l_ref, acc_ref, *, tq, tk):
    g = pl.program_id(1)
    mcols = NSA_REP * tq
    nc = kc_ref.shape[2]
    seq = kx_ref.shape[0]
    nsel = seq // SEL_BLOCK
    span = WINDOW + tq

    _stage_kv(ksa_ref, vst_ref, ks_ref, kx_ref, vs_ref, tk)
    _stage_kv(kwa_ref, vwt_ref, kw_ref, kx_ref, vw_ref, tk, pad=WINDOW)
    pad_lane = lax.broadcasted_iota(jnp.int32, (WINDOW, 2 * LANES), 1)
    kwa_ref[0:WINDOW, :] = jnp.where(pad_lane == LANES + AUG_PAD, SEL_MASK, 0.0).astype(BF16)
    vwt_ref[:, 0:WINDOW] = jnp.zeros((LANES, WINDOW), BF16)
    vct_ref[...] = vc_ref[0, 0].astype(F32).T.astype(BF16)
    for o in range(tk // tq):
        dbias_ref[o] = _causal_bias(tk, mcols, tq, o * tq)
    rk = lax.broadcasted_iota(jnp.int32, (span, mcols), 0)
    rq = lax.broadcasted_iota(jnp.int32, (span, mcols), 1) & (tq - 1)
    wbias_ref[...] = jnp.where((rk > rq) & (rk <= rq + WINDOW), 0.0, NEG_INF)
    qx = jnp.concatenate([jnp.broadcast_to(qx_ref[0, r:r + 1, :], (tq, LANES)) for r in range(NSA_REP)], axis=0)
    kca = jnp.concatenate([kc_ref[0, 0], cx_ref[...]], axis=1)

    def qstep(qi, carry):
        _nsa_query_tile(qi, g, qx, kca, q_ref, ovt_ref, misc_ref, o_ref, ksa_ref, kwa_ref, vst_ref, vwt_ref,
                        vct_ref, gt_ref, dbias_ref, wbias_ref, comb_ref, s_ref, m_ref, l_ref, acc_ref,
                        tq=tq, tk=tk, nc=nc, nsel=nsel)
        return carry

    lax.fori_loop(0, seq // tq, qstep, 0)


def _nsa_query_tile(qi, g, qx, kca, q_ref, ovt_ref, misc_ref, o_ref, ksa_ref, kwa_ref, vst_ref, vwt_ref, vct_ref,
                    gt_ref, dbias_ref, wbias_ref, comb_ref, s_ref, m_ref, l_ref, acc_ref, *, tq, tk, nc, nsel):
    mcols = NSA_REP * tq
    span = WINDOW + tq
    q0 = pl.multiple_of(qi * tq, tq)
    q4 = q_ref[:, pl.ds(q0, tq), :].reshape(mcols, LANES)
    qa = jnp.concatenate([q4, qx.astype(BF16)], axis=1)
    col = lax.broadcasted_iota(jnp.int32, (1, mcols), 1)
    col_pos = q0 + (col & (tq - 1))

    sc = _dot_nt(kca, qa)
    cend = lax.broadcasted_iota(jnp.int32, (nc, mcols), 0) * CMP_STRIDE + (CMP_BLOCK - 1)
    sc = jnp.where(cend <= col_pos, sc, NEG_INF)
    e = jnp.exp2(sc - jnp.max(sc, axis=0, keepdims=True))
    inv = jnp.where(col_pos >= CMP_BLOCK - 1, 1.0 / jnp.sum(e, axis=0, keepdims=True), 0.0)
    p = e * inv
    o_cmp = _dot(vct_ref[...], p.astype(BF16))

    psum = p[:, 0:tq]
    for r in range(1, NSA_REP):
        psum = psum + p[:, r * tq:(r + 1) * tq]
    p_hi = psum.astype(BF16)
    p_lo = (psum - p_hi.astype(F32)).astype(BF16)
    imp = _dot(ovt_ref[...], p_hi) + _dot(ovt_ref[...], p_lo)
    qpos = q0 + lax.broadcasted_iota(jnp.int32, (LANES, tq), 1)
    blk = lax.broadcasted_iota(jnp.int32, (LANES, tq), 0)
    cur = jnp.right_shift(qpos, SEL_BLOCK.bit_length() - 1)
    forced = (blk == 0) | (blk == cur) | (blk == cur - 1)
    imp = jnp.where(blk <= cur, jnp.where(forced, FORCE_SCORE, imp), -1.0)
    bias_t = _topk_bias(imp[0:nsel], min(SEL_TOPK, nsel))
    if nsel < LANES:
        bias_t = jnp.concatenate([bias_t, jnp.zeros((LANES - nsel, tq), F32)], axis=0)
    selbias = jnp.concatenate([bias_t.T] * NSA_REP, axis=0)
    lane4 = lax.broadcasted_iota(jnp.int32, (mcols, LANES), 1)
    qa_sel = jnp.concatenate([q4, jnp.where(lane4 < SEL_BLOCK, selbias, qx).astype(BF16)], axis=1)

    sw = _dot_nt(kwa_ref[pl.ds(q0, span), :], qa) + wbias_ref[...]
    ew = jnp.exp2(sw - jnp.max(sw, axis=0, keepdims=True))
    o_win = _dot(vwt_ref[:, pl.ds(q0, span)], ew.astype(BF16)) * (1.0 / jnp.sum(ew, axis=0, keepdims=True))

    gt_ref[...] = jax.nn.sigmoid(misc_ref[pl.ds(q0, tq), :]).T
    gate = lambda r, i: gt_ref[pl.ds(MISC_NG + 3 * (NSA_REP * g + r) + i, 1), :]
    for r in range(NSA_REP):
        sl = slice(r * tq, (r + 1) * tq)
        comb_ref[:, sl] = gate(r, 0) * o_cmp[:, sl] + gate(r, 2) * o_win[:, sl]

    _softmax_init(m_ref, l_ref, acc_ref)
    _flash_tiles(qa_sel, ksa_ref, vst_ref, q0 // tk, dbias_ref.at[qi & (tk // tq - 1)], tk, s_ref, m_ref, l_ref,
                 acc_ref)
    o_sel = acc_ref[...] * (1.0 / l_ref[...])
    for r in range(NSA_REP):
        sl = slice(r * tq, (r + 1) * tq)
        out = comb_ref[:, sl] + gate(r, 1) * o_sel[:, sl]
        o_ref[pl.ds(q0, tq), r * LANES:(r + 1) * LANES] = out.T.astype(BF16)


def _nsa(slabs, kvc, kx, cx_cmp, ovt, qx_nsa, misc, batch, seq):
    tq = _pick(seq, (256, 128))
    tk = _pick(seq, (512, 256, 128))
    assert tq & (tq - 1) == 0 and tk % tq == 0 and seq >= WINDOW + tq
    m = batch * seq
    mcols = NSA_REP * tq
    nc = kvc.shape[2]
    kv_spec = lambda base: pl.BlockSpec((1, seq, LANES), lambda b, g: (base + g, b, 0))
    const = lambda shape: pl.BlockSpec(shape, lambda b, g: (0,) * len(shape))
    return pl.pallas_call(
        functools.partial(_nsa_kernel, tq=tq, tk=tk),
        grid=(batch, NSA_KV_HEADS),
        in_specs=[pl.BlockSpec((NSA_REP, seq, LANES), lambda b, g: (SLAB_NQ // NSA_REP + g, b, 0)),
                  pl.BlockSpec((1, 1, nc, LANES), lambda b, g: (g, b, 0, 0)),
                  pl.BlockSpec((1, 1, nc, LANES), lambda b, g: (2 + g, b, 0, 0)),
                  kv_spec(SLAB_NKS), kv_spec(SLAB_NVS), kv_spec(SLAB_NKW), kv_spec(SLAB_NVW),
                  const((seq, LANES)), const((nc, LANES)), const((LANES, nc)),
                  pl.BlockSpec((1, NSA_REP, LANES), lambda b, g: (g, 0, 0)),
                  pl.BlockSpec((seq, LANES), lambda b, g: (b, 0))],
        out_specs=pl.BlockSpec((seq, NSA_REP * LANES), lambda b, g: (b, g)),
        out_shape=jax.ShapeDtypeStruct((m, NSA_HEADS * HEAD_DIM), BF16),
        scratch_shapes=[pltpu.VMEM((seq, 2 * LANES), BF16), pltpu.VMEM((WINDOW + seq, 2 * LANES), BF16),
                        pltpu.VMEM((LANES, seq), BF16), pltpu.VMEM((LANES, WINDOW + seq), BF16),
                        pltpu.VMEM((LANES, nc), BF16), pltpu.VMEM((LANES, tq), F32),
                        pltpu.VMEM((tk // tq, tk, mcols), F32), pltpu.VMEM((WINDOW + tq, mcols), F32),
                        pltpu.VMEM((LANES, mcols), F32), pltpu.VMEM((tk, mcols), F32), pltpu.VMEM((1, mcols), F32), pltpu.VMEM((1, mcols), F32),
                        pltpu.VMEM((LANES, mcols), F32)],
        compiler_params=_params(("parallel", "parallel")),
        name="nsa_attn",
    )(slabs, kvc, kvc, slabs, slabs, slabs, slabs, kx, cx_cmp, ovt, qx_nsa, misc)


def _merge_kernel(h_ref, of_ref, od_ref, on_ref, wg0_ref, wg1_ref, wg2_ref, wf_ref, wd_ref, wn_ref, o_ref):
    h = h_ref[...]
    acc = jax.nn.sigmoid(_dot(h, wg0_ref[...])) * _dot(of_ref[...], wf_ref[...])
    acc = acc + jax.nn.sigmoid(_dot(h, wg1_ref[...])) * _dot(od_ref[...], wd_ref[...])
    acc = acc + jax.nn.sigmoid(_dot(h, wg2_ref[...])) * _dot(on_ref[...], wn_ref[...])
    o_ref[...] = acc.astype(BF16)


def _merge(h, o_fox, o_diff, o_nsa, w_gate, wb_fox, wb_diff, wb_nsa, l):
    m, d = h.shape
    tm = _pick(m, (1024, 512, 256))
    tn = _pick(d, (256, 128))
    nj = d // tn
    row = lambda width: pl.BlockSpec((tm, width), lambda i, j: (i, 0))
    gate = lambda t: pl.BlockSpec((None, d, tn), lambda i, j: (l, 0, t * nj + j))
    col = lambda k: pl.BlockSpec((None, k, tn), lambda i, j: (l, 0, j))
    return pl.pallas_call(
        _merge_kernel,
        grid=(m // tm, nj),
        in_specs=[row(d), row(o_fox.shape[1]), row(o_diff.shape[1]), row(o_nsa.shape[1]),
                  gate(0), gate(1), gate(2),
                  col(wb_fox.shape[1]), col(wb_diff.shape[1]), col(wb_nsa.shape[1])],
        out_specs=pl.BlockSpec((tm, tn), lambda i, j: (i, j)),
        out_shape=jax.ShapeDtypeStruct((m, d), BF16),
        compiler_params=_params(("parallel", "arbitrary")),
        name="gate_merge",
    )(h, o_fox, o_diff, o_nsa, w_gate, w_gate, w_gate, wb_fox, wb_diff, wb_nsa)


def _wout_kernel(a_ref, w_ref, x_ref, gp_ref, gn_ref, xo_ref, ho_ref):
    y = _dot(a_ref[...], w_ref[...])
    x_new = x_ref[...] + _rms(y, gp_ref[...])
    xo_ref[...] = x_new
    ho_ref[...] = _rms(x_new, gn_ref[...]).astype(BF16)


def _wout(a, w, x, g_post, g_next, l, name):
    m, d = x.shape
    k = a.shape[1]
    tm = _pick(m, (256, 128))
    row = lambda width: pl.BlockSpec((tm, width), lambda i: (i, 0))
    vec = lambda: pl.BlockSpec((1, d), lambda i: (0, 0))
    return pl.pallas_call(
        _wout_kernel,
        grid=(m // tm,),
        in_specs=[row(k), pl.BlockSpec((None, k, d), lambda i: (l, 0, 0), pipeline_mode=pl.Buffered(1)),
                  row(d), vec(), vec()],
        out_specs=[row(d), row(d)],
        out_shape=[jax.ShapeDtypeStruct((m, d), F32), jax.ShapeDtypeStruct((m, d), BF16)],
        compiler_params=_params(("parallel",)),
        name=name,
    )(a, w, x, g_post, g_next)


def _ffn_up_kernel(h_ref, wg_ref, wu_ref, o_ref):
    h = h_ref[...]
    gate = _dot(h, wg_ref[...])
    o_ref[...] = (gate * jax.nn.sigmoid(gate) * _dot(h, wu_ref[...])).astype(BF16)


def _ffn_up(h, w_up, l):
    m, d = h.shape
    dff = w_up.shape[2] // 2
    tm = _pick(m, (1024, 512, 256))
    tn = _pick(dff, (512, 256, 128))
    nj = dff // tn
    return pl.pallas_call(
        _ffn_up_kernel,
        grid=(m // tm, nj),
        in_specs=[pl.BlockSpec((tm, d), lambda i, j: (i, 0)),
                  pl.BlockSpec((None, d, tn), lambda i, j: (l, 0, j)),
                  pl.BlockSpec((None, d, tn), lambda i, j: (l, 0, nj + j))],
        out_specs=pl.BlockSpec((tm, tn), lambda i, j: (i, j)),
        out_shape=jax.ShapeDtypeStruct((m, dff), BF16),
        compiler_params=_params(("parallel", "arbitrary")),
        name="ffn_up",
    )(h, w_up, w_up)


def _pos_columns(pos):
    lane = jnp.arange(LANES)[None, :]
    hi = (lane >= AUG_HI) & (lane < AUG_HI + AUG_TERMS)
    lo = (lane >= AUG_LO) & (lane < AUG_LO + AUG_TERMS)
    return jnp.where(hi, (pos // LANES)[:, None], jnp.where(lo, (pos % LANES)[:, None], 0))


def _key_aug_table(seq):
    j = jnp.arange(seq)
    lane = jnp.arange(LANES)[None, :]
    onehot = (lane == (j // SEL_BLOCK)[:, None]) & (lane < SEL_BLOCK)
    return (_pos_columns(j) + onehot.astype(jnp.int32)).astype(BF16)


def _cmp_aug_table(nrow):
    return _pos_columns(jnp.arange(nrow) * CMP_STRIDE + CMP_BLOCK - 1).astype(BF16)


def _overlap_table_t(nrow, seq):
    start = jnp.arange(nrow)[None, :] * CMP_STRIDE
    blk = jnp.arange(LANES)[:, None]
    sel = blk * SEL_BLOCK
    ov = ((start < sel + SEL_BLOCK) & (start + CMP_BLOCK - 1 >= sel) & (blk < seq // SEL_BLOCK)
          & (jnp.arange(nrow)[None, :] < (seq - CMP_BLOCK) // CMP_STRIDE + 1))
    return ov.astype(BF16)


def _query_aug_rows(n_heads):
    slopes = 2.0 ** (-8.0 * jnp.arange(1, n_heads + 1, dtype=F32) / n_heads)
    terms = [t.astype(F32) for t in _split3(jnp.float32(LOG2E))]
    lane = jnp.arange(LANES)[None, :]
    out = jnp.where(lane == AUG_PAD, 1.0, jnp.zeros((n_heads, LANES), F32))
    for i, t in enumerate(terms):
        out = jnp.where(lane == AUG_HI + i, slopes[:, None] * t * LANES, out)
        out = jnp.where(lane == AUG_LO + i, slopes[:, None] * t, out)
    return out


def kernel(x, w_in, fox_forget_bias, diff_lambda, diff_subln, nsa_cmp_pos, nsa_cmp_w1, nsa_cmp_w2,
           w_branch_fox, w_branch_diff, w_branch_nsa, w_gate, w_out, norm_gains, w_ffn_up, w_ffn_down):
    batch, seq, d = x.shape
    depth = w_in.shape[0]
    m = batch * seq
    fw, dw, nw, kvw = FOX_HEADS * HEAD_DIM, DIFF_HEADS * HEAD_DIM, NSA_HEADS * HEAD_DIM, NSA_KV_HEADS * HEAD_DIM
    ff0 = 3 * fw
    dq0 = ff0 + FOX_HEADS
    ng0 = dq0 + 3 * dw + nw + 6 * kvw
    n_main = ng0 - FOX_HEADS
    assert n_main == N_SLABS * LANES and w_in.shape[2] == ng0 + 3 * NSA_HEADS

    w_main = jnp.concatenate([w_in[:, :, :ff0], w_in[:, :, dq0:ng0]], axis=2).astype(BF16)
    w_misc = jnp.concatenate([w_in[:, :, ff0:dq0], w_in[:, :, ng0:],
                              jnp.zeros((depth, d, LANES - FOX_HEADS - 3 * NSA_HEADS), F32)], axis=2).astype(BF16)
    colscale = jnp.ones((n_main,), F32)
    colscale = colscale.at[SLAB_FQ * LANES:SLAB_FK * LANES].set(HEAD_DIM ** -0.5 * LOG2E)
    colscale = colscale.at[SLAB_DQ * LANES:SLAB_DK * LANES].set(DIFF_HALF ** -0.5 * LOG2E)
    colscale = colscale.at[SLAB_NQ * LANES:SLAB_NKC * LANES].set(HEAD_DIM ** -0.5 * LOG2E)
    colscale = colscale[None, :]
    fbias = jnp.pad(fox_forget_bias.astype(F32), ((0, 0), (0, LANES - FOX_HEADS)))[:, None, :]
    half = CMP_STRIDE * HEAD_DIM
    pos2 = nsa_cmp_pos.astype(F32).reshape(depth, 2, 2, half)
    w1 = nsa_cmp_w1.astype(BF16)
    w2 = nsa_cmp_w2.astype(BF16)
    wbf, wbd, wbn = w_branch_fox.astype(BF16), w_branch_diff.astype(BF16), w_branch_nsa.astype(BF16)
    wg, wo = w_gate.astype(BF16), w_out.astype(BF16)
    wup, wdn = w_ffn_up.astype(BF16), w_ffn_down.astype(BF16)
    gains = norm_gains.astype(F32)

    nrow = seq // CMP_STRIDE
    kx = _key_aug_table(seq)
    cx_cmp = _cmp_aug_table(nrow)
    ovt = _overlap_table_t(nrow, seq)
    qx_diff = _query_aug_rows(DIFF_HEADS)[:, None, :]
    qx_nsa = _query_aug_rows(NSA_HEADS).reshape(NSA_KV_HEADS, NSA_REP, LANES)

    xf = x.reshape(m, d).astype(F32)
    h = _norm(xf, gains[0, 0][None, :])
    for l in range(depth):
        lam_init = 0.8 - 0.6 * math.exp(-0.3 * l)
        slabs = _inproj(h, w_main, colscale, l)
        misc = _miscproj(h, w_misc, l)
        cx_fox = _logf(misc, fbias[l], batch, seq)
        o_fox = _fox(slabs, cx_fox, batch, seq)
        o_diff = _diff(slabs, kx, qx_diff, diff_lambda[l].astype(F32), diff_subln[l].astype(F32)[None, :],
                       batch, seq, lam_init)
        ykv = slabs[SLAB_NKC:SLAB_NKC + 4].reshape(4, batch, nrow, half)
        kvc = _compress(ykv, pos2, w1, w2, batch, l)
        o_nsa = _nsa(slabs, kvc, kx, cx_cmp, ovt, qx_nsa, misc, batch, seq)
        merged = _merge(h, o_fox, o_diff, o_nsa, wg, wbf, wbd, wbn, l)
        xf, h2 = _wout(merged, wo, xf, gains[l, 1][None, :], gains[l, 2][None, :], l, "out_proj")
        act = _ffn_up(h2, wup, l)
        g_next = gains[min(l + 1, depth - 1), 0][None, :]
        xf, h = _wout(act, wdn, xf, gains[l, 3][None, :], g_next, l, "ffn_down")
    return xf.reshape(batch, seq, d).astype(x.dtype)
```

```python
import functools
import math

import jax
import jax.numpy as jnp
from jax import lax
from jax.experimental import pallas as pl
from jax.experimental.pallas import tpu as pltpu

F32 = jnp.float32
BF16 = jnp.bfloat16

HEAD_DIM = 128
FOX_HEADS = 4
DIFF_HEADS = 4
DIFF_HALF = HEAD_DIM // 2
NSA_HEADS = 8
NSA_KV_HEADS = 2
NSA_REP = NSA_HEADS // NSA_KV_HEADS
CMP_BLOCK = 32
CMP_STRIDE = 16
CMP_HIDDEN = 256
SEL_BLOCK = 64
SEL_TOPK = 16
WINDOW = 512
N_BRANCHES = 3
EPS = 1e-6
NEG_INF = -1e30
FORCE_SCORE = 1e4
SEL_MASK = -32768.0
LOG2E = math.log2(math.e)
LANES = 128

SLAB_FQ, SLAB_FK, SLAB_FV = 0, 4, 8
SLAB_DQ, SLAB_DK, SLAB_DV = 12, 16, 20
SLAB_NQ = 24
SLAB_NKC, SLAB_NVC, SLAB_NKS, SLAB_NVS, SLAB_NKW, SLAB_NVW = 32, 34, 36, 38, 40, 42
N_SLABS = 44
MISC_FF = 0
MISC_NG = 4
AUG_HI = 64
AUG_LO = 67
AUG_TERMS = 3
AUG_PAD = 70

VMEM_LIMIT = 56 * 1024 * 1024


def _pick(n, prefs):
    for p in prefs:
        if p <= n and n % p == 0:
            return p
    return n


def _params(sem):
    return pltpu.CompilerParams(dimension_semantics=sem, vmem_limit_bytes=VMEM_LIMIT)


def _rms(y, g):
    return y * lax.rsqrt(jnp.mean(y * y, axis=-1, keepdims=True) + EPS) * g


def _dot(a, b):
    return jnp.dot(a, b, preferred_element_type=F32)


def _dot_nt(a, b):
    return lax.dot_general(a, b, (((1,), (1,)), ((), ())), preferred_element_type=F32)


def _split3(x):
    hi = x.astype(BF16)
    r = x - hi.astype(F32)
    mid = r.astype(BF16)
    lo = (r - mid.astype(F32)).astype(BF16)
    return hi, mid, lo


def _norm_kernel(x_ref, g_ref, h_ref):
    h_ref[...] = _rms(x_ref[...], g_ref[...]).astype(BF16)


def _norm(x, g):
    m, d = x.shape
    tm = _pick(m, (512, 256, 128))
    return pl.pallas_call(
        _norm_kernel,
        grid=(m // tm,),
        in_specs=[pl.BlockSpec((tm, d), lambda i: (i, 0)), pl.BlockSpec((1, d), lambda i: (0, 0))],
        out_specs=pl.BlockSpec((tm, d), lambda i: (i, 0)),
        out_shape=jax.ShapeDtypeStruct((m, d), BF16),
        compiler_params=_params(("parallel",)),
        name="norm_in",
    )(x, g)


def _inproj_kernel(h_ref, w_ref, cs_ref, wm_ref, o_ref, misc_ref):
    h = h_ref[...]
    acc = _dot(h, w_ref[...]) * cs_ref[...]
    for s in range(o_ref.shape[0]):
        o_ref[s] = acc[:, s * LANES:(s + 1) * LANES].astype(BF16)

    @pl.when(pl.program_id(1) == 0)
    def _():
        misc_ref[...] = _dot(h, wm_ref[...])


def _inproj(h, w_main, colscale, w_misc, l):
    m, d = h.shape
    n = w_main.shape[2]
    tm = _pick(m, (1024, 512, 256))
    tn = 512
    return pl.pallas_call(
        _inproj_kernel,
        grid=(m // tm, n // tn),
        in_specs=[pl.BlockSpec((tm, d), lambda i, j: (i, 0)),
                  pl.BlockSpec((None, d, tn), lambda i, j: (l, 0, j)),
                  pl.BlockSpec((1, tn), lambda i, j: (0, j)),
                  pl.BlockSpec((None, d, LANES), lambda i, j: (l, 0, 0))],
        out_specs=[pl.BlockSpec((tn // LANES, tm, LANES), lambda i, j: (j, i, 0)),
                   pl.BlockSpec((tm, LANES), lambda i, j: (i, 0))],
        out_shape=[jax.ShapeDtypeStruct((n // LANES, m, LANES), BF16),
                   jax.ShapeDtypeStruct((m, LANES), F32)],
        compiler_params=_params(("parallel", "arbitrary")),
        name="inproj",
    )(h, w_main, colscale, w_misc)


def _logf_kernel(misc_ref, bias_ref, o_ref, *, tc):
    s = misc_ref.shape[0]
    row = lax.broadcasted_iota(jnp.int32, (tc, tc), 0)
    col = lax.broadcasted_iota(jnp.int32, (tc, tc), 1)
    tri = jnp.where(col <= row, 1.0, 0.0).astype(BF16)
    lane = lax.broadcasted_iota(jnp.int32, (tc, LANES), 1)

    def chunk(c, carry):
        r0 = pl.multiple_of(c * tc, tc)
        z = misc_ref[pl.ds(r0, tc), :] + bias_ref[...]
        lf = jnp.minimum(z, 0.0) - jnp.log1p(jnp.exp(-jnp.abs(z)))
        hi, mid, lo = _split3(lf)
        cum = _dot(tri, hi) + _dot(tri, mid) + _dot(tri, lo) + carry
        for hd in range(FOX_HEADS):
            c2 = jnp.broadcast_to(cum[:, hd:hd + 1], (tc, LANES)) * LOG2E
            c_hi = c2.astype(BF16).astype(F32)
            c_mid = (c2 - c_hi).astype(BF16).astype(F32)
            aug = jnp.where(lane == 0, c_hi, jnp.where(lane == 1, c_mid,
                                                      jnp.where(lane == 2, c2 - c_hi - c_mid, 0.0)))
            o_ref[hd, pl.ds(r0, tc), :] = aug.astype(BF16)
        return cum[tc - 1:tc, :]

    lax.fori_loop(0, s // tc, chunk, jnp.zeros((1, LANES), F32))


def _logf(misc, bias_row, batch, seq):
    tc = _pick(seq, (256, 128))
    return pl.pallas_call(
        functools.partial(_logf_kernel, tc=tc),
        grid=(batch,),
        in_specs=[pl.BlockSpec((seq, LANES), lambda b: (b, 0)), pl.BlockSpec((1, LANES), lambda b: (0, 0))],
        out_specs=pl.BlockSpec((FOX_HEADS, seq, LANES), lambda b: (0, b, 0)),
        out_shape=jax.ShapeDtypeStruct((FOX_HEADS, batch * seq, LANES), BF16),
        compiler_params=_params(("parallel",)),
        name="fox_logf",
    )(misc, bias_row)


def _flash_scratch(tk, mcols):
    return [pltpu.VMEM((tk, mcols), F32), pltpu.VMEM((1, mcols), F32), pltpu.VMEM((1, mcols), F32),
            pltpu.VMEM((LANES, mcols), F32)]


def _causal_bias(tk, mcols, tq, offset):
    rk = lax.broadcasted_iota(jnp.int32, (tk, mcols), 0)
    rq = lax.broadcasted_iota(jnp.int32, (tk, mcols), 1) & (tq - 1)
    return jnp.where(rk <= rq + offset, 0.0, NEG_INF)


def _flash_tiles(qa, ka_ref, vt_ref, n_full, diag_bias_ref, tk, scratch, next_qa=None):
    s_ref, m_ref, l_ref, acc_ref = scratch

    def logits(j, queries=qa):
        k0 = pl.multiple_of(j * tk, tk)
        return _dot_nt(ka_ref[pl.ds(k0, tk), :], queries)

    def step(s, j):
        k0 = pl.multiple_of(j * tk, tk)
        m_prev = m_ref[...]
        m_new = jnp.maximum(m_prev, jnp.max(s, axis=0, keepdims=True))
        alpha = jnp.exp2(m_prev - m_new)
        p = jnp.exp2(s - m_new)
        l_ref[...] = alpha * l_ref[...] + jnp.sum(p, axis=0, keepdims=True)
        acc_ref[...] = alpha * acc_ref[...] + _dot(vt_ref[:, pl.ds(k0, tk)], p.astype(BF16))
        m_ref[...] = m_new

    m_ref[...] = jnp.full(m_ref.shape, NEG_INF, F32)
    l_ref[...] = jnp.zeros(l_ref.shape, F32)
    acc_ref[...] = jnp.zeros(acc_ref.shape, F32)
    if next_qa is None:
        s_ref[...] = logits(0)

    def full(j, c):
        s_next = logits(j + 1)
        step(s_ref[...], j)
        s_ref[...] = s_next
        return c

    lax.fori_loop(0, n_full, full, 0)
    if next_qa is None:
        step(s_ref[...] + diag_bias_ref[...].astype(F32), n_full)
    else:
        s_next = logits(0, next_qa)
        step(s_ref[...] + diag_bias_ref[...].astype(F32), n_full)
        s_ref[...] = s_next
    return acc_ref[...] * (1.0 / l_ref[...])


def _stage_kv(ka_ref, vt_ref, k_ref, kx_ref, v_ref, chunk, pad=0):
    def body(c, carry):
        r0 = pl.multiple_of(c * chunk, chunk)
        ka_ref[pl.ds(pad + r0, chunk), 0:LANES] = k_ref[0, pl.ds(r0, chunk), :]
        ka_ref[pl.ds(pad + r0, chunk), LANES:2 * LANES] = kx_ref[pl.ds(r0, chunk), :]
        vt_ref[:, pl.ds(pad + r0, chunk)] = v_ref[0, pl.ds(r0, chunk), :].astype(F32).T.astype(BF16)
        return carry

    lax.fori_loop(0, v_ref.shape[1] // chunk, body, 0)


def _fox_kernel(q_ref, k_ref, v_ref, cx_ref, o_ref, ka_ref, vt_ref, tri_ref, *flash, t):
    _stage_kv(ka_ref, vt_ref, k_ref, cx_ref.at[0], v_ref, t)
    tri_ref[...] = _causal_bias(t, t, t, 0)
    lane = lax.broadcasted_iota(jnp.int32, (t, LANES), 1)
    qx = jnp.where(lane < 3, -1.0, 0.0).astype(BF16)

    nq = q_ref.shape[1] // t

    def queries(qi):
        q0 = pl.multiple_of(qi * t, t)
        return jnp.concatenate([q_ref[0, pl.ds(q0, t), :], qx], axis=1)

    flash[0][...] = _dot_nt(ka_ref[0:t, :], queries(0))

    def qstep(qi, c):
        o = _flash_tiles(queries(qi), ka_ref, vt_ref, qi, tri_ref, t, flash,
                         next_qa=queries(jnp.minimum(qi + 1, nq - 1)))
        o_ref[pl.ds(pl.multiple_of(qi * t, t), t), :] = o.T.astype(BF16)
        return c

    lax.fori_loop(0, nq, qstep, 0)


def _fox(slabs, cx, batch, seq):
    t = _pick(seq, (512, 256, 128))
    m = batch * seq
    head = lambda base: pl.BlockSpec((1, seq, LANES), lambda b, h: (base + h, b, 0))
    return pl.pallas_call(
        functools.partial(_fox_kernel, t=t),
        grid=(batch, FOX_HEADS),
        in_specs=[head(SLAB_FQ), head(SLAB_FK), head(SLAB_FV), head(0)],
        out_specs=pl.BlockSpec((seq, LANES), lambda b, h: (b, h)),
        out_shape=jax.ShapeDtypeStruct((m, FOX_HEADS * HEAD_DIM), BF16),
        scratch_shapes=[pltpu.VMEM((seq, 2 * LANES), BF16), pltpu.VMEM((LANES, seq), BF16),
                        pltpu.VMEM((t, t), F32)] + _flash_scratch(t, t),
        compiler_params=_params(("parallel", "parallel")),
        name="fox_attn",
    )(slabs, slabs, slabs, cx)


def _diff_kernel(q_ref, k_ref, v_ref, kx_ref, qx_ref, lam_ref, sub_ref, o_ref, ka_ref, vt_ref, tri_ref, *flash,
                 t, lam_init):
    _stage_kv(ka_ref, vt_ref, k_ref, kx_ref, v_ref, t)
    tri_ref[...] = _causal_bias(t, 2 * t, t, 0)
    lane = lax.broadcasted_iota(jnp.int32, (t, LANES), 1)
    qx = jnp.broadcast_to(qx_ref[0], (t, LANES)).astype(BF16)
    lv = lam_ref[...]
    lam = (jnp.exp(jnp.sum(lv[0:1] * lv[1:2], axis=-1, keepdims=True))
           - jnp.exp(jnp.sum(lv[2:3] * lv[3:4], axis=-1, keepdims=True)) + lam_init)

    nq = q_ref.shape[1] // t

    def queries(qi):
        q = q_ref[0, pl.ds(pl.multiple_of(qi * t, t), t), :].astype(F32)
        return jnp.concatenate([
            jnp.concatenate([jnp.where(lane < DIFF_HALF, q, 0.0).astype(BF16), qx], axis=1),
            jnp.concatenate([jnp.where(lane >= DIFF_HALF, q, 0.0).astype(BF16), qx], axis=1)], axis=0)

    flash[0][...] = _dot_nt(ka_ref[0:t, :], queries(0))

    def qstep(qi, c):
        o = _flash_tiles(queries(qi), ka_ref, vt_ref, qi, tri_ref, t, flash,
                         next_qa=queries(jnp.minimum(qi + 1, nq - 1)))
        o = (o[:, :t] - lam * o[:, t:]).T
        o_ref[pl.ds(pl.multiple_of(qi * t, t), t), :] = (_rms(o, sub_ref[...]) * (1.0 - lam_init)).astype(BF16)
        return c

    lax.fori_loop(0, nq, qstep, 0)


def _diff(slabs, kx, qx_diff, lam_vec, subln, batch, seq, lam_init):
    t = _pick(seq, (512, 256, 128))
    m = batch * seq
    head = lambda base: pl.BlockSpec((1, seq, LANES), lambda b, h: (base + h, b, 0))
    return pl.pallas_call(
        functools.partial(_diff_kernel, t=t, lam_init=lam_init),
        grid=(batch, DIFF_HEADS),
        in_specs=[head(SLAB_DQ), head(SLAB_DK), head(SLAB_DV),
                  pl.BlockSpec((seq, LANES), lambda b, h: (0, 0)),
                  pl.BlockSpec((1, 1, LANES), lambda b, h: (h, 0, 0)),
                  pl.BlockSpec((4, DIFF_HALF), lambda b, h: (0, 0)),
                  pl.BlockSpec((1, LANES), lambda b, h: (0, 0))],
        out_specs=pl.BlockSpec((seq, LANES), lambda b, h: (b, h)),
        out_shape=jax.ShapeDtypeStruct((m, DIFF_HEADS * HEAD_DIM), BF16),
        scratch_shapes=[pltpu.VMEM((seq, 2 * LANES), BF16), pltpu.VMEM((LANES, seq), BF16),
                        pltpu.VMEM((t, 2 * t), F32)] + _flash_scratch(t, 2 * t),
        compiler_params=_params(("parallel", "parallel")),
        name="diff_attn",
    )(slabs, slabs, slabs, kx, qx_diff, lam_vec, subln)


def _compress_kernel(y_ref, pos_ref, w1_ref, w2_ref, o_ref):
    half = y_ref.shape[3]
    y = y_ref[0, 0].astype(F32)
    top = (y + pos_ref[0, 0:1, :]).astype(BF16)
    bot = (y + pos_ref[0, 1:2, :]).astype(BF16)
    a = _dot(top, w1_ref[0, 0:half, :])
    b = _dot(bot, w1_ref[0, half:2 * half, :])
    nrow = a.shape[0]
    hid = a + pltpu.roll(b, nrow - 1, 0)
    hid = hid * jax.nn.sigmoid(hid)
    o_ref[0, 0] = _dot(hid.astype(BF16), w2_ref[0]).astype(BF16)


def _compress(ykv, pos2, w1, w2, batch, l):
    nrow, half = ykv.shape[2], ykv.shape[3]
    return pl.pallas_call(
        _compress_kernel,
        grid=(4, batch),
        in_specs=[pl.BlockSpec((1, 1, nrow, half), lambda s, b: (s, b, 0, 0)),
                  pl.BlockSpec((None, 1, 2, half), lambda s, b: (l, s // 2, 0, 0)),
                  pl.BlockSpec((None, 1, 2 * half, CMP_HIDDEN), lambda s, b: (l, s // 2, 0, 0)),
                  pl.BlockSpec((None, 1, CMP_HIDDEN, HEAD_DIM), lambda s, b: (l, s // 2, 0, 0))],
        out_specs=pl.BlockSpec((1, 1, nrow, HEAD_DIM), lambda s, b: (s, b, 0, 0)),
        out_shape=jax.ShapeDtypeStruct((4, batch, nrow, HEAD_DIM), BF16),
        compiler_params=_params(("parallel", "parallel")),
        name="nsa_compress",
    )(ykv, pos2, w1, w2)


def _topk_bias(cand, topk):
    nsel, tq = cand.shape
    sub = 8
    blocks = [cand[b * sub:(b + 1) * sub] for b in range(nsel // sub)]
    jidx = lax.broadcasted_iota(jnp.int32, (sub, tq), 0)
    ranks = [jnp.zeros((sub, tq), F32) for _ in blocks]
    for k in range(nsel):
        rk = cand[k:k + 1, :]
        for b, cb in enumerate(blocks):
            if b * sub > k:
                beats = rk >= cb
            elif b * sub + sub - 1 < k:
                beats = rk > cb
            else:
                beats = (rk > cb) | ((rk == cb) & (jidx + b * sub > k))
            ranks[b] = ranks[b] + jnp.where(beats, 1.0, 0.0)
    rank = jnp.concatenate(ranks, axis=0)
    return jnp.where(rank < topk, 0.0, SEL_MASK)


def _nsa_kernel(q_ref, kc_ref, vc_ref, ks_ref, vs_ref, kw_ref, vw_ref, kx_ref, cx_ref, ovt_ref, qx_ref,
                misc_ref, o_ref, ksa_ref, kwa_ref, vst_ref, vwt_ref, vct_ref, gt_ref, dbias_ref, wbias_ref,
                comb_ref, *flash, tq, tk):
    g = pl.program_id(1)
    mcols = NSA_REP * tq
    nc = kc_ref.shape[2]
    seq = kx_ref.shape[0]
    nsel = seq // SEL_BLOCK
    span = WINDOW + tq

    _stage_kv(ksa_ref, vst_ref, ks_ref, kx_ref, vs_ref, tk)
    _stage_kv(kwa_ref, vwt_ref, kw_ref, kx_ref, vw_ref, tk, pad=WINDOW)
    pad_lane = lax.broadcasted_iota(jnp.int32, (WINDOW, 2 * LANES), 1)
    kwa_ref[0:WINDOW, :] = jnp.where(pad_lane == LANES + AUG_PAD, SEL_MASK, 0.0).astype(BF16)
    vwt_ref[:, 0:WINDOW] = jnp.zeros((LANES, WINDOW), BF16)
    vct_ref[...] = vc_ref[0, 0].astype(F32).T.astype(BF16)
    for o in range(tk // tq):
        dbias_ref[o] = _causal_bias(tk, mcols, tq, o * tq).astype(BF16)
    rk = lax.broadcasted_iota(jnp.int32, (span, mcols), 0)
    rq = lax.broadcasted_iota(jnp.int32, (span, mcols), 1) & (tq - 1)
    wbias_ref[...] = jnp.where((rk > rq) & (rk <= rq + WINDOW), 0.0, NEG_INF).astype(BF16)
    qx = jnp.concatenate([jnp.broadcast_to(qx_ref[0, r:r + 1, :], (tq, LANES)) for r in range(NSA_REP)], axis=0)
    kca = jnp.concatenate([kc_ref[0, 0], cx_ref[...]], axis=1)

    def qstep(qi, carry):
        _nsa_query_tile(qi, g, qx, kca, q_ref, ovt_ref, misc_ref, o_ref, ksa_ref, kwa_ref, vst_ref, vwt_ref,
                        vct_ref, gt_ref, dbias_ref, wbias_ref, comb_ref, flash, tq=tq, tk=tk, nc=nc, nsel=nsel)
        return carry

    lax.fori_loop(0, seq // tq, qstep, 0)


def _nsa_query_tile(qi, g, qx, kca, q_ref, ovt_ref, misc_ref, o_ref, ksa_ref, kwa_ref, vst_ref, vwt_ref, vct_ref,
                    gt_ref, dbias_ref, wbias_ref, comb_ref, flash, *, tq, tk, nc, nsel):
    mcols = NSA_REP * tq
    span = WINDOW + tq
    q0 = pl.multiple_of(qi * tq, tq)
    q4 = q_ref[:, pl.ds(q0, tq), :].reshape(mcols, LANES)
    qa = jnp.concatenate([q4, qx.astype(BF16)], axis=1)
    col = lax.broadcasted_iota(jnp.int32, (1, mcols), 1)
    col_pos = q0 + (col & (tq - 1))

    sc = _dot_nt(kca, qa)
    cend = lax.broadcasted_iota(jnp.int32, (nc, mcols), 0) * CMP_STRIDE + (CMP_BLOCK - 1)
    sc = jnp.where(cend <= col_pos, sc, NEG_INF)
    e = jnp.exp2(sc - jnp.max(sc, axis=0, keepdims=True))
    inv = jnp.where(col_pos >= CMP_BLOCK - 1, 1.0 / jnp.sum(e, axis=0, keepdims=True), 0.0)
    p = e * inv
    o_cmp = _dot(vct_ref[...], p.astype(BF16))

    psum = p[:, 0:tq]
    for r in range(1, NSA_REP):
        psum = psum + p[:, r * tq:(r + 1) * tq]
    p_hi = psum.astype(BF16)
    p_lo = (psum - p_hi.astype(F32)).astype(BF16)
    imp = _dot(ovt_ref[...], p_hi) + _dot(ovt_ref[...], p_lo)
    qpos = q0 + lax.broadcasted_iota(jnp.int32, (LANES, tq), 1)
    blk = lax.broadcasted_iota(jnp.int32, (LANES, tq), 0)
    cur = jnp.right_shift(qpos, SEL_BLOCK.bit_length() - 1)
    forced = (blk == 0) | (blk == cur) | (blk == cur - 1)
    imp = jnp.where(blk <= cur, jnp.where(forced, FORCE_SCORE, imp), -1.0)
    bias_t = _topk_bias(imp[0:nsel], min(SEL_TOPK, nsel))
    if nsel < LANES:
        bias_t = jnp.concatenate([bias_t, jnp.zeros((LANES - nsel, tq), F32)], axis=0)
    selbias = jnp.concatenate([bias_t.T] * NSA_REP, axis=0)
    lane4 = lax.broadcasted_iota(jnp.int32, (mcols, LANES), 1)
    qa_sel = jnp.concatenate([q4, jnp.where(lane4 < SEL_BLOCK, selbias, qx).astype(BF16)], axis=1)

    sw = _dot_nt(kwa_ref[pl.ds(q0, span), :], qa) + wbias_ref[...].astype(F32)
    ew = jnp.exp2(sw - jnp.max(sw, axis=0, keepdims=True))
    o_win = _dot(vwt_ref[:, pl.ds(q0, span)], ew.astype(BF16)) * (1.0 / jnp.sum(ew, axis=0, keepdims=True))

    gt_ref[...] = jax.nn.sigmoid(misc_ref[pl.ds(q0, tq), :]).T
    gate = lambda r, i: gt_ref[pl.ds(MISC_NG + 3 * (NSA_REP * g + r) + i, 1), :]
    for r in range(NSA_REP):
        sl = slice(r * tq, (r + 1) * tq)
        comb_ref[:, sl] = gate(r, 0) * o_cmp[:, sl] + gate(r, 2) * o_win[:, sl]

    o_sel = _flash_tiles(qa_sel, ksa_ref, vst_ref, q0 // tk, dbias_ref.at[qi & (tk // tq - 1)], tk, flash)
    for r in range(NSA_REP):
        sl = slice(r * tq, (r + 1) * tq)
        out = comb_ref[:, sl] + gate(r, 1) * o_sel[:, sl]
        o_ref[pl.ds(q0, tq), r * LANES:(r + 1) * LANES] = out.T.astype(BF16)


def _nsa(slabs, kvc, kx, cx_cmp, ovt, qx_nsa, misc, batch, seq):
    tq = _pick(seq, (256, 128))
    tk = _pick(seq, (512, 256, 128))
    assert tq & (tq - 1) == 0 and tk % tq == 0 and seq >= WINDOW + tq
    m = batch * seq
    mcols = NSA_REP * tq
    nc = kvc.shape[2]
    kv_spec = lambda base: pl.BlockSpec((1, seq, LANES), lambda b, g: (base + g, b, 0))
    const = lambda shape: pl.BlockSpec(shape, lambda b, g: (0,) * len(shape), pipeline_mode=pl.Buffered(1))
    return pl.pallas_call(
        functools.partial(_nsa_kernel, tq=tq, tk=tk),
        grid=(batch, NSA_KV_HEADS),
        in_specs=[pl.BlockSpec((NSA_REP, seq, LANES), lambda b, g: (SLAB_NQ // NSA_REP + g, b, 0)),
                  pl.BlockSpec((1, 1, nc, LANES), lambda b, g: (g, b, 0, 0)),
                  pl.BlockSpec((1, 1, nc, LANES), lambda b, g: (2 + g, b, 0, 0)),
                  kv_spec(SLAB_NKS), kv_spec(SLAB_NVS), kv_spec(SLAB_NKW), kv_spec(SLAB_NVW),
                  const((seq, LANES)), const((nc, LANES)), const((LANES, nc)),
                  pl.BlockSpec((1, NSA_REP, LANES), lambda b, g: (g, 0, 0)),
                  pl.BlockSpec((seq, LANES), lambda b, g: (b, 0))],
        out_specs=pl.BlockSpec((seq, NSA_REP * LANES), lambda b, g: (b, g)),
        out_shape=jax.ShapeDtypeStruct((m, NSA_HEADS * HEAD_DIM), BF16),
        scratch_shapes=[pltpu.VMEM((seq, 2 * LANES), BF16), pltpu.VMEM((WINDOW + seq, 2 * LANES), BF16),
                        pltpu.VMEM((LANES, seq), BF16), pltpu.VMEM((LANES, WINDOW + seq), BF16),
                        pltpu.VMEM((LANES, nc), BF16), pltpu.VMEM((LANES, tq), F32),
                        pltpu.VMEM((tk // tq, tk, mcols), BF16), pltpu.VMEM((WINDOW + tq, mcols), BF16),
                        pltpu.VMEM((LANES, mcols), F32)] + _flash_scratch(tk, mcols),
        compiler_params=_params(("parallel", "parallel")),
        name="nsa_attn",
    )(slabs, kvc, kvc, slabs, slabs, slabs, slabs, kx, cx_cmp, ovt, qx_nsa, misc)


def _merge_kernel(h_ref, of_ref, od_ref, on_ref, wg0_ref, wg1_ref, wg2_ref, wf_ref, wd_ref, wn_ref, o_ref):
    h = h_ref[...]
    acc = jax.nn.sigmoid(_dot(h, wg0_ref[...])) * _dot(of_ref[...], wf_ref[...])
    acc = acc + jax.nn.sigmoid(_dot(h, wg1_ref[...])) * _dot(od_ref[...], wd_ref[...])
    acc = acc + jax.nn.sigmoid(_dot(h, wg2_ref[...])) * _dot(on_ref[...], wn_ref[...])
    o_ref[...] = acc.astype(BF16)


def _merge(h, o_fox, o_diff, o_nsa, w_gate, wb_fox, wb_diff, wb_nsa, l):
    m, d = h.shape
    tm = _pick(m, (1024, 512, 256))
    tn = _pick(d, (256, 128))
    nj = d // tn
    row = lambda width: pl.BlockSpec((tm, width), lambda i, j: (i, 0))
    gate = lambda t: pl.BlockSpec((None, d, tn), lambda i, j: (l, 0, t * nj + j))
    col = lambda k: pl.BlockSpec((None, k, tn), lambda i, j: (l, 0, j))
    return pl.pallas_call(
        _merge_kernel,
        grid=(m // tm, nj),
        in_specs=[row(d), row(o_fox.shape[1]), row(o_diff.shape[1]), row(o_nsa.shape[1]),
                  gate(0), gate(1), gate(2),
                  col(wb_fox.shape[1]), col(wb_diff.shape[1]), col(wb_nsa.shape[1])],
        out_specs=pl.BlockSpec((tm, tn), lambda i, j: (i, j)),
        out_shape=jax.ShapeDtypeStruct((m, d), BF16),
        compiler_params=_params(("parallel", "arbitrary")),
        name="gate_merge",
    )(h, o_fox, o_diff, o_nsa, w_gate, w_gate, w_gate, wb_fox, wb_diff, wb_nsa)


def _wout_kernel(a_ref, w_ref, x_ref, gp_ref, gn_ref, xo_ref, ho_ref):
    y = _dot(a_ref[...], w_ref[...])
    x_new = x_ref[...] + _rms(y, gp_ref[...])
    xo_ref[...] = x_new
    ho_ref[...] = _rms(x_new, gn_ref[...]).astype(BF16)


def _wout(a, w, x, g_post, g_next, l, name):
    m, d = x.shape
    k = a.shape[1]
    tm = _pick(m, (256, 128))
    row = lambda width: pl.BlockSpec((tm, width), lambda i: (i, 0))
    vec = lambda: pl.BlockSpec((1, d), lambda i: (0, 0))
    return pl.pallas_call(
        _wout_kernel,
        grid=(m // tm,),
        in_specs=[row(k), pl.BlockSpec((None, k, d), lambda i: (l, 0, 0), pipeline_mode=pl.Buffered(1)),
                  row(d), vec(), vec()],
        out_specs=[row(d), row(d)],
        out_shape=[jax.ShapeDtypeStruct((m, d), F32), jax.ShapeDtypeStruct((m, d), BF16)],
        compiler_params=_params(("parallel",)),
        name=name,
    )(a, w, x, g_post, g_next)


def _ffn_up_kernel(h_ref, wg_ref, wu_ref, o_ref, wgb_ref, wub_ref):
    @pl.when(pl.program_id(1) == 0)
    def _():
        wgb_ref[...] = wg_ref[...].astype(BF16)
        wub_ref[...] = wu_ref[...].astype(BF16)

    h = h_ref[...]
    gate = _dot(h, wgb_ref[...])
    o_ref[...] = (gate * jax.nn.sigmoid(gate) * _dot(h, wub_ref[...])).astype(BF16)


def _ffn_up(h, w_up, l):
    m, d = h.shape
    dff = w_up.shape[2] // 2
    tm = _pick(m, (1024, 512, 256))
    tn = _pick(dff, (512, 256, 128))
    nj = dff // tn
    return pl.pallas_call(
        _ffn_up_kernel,
        grid=(nj, m // tm),
        in_specs=[pl.BlockSpec((tm, d), lambda j, i: (i, 0)),
                  pl.BlockSpec((None, d, tn), lambda j, i: (l, 0, j)),
                  pl.BlockSpec((None, d, tn), lambda j, i: (l, 0, nj + j))],
        out_specs=pl.BlockSpec((tm, tn), lambda j, i: (i, j)),
        out_shape=jax.ShapeDtypeStruct((m, dff), BF16),
        scratch_shapes=[pltpu.VMEM((d, tn), BF16), pltpu.VMEM((d, tn), BF16)],
        compiler_params=_params(("parallel", "arbitrary")),
        name="ffn_up",
    )(h, w_up, w_up)


def _pos_columns(pos):
    lane = jnp.arange(LANES)[None, :]
    hi = (lane >= AUG_HI) & (lane < AUG_HI + AUG_TERMS)
    lo = (lane >= AUG_LO) & (lane < AUG_LO + AUG_TERMS)
    return jnp.where(hi, (pos // LANES)[:, None], jnp.where(lo, (pos % LANES)[:, None], 0))


def _key_aug_table(seq):
    j = jnp.arange(seq)
    lane = jnp.arange(LANES)[None, :]
    onehot = (lane == (j // SEL_BLOCK)[:, None]) & (lane < SEL_BLOCK)
    return (_pos_columns(j) + onehot.astype(jnp.int32)).astype(BF16)


def _cmp_aug_table(nrow):
    return _pos_columns(jnp.arange(nrow) * CMP_STRIDE + CMP_BLOCK - 1).astype(BF16)


def _overlap_table_t(nrow, seq):
    start = jnp.arange(nrow)[None, :] * CMP_STRIDE
    blk = jnp.arange(LANES)[:, None]
    sel = blk * SEL_BLOCK
    ov = ((start < sel + SEL_BLOCK) & (start + CMP_BLOCK - 1 >= sel) & (blk < seq // SEL_BLOCK)
          & (jnp.arange(nrow)[None, :] < (seq - CMP_BLOCK) // CMP_STRIDE + 1))
    return ov.astype(BF16)


def _query_aug_rows(n_heads):
    slopes = 2.0 ** (-8.0 * jnp.arange(1, n_heads + 1, dtype=F32) / n_heads)
    terms = [t.astype(F32) for t in _split3(jnp.float32(LOG2E))]
    lane = jnp.arange(LANES)[None, :]
    out = jnp.where(lane == AUG_PAD, 1.0, jnp.zeros((n_heads, LANES), F32))
    for i, t in enumerate(terms):
        out = jnp.where(lane == AUG_HI + i, slopes[:, None] * t * LANES, out)
        out = jnp.where(lane == AUG_LO + i, slopes[:, None] * t, out)
    return out


def kernel(x, w_in, fox_forget_bias, diff_lambda, diff_subln, nsa_cmp_pos, nsa_cmp_w1, nsa_cmp_w2,
           w_branch_fox, w_branch_diff, w_branch_nsa, w_gate, w_out, norm_gains, w_ffn_up, w_ffn_down):
    batch, seq, d = x.shape
    depth = w_in.shape[0]
    m = batch * seq
    fw, dw, nw, kvw = FOX_HEADS * HEAD_DIM, DIFF_HEADS * HEAD_DIM, NSA_HEADS * HEAD_DIM, NSA_KV_HEADS * HEAD_DIM
    ff0 = 3 * fw
    dq0 = ff0 + FOX_HEADS
    ng0 = dq0 + 3 * dw + nw + 6 * kvw
    n_main = ng0 - FOX_HEADS
    assert n_main == N_SLABS * LANES and w_in.shape[2] == ng0 + 3 * NSA_HEADS

    w_main = jnp.concatenate([w_in[:, :, :ff0], w_in[:, :, dq0:ng0]], axis=2).astype(BF16)
    w_misc = jnp.concatenate([w_in[:, :, ff0:dq0], w_in[:, :, ng0:],
                              jnp.zeros((depth, d, LANES - FOX_HEADS - 3 * NSA_HEADS), F32)], axis=2).astype(BF16)
    colscale = jnp.ones((n_main,), F32)
    colscale = colscale.at[SLAB_FQ * LANES:SLAB_FK * LANES].set(HEAD_DIM ** -0.5 * LOG2E)
    colscale = colscale.at[SLAB_DQ * LANES:SLAB_DK * LANES].set(DIFF_HALF ** -0.5 * LOG2E)
    colscale = colscale.at[SLAB_NQ * LANES:SLAB_NKC * LANES].set(HEAD_DIM ** -0.5 * LOG2E)
    colscale = colscale[None, :]
    fbias = jnp.pad(fox_forget_bias.astype(F32), ((0, 0), (0, LANES - FOX_HEADS)))[:, None, :]
    half = CMP_STRIDE * HEAD_DIM
    pos2 = nsa_cmp_pos.astype(F32).reshape(depth, 2, 2, half)
    w1 = nsa_cmp_w1.astype(BF16)
    w2 = nsa_cmp_w2.astype(BF16)
    wbf, wbd, wbn = w_branch_fox.astype(BF16), w_branch_diff.astype(BF16), w_branch_nsa.astype(BF16)
    wg, wo = w_gate.astype(BF16), w_out.astype(BF16)
    wup, wdn = w_ffn_up.astype(F32), w_ffn_down.astype(BF16)
    gains = norm_gains.astype(F32)

    nrow = seq // CMP_STRIDE
    kx = _key_aug_table(seq)
    cx_cmp = _cmp_aug_table(nrow)
    ovt = _overlap_table_t(nrow, seq)
    qx_diff = _query_aug_rows(DIFF_HEADS)[:, None, :]
    qx_nsa = _query_aug_rows(NSA_HEADS).reshape(NSA_KV_HEADS, NSA_REP, LANES)

    xf = x.reshape(m, d).astype(F32)
    h = _norm(xf, gains[0, 0][None, :])
    for l in range(depth):
        lam_init = 0.8 - 0.6 * math.exp(-0.3 * l)
        slabs, misc = _inproj(h, w_main, colscale, w_misc, l)
        cx_fox = _logf(misc, fbias[l], batch, seq)
        o_fox = _fox(slabs, cx_fox, batch, seq)
        o_diff = _diff(slabs, kx, qx_diff, diff_lambda[l].astype(F32), diff_subln[l].astype(F32)[None, :],
                       batch, seq, lam_init)
        ykv = slabs[SLAB_NKC:SLAB_NKC + 4].reshape(4, batch, nrow, half)
        kvc = _compress(ykv, pos2, w1, w2, batch, l)
        o_nsa = _nsa(slabs, kvc, kx, cx_cmp, ovt, qx_nsa, misc, batch, seq)
        merged = _merge(h, o_fox, o_diff, o_nsa, wg, wbf, wbd, wbn, l)
        xf, h2 = _wout(merged, wo, xf, gains[l, 1][None, :], gains[l, 2][None, :], l, "out_proj")
        act = _ffn_up(h2, wup, l)
        g_next = gains[min(l + 1, depth - 1), 0][None, :]
        xf, h = _wout(act, wdn, xf, gains[l, 3][None, :], g_next, l, "ffn_down")
    return xf.reshape(batch, seq, d).astype(x.dtype)
```

```python
import functools
import math

import jax
import jax.numpy as jnp
from jax import lax
from jax.experimental import pallas as pl
from jax.experimental.pallas import tpu as pltpu

F32 = jnp.float32
BF16 = jnp.bfloat16

HEAD_DIM = 128
FOX_HEADS = 4
DIFF_HEADS = 4
DIFF_HALF = HEAD_DIM // 2
NSA_HEADS = 8
NSA_KV_HEADS = 2
NSA_REP = NSA_HEADS // NSA_KV_HEADS
CMP_BLOCK = 32
CMP_STRIDE = 16
CMP_HIDDEN = 256
SEL_BLOCK = 64
SEL_TOPK = 16
WINDOW = 512
N_BRANCHES = 3
EPS = 1e-6
NEG_INF = -1e30
FORCE_SCORE = 1e4
SEL_MASK = -32768.0
LOG2E = math.log2(math.e)
LANES = 128

SLAB_FQ, SLAB_FK, SLAB_FV = 0, 4, 8
SLAB_DQ, SLAB_DK, SLAB_DV = 12, 16, 20
SLAB_NQ = 24
SLAB_NKC, SLAB_NVC, SLAB_NKS, SLAB_NVS, SLAB_NKW, SLAB_NVW = 32, 34, 36, 38, 40, 42
N_SLABS = 44
MISC_FF = 0
MISC_NG = 4
AUG_HI = 64
AUG_LO = 67
AUG_TERMS = 3
AUG_PAD = 70

VMEM_LIMIT = 56 * 1024 * 1024


def _pick(n, prefs):
    for p in prefs:
        if p <= n and n % p == 0:
            return p
    return n


def _params(sem):
    return pltpu.CompilerParams(dimension_semantics=sem, vmem_limit_bytes=VMEM_LIMIT)


def _rms(y, g):
    return y * lax.rsqrt(jnp.mean(y * y, axis=-1, keepdims=True) + EPS) * g


def _dot(a, b):
    return jnp.dot(a, b, preferred_element_type=F32)


def _dot_nt(a, b):
    return lax.dot_general(a, b, (((1,), (1,)), ((), ())), preferred_element_type=F32)


def _split3(x):
    hi = x.astype(BF16)
    r = x - hi.astype(F32)
    mid = r.astype(BF16)
    lo = (r - mid.astype(F32)).astype(BF16)
    return hi, mid, lo


def _norm_kernel(x_ref, g_ref, h_ref):
    h_ref[...] = _rms(x_ref[...], g_ref[...]).astype(BF16)


def _norm(x, g):
    m, d = x.shape
    tm = _pick(m, (512, 256, 128))
    return pl.pallas_call(
        _norm_kernel,
        grid=(m // tm,),
        in_specs=[pl.BlockSpec((tm, d), lambda i: (i, 0)), pl.BlockSpec((1, d), lambda i: (0, 0))],
        out_specs=pl.BlockSpec((tm, d), lambda i: (i, 0)),
        out_shape=jax.ShapeDtypeStruct((m, d), BF16),
        compiler_params=_params(("parallel",)),
        name="norm_in",
    )(x, g)


def _inproj_kernel(h_ref, w_ref, cs_ref, wm_ref, o_ref, misc_ref):
    h = h_ref[...]
    acc = _dot(h, w_ref[...]) * cs_ref[...]
    for s in range(o_ref.shape[0]):
        o_ref[s] = acc[:, s * LANES:(s + 1) * LANES].astype(BF16)

    @pl.when(pl.program_id(1) == 0)
    def _():
        misc_ref[...] = _dot(h, wm_ref[...])


def _inproj(h, w_main, colscale, w_misc, l):
    m, d = h.shape
    n = w_main.shape[2]
    tm = _pick(m, (1024, 512, 256))
    tn = _pick(n, (11 * LANES, 4 * LANES))
    return pl.pallas_call(
        _inproj_kernel,
        grid=(m // tm, n // tn),
        in_specs=[pl.BlockSpec((tm, d), lambda i, j: (i, 0)),
                  pl.BlockSpec((None, d, tn), lambda i, j: (l, 0, j)),
                  pl.BlockSpec((1, tn), lambda i, j: (0, j)),
                  pl.BlockSpec((None, d, LANES), lambda i, j: (l, 0, 0))],
        out_specs=[pl.BlockSpec((tn // LANES, tm, LANES), lambda i, j: (j, i, 0)),
                   pl.BlockSpec((tm, LANES), lambda i, j: (i, 0))],
        out_shape=[jax.ShapeDtypeStruct((n // LANES, m, LANES), BF16),
                   jax.ShapeDtypeStruct((m, LANES), F32)],
        compiler_params=_params(("parallel", "arbitrary")),
        name="inproj",
    )(h, w_main, colscale, w_misc)


def _logf_kernel(misc_ref, bias_ref, o_ref, *, tc):
    s = misc_ref.shape[0]
    row = lax.broadcasted_iota(jnp.int32, (tc, tc), 0)
    col = lax.broadcasted_iota(jnp.int32, (tc, tc), 1)
    tri = jnp.where(col <= row, 1.0, 0.0).astype(BF16)
    lane = lax.broadcasted_iota(jnp.int32, (tc, LANES), 1)

    def chunk(c, carry):
        r0 = pl.multiple_of(c * tc, tc)
        z = misc_ref[pl.ds(r0, tc), :] + bias_ref[...]
        lf = jnp.minimum(z, 0.0) - jnp.log1p(jnp.exp(-jnp.abs(z)))
        hi, mid, lo = _split3(lf)
        cum = _dot(tri, hi) + _dot(tri, mid) + _dot(tri, lo) + carry
        for hd in range(FOX_HEADS):
            c2 = jnp.broadcast_to(cum[:, hd:hd + 1], (tc, LANES)) * LOG2E
            c_hi = c2.astype(BF16).astype(F32)
            c_mid = (c2 - c_hi).astype(BF16).astype(F32)
            aug = jnp.where(lane == 0, c_hi, jnp.where(lane == 1, c_mid,
                                                      jnp.where(lane == 2, c2 - c_hi - c_mid, 0.0)))
            o_ref[hd, pl.ds(r0, tc), :] = aug.astype(BF16)
        return cum[tc - 1:tc, :]

    lax.fori_loop(0, s // tc, chunk, jnp.zeros((1, LANES), F32))


def _logf(misc, bias_row, batch, seq):
    tc = _pick(seq, (256, 128))
    return pl.pallas_call(
        functools.partial(_logf_kernel, tc=tc),
        grid=(batch,),
        in_specs=[pl.BlockSpec((seq, LANES), lambda b: (b, 0)), pl.BlockSpec((1, LANES), lambda b: (0, 0))],
        out_specs=pl.BlockSpec((FOX_HEADS, seq, LANES), lambda b: (0, b, 0)),
        out_shape=jax.ShapeDtypeStruct((FOX_HEADS, batch * seq, LANES), BF16),
        compiler_params=_params(("parallel",)),
        name="fox_logf",
    )(misc, bias_row)


def _flash_scratch(tk, mcols):
    return [pltpu.VMEM((tk, mcols), F32), pltpu.VMEM((1, mcols), F32), pltpu.VMEM((1, mcols), F32),
            pltpu.VMEM((LANES, mcols), F32)]


def _causal_bias(tk, mcols, tq, offset):
    rk = lax.broadcasted_iota(jnp.int32, (tk, mcols), 0)
    rq = lax.broadcasted_iota(jnp.int32, (tk, mcols), 1) & (tq - 1)
    return jnp.where(rk <= rq + offset, 0.0, NEG_INF)


def _flash_tiles(qa, ka_ref, vt_ref, n_full, diag_bias_ref, tk, scratch, next_qa=None):
    s_ref, m_ref, l_ref, acc_ref = scratch

    def logits(j, queries=qa):
        k0 = pl.multiple_of(j * tk, tk)
        return _dot_nt(ka_ref[pl.ds(k0, tk), :], queries)

    def step(s, j):
        k0 = pl.multiple_of(j * tk, tk)
        m_prev = m_ref[...]
        m_new = jnp.maximum(m_prev, jnp.max(s, axis=0, keepdims=True))
        alpha = jnp.exp2(m_prev - m_new)
        p = jnp.exp2(s - m_new)
        l_ref[...] = alpha * l_ref[...] + jnp.sum(p, axis=0, keepdims=True)
        acc_ref[...] = alpha * acc_ref[...] + _dot(vt_ref[:, pl.ds(k0, tk)], p.astype(BF16))
        m_ref[...] = m_new

    m_ref[...] = jnp.full(m_ref.shape, NEG_INF, F32)
    l_ref[...] = jnp.zeros(l_ref.shape, F32)
    acc_ref[...] = jnp.zeros(acc_ref.shape, F32)
    if next_qa is None:
        s_ref[...] = logits(0)

    def full(j, c):
        s_next = logits(j + 1)
        step(s_ref[...], j)
        s_ref[...] = s_next
        return c

    lax.fori_loop(0, n_full, full, 0)
    if next_qa is None:
        step(s_ref[...] + diag_bias_ref[...], n_full)
    else:
        s_next = logits(0, next_qa)
        step(s_ref[...] + diag_bias_ref[...], n_full)
        s_ref[...] = s_next
    return acc_ref[...] * (1.0 / l_ref[...])


def _stage_kv(ka_ref, vt_ref, k_ref, kx_ref, v_ref, chunk, pad=0):
    def body(c, carry):
        r0 = pl.multiple_of(c * chunk, chunk)
        ka_ref[pl.ds(pad + r0, chunk), 0:LANES] = k_ref[0, pl.ds(r0, chunk), :]
        ka_ref[pl.ds(pad + r0, chunk), LANES:2 * LANES] = kx_ref[pl.ds(r0, chunk), :]
        vt_ref[:, pl.ds(pad + r0, chunk)] = v_ref[0, pl.ds(r0, chunk), :].astype(F32).T.astype(BF16)
        return carry

    lax.fori_loop(0, v_ref.shape[1] // chunk, body, 0)


def _fox_kernel(q_ref, k_ref, v_ref, cx_ref, o_ref, ka_ref, vt_ref, tri_ref, *flash, t):
    _stage_kv(ka_ref, vt_ref, k_ref, cx_ref.at[0], v_ref, t)
    tri_ref[...] = _causal_bias(t, t, t, 0)
    lane = lax.broadcasted_iota(jnp.int32, (t, LANES), 1)
    qx = jnp.where(lane < 3, -1.0, 0.0).astype(BF16)

    nq = q_ref.shape[1] // t

    def queries(qi):
        q0 = pl.multiple_of(qi * t, t)
        return jnp.concatenate([q_ref[0, pl.ds(q0, t), :], qx], axis=1)

    flash[0][...] = _dot_nt(ka_ref[0:t, :], queries(0))

    def qstep(qi, c):
        o = _flash_tiles(queries(qi), ka_ref, vt_ref, qi, tri_ref, t, flash,
                         next_qa=queries(jnp.minimum(qi + 1, nq - 1)))
        o_ref[pl.ds(pl.multiple_of(qi * t, t), t), :] = o.T.astype(BF16)
        return c

    lax.fori_loop(0, nq, qstep, 0)


def _fox(slabs, cx, batch, seq):
    t = _pick(seq, (512, 256, 128))
    m = batch * seq
    head = lambda base: pl.BlockSpec((1, seq, LANES), lambda b, h: (base + h, b, 0))
    return pl.pallas_call(
        functools.partial(_fox_kernel, t=t),
        grid=(batch, FOX_HEADS),
        in_specs=[head(SLAB_FQ), head(SLAB_FK), head(SLAB_FV), head(0)],
        out_specs=pl.BlockSpec((seq, LANES), lambda b, h: (b, h)),
        out_shape=jax.ShapeDtypeStruct((m, FOX_HEADS * HEAD_DIM), BF16),
        scratch_shapes=[pltpu.VMEM((seq, 2 * LANES), BF16), pltpu.VMEM((LANES, seq), BF16),
                        pltpu.VMEM((t, t), F32)] + _flash_scratch(t, t),
        compiler_params=_params(("parallel", "parallel")),
        name="fox_attn",
    )(slabs, slabs, slabs, cx)


def _diff_kernel(q_ref, k_ref, v_ref, kx_ref, qx_ref, lam_ref, sub_ref, o_ref, ka_ref, vt_ref, tri_ref, *flash,
                 t, lam_init):
    _stage_kv(ka_ref, vt_ref, k_ref, kx_ref, v_ref, t)
    tri_ref[...] = _causal_bias(t, 2 * t, t, 0)
    lane = lax.broadcasted_iota(jnp.int32, (t, LANES), 1)
    qx = jnp.broadcast_to(qx_ref[0], (t, LANES)).astype(BF16)
    lv = lam_ref[...]
    lam = (jnp.exp(jnp.sum(lv[0:1] * lv[1:2], axis=-1, keepdims=True))
           - jnp.exp(jnp.sum(lv[2:3] * lv[3:4], axis=-1, keepdims=True)) + lam_init)

    nq = q_ref.shape[1] // t

    def queries(qi):
        q = q_ref[0, pl.ds(pl.multiple_of(qi * t, t), t), :].astype(F32)
        return jnp.concatenate([
            jnp.concatenate([jnp.where(lane < DIFF_HALF, q, 0.0).astype(BF16), qx], axis=1),
            jnp.concatenate([jnp.where(lane >= DIFF_HALF, q, 0.0).astype(BF16), qx], axis=1)], axis=0)

    flash[0][...] = _dot_nt(ka_ref[0:t, :], queries(0))

    def qstep(qi, c):
        o = _flash_tiles(queries(qi), ka_ref, vt_ref, qi, tri_ref, t, flash,
                         next_qa=queries(jnp.minimum(qi + 1, nq - 1)))
        o = (o[:, :t] - lam * o[:, t:]).T
        o_ref[pl.ds(pl.multiple_of(qi * t, t), t), :] = (_rms(o, sub_ref[...]) * (1.0 - lam_init)).astype(BF16)
        return c

    lax.fori_loop(0, nq, qstep, 0)


def _diff(slabs, kx, qx_diff, lam_vec, subln, batch, seq, lam_init):
    t = _pick(seq, (512, 256, 128))
    m = batch * seq
    head = lambda base: pl.BlockSpec((1, seq, LANES), lambda b, h: (base + h, b, 0))
    return pl.pallas_call(
        functools.partial(_diff_kernel, t=t, lam_init=lam_init),
        grid=(batch, DIFF_HEADS),
        in_specs=[head(SLAB_DQ), head(SLAB_DK), head(SLAB_DV),
                  pl.BlockSpec((seq, LANES), lambda b, h: (0, 0)),
                  pl.BlockSpec((1, 1, LANES), lambda b, h: (h, 0, 0)),
                  pl.BlockSpec((4, DIFF_HALF), lambda b, h: (0, 0)),
                  pl.BlockSpec((1, LANES), lambda b, h: (0, 0))],
        out_specs=pl.BlockSpec((seq, LANES), lambda b, h: (b, h)),
        out_shape=jax.ShapeDtypeStruct((m, DIFF_HEADS * HEAD_DIM), BF16),
        scratch_shapes=[pltpu.VMEM((seq, 2 * LANES), BF16), pltpu.VMEM((LANES, seq), BF16),
                        pltpu.VMEM((t, 2 * t), F32)] + _flash_scratch(t, 2 * t),
        compiler_params=_params(("parallel", "parallel")),
        name="diff_attn",
    )(slabs, slabs, slabs, kx, qx_diff, lam_vec, subln)


def _compress_kernel(y_ref, pos_ref, w1_ref, w2_ref, o_ref):
    half = y_ref.shape[3]
    y = y_ref[0, 0].astype(F32)
    top = (y + pos_ref[0, 0:1, :]).astype(BF16)
    bot = (y + pos_ref[0, 1:2, :]).astype(BF16)
    a = _dot(top, w1_ref[0, 0:half, :])
    b = _dot(bot, w1_ref[0, half:2 * half, :])
    nrow = a.shape[0]
    hid = a + pltpu.roll(b, nrow - 1, 0)
    hid = hid * jax.nn.sigmoid(hid)
    o_ref[0, 0] = _dot(hid.astype(BF16), w2_ref[0]).astype(BF16)


def _compress(ykv, pos2, w1, w2, batch, l):
    nrow, half = ykv.shape[2], ykv.shape[3]
    return pl.pallas_call(
        _compress_kernel,
        grid=(4, batch),
        in_specs=[pl.BlockSpec((1, 1, nrow, half), lambda s, b: (s, b, 0, 0)),
                  pl.BlockSpec((None, 1, 2, half), lambda s, b: (l, s // 2, 0, 0)),
                  pl.BlockSpec((None, 1, 2 * half, CMP_HIDDEN), lambda s, b: (l, s // 2, 0, 0)),
                  pl.BlockSpec((None, 1, CMP_HIDDEN, HEAD_DIM), lambda s, b: (l, s // 2, 0, 0))],
        out_specs=pl.BlockSpec((1, 1, nrow, HEAD_DIM), lambda s, b: (s, b, 0, 0)),
        out_shape=jax.ShapeDtypeStruct((4, batch, nrow, HEAD_DIM), BF16),
        compiler_params=_params(("parallel", "parallel")),
        name="nsa_compress",
    )(ykv, pos2, w1, w2)


def _topk_bias(cand, topk):
    nsel, tq = cand.shape
    sub = 8
    blocks = [cand[b * sub:(b + 1) * sub] for b in range(nsel // sub)]
    jidx = lax.broadcasted_iota(jnp.int32, (sub, tq), 0)
    ranks = [jnp.zeros((sub, tq), F32) for _ in blocks]
    for k in range(nsel):
        rk = cand[k:k + 1, :]
        for b, cb in enumerate(blocks):
            if b * sub > k:
                beats = rk >= cb
            elif b * sub + sub - 1 < k:
                beats = rk > cb
            else:
                beats = (rk > cb) | ((rk == cb) & (jidx + b * sub > k))
            ranks[b] = ranks[b] + jnp.where(beats, 1.0, 0.0)
    rank = jnp.concatenate(ranks, axis=0)
    return jnp.where(rank < topk, 0.0, SEL_MASK)


def _nsa_kernel(q_ref, kc_ref, vc_ref, ks_ref, vs_ref, kw_ref, vw_ref, kx_ref, cx_ref, ovt_ref, qx_ref,
                misc_ref, o_ref, ksa_ref, kwa_ref, vst_ref, vwt_ref, vct_ref, gt_ref, dbias_ref, wbias_ref,
                comb_ref, *flash, tq, tk):
    g = pl.program_id(1)
    mcols = NSA_REP * tq
    nc = kc_ref.shape[2]
    seq = kx_ref.shape[0]
    nsel = seq // SEL_BLOCK
    span = WINDOW + tq

    _stage_kv(ksa_ref, vst_ref, ks_ref, kx_ref, vs_ref, tk)
    _stage_kv(kwa_ref, vwt_ref, kw_ref, kx_ref, vw_ref, tk, pad=WINDOW)
    pad_lane = lax.broadcasted_iota(jnp.int32, (WINDOW, 2 * LANES), 1)
    kwa_ref[0:WINDOW, :] = jnp.where(pad_lane == LANES + AUG_PAD, SEL_MASK, 0.0).astype(BF16)
    vwt_ref[:, 0:WINDOW] = jnp.zeros((LANES, WINDOW), BF16)
    vct_ref[...] = vc_ref[0, 0].astype(F32).T.astype(BF16)
    for o in range(tk // tq):
        dbias_ref[o] = _causal_bias(tk, mcols, tq, o * tq)
    rk = lax.broadcasted_iota(jnp.int32, (span, mcols), 0)
    rq = lax.broadcasted_iota(jnp.int32, (span, mcols), 1) & (tq - 1)
    wbias_ref[...] = jnp.where((rk > rq) & (rk <= rq + WINDOW), 0.0, NEG_INF)
    qx = jnp.concatenate([jnp.broadcast_to(qx_ref[0, r:r + 1, :], (tq, LANES)) for r in range(NSA_REP)], axis=0)
    kca = jnp.concatenate([kc_ref[0, 0], cx_ref[...]], axis=1)

    def qstep(qi, carry):
        _nsa_query_tile(qi, g, qx, kca, q_ref, ovt_ref, misc_ref, o_ref, ksa_ref, kwa_ref, vst_ref, vwt_ref,
                        vct_ref, gt_ref, dbias_ref, wbias_ref, comb_ref, flash, tq=tq, tk=tk, nc=nc, nsel=nsel)
        return carry

    lax.fori_loop(0, seq // tq, qstep, 0)


def _nsa_query_tile(qi, g, qx, kca, q_ref, ovt_ref, misc_ref, o_ref, ksa_ref, kwa_ref, vst_ref, vwt_ref, vct_ref,
                    gt_ref, dbias_ref, wbias_ref, comb_ref, flash, *, tq, tk, nc, nsel):
    mcols = NSA_REP * tq
    span = WINDOW + tq
    q0 = pl.multiple_of(qi * tq, tq)
    q4 = q_ref[:, pl.ds(q0, tq), :].reshape(mcols, LANES)
    qa = jnp.concatenate([q4, qx.astype(BF16)], axis=1)
    col = lax.broadcasted_iota(jnp.int32, (1, mcols), 1)
    col_pos = q0 + (col & (tq - 1))

    sc = _dot_nt(kca, qa)
    cend = lax.broadcasted_iota(jnp.int32, (nc, mcols), 0) * CMP_STRIDE + (CMP_BLOCK - 1)
    sc = jnp.where(cend <= col_pos, sc, NEG_INF)
    e = jnp.exp2(sc - jnp.max(sc, axis=0, keepdims=True))
    inv = jnp.where(col_pos >= CMP_BLOCK - 1, 1.0 / jnp.sum(e, axis=0, keepdims=True), 0.0)
    p = e * inv
    o_cmp = _dot(vct_ref[...], p.astype(BF16))

    psum = p[:, 0:tq]
    for r in range(1, NSA_REP):
        psum = psum + p[:, r * tq:(r + 1) * tq]
    p_hi = psum.astype(BF16)
    p_lo = (psum - p_hi.astype(F32)).astype(BF16)
    imp = _dot(ovt_ref[...], p_hi) + _dot(ovt_ref[...], p_lo)
    qpos = q0 + lax.broadcasted_iota(jnp.int32, (LANES, tq), 1)
    blk = lax.broadcasted_iota(jnp.int32, (LANES, tq), 0)
    cur = jnp.right_shift(qpos, SEL_BLOCK.bit_length() - 1)
    forced = (blk == 0) | (blk == cur) | (blk == cur - 1)
    imp = jnp.where(blk <= cur, jnp.where(forced, FORCE_SCORE, imp), -1.0)
    bias_t = _topk_bias(imp[0:nsel], min(SEL_TOPK, nsel))
    if nsel < LANES:
        bias_t = jnp.concatenate([bias_t, jnp.zeros((LANES - nsel, tq), F32)], axis=0)
    selbias = jnp.concatenate([bias_t.T] * NSA_REP, axis=0)
    lane4 = lax.broadcasted_iota(jnp.int32, (mcols, LANES), 1)
    qa_sel = jnp.concatenate([q4, jnp.where(lane4 < SEL_BLOCK, selbias, qx).astype(BF16)], axis=1)

    sw = _dot_nt(kwa_ref[pl.ds(q0, span), :], qa) + wbias_ref[...]
    ew = jnp.exp2(sw - jnp.max(sw, axis=0, keepdims=True))
    o_win = _dot(vwt_ref[:, pl.ds(q0, span)], ew.astype(BF16)) * (1.0 / jnp.sum(ew, axis=0, keepdims=True))

    gt_ref[...] = jax.nn.sigmoid(misc_ref[pl.ds(q0, tq), :]).T
    gate = lambda r, i: gt_ref[pl.ds(MISC_NG + 3 * (NSA_REP * g + r) + i, 1), :]
    for r in range(NSA_REP):
        sl = slice(r * tq, (r + 1) * tq)
        comb_ref[:, sl] = gate(r, 0) * o_cmp[:, sl] + gate(r, 2) * o_win[:, sl]

    o_sel = _flash_tiles(qa_sel, ksa_ref, vst_ref, q0 // tk, dbias_ref.at[qi & (tk // tq - 1)], tk, flash)
    for r in range(NSA_REP):
        sl = slice(r * tq, (r + 1) * tq)
        out = comb_ref[:, sl] + gate(r, 1) * o_sel[:, sl]
        o_ref[pl.ds(q0, tq), r * LANES:(r + 1) * LANES] = out.T.astype(BF16)


def _nsa(slabs, kvc, kx, cx_cmp, ovt, qx_nsa, misc, batch, seq):
    tq = _pick(seq, (256, 128))
    tk = _pick(seq, (512, 256, 128))
    assert tq & (tq - 1) == 0 and tk % tq == 0 and seq >= WINDOW + tq
    m = batch * seq
    mcols = NSA_REP * tq
    nc = kvc.shape[2]
    kv_spec = lambda base: pl.BlockSpec((1, seq, LANES), lambda b, g: (base + g, b, 0))
    const = lambda shape: pl.BlockSpec(shape, lambda b, g: (0,) * len(shape), pipeline_mode=pl.Buffered(1))
    return pl.pallas_call(
        functools.partial(_nsa_kernel, tq=tq, tk=tk),
        grid=(batch, NSA_KV_HEADS),
        in_specs=[pl.BlockSpec((NSA_REP, seq, LANES), lambda b, g: (SLAB_NQ // NSA_REP + g, b, 0)),
                  pl.BlockSpec((1, 1, nc, LANES), lambda b, g: (g, b, 0, 0)),
                  pl.BlockSpec((1, 1, nc, LANES), lambda b, g: (2 + g, b, 0, 0)),
                  kv_spec(SLAB_NKS), kv_spec(SLAB_NVS), kv_spec(SLAB_NKW), kv_spec(SLAB_NVW),
                  const((seq, LANES)), const((nc, LANES)), const((LANES, nc)),
                  pl.BlockSpec((1, NSA_REP, LANES), lambda b, g: (g, 0, 0)),
                  pl.BlockSpec((seq, LANES), lambda b, g: (b, 0))],
        out_specs=pl.BlockSpec((seq, NSA_REP * LANES), lambda b, g: (b, g)),
        out_shape=jax.ShapeDtypeStruct((m, NSA_HEADS * HEAD_DIM), BF16),
        scratch_shapes=[pltpu.VMEM((seq, 2 * LANES), BF16), pltpu.VMEM((WINDOW + seq, 2 * LANES), BF16),
                        pltpu.VMEM((LANES, seq), BF16), pltpu.VMEM((LANES, WINDOW + seq), BF16),
                        pltpu.VMEM((LANES, nc), BF16), pltpu.VMEM((LANES, tq), F32),
                        pltpu.VMEM((tk // tq, tk, mcols), F32), pltpu.VMEM((WINDOW + tq, mcols), F32),
                        pltpu.VMEM((LANES, mcols), F32)] + _flash_scratch(tk, mcols),
        compiler_params=_params(("parallel", "parallel")),
        name="nsa_attn",
    )(slabs, kvc, kvc, slabs, slabs, slabs, slabs, kx, cx_cmp, ovt, qx_nsa, misc)


def _merge_kernel(h_ref, of_ref, od_ref, on_ref, wg0_ref, wg1_ref, wg2_ref, wf_ref, wd_ref, wn_ref, o_ref):
    h = h_ref[...]
    acc = jax.nn.sigmoid(_dot(h, wg0_ref[...])) * _dot(of_ref[...], wf_ref[...])
    acc = acc + jax.nn.sigmoid(_dot(h, wg1_ref[...])) * _dot(od_ref[...], wd_ref[...])
    acc = acc + jax.nn.sigmoid(_dot(h, wg2_ref[...])) * _dot(on_ref[...], wn_ref[...])
    o_ref[...] = acc.astype(BF16)


def _merge(h, o_fox, o_diff, o_nsa, w_gate, wb_fox, wb_diff, wb_nsa, l):
    m, d = h.shape
    tm = _pick(m, (1024, 512, 256))
    tn = _pick(d, (256, 128))
    nj = d // tn
    row = lambda width: pl.BlockSpec((tm, width), lambda i, j: (i, 0))
    gate = lambda t: pl.BlockSpec((None, d, tn), lambda i, j: (l, 0, t * nj + j))
    col = lambda k: pl.BlockSpec((None, k, tn), lambda i, j: (l, 0, j))
    return pl.pallas_call(
        _merge_kernel,
        grid=(m // tm, nj),
        in_specs=[row(d), row(o_fox.shape[1]), row(o_diff.shape[1]), row(o_nsa.shape[1]),
                  gate(0), gate(1), gate(2),
                  col(wb_fox.shape[1]), col(wb_diff.shape[1]), col(wb_nsa.shape[1])],
        out_specs=pl.BlockSpec((tm, tn), lambda i, j: (i, j)),
        out_shape=jax.ShapeDtypeStruct((m, d), BF16),
        compiler_params=_params(("parallel", "arbitrary")),
        name="gate_merge",
    )(h, o_fox, o_diff, o_nsa, w_gate, w_gate, w_gate, wb_fox, wb_diff, wb_nsa)


def _wout_kernel(a_ref, w_ref, x_ref, gp_ref, gn_ref, xo_ref, ho_ref):
    y = _dot(a_ref[...], w_ref[...])
    x_new = x_ref[...] + _rms(y, gp_ref[...])
    xo_ref[...] = x_new
    ho_ref[...] = _rms(x_new, gn_ref[...]).astype(BF16)


def _wout(a, w, x, g_post, g_next, l, name):
    m, d = x.shape
    k = a.shape[1]
    tm = _pick(m, (512, 256, 128)) if k * d * 2 <= 8 * 1024 * 1024 else _pick(m, (256, 128))
    row = lambda width: pl.BlockSpec((tm, width), lambda i: (i, 0))
    vec = lambda: pl.BlockSpec((1, d), lambda i: (0, 0))
    return pl.pallas_call(
        _wout_kernel,
        grid=(m // tm,),
        in_specs=[row(k), pl.BlockSpec((None, k, d), lambda i: (l, 0, 0), pipeline_mode=pl.Buffered(1)),
                  row(d), vec(), vec()],
        out_specs=[row(d), row(d)],
        out_shape=[jax.ShapeDtypeStruct((m, d), F32), jax.ShapeDtypeStruct((m, d), BF16)],
        compiler_params=_params(("parallel",)),
        name=name,
    )(a, w, x, g_post, g_next)


def _ffn_up_kernel(h_ref, wg_ref, wu_ref, o_ref, wgb_ref, wub_ref):
    @pl.when(pl.program_id(1) == 0)
    def _():
        wgb_ref[...] = wg_ref[...].astype(BF16)
        wub_ref[...] = wu_ref[...].astype(BF16)

    h = h_ref[...]
    gate = _dot(h, wgb_ref[...])
    o_ref[...] = (gate * jax.nn.sigmoid(gate) * _dot(h, wub_ref[...])).astype(BF16)


def _ffn_up(h, w_up, l):
    m, d = h.shape
    dff = w_up.shape[2] // 2
    tm = _pick(m, (1024, 512, 256))
    tn = _pick(dff, (512, 256, 128))
    nj = dff // tn
    return pl.pallas_call(
        _ffn_up_kernel,
        grid=(nj, m // tm),
        in_specs=[pl.BlockSpec((tm, d), lambda j, i: (i, 0)),
                  pl.BlockSpec((None, d, tn), lambda j, i: (l, 0, j)),
                  pl.BlockSpec((None, d, tn), lambda j, i: (l, 0, nj + j))],
        out_specs=pl.BlockSpec((tm, tn), lambda j, i: (i, j)),
        out_shape=jax.ShapeDtypeStruct((m, dff), BF16),
        scratch_shapes=[pltpu.VMEM((d, tn), BF16), pltpu.VMEM((d, tn), BF16)],
        compiler_params=_params(("parallel", "arbitrary")),
        name="ffn_up",
    )(h, w_up, w_up)


def _pos_columns(pos):
    lane = jnp.arange(LANES)[None, :]
    hi = (lane >= AUG_HI) & (lane < AUG_HI + AUG_TERMS)
    lo = (lane >= AUG_LO) & (lane < AUG_LO + AUG_TERMS)
    return jnp.where(hi, (pos // LANES)[:, None], jnp.where(lo, (pos % LANES)[:, None], 0))


def _key_aug_table(seq):
    j = jnp.arange(seq)
    lane = jnp.arange(LANES)[None, :]
    onehot = (lane == (j // SEL_BLOCK)[:, None]) & (lane < SEL_BLOCK)
    return (_pos_columns(j) + onehot.astype(jnp.int32)).astype(BF16)


def _cmp_aug_table(nrow):
    return _pos_columns(jnp.arange(nrow) * CMP_STRIDE + CMP_BLOCK - 1).astype(BF16)


def _overlap_table_t(nrow, seq):
    start = jnp.arange(nrow)[None, :] * CMP_STRIDE
    blk = jnp.arange(LANES)[:, None]
    sel = blk * SEL_BLOCK
    ov = ((start < sel + SEL_BLOCK) & (start + CMP_BLOCK - 1 >= sel) & (blk < seq // SEL_BLOCK)
          & (jnp.arange(nrow)[None, :] < (seq - CMP_BLOCK) // CMP_STRIDE + 1))
    return ov.astype(BF16)


def _query_aug_rows(n_heads):
    slopes = 2.0 ** (-8.0 * jnp.arange(1, n_heads + 1, dtype=F32) / n_heads)
    terms = [t.astype(F32) for t in _split3(jnp.float32(LOG2E))]
    lane = jnp.arange(LANES)[None, :]
    out = jnp.where(lane == AUG_PAD, 1.0, jnp.zeros((n_heads, LANES), F32))
    for i, t in enumerate(terms):
        out = jnp.where(lane == AUG_HI + i, slopes[:, None] * t * LANES, out)
        out = jnp.where(lane == AUG_LO + i, slopes[:, None] * t, out)
    return out


def kernel(x, w_in, fox_forget_bias, diff_lambda, diff_subln, nsa_cmp_pos, nsa_cmp_w1, nsa_cmp_w2,
           w_branch_fox, w_branch_diff, w_branch_nsa, w_gate, w_out, norm_gains, w_ffn_up, w_ffn_down):
    batch, seq, d = x.shape
    depth = w_in.shape[0]
    m = batch * seq
    fw, dw, nw, kvw = FOX_HEADS * HEAD_DIM, DIFF_HEADS * HEAD_DIM, NSA_HEADS * HEAD_DIM, NSA_KV_HEADS * HEAD_DIM
    ff0 = 3 * fw
    dq0 = ff0 + FOX_HEADS
    ng0 = dq0 + 3 * dw + nw + 6 * kvw
    n_main = ng0 - FOX_HEADS
    assert n_main == N_SLABS * LANES and w_in.shape[2] == ng0 + 3 * NSA_HEADS

    w_main = jnp.concatenate([w_in[:, :, :ff0], w_in[:, :, dq0:ng0]], axis=2).astype(BF16)
    w_misc = jnp.concatenate([w_in[:, :, ff0:dq0], w_in[:, :, ng0:],
                              jnp.zeros((depth, d, LANES - FOX_HEADS - 3 * NSA_HEADS), F32)], axis=2).astype(BF16)
    colscale = jnp.ones((n_main,), F32)
    colscale = colscale.at[SLAB_FQ * LANES:SLAB_FK * LANES].set(HEAD_DIM ** -0.5 * LOG2E)
    colscale = colscale.at[SLAB_DQ * LANES:SLAB_DK * LANES].set(DIFF_HALF ** -0.5 * LOG2E)
    colscale = colscale.at[SLAB_NQ * LANES:SLAB_NKC * LANES].set(HEAD_DIM ** -0.5 * LOG2E)
    colscale = colscale[None, :]
    fbias = jnp.pad(fox_forget_bias.astype(F32), ((0, 0), (0, LANES - FOX_HEADS)))[:, None, :]
    half = CMP_STRIDE * HEAD_DIM
    pos2 = nsa_cmp_pos.astype(F32).reshape(depth, 2, 2, half)
    w1 = nsa_cmp_w1.astype(BF16)
    w2 = nsa_cmp_w2.astype(BF16)
    wbf, wbd, wbn = w_branch_fox.astype(BF16), w_branch_diff.astype(BF16), w_branch_nsa.astype(BF16)
    wg, wo = w_gate.astype(BF16), w_out.astype(BF16)
    wup, wdn = w_ffn_up.astype(F32), w_ffn_down.astype(BF16)
    gains = norm_gains.astype(F32)

    nrow = seq // CMP_STRIDE
    kx = _key_aug_table(seq)
    cx_cmp = _cmp_aug_table(nrow)
    ovt = _overlap_table_t(nrow, seq)
    qx_diff = _query_aug_rows(DIFF_HEADS)[:, None, :]
    qx_nsa = _query_aug_rows(NSA_HEADS).reshape(NSA_KV_HEADS, NSA_REP, LANES)

    xf = x.reshape(m, d).astype(F32)
    h = _norm(xf, gains[0, 0][None, :])
    for l in range(depth):
        lam_init = 0.8 - 0.6 * math.exp(-0.3 * l)
        slabs, misc = _inproj(h, w_main, colscale, w_misc, l)
        cx_fox = _logf(misc, fbias[l], batch, seq)
        o_fox = _fox(slabs, cx_fox, batch, seq)
        o_diff = _diff(slabs, kx, qx_diff, diff_lambda[l].astype(F32), diff_subln[l].astype(F32)[None, :],
                       batch, seq, lam_init)
        ykv = slabs[SLAB_NKC:SLAB_NKC + 4].reshape(4, batch, nrow, half)
        kvc = _compress(ykv, pos2, w1, w2, batch, l)
        o_nsa = _nsa(slabs, kvc, kx, cx_cmp, ovt, qx_nsa, misc, batch, seq)
        merged = _merge(h, o_fox, o_diff, o_nsa, wg, wbf, wbd, wbn, l)
        xf, h2 = _wout(merged, wo, xf, gains[l, 1][None, :], gains[l, 2][None, :], l, "out_proj")
        act = _ffn_up(h2, wup, l)
        g_next = gains[min(l + 1, depth - 1), 0][None, :]
        xf, h = _wout(act, wdn, xf, gains[l, 3][None, :], g_next, l, "ffn_down")
    return xf.reshape(batch, seq, d).astype(x.dtype)
```

```python
import functools
import math

import jax
import jax.numpy as jnp
from jax import lax
from jax.experimental import pallas as pl
from jax.experimental.pallas import tpu as pltpu

F32 = jnp.float32
BF16 = jnp.bfloat16

HEAD_DIM = 128
FOX_HEADS = 4
DIFF_HEADS = 4
DIFF_HALF = HEAD_DIM // 2
NSA_HEADS = 8
NSA_KV_HEADS = 2
NSA_REP = NSA_HEADS // NSA_KV_HEADS
CMP_BLOCK = 32
CMP_STRIDE = 16
CMP_HIDDEN = 256
SEL_BLOCK = 64
SEL_TOPK = 16
WINDOW = 512
N_BRANCHES = 3
EPS = 1e-6
NEG_INF = -1e30
FORCE_SCORE = 1e4
SEL_MASK = -32768.0
LOG2E = math.log2(math.e)
LANES = 128

SLAB_FQ, SLAB_FK, SLAB_FV = 0, 4, 8
SLAB_DQ, SLAB_DK, SLAB_DV = 12, 16, 20
SLAB_NQ = 24
SLAB_NKC, SLAB_NVC, SLAB_NKS, SLAB_NVS, SLAB_NKW, SLAB_NVW = 32, 34, 36, 38, 40, 42
N_SLABS = 44
MISC_FF = 0
MISC_NG = 4
AUG_HI = 64
AUG_LO = 67
AUG_TERMS = 3
AUG_PAD = 70

VMEM_LIMIT = 56 * 1024 * 1024


def _pick(n, prefs):
    for p in prefs:
        if p <= n and n % p == 0:
            return p
    return n


def _params(sem):
    return pltpu.CompilerParams(dimension_semantics=sem, vmem_limit_bytes=VMEM_LIMIT)


def _rms(y, g):
    return y * lax.rsqrt(jnp.mean(y * y, axis=-1, keepdims=True) + EPS) * g


def _dot(a, b):
    return jnp.dot(a, b, preferred_element_type=F32)


def _t(x):
    return x.astype(F32).T.astype(BF16)


def _split3(x):
    hi = x.astype(BF16)
    r = x - hi.astype(F32)
    mid = r.astype(BF16)
    lo = (r - mid.astype(F32)).astype(BF16)
    return hi, mid, lo


def _norm_kernel(x_ref, g_ref, h_ref):
    h_ref[...] = _rms(x_ref[...], g_ref[...]).astype(BF16)


def _norm(x, g):
    m, d = x.shape
    tm = _pick(m, (512, 256, 128))
    return pl.pallas_call(
        _norm_kernel,
        grid=(m // tm,),
        in_specs=[pl.BlockSpec((tm, d), lambda i: (i, 0)), pl.BlockSpec((1, d), lambda i: (0, 0))],
        out_specs=pl.BlockSpec((tm, d), lambda i: (i, 0)),
        out_shape=jax.ShapeDtypeStruct((m, d), BF16),
        compiler_params=_params(("parallel",)),
        name="norm_in",
    )(x, g)


def _inproj_kernel(h_ref, w_ref, cs_ref, wm_ref, o_ref, misc_ref):
    h = h_ref[...]
    acc = _dot(h, w_ref[...]) * cs_ref[...]
    for s in range(o_ref.shape[0]):
        o_ref[s] = acc[:, s * LANES:(s + 1) * LANES].astype(BF16)

    @pl.when(pl.program_id(1) == 0)
    def _():
        misc_ref[...] = _dot(h, wm_ref[...])


def _inproj(h, w_main, colscale, w_misc, l):
    m, d = h.shape
    n = w_main.shape[2]
    tm = _pick(m, (1024, 512, 256))
    tn = _pick(n, (11 * LANES, 4 * LANES))
    return pl.pallas_call(
        _inproj_kernel,
        grid=(m // tm, n // tn),
        in_specs=[pl.BlockSpec((tm, d), lambda i, j: (i, 0)),
                  pl.BlockSpec((None, d, tn), lambda i, j: (l, 0, j)),
                  pl.BlockSpec((1, tn), lambda i, j: (0, j)),
                  pl.BlockSpec((None, d, LANES), lambda i, j: (l, 0, 0))],
        out_specs=[pl.BlockSpec((tn // LANES, tm, LANES), lambda i, j: (j, i, 0)),
                   pl.BlockSpec((tm, LANES), lambda i, j: (i, 0))],
        out_shape=[jax.ShapeDtypeStruct((n // LANES, m, LANES), BF16),
                   jax.ShapeDtypeStruct((m, LANES), F32)],
        compiler_params=_params(("parallel", "arbitrary")),
        name="inproj",
    )(h, w_main, colscale, w_misc)


def _logf_kernel(misc_ref, bias_ref, o_ref, *, tc):
    s = misc_ref.shape[0]
    row = lax.broadcasted_iota(jnp.int32, (tc, tc), 0)
    col = lax.broadcasted_iota(jnp.int32, (tc, tc), 1)
    tri = jnp.where(col <= row, 1.0, 0.0).astype(BF16)
    lane = lax.broadcasted_iota(jnp.int32, (tc, LANES), 1)

    def chunk(c, carry):
        r0 = pl.multiple_of(c * tc, tc)
        z = misc_ref[pl.ds(r0, tc), :] + bias_ref[...]
        lf = jnp.minimum(z, 0.0) - jnp.log1p(jnp.exp(-jnp.abs(z)))
        hi, mid, lo = _split3(lf)
        cum = _dot(tri, hi) + _dot(tri, mid) + _dot(tri, lo) + carry
        for hd in range(FOX_HEADS):
            c2 = jnp.broadcast_to(cum[:, hd:hd + 1], (tc, LANES)) * LOG2E
            c_hi = c2.astype(BF16).astype(F32)
            c_mid = (c2 - c_hi).astype(BF16).astype(F32)
            aug = jnp.where(lane == 0, c_hi, jnp.where(lane == 1, c_mid,
                                                      jnp.where(lane == 2, c2 - c_hi - c_mid, 0.0)))
            o_ref[hd, pl.ds(r0, tc), :] = aug.astype(BF16)
        return cum[tc - 1:tc, :]

    lax.fori_loop(0, s // tc, chunk, jnp.zeros((1, LANES), F32))


def _logf(misc, bias_row, batch, seq):
    tc = _pick(seq, (256, 128))
    return pl.pallas_call(
        functools.partial(_logf_kernel, tc=tc),
        grid=(batch,),
        in_specs=[pl.BlockSpec((seq, LANES), lambda b: (b, 0)), pl.BlockSpec((1, LANES), lambda b: (0, 0))],
        out_specs=pl.BlockSpec((FOX_HEADS, seq, LANES), lambda b: (0, b, 0)),
        out_shape=jax.ShapeDtypeStruct((FOX_HEADS, batch * seq, LANES), BF16),
        compiler_params=_params(("parallel",)),
        name="fox_logf",
    )(misc, bias_row)


def _flash_scratch(tk, mcols):
    return [pltpu.VMEM((tk, mcols), F32), pltpu.VMEM((1, mcols), F32), pltpu.VMEM((1, mcols), F32),
            pltpu.VMEM((LANES, mcols), F32)]


def _causal_bias(tk, mcols, tq, offset):
    rk = lax.broadcasted_iota(jnp.int32, (tk, mcols), 0)
    rq = lax.broadcasted_iota(jnp.int32, (tk, mcols), 1) & (tq - 1)
    return jnp.where(rk <= rq + offset, 0.0, NEG_INF)


def _flash_tiles(qa, ka_ref, vt_ref, n_full, diag_bias_ref, tk, scratch, next_qa=None):
    s_ref, m_ref, l_ref, acc_ref = scratch

    def logits(j, queries=qa):
        k0 = pl.multiple_of(j * tk, tk)
        return _dot(ka_ref[pl.ds(k0, tk), :], queries)

    def step(s, j):
        k0 = pl.multiple_of(j * tk, tk)
        m_prev = m_ref[...]
        m_new = jnp.maximum(m_prev, jnp.max(s, axis=0, keepdims=True))
        alpha = jnp.exp2(m_prev - m_new)
        p = jnp.exp2(s - m_new)
        l_ref[...] = alpha * l_ref[...] + jnp.sum(p, axis=0, keepdims=True)
        acc_ref[...] = alpha * acc_ref[...] + _dot(vt_ref[:, pl.ds(k0, tk)], p.astype(BF16))
        m_ref[...] = m_new

    m_ref[...] = jnp.full(m_ref.shape, NEG_INF, F32)
    l_ref[...] = jnp.zeros(l_ref.shape, F32)
    acc_ref[...] = jnp.zeros(acc_ref.shape, F32)
    if next_qa is None:
        s_ref[...] = logits(0)

    def full(j, c):
        s_next = logits(j + 1)
        step(s_ref[...], j)
        s_ref[...] = s_next
        return c

    lax.fori_loop(0, n_full, full, 0)
    if next_qa is None:
        step(s_ref[...] + diag_bias_ref[...], n_full)
    else:
        s_next = logits(0, next_qa)
        step(s_ref[...] + diag_bias_ref[...], n_full)
        s_ref[...] = s_next
    return acc_ref[...] * (1.0 / l_ref[...])


def _stage_kv(ka_ref, vt_ref, k_ref, kx_ref, v_ref, chunk, pad=0):
    def body(c, carry):
        r0 = pl.multiple_of(c * chunk, chunk)
        ka_ref[pl.ds(pad + r0, chunk), 0:LANES] = k_ref[0, pl.ds(r0, chunk), :]
        ka_ref[pl.ds(pad + r0, chunk), LANES:2 * LANES] = kx_ref[pl.ds(r0, chunk), :]
        vt_ref[:, pl.ds(pad + r0, chunk)] = v_ref[0, pl.ds(r0, chunk), :].astype(F32).T.astype(BF16)
        return carry

    lax.fori_loop(0, v_ref.shape[1] // chunk, body, 0)


def _fox_kernel(q_ref, k_ref, v_ref, cx_ref, o_ref, ka_ref, vt_ref, tri_ref, *flash, t):
    _stage_kv(ka_ref, vt_ref, k_ref, cx_ref.at[0], v_ref, t)
    tri_ref[...] = _causal_bias(t, t, t, 0)
    lane = lax.broadcasted_iota(jnp.int32, (t, LANES), 1)
    qx = jnp.where(lane < 3, -1.0, 0.0).astype(BF16)

    nq = q_ref.shape[1] // t

    def queries(qi):
        q0 = pl.multiple_of(qi * t, t)
        return _t(jnp.concatenate([q_ref[0, pl.ds(q0, t), :], qx], axis=1))

    flash[0][...] = _dot(ka_ref[0:t, :], queries(0))

    def qstep(qi, c):
        o = _flash_tiles(queries(qi), ka_ref, vt_ref, qi, tri_ref, t, flash,
                         next_qa=queries(jnp.minimum(qi + 1, nq - 1)))
        o_ref[pl.ds(pl.multiple_of(qi * t, t), t), :] = o.T.astype(BF16)
        return c

    lax.fori_loop(0, nq, qstep, 0)


def _fox(slabs, cx, batch, seq):
    t = _pick(seq, (512, 256, 128))
    m = batch * seq
    head = lambda base: pl.BlockSpec((1, seq, LANES), lambda b, h: (base + h, b, 0))
    return pl.pallas_call(
        functools.partial(_fox_kernel, t=t),
        grid=(batch, FOX_HEADS),
        in_specs=[head(SLAB_FQ), head(SLAB_FK), head(SLAB_FV), head(0)],
        out_specs=pl.BlockSpec((seq, LANES), lambda b, h: (b, h)),
        out_shape=jax.ShapeDtypeStruct((m, FOX_HEADS * HEAD_DIM), BF16),
        scratch_shapes=[pltpu.VMEM((seq, 2 * LANES), BF16), pltpu.VMEM((LANES, seq), BF16),
                        pltpu.VMEM((t, t), F32)] + _flash_scratch(t, t),
        compiler_params=_params(("parallel", "parallel")),
        name="fox_attn",
    )(slabs, slabs, slabs, cx)


def _diff_kernel(q_ref, k_ref, v_ref, kx_ref, qx_ref, lam_ref, sub_ref, o_ref, ka_ref, vt_ref, tri_ref, *flash,
                 t, lam_init):
    _stage_kv(ka_ref, vt_ref, k_ref, kx_ref, v_ref, t)
    tri_ref[...] = _causal_bias(t, 2 * t, t, 0)
    lane = lax.broadcasted_iota(jnp.int32, (t, LANES), 1)
    qx = jnp.broadcast_to(qx_ref[0], (t, LANES)).astype(BF16)
    lv = lam_ref[...]
    lam = (jnp.exp(jnp.sum(lv[0:1] * lv[1:2], axis=-1, keepdims=True))
           - jnp.exp(jnp.sum(lv[2:3] * lv[3:4], axis=-1, keepdims=True)) + lam_init)

    nq = q_ref.shape[1] // t

    def queries(qi):
        q = q_ref[0, pl.ds(pl.multiple_of(qi * t, t), t), :].astype(F32)
        return _t(jnp.concatenate([
            jnp.concatenate([jnp.where(lane < DIFF_HALF, q, 0.0).astype(BF16), qx], axis=1),
            jnp.concatenate([jnp.where(lane >= DIFF_HALF, q, 0.0).astype(BF16), qx], axis=1)], axis=0))

    flash[0][...] = _dot(ka_ref[0:t, :], queries(0))

    def qstep(qi, c):
        o = _flash_tiles(queries(qi), ka_ref, vt_ref, qi, tri_ref, t, flash,
                         next_qa=queries(jnp.minimum(qi + 1, nq - 1)))
        o = (o[:, :t] - lam * o[:, t:]).T
        o_ref[pl.ds(pl.multiple_of(qi * t, t), t), :] = (_rms(o, sub_ref[...]) * (1.0 - lam_init)).astype(BF16)
        return c

    lax.fori_loop(0, nq, qstep, 0)


def _diff(slabs, kx, qx_diff, lam_vec, subln, batch, seq, lam_init):
    t = _pick(seq, (512, 256, 128))
    m = batch * seq
    head = lambda base: pl.BlockSpec((1, seq, LANES), lambda b, h: (base + h, b, 0))
    return pl.pallas_call(
        functools.partial(_diff_kernel, t=t, lam_init=lam_init),
        grid=(batch, DIFF_HEADS),
        in_specs=[head(SLAB_DQ), head(SLAB_DK), head(SLAB_DV),
                  pl.BlockSpec((seq, LANES), lambda b, h: (0, 0)),
                  pl.BlockSpec((1, 1, LANES), lambda b, h: (h, 0, 0)),
                  pl.BlockSpec((4, DIFF_HALF), lambda b, h: (0, 0)),
                  pl.BlockSpec((1, LANES), lambda b, h: (0, 0))],
        out_specs=pl.BlockSpec((seq, LANES), lambda b, h: (b, h)),
        out_shape=jax.ShapeDtypeStruct((m, DIFF_HEADS * HEAD_DIM), BF16),
        scratch_shapes=[pltpu.VMEM((seq, 2 * LANES), BF16), pltpu.VMEM((LANES, seq), BF16),
                        pltpu.VMEM((t, 2 * t), F32)] + _flash_scratch(t, 2 * t),
        compiler_params=_params(("parallel", "parallel")),
        name="diff_attn",
    )(slabs, slabs, slabs, kx, qx_diff, lam_vec, subln)


def _compress_kernel(y_ref, pos_ref, w1_ref, w2_ref, o_ref):
    half = y_ref.shape[3]
    y = y_ref[0, 0].astype(F32)
    top = (y + pos_ref[0, 0:1, :]).astype(BF16)
    bot = (y + pos_ref[0, 1:2, :]).astype(BF16)
    a = _dot(top, w1_ref[0, 0:half, :])
    b = _dot(bot, w1_ref[0, half:2 * half, :])
    nrow = a.shape[0]
    hid = a + pltpu.roll(b, nrow - 1, 0)
    hid = hid * jax.nn.sigmoid(hid)
    o_ref[0, 0] = _dot(hid.astype(BF16), w2_ref[0]).astype(BF16)


def _compress(ykv, pos2, w1, w2, batch, l):
    nrow, half = ykv.shape[2], ykv.shape[3]
    return pl.pallas_call(
        _compress_kernel,
        grid=(4, batch),
        in_specs=[pl.BlockSpec((1, 1, nrow, half), lambda s, b: (s, b, 0, 0)),
                  pl.BlockSpec((None, 1, 2, half), lambda s, b: (l, s // 2, 0, 0)),
                  pl.BlockSpec((None, 1, 2 * half, CMP_HIDDEN), lambda s, b: (l, s // 2, 0, 0)),
                  pl.BlockSpec((None, 1, CMP_HIDDEN, HEAD_DIM), lambda s, b: (l, s // 2, 0, 0))],
        out_specs=pl.BlockSpec((1, 1, nrow, HEAD_DIM), lambda s, b: (s, b, 0, 0)),
        out_shape=jax.ShapeDtypeStruct((4, batch, nrow, HEAD_DIM), BF16),
        compiler_params=_params(("parallel", "parallel")),
        name="nsa_compress",
    )(ykv, pos2, w1, w2)


def _topk_bias(cand, topk):
    nsel, tq = cand.shape
    sub = 8
    blocks = [cand[b * sub:(b + 1) * sub] for b in range(nsel // sub)]
    jidx = lax.broadcasted_iota(jnp.int32, (sub, tq), 0)
    ranks = [jnp.zeros((sub, tq), F32) for _ in blocks]
    for k in range(nsel):
        rk = cand[k:k + 1, :]
        for b, cb in enumerate(blocks):
            if b * sub > k:
                beats = rk >= cb
            elif b * sub + sub - 1 < k:
                beats = rk > cb
            else:
                beats = (rk > cb) | ((rk == cb) & (jidx + b * sub > k))
            ranks[b] = ranks[b] + jnp.where(beats, 1.0, 0.0)
    rank = jnp.concatenate(ranks, axis=0)
    return jnp.where(rank < topk, 0.0, SEL_MASK)


def _nsa_kernel(q_ref, kc_ref, vc_ref, ks_ref, vs_ref, kw_ref, vw_ref, kx_ref, cx_ref, ovt_ref, qx_ref,
                misc_ref, o_ref, ksa_ref, kwa_ref, vst_ref, vwt_ref, vct_ref, gt_ref, dbias_ref, wbias_ref,
                comb_ref, *flash, tq, tk):
    g = pl.program_id(1)
    mcols = NSA_REP * tq
    nc = kc_ref.shape[2]
    seq = kx_ref.shape[0]
    nsel = seq // SEL_BLOCK
    span = WINDOW + tq

    _stage_kv(ksa_ref, vst_ref, ks_ref, kx_ref, vs_ref, tk)
    _stage_kv(kwa_ref, vwt_ref, kw_ref, kx_ref, vw_ref, tk, pad=WINDOW)
    pad_lane = lax.broadcasted_iota(jnp.int32, (WINDOW, 2 * LANES), 1)
    kwa_ref[0:WINDOW, :] = jnp.where(pad_lane == LANES + AUG_PAD, SEL_MASK, 0.0).astype(BF16)
    vwt_ref[:, 0:WINDOW] = jnp.zeros((LANES, WINDOW), BF16)
    vct_ref[...] = vc_ref[0, 0].astype(F32).T.astype(BF16)
    for o in range(tk // tq):
        dbias_ref[o] = _causal_bias(tk, mcols, tq, o * tq)
    rk = lax.broadcasted_iota(jnp.int32, (span, mcols), 0)
    rq = lax.broadcasted_iota(jnp.int32, (span, mcols), 1) & (tq - 1)
    wbias_ref[...] = jnp.where((rk > rq) & (rk <= rq + WINDOW), 0.0, NEG_INF)
    qx = jnp.concatenate([jnp.broadcast_to(qx_ref[0, r:r + 1, :], (tq, LANES)) for r in range(NSA_REP)], axis=0).T
    kca = jnp.concatenate([kc_ref[0, 0], cx_ref[...]], axis=1)

    def qstep(qi, carry):
        _nsa_query_tile(qi, g, qx, kca, q_ref, ovt_ref, misc_ref, o_ref, ksa_ref, kwa_ref, vst_ref, vwt_ref,
                        vct_ref, gt_ref, dbias_ref, wbias_ref, comb_ref, flash, tq=tq, tk=tk, nc=nc, nsel=nsel)
        return carry

    lax.fori_loop(0, seq // tq, qstep, 0)


def _nsa_query_tile(qi, g, qx, kca, q_ref, ovt_ref, misc_ref, o_ref, ksa_ref, kwa_ref, vst_ref, vwt_ref, vct_ref,
                    gt_ref, dbias_ref, wbias_ref, comb_ref, flash, *, tq, tk, nc, nsel):
    mcols = NSA_REP * tq
    span = WINDOW + tq
    q0 = pl.multiple_of(qi * tq, tq)
    q4 = _t(q_ref[:, pl.ds(q0, tq), :].reshape(mcols, LANES))
    qa = jnp.concatenate([q4, qx.astype(BF16)], axis=0)
    col = lax.broadcasted_iota(jnp.int32, (1, mcols), 1)
    col_pos = q0 + (col & (tq - 1))

    sc = _dot(kca, qa)
    cend = lax.broadcasted_iota(jnp.int32, (nc, mcols), 0) * CMP_STRIDE + (CMP_BLOCK - 1)
    sc = jnp.where(cend <= col_pos, sc, NEG_INF)
    e = jnp.exp2(sc - jnp.max(sc, axis=0, keepdims=True))
    inv = jnp.where(col_pos >= CMP_BLOCK - 1, 1.0 / jnp.sum(e, axis=0, keepdims=True), 0.0)
    p = e * inv
    o_cmp = _dot(vct_ref[...], p.astype(BF16))

    psum = p[:, 0:tq]
    for r in range(1, NSA_REP):
        psum = psum + p[:, r * tq:(r + 1) * tq]
    p_hi = psum.astype(BF16)
    p_lo = (psum - p_hi.astype(F32)).astype(BF16)
    imp = _dot(ovt_ref[...], p_hi) + _dot(ovt_ref[...], p_lo)
    qpos = q0 + lax.broadcasted_iota(jnp.int32, (LANES, tq), 1)
    blk = lax.broadcasted_iota(jnp.int32, (LANES, tq), 0)
    cur = jnp.right_shift(qpos, SEL_BLOCK.bit_length() - 1)
    forced = (blk == 0) | (blk == cur) | (blk == cur - 1)
    imp = jnp.where(blk <= cur, jnp.where(forced, FORCE_SCORE, imp), -1.0)
    bias_t = _topk_bias(imp[0:nsel], min(SEL_TOPK, nsel))
    if nsel < LANES:
        bias_t = jnp.concatenate([bias_t, jnp.zeros((LANES - nsel, tq), F32)], axis=0)
    selbias = jnp.concatenate([bias_t] * NSA_REP, axis=1)
    row4 = lax.broadcasted_iota(jnp.int32, (LANES, mcols), 0)
    qa_sel = jnp.concatenate([q4, jnp.where(row4 < SEL_BLOCK, selbias, qx).astype(BF16)], axis=0)

    sw = _dot(kwa_ref[pl.ds(q0, span), :], qa) + wbias_ref[...]
    ew = jnp.exp2(sw - jnp.max(sw, axis=0, keepdims=True))
    o_win = _dot(vwt_ref[:, pl.ds(q0, span)], ew.astype(BF16)) * (1.0 / jnp.sum(ew, axis=0, keepdims=True))

    gt_ref[...] = jax.nn.sigmoid(misc_ref[pl.ds(q0, tq), :]).T
    gate = lambda r, i: gt_ref[pl.ds(MISC_NG + 3 * (NSA_REP * g + r) + i, 1), :]
    for r in range(NSA_REP):
        sl = slice(r * tq, (r + 1) * tq)
        comb_ref[:, sl] = gate(r, 0) * o_cmp[:, sl] + gate(r, 2) * o_win[:, sl]

    o_sel = _flash_tiles(qa_sel, ksa_ref, vst_ref, q0 // tk, dbias_ref.at[qi & (tk // tq - 1)], tk, flash)
    for r in range(NSA_REP):
        sl = slice(r * tq, (r + 1) * tq)
        out = comb_ref[:, sl] + gate(r, 1) * o_sel[:, sl]
        o_ref[pl.ds(q0, tq), r * LANES:(r + 1) * LANES] = out.T.astype(BF16)


def _nsa(slabs, kvc, kx, cx_cmp, ovt, qx_nsa, misc, batch, seq):
    tq = _pick(seq, (256, 128))
    tk = _pick(seq, (512, 256, 128))
    assert tq & (tq - 1) == 0 and tk % tq == 0 and seq >= WINDOW + tq
    m = batch * seq
    mcols = NSA_REP * tq
    nc = kvc.shape[2]
    kv_spec = lambda base: pl.BlockSpec((1, seq, LANES), lambda b, g: (base + g, b, 0))
    const = lambda shape: pl.BlockSpec(shape, lambda b, g: (0,) * len(shape), pipeline_mode=pl.Buffered(1))
    return pl.pallas_call(
        functools.partial(_nsa_kernel, tq=tq, tk=tk),
        grid=(batch, NSA_KV_HEADS),
        in_specs=[pl.BlockSpec((NSA_REP, seq, LANES), lambda b, g: (SLAB_NQ // NSA_REP + g, b, 0)),
                  pl.BlockSpec((1, 1, nc, LANES), lambda b, g: (g, b, 0, 0)),
                  pl.BlockSpec((1, 1, nc, LANES), lambda b, g: (2 + g, b, 0, 0)),
                  kv_spec(SLAB_NKS), kv_spec(SLAB_NVS), kv_spec(SLAB_NKW), kv_spec(SLAB_NVW),
                  const((seq, LANES)), const((nc, LANES)), const((LANES, nc)),
                  pl.BlockSpec((1, NSA_REP, LANES), lambda b, g: (g, 0, 0)),
                  pl.BlockSpec((seq, LANES), lambda b, g: (b, 0))],
        out_specs=pl.BlockSpec((seq, NSA_REP * LANES), lambda b, g: (b, g)),
        out_shape=jax.ShapeDtypeStruct((m, NSA_HEADS * HEAD_DIM), BF16),
        scratch_shapes=[pltpu.VMEM((seq, 2 * LANES), BF16), pltpu.VMEM((WINDOW + seq, 2 * LANES), BF16),
                        pltpu.VMEM((LANES, seq), BF16), pltpu.VMEM((LANES, WINDOW + seq), BF16),
                        pltpu.VMEM((LANES, nc), BF16), pltpu.VMEM((LANES, tq), F32),
                        pltpu.VMEM((tk // tq, tk, mcols), F32), pltpu.VMEM((WINDOW + tq, mcols), F32),
                        pltpu.VMEM((LANES, mcols), F32)] + _flash_scratch(tk, mcols),
        compiler_params=_params(("parallel", "parallel")),
        name="nsa_attn",
    )(slabs, kvc, kvc, slabs, slabs, slabs, slabs, kx, cx_cmp, ovt, qx_nsa, misc)


def _merge_kernel(h_ref, of_ref, od_ref, on_ref, wg0_ref, wg1_ref, wg2_ref, wf_ref, wd_ref, wn_ref, o_ref):
    h = h_ref[...]
    acc = jax.nn.sigmoid(_dot(h, wg0_ref[...])) * _dot(of_ref[...], wf_ref[...])
    acc = acc + jax.nn.sigmoid(_dot(h, wg1_ref[...])) * _dot(od_ref[...], wd_ref[...])
    acc = acc + jax.nn.sigmoid(_dot(h, wg2_ref[...])) * _dot(on_ref[...], wn_ref[...])
    o_ref[...] = acc.astype(BF16)


def _merge(h, o_fox, o_diff, o_nsa, w_gate, wb_fox, wb_diff, wb_nsa, l):
    m, d = h.shape
    tm = _pick(m, (1024, 512, 256))
    tn = _pick(d, (256, 128))
    nj = d // tn
    row = lambda width: pl.BlockSpec((tm, width), lambda i, j: (i, 0))
    gate = lambda t: pl.BlockSpec((None, d, tn), lambda i, j: (l, 0, t * nj + j))
    col = lambda k: pl.BlockSpec((None, k, tn), lambda i, j: (l, 0, j))
    return pl.pallas_call(
        _merge_kernel,
        grid=(m // tm, nj),
        in_specs=[row(d), row(o_fox.shape[1]), row(o_diff.shape[1]), row(o_nsa.shape[1]),
                  gate(0), gate(1), gate(2),
                  col(wb_fox.shape[1]), col(wb_diff.shape[1]), col(wb_nsa.shape[1])],
        out_specs=pl.BlockSpec((tm, tn), lambda i, j: (i, j)),
        out_shape=jax.ShapeDtypeStruct((m, d), BF16),
        compiler_params=_params(("parallel", "arbitrary")),
        name="gate_merge",
    )(h, o_fox, o_diff, o_nsa, w_gate, w_gate, w_gate, wb_fox, wb_diff, wb_nsa)


def _wout_kernel(a_ref, w_ref, x_ref, gp_ref, gn_ref, xo_ref, ho_ref):
    y = _dot(a_ref[...], w_ref[...])
    x_new = x_ref[...] + _rms(y, gp_ref[...])
    xo_ref[...] = x_new
    ho_ref[...] = _rms(x_new, gn_ref[...]).astype(BF16)


def _wout(a, w, x, g_post, g_next, l, name):
    m, d = x.shape
    k = a.shape[1]
    tm = _pick(m, (512, 256, 128)) if k * d * 2 <= 8 * 1024 * 1024 else _pick(m, (256, 128))
    row = lambda width: pl.BlockSpec((tm, width), lambda i: (i, 0))
    vec = lambda: pl.BlockSpec((1, d), lambda i: (0, 0))
    return pl.pallas_call(
        _wout_kernel,
        grid=(m // tm,),
        in_specs=[row(k), pl.BlockSpec((None, k, d), lambda i: (l, 0, 0), pipeline_mode=pl.Buffered(1)),
                  row(d), vec(), vec()],
        out_specs=[row(d), row(d)],
        out_shape=[jax.ShapeDtypeStruct((m, d), F32), jax.ShapeDtypeStruct((m, d), BF16)],
        compiler_params=_params(("parallel",)),
        name=name,
    )(a, w, x, g_post, g_next)


def _ffn_up_kernel(h_ref, wg_ref, wu_ref, o_ref, wgb_ref, wub_ref):
    @pl.when(pl.program_id(1) == 0)
    def _():
        wgb_ref[...] = wg_ref[...].astype(BF16)
        wub_ref[...] = wu_ref[...].astype(BF16)

    h = h_ref[...]
    gate = _dot(h, wgb_ref[...])
    o_ref[...] = (gate * jax.nn.sigmoid(gate) * _dot(h, wub_ref[...])).astype(BF16)


def _ffn_up(h, w_up, l):
    m, d = h.shape
    dff = w_up.shape[2] // 2
    tm = _pick(m, (1024, 512, 256))
    tn = _pick(dff, (512, 256, 128))
    nj = dff // tn
    return pl.pallas_call(
        _ffn_up_kernel,
        grid=(nj, m // tm),
        in_specs=[pl.BlockSpec((tm, d), lambda j, i: (i, 0)),
                  pl.BlockSpec((None, d, tn), lambda j, i: (l, 0, j)),
                  pl.BlockSpec((None, d, tn), lambda j, i: (l, 0, nj + j))],
        out_specs=pl.BlockSpec((tm, tn), lambda j, i: (i, j)),
        out_shape=jax.ShapeDtypeStruct((m, dff), BF16),
        scratch_shapes=[pltpu.VMEM((d, tn), BF16), pltpu.VMEM((d, tn), BF16)],
        compiler_params=_params(("parallel", "arbitrary")),
        name="ffn_up",
    )(h, w_up, w_up)


def _pos_columns(pos):
    lane = jnp.arange(LANES)[None, :]
    hi = (lane >= AUG_HI) & (lane < AUG_HI + AUG_TERMS)
    lo = (lane >= AUG_LO) & (lane < AUG_LO + AUG_TERMS)
    return jnp.where(hi, (pos // LANES)[:, None], jnp.where(lo, (pos % LANES)[:, None], 0))


def _key_aug_table(seq):
    j = jnp.arange(seq)
    lane = jnp.arange(LANES)[None, :]
    onehot = (lane == (j // SEL_BLOCK)[:, None]) & (lane < SEL_BLOCK)
    return (_pos_columns(j) + onehot.astype(jnp.int32)).astype(BF16)


def _cmp_aug_table(nrow):
    return _pos_columns(jnp.arange(nrow) * CMP_STRIDE + CMP_BLOCK - 1).astype(BF16)


def _overlap_table_t(nrow, seq):
    start = jnp.arange(nrow)[None, :] * CMP_STRIDE
    blk = jnp.arange(LANES)[:, None]
    sel = blk * SEL_BLOCK
    ov = ((start < sel + SEL_BLOCK) & (start + CMP_BLOCK - 1 >= sel) & (blk < seq // SEL_BLOCK)
          & (jnp.arange(nrow)[None, :] < (seq - CMP_BLOCK) // CMP_STRIDE + 1))
    return ov.astype(BF16)


def _query_aug_rows(n_heads):
    slopes = 2.0 ** (-8.0 * jnp.arange(1, n_heads + 1, dtype=F32) / n_heads)
    terms = [t.astype(F32) for t in _split3(jnp.float32(LOG2E))]
    lane = jnp.arange(LANES)[None, :]
    out = jnp.where(lane == AUG_PAD, 1.0, jnp.zeros((n_heads, LANES), F32))
    for i, t in enumerate(terms):
        out = jnp.where(lane == AUG_HI + i, slopes[:, None] * t * LANES, out)
        out = jnp.where(lane == AUG_LO + i, slopes[:, None] * t, out)
    return out


def kernel(x, w_in, fox_forget_bias, diff_lambda, diff_subln, nsa_cmp_pos, nsa_cmp_w1, nsa_cmp_w2,
           w_branch_fox, w_branch_diff, w_branch_nsa, w_gate, w_out, norm_gains, w_ffn_up, w_ffn_down):
    batch, seq, d = x.shape
    depth = w_in.shape[0]
    m = batch * seq
    fw, dw, nw, kvw = FOX_HEADS * HEAD_DIM, DIFF_HEADS * HEAD_DIM, NSA_HEADS * HEAD_DIM, NSA_KV_HEADS * HEAD_DIM
    ff0 = 3 * fw
    dq0 = ff0 + FOX_HEADS
    ng0 = dq0 + 3 * dw + nw + 6 * kvw
    n_main = ng0 - FOX_HEADS
    assert n_main == N_SLABS * LANES and w_in.shape[2] == ng0 + 3 * NSA_HEADS

    w_main = jnp.concatenate([w_in[:, :, :ff0], w_in[:, :, dq0:ng0]], axis=2).astype(BF16)
    w_misc = jnp.concatenate([w_in[:, :, ff0:dq0], w_in[:, :, ng0:],
                              jnp.zeros((depth, d, LANES - FOX_HEADS - 3 * NSA_HEADS), F32)], axis=2).astype(BF16)
    colscale = jnp.ones((n_main,), F32)
    colscale = colscale.at[SLAB_FQ * LANES:SLAB_FK * LANES].set(HEAD_DIM ** -0.5 * LOG2E)
    colscale = colscale.at[SLAB_DQ * LANES:SLAB_DK * LANES].set(DIFF_HALF ** -0.5 * LOG2E)
    colscale = colscale.at[SLAB_NQ * LANES:SLAB_NKC * LANES].set(HEAD_DIM ** -0.5 * LOG2E)
    colscale = colscale[None, :]
    fbias = jnp.pad(fox_forget_bias.astype(F32), ((0, 0), (0, LANES - FOX_HEADS)))[:, None, :]
    half = CMP_STRIDE * HEAD_DIM
    pos2 = nsa_cmp_pos.astype(F32).reshape(depth, 2, 2, half)
    w1 = nsa_cmp_w1.astype(BF16)
    w2 = nsa_cmp_w2.astype(BF16)
    wbf, wbd, wbn = w_branch_fox.astype(BF16), w_branch_diff.astype(BF16), w_branch_nsa.astype(BF16)
    wg, wo = w_gate.astype(BF16), w_out.astype(BF16)
    wup, wdn = w_ffn_up.astype(F32), w_ffn_down.astype(BF16)
    gains = norm_gains.astype(F32)

    nrow = seq // CMP_STRIDE
    kx = _key_aug_table(seq)
    cx_cmp = _cmp_aug_table(nrow)
    ovt = _overlap_table_t(nrow, seq)
    qx_diff = _query_aug_rows(DIFF_HEADS)[:, None, :]
    qx_nsa = _query_aug_rows(NSA_HEADS).reshape(NSA_KV_HEADS, NSA_REP, LANES)

    xf = x.reshape(m, d).astype(F32)
    h = _norm(xf, gains[0, 0][None, :])
    for l in range(depth):
        lam_init = 0.8 - 0.6 * math.exp(-0.3 * l)
        slabs, misc = _inproj(h, w_main, colscale, w_misc, l)
        cx_fox = _logf(misc, fbias[l], batch, seq)
        o_fox = _fox(slabs, cx_fox, batch, seq)
        o_diff = _diff(slabs, kx, qx_diff, diff_lambda[l].astype(F32), diff_subln[l].astype(F32)[None, :],
                       batch, seq, lam_init)
        ykv = slabs[SLAB_NKC:SLAB_NKC + 4].reshape(4, batch, nrow, half)
        kvc = _compress(ykv, pos2, w1, w2, batch, l)
        o_nsa = _nsa(slabs, kvc, kx, cx_cmp, ovt, qx_nsa, misc, batch, seq)
        merged = _merge(h, o_fox, o_diff, o_nsa, wg, wbf, wbd, wbn, l)
        xf, h2 = _wout(merged, wo, xf, gains[l, 1][None, :], gains[l, 2][None, :], l, "out_proj")
        act = _ffn_up(h2, wup, l)
        g_next = gains[min(l + 1, depth - 1), 0][None, :]
        xf, h = _wout(act, wdn, xf, gains[l, 3][None, :], g_next, l, "ffn_down")
    return xf.reshape(batch, seq, d).astype(x.dtype)
```

```python
import functools
import math

import jax
import jax.numpy as jnp
from jax import lax
from jax.experimental import pallas as pl
from jax.experimental.pallas import tpu as pltpu

F32 = jnp.float32
BF16 = jnp.bfloat16

HEAD_DIM = 128
FOX_HEADS = 4
DIFF_HEADS = 4
DIFF_HALF = HEAD_DIM // 2
NSA_HEADS = 8
NSA_KV_HEADS = 2
NSA_REP = NSA_HEADS // NSA_KV_HEADS
CMP_BLOCK = 32
CMP_STRIDE = 16
CMP_HIDDEN = 256
SEL_BLOCK = 64
SEL_TOPK = 16
WINDOW = 512
N_BRANCHES = 3
EPS = 1e-6
NEG_INF = -1e30
FORCE_SCORE = 1e4
SEL_MASK = -32768.0
LOG2E = math.log2(math.e)
LANES = 128

SLAB_FQ, SLAB_FK, SLAB_FV = 0, 4, 8
SLAB_DQ, SLAB_DK, SLAB_DV = 12, 16, 20
SLAB_NQ = 24
SLAB_NKC, SLAB_NVC, SLAB_NKS, SLAB_NVS, SLAB_NKW, SLAB_NVW = 32, 34, 36, 38, 40, 42
N_SLABS = 44
MISC_FF = 0
MISC_NG = 4
AUG_HI = 64
AUG_LO = 67
AUG_TERMS = 3
AUG_PAD = 70

VMEM_LIMIT = 56 * 1024 * 1024


def _pick(n, prefs):
    for p in prefs:
        if p <= n and n % p == 0:
            return p
    return n


def _params(sem):
    return pltpu.CompilerParams(dimension_semantics=sem, vmem_limit_bytes=VMEM_LIMIT)


def _rms(y, g):
    return y * lax.rsqrt(jnp.mean(y * y, axis=-1, keepdims=True) + EPS) * g


def _dot(a, b):
    return jnp.dot(a, b, preferred_element_type=F32)


def _dot_nt(a, b):
    return lax.dot_general(a, b, (((1,), (1,)), ((), ())), preferred_element_type=F32)


def _split3(x):
    hi = x.astype(BF16)
    r = x - hi.astype(F32)
    mid = r.astype(BF16)
    lo = (r - mid.astype(F32)).astype(BF16)
    return hi, mid, lo


def _norm_kernel(x_ref, g_ref, h_ref):
    h_ref[...] = _rms(x_ref[...], g_ref[...]).astype(BF16)


def _norm(x, g):
    m, d = x.shape
    tm = _pick(m, (512, 256, 128))
    return pl.pallas_call(
        _norm_kernel,
        grid=(m // tm,),
        in_specs=[pl.BlockSpec((tm, d), lambda i: (i, 0)), pl.BlockSpec((1, d), lambda i: (0, 0))],
        out_specs=pl.BlockSpec((tm, d), lambda i: (i, 0)),
        out_shape=jax.ShapeDtypeStruct((m, d), BF16),
        compiler_params=_params(("parallel",)),
        name="norm_in",
    )(x, g)


def _inproj_kernel(h_ref, w_ref, cs_ref, wm_ref, o_ref, misc_ref):
    h = h_ref[...]
    acc = _dot(h, w_ref[...]) * cs_ref[...]
    for s in range(o_ref.shape[0]):
        o_ref[s] = acc[:, s * LANES:(s + 1) * LANES].astype(BF16)

    @pl.when(pl.program_id(1) == 0)
    def _():
        misc_ref[...] = _dot(h, wm_ref[...])


def _inproj(h, w_main, colscale, w_misc, l):
    m, d = h.shape
    n = w_main.shape[2]
    tm = _pick(m, (1024, 512, 256))
    tn = _pick(n, (11 * LANES, 4 * LANES))
    return pl.pallas_call(
        _inproj_kernel,
        grid=(m // tm, n // tn),
        in_specs=[pl.BlockSpec((tm, d), lambda i, j: (i, 0)),
                  pl.BlockSpec((None, d, tn), lambda i, j: (l, 0, j)),
                  pl.BlockSpec((1, tn), lambda i, j: (0, j)),
                  pl.BlockSpec((None, d, LANES), lambda i, j: (l, 0, 0))],
        out_specs=[pl.BlockSpec((tn // LANES, tm, LANES), lambda i, j: (j, i, 0)),
                   pl.BlockSpec((tm, LANES), lambda i, j: (i, 0))],
        out_shape=[jax.ShapeDtypeStruct((n // LANES, m, LANES), BF16),
                   jax.ShapeDtypeStruct((m, LANES), F32)],
        compiler_params=_params(("parallel", "arbitrary")),
        name="inproj",
    )(h, w_main, colscale, w_misc)


def _logf_kernel(misc_ref, bias_ref, o_ref, *, tc):
    s = misc_ref.shape[0]
    row = lax.broadcasted_iota(jnp.int32, (tc, tc), 0)
    col = lax.broadcasted_iota(jnp.int32, (tc, tc), 1)
    tri = jnp.where(col <= row, 1.0, 0.0).astype(BF16)
    lane = lax.broadcasted_iota(jnp.int32, (tc, LANES), 1)

    def chunk(c, carry):
        r0 = pl.multiple_of(c * tc, tc)
        z = misc_ref[pl.ds(r0, tc), :] + bias_ref[...]
        lf = jnp.minimum(z, 0.0) - jnp.log1p(jnp.exp(-jnp.abs(z)))
        hi, mid, lo = _split3(lf)
        cum = _dot(tri, hi) + _dot(tri, mid) + _dot(tri, lo) + carry
        for hd in range(FOX_HEADS):
            c2 = jnp.broadcast_to(cum[:, hd:hd + 1], (tc, LANES)) * LOG2E
            c_hi = c2.astype(BF16).astype(F32)
            c_mid = (c2 - c_hi).astype(BF16).astype(F32)
            aug = jnp.where(lane == 0, c_hi, jnp.where(lane == 1, c_mid,
                                                      jnp.where(lane == 2, c2 - c_hi - c_mid, 0.0)))
            o_ref[hd, pl.ds(r0, tc), :] = aug.astype(BF16)
        return cum[tc - 1:tc, :]

    lax.fori_loop(0, s // tc, chunk, jnp.zeros((1, LANES), F32))


def _logf(misc, bias_row, batch, seq):
    tc = _pick(seq, (256, 128))
    return pl.pallas_call(
        functools.partial(_logf_kernel, tc=tc),
        grid=(batch,),
        in_specs=[pl.BlockSpec((seq, LANES), lambda b: (b, 0)), pl.BlockSpec((1, LANES), lambda b: (0, 0))],
        out_specs=pl.BlockSpec((FOX_HEADS, seq, LANES), lambda b: (0, b, 0)),
        out_shape=jax.ShapeDtypeStruct((FOX_HEADS, batch * seq, LANES), BF16),
        compiler_params=_params(("parallel",)),
        name="fox_logf",
    )(misc, bias_row)


def _flash_scratch(tk, mcols):
    return [pltpu.VMEM((tk, mcols + LANES), F32), pltpu.VMEM((1, mcols), F32), pltpu.VMEM((1, mcols), F32),
            pltpu.VMEM((LANES, mcols), F32)]


def _causal_bias(tk, mcols, tq, offset):
    rk = lax.broadcasted_iota(jnp.int32, (tk, mcols), 0)
    rq = lax.broadcasted_iota(jnp.int32, (tk, mcols), 1) & (tq - 1)
    return jnp.where(rk <= rq + offset, 0.0, NEG_INF)


def _flash_tiles(qa, ka_ref, vt_ref, n_full, diag_bias_ref, tk, scratch, next_qa=None):
    s_buf, m_ref, l_ref, acc_ref = scratch
    s_ref = s_buf.at[:, 0:acc_ref.shape[1]]

    def logits(j, queries=qa):
        k0 = pl.multiple_of(j * tk, tk)
        return _dot_nt(ka_ref[pl.ds(k0, tk), :], queries)

    def step(s, j):
        k0 = pl.multiple_of(j * tk, tk)
        m_prev = m_ref[...]
        m_new = jnp.maximum(m_prev, jnp.max(s, axis=0, keepdims=True))
        alpha = jnp.exp2(m_prev - m_new)
        p = jnp.exp2(s - m_new)
        l_ref[...] = alpha * l_ref[...] + jnp.sum(p, axis=0, keepdims=True)
        acc_ref[...] = alpha * acc_ref[...] + _dot(vt_ref[:, pl.ds(k0, tk)], p.astype(BF16))
        m_ref[...] = m_new

    m_ref[...] = jnp.full(m_ref.shape, NEG_INF, F32)
    l_ref[...] = jnp.zeros(l_ref.shape, F32)
    acc_ref[...] = jnp.zeros(acc_ref.shape, F32)
    if next_qa is None:
        s_ref[...] = logits(0)

    def full(j, c):
        s_next = logits(j + 1)
        step(s_ref[...], j)
        s_ref[...] = s_next
        return c

    lax.fori_loop(0, n_full, full, 0)
    if next_qa is None:
        step(s_ref[...] + diag_bias_ref[...], n_full)
    else:
        s_next = logits(0, next_qa)
        step(s_ref[...] + diag_bias_ref[...], n_full)
        s_ref[...] = s_next
    return acc_ref[...] * (1.0 / l_ref[...])


def _stage_kv(ka_ref, vt_ref, k_ref, kx_ref, v_ref, chunk, pad=0):
    def body(c, carry):
        r0 = pl.multiple_of(c * chunk, chunk)
        ka_ref[pl.ds(pad + r0, chunk), 0:LANES] = k_ref[0, pl.ds(r0, chunk), :]
        ka_ref[pl.ds(pad + r0, chunk), LANES:2 * LANES] = kx_ref[pl.ds(r0, chunk), :]
        vt_ref[:, pl.ds(pad + r0, chunk)] = v_ref[0, pl.ds(r0, chunk), :].astype(F32).T.astype(BF16)
        return carry

    lax.fori_loop(0, v_ref.shape[1] // chunk, body, 0)


def _fox_kernel(q_ref, k_ref, v_ref, cx_ref, o_ref, ka_ref, vt_ref, tri_ref, *flash, t):
    _stage_kv(ka_ref, vt_ref, k_ref, cx_ref.at[0], v_ref, t)
    tri_ref[...] = _causal_bias(t, t, t, 0)
    lane = lax.broadcasted_iota(jnp.int32, (t, LANES), 1)
    qx = jnp.where(lane < 3, -1.0, 0.0).astype(BF16)

    nq = q_ref.shape[1] // t

    def queries(qi):
        q0 = pl.multiple_of(qi * t, t)
        return jnp.concatenate([q_ref[0, pl.ds(q0, t), :], qx], axis=1)

    flash[0][:, 0:flash[3].shape[1]] = _dot_nt(ka_ref[0:t, :], queries(0))

    def qstep(qi, c):
        o = _flash_tiles(queries(qi), ka_ref, vt_ref, qi, tri_ref, t, flash,
                         next_qa=queries(jnp.minimum(qi + 1, nq - 1)))
        o_ref[pl.ds(pl.multiple_of(qi * t, t), t), :] = o.T.astype(BF16)
        return c

    lax.fori_loop(0, nq, qstep, 0)


def _fox(slabs, cx, batch, seq):
    t = _pick(seq, (512, 256, 128))
    m = batch * seq
    head = lambda base: pl.BlockSpec((1, seq, LANES), lambda b, h: (base + h, b, 0))
    return pl.pallas_call(
        functools.partial(_fox_kernel, t=t),
        grid=(batch, FOX_HEADS),
        in_specs=[head(SLAB_FQ), head(SLAB_FK), head(SLAB_FV), head(0)],
        out_specs=pl.BlockSpec((seq, LANES), lambda b, h: (b, h)),
        out_shape=jax.ShapeDtypeStruct((m, FOX_HEADS * HEAD_DIM), BF16),
        scratch_shapes=[pltpu.VMEM((seq, 2 * LANES), BF16), pltpu.VMEM((LANES, seq), BF16),
                        pltpu.VMEM((t, t), F32)] + _flash_scratch(t, t),
        compiler_params=_params(("parallel", "parallel")),
        name="fox_attn",
    )(slabs, slabs, slabs, cx)


def _diff_kernel(q_ref, k_ref, v_ref, kx_ref, qx_ref, lam_ref, sub_ref, o_ref, ka_ref, vt_ref, tri_ref, *flash,
                 t, lam_init):
    _stage_kv(ka_ref, vt_ref, k_ref, kx_ref, v_ref, t)
    tri_ref[...] = _causal_bias(t, 2 * t, t, 0)
    lane = lax.broadcasted_iota(jnp.int32, (t, LANES), 1)
    qx = jnp.broadcast_to(qx_ref[0], (t, LANES)).astype(BF16)
    lv = lam_ref[...]
    lam = (jnp.exp(jnp.sum(lv[0:1] * lv[1:2], axis=-1, keepdims=True))
           - jnp.exp(jnp.sum(lv[2:3] * lv[3:4], axis=-1, keepdims=True)) + lam_init)

    nq = q_ref.shape[1] // t

    def queries(qi):
        q = q_ref[0, pl.ds(pl.multiple_of(qi * t, t), t), :].astype(F32)
        return jnp.concatenate([
            jnp.concatenate([jnp.where(lane < DIFF_HALF, q, 0.0).astype(BF16), qx], axis=1),
            jnp.concatenate([jnp.where(lane >= DIFF_HALF, q, 0.0).astype(BF16), qx], axis=1)], axis=0)

    flash[0][:, 0:flash[3].shape[1]] = _dot_nt(ka_ref[0:t, :], queries(0))

    def qstep(qi, c):
        o = _flash_tiles(queries(qi), ka_ref, vt_ref, qi, tri_ref, t, flash,
                         next_qa=queries(jnp.minimum(qi + 1, nq - 1)))
        o = (o[:, :t] - lam * o[:, t:]).T
        o_ref[pl.ds(pl.multiple_of(qi * t, t), t), :] = (_rms(o, sub_ref[...]) * (1.0 - lam_init)).astype(BF16)
        return c

    lax.fori_loop(0, nq, qstep, 0)


def _diff(slabs, kx, qx_diff, lam_vec, subln, batch, seq, lam_init):
    t = _pick(seq, (512, 256, 128))
    m = batch * seq
    head = lambda base: pl.BlockSpec((1, seq, LANES), lambda b, h: (base + h, b, 0))
    return pl.pallas_call(
        functools.partial(_diff_kernel, t=t, lam_init=lam_init),
        grid=(batch, DIFF_HEADS),
        in_specs=[head(SLAB_DQ), head(SLAB_DK), head(SLAB_DV),
                  pl.BlockSpec((seq, LANES), lambda b, h: (0, 0)),
                  pl.BlockSpec((1, 1, LANES), lambda b, h: (h, 0, 0)),
                  pl.BlockSpec((4, DIFF_HALF), lambda b, h: (0, 0)),
                  pl.BlockSpec((1, LANES), lambda b, h: (0, 0))],
        out_specs=pl.BlockSpec((seq, LANES), lambda b, h: (b, h)),
        out_shape=jax.ShapeDtypeStruct((m, DIFF_HEADS * HEAD_DIM), BF16),
        scratch_shapes=[pltpu.VMEM((seq, 2 * LANES), BF16), pltpu.VMEM((LANES, seq), BF16),
                        pltpu.VMEM((t, 2 * t), F32)] + _flash_scratch(t, 2 * t),
        compiler_params=_params(("parallel", "parallel")),
        name="diff_attn",
    )(slabs, slabs, slabs, kx, qx_diff, lam_vec, subln)


def _compress_kernel(y_ref, pos_ref, w1_ref, w2_ref, o_ref):
    half = y_ref.shape[3]
    y = y_ref[0, 0].astype(F32)
    top = (y + pos_ref[0, 0:1, :]).astype(BF16)
    bot = (y + pos_ref[0, 1:2, :]).astype(BF16)
    a = _dot(top, w1_ref[0, 0:half, :])
    b = _dot(bot, w1_ref[0, half:2 * half, :])
    nrow = a.shape[0]
    hid = a + pltpu.roll(b, nrow - 1, 0)
    hid = hid * jax.nn.sigmoid(hid)
    o_ref[0, 0] = _dot(hid.astype(BF16), w2_ref[0]).astype(BF16)


def _compress(ykv, pos2, w1, w2, batch, l):
    nrow, half = ykv.shape[2], ykv.shape[3]
    return pl.pallas_call(
        _compress_kernel,
        grid=(4, batch),
        in_specs=[pl.BlockSpec((1, 1, nrow, half), lambda s, b: (s, b, 0, 0)),
                  pl.BlockSpec((None, 1, 2, half), lambda s, b: (l, s // 2, 0, 0)),
                  pl.BlockSpec((None, 1, 2 * half, CMP_HIDDEN), lambda s, b: (l, s // 2, 0, 0)),
                  pl.BlockSpec((None, 1, CMP_HIDDEN, HEAD_DIM), lambda s, b: (l, s // 2, 0, 0))],
        out_specs=pl.BlockSpec((1, 1, nrow, HEAD_DIM), lambda s, b: (s, b, 0, 0)),
        out_shape=jax.ShapeDtypeStruct((4, batch, nrow, HEAD_DIM), BF16),
        compiler_params=_params(("parallel", "parallel")),
        name="nsa_compress",
    )(ykv, pos2, w1, w2)


def _topk_bias(cand, topk):
    nsel, tq = cand.shape
    sub = 8
    blocks = [cand[b * sub:(b + 1) * sub] for b in range(nsel // sub)]
    jidx = lax.broadcasted_iota(jnp.int32, (sub, tq), 0)
    ranks = [jnp.zeros((sub, tq), F32) for _ in blocks]
    for k in range(nsel):
        rk = cand[k:k + 1, :]
        for b, cb in enumerate(blocks):
            if b * sub > k:
                beats = rk >= cb
            elif b * sub + sub - 1 < k:
                beats = rk > cb
            else:
                beats = (rk > cb) | ((rk == cb) & (jidx + b * sub > k))
            ranks[b] = ranks[b] + jnp.where(beats, 1.0, 0.0)
    rank = jnp.concatenate(ranks, axis=0)
    return jnp.where(rank < topk, 0.0, SEL_MASK)


def _nsa_kernel(q_ref, kc_ref, vc_ref, ks_ref, vs_ref, kw_ref, vw_ref, kx_ref, cx_ref, ovt_ref, qx_ref,
                misc_ref, o_ref, ksa_ref, kwa_ref, vst_ref, vwt_ref, vct_ref, gt_ref, dbias_ref, wbias_ref,
                comb_ref, *flash, tq, tk):
    g = pl.program_id(1)
    mcols = NSA_REP * tq
    nc = kc_ref.shape[2]
    seq = kx_ref.shape[0]
    nsel = seq // SEL_BLOCK
    span = WINDOW + tq

    _stage_kv(ksa_ref, vst_ref, ks_ref, kx_ref, vs_ref, tk)
    _stage_kv(kwa_ref, vwt_ref, kw_ref, kx_ref, vw_ref, tk, pad=WINDOW)
    pad_lane = lax.broadcasted_iota(jnp.int32, (WINDOW, 2 * LANES), 1)
    kwa_ref[0:WINDOW, :] = jnp.where(pad_lane == LANES + AUG_PAD, SEL_MASK, 0.0).astype(BF16)
    vwt_ref[:, 0:WINDOW] = jnp.zeros((LANES, WINDOW), BF16)
    vct_ref[...] = vc_ref[0, 0].astype(F32).T.astype(BF16)
    for o in range(tk // tq):
        dbias_ref[o] = _causal_bias(tk, mcols, tq, o * tq)
    rk = lax.broadcasted_iota(jnp.int32, (span, mcols), 0)
    rq = lax.broadcasted_iota(jnp.int32, (span, mcols), 1) & (tq - 1)
    wbias_ref[...] = jnp.where((rk > rq) & (rk <= rq + WINDOW), 0.0, NEG_INF)
    qx = jnp.concatenate([jnp.broadcast_to(qx_ref[0, r:r + 1, :], (tq, LANES)) for r in range(NSA_REP)], axis=0)
    kca = jnp.concatenate([kc_ref[0, 0], cx_ref[...]], axis=1)

    def qstep(qi, carry):
        _nsa_query_tile(qi, g, qx, kca, q_ref, ovt_ref, misc_ref, o_ref, ksa_ref, kwa_ref, vst_ref, vwt_ref,
                        vct_ref, gt_ref, dbias_ref, wbias_ref, comb_ref, flash, tq=tq, tk=tk, nc=nc, nsel=nsel)
        return carry

    lax.fori_loop(0, seq // tq, qstep, 0)


def _nsa_query_tile(qi, g, qx, kca, q_ref, ovt_ref, misc_ref, o_ref, ksa_ref, kwa_ref, vst_ref, vwt_ref, vct_ref,
                    gt_ref, dbias_ref, wbias_ref, comb_ref, flash, *, tq, tk, nc, nsel):
    mcols = NSA_REP * tq
    span = WINDOW + tq
    q0 = pl.multiple_of(qi * tq, tq)
    q4 = q_ref[:, pl.ds(q0, tq), :].reshape(mcols, LANES)
    qa = jnp.concatenate([q4, qx.astype(BF16)], axis=1)
    col = lax.broadcasted_iota(jnp.int32, (1, mcols), 1)
    col_pos = q0 + (col & (tq - 1))

    sc = _dot_nt(kca, qa)
    cend = lax.broadcasted_iota(jnp.int32, (nc, mcols), 0) * CMP_STRIDE + (CMP_BLOCK - 1)
    sc = jnp.where(cend <= col_pos, sc, NEG_INF)
    e = jnp.exp2(sc - jnp.max(sc, axis=0, keepdims=True))
    inv = jnp.where(col_pos >= CMP_BLOCK - 1, 1.0 / jnp.sum(e, axis=0, keepdims=True), 0.0)
    p = e * inv
    o_cmp = _dot(vct_ref[...], p.astype(BF16))

    psum = p[:, 0:tq]
    for r in range(1, NSA_REP):
        psum = psum + p[:, r * tq:(r + 1) * tq]
    p_hi = psum.astype(BF16)
    p_lo = (psum - p_hi.astype(F32)).astype(BF16)
    imp = _dot(ovt_ref[...], p_hi) + _dot(ovt_ref[...], p_lo)
    qpos = q0 + lax.broadcasted_iota(jnp.int32, (LANES, tq), 1)
    blk = lax.broadcasted_iota(jnp.int32, (LANES, tq), 0)
    cur = jnp.right_shift(qpos, SEL_BLOCK.bit_length() - 1)
    forced = (blk == 0) | (blk == cur) | (blk == cur - 1)
    imp = jnp.where(blk <= cur, jnp.where(forced, FORCE_SCORE, imp), -1.0)
    bias_t = _topk_bias(imp[0:nsel], min(SEL_TOPK, nsel))
    if nsel < LANES:
        bias_t = jnp.concatenate([bias_t, jnp.zeros((LANES - nsel, tq), F32)], axis=0)
    selbias = jnp.concatenate([bias_t.T] * NSA_REP, axis=0)
    lane4 = lax.broadcasted_iota(jnp.int32, (mcols, LANES), 1)
    qa_sel = jnp.concatenate([q4, jnp.where(lane4 < SEL_BLOCK, selbias, qx).astype(BF16)], axis=1)

    sw = _dot_nt(kwa_ref[pl.ds(q0, span), :], qa) + wbias_ref[...]
    ew = jnp.exp2(sw - jnp.max(sw, axis=0, keepdims=True))
    o_win = _dot(vwt_ref[:, pl.ds(q0, span)], ew.astype(BF16)) * (1.0 / jnp.sum(ew, axis=0, keepdims=True))

    gt_ref[...] = jax.nn.sigmoid(misc_ref[pl.ds(q0, tq), :]).T
    gate = lambda r, i: gt_ref[pl.ds(MISC_NG + 3 * (NSA_REP * g + r) + i, 1), :]
    for r in range(NSA_REP):
        sl = slice(r * tq, (r + 1) * tq)
        comb_ref[:, sl] = gate(r, 0) * o_cmp[:, sl] + gate(r, 2) * o_win[:, sl]

    o_sel = _flash_tiles(qa_sel, ksa_ref, vst_ref, q0 // tk, dbias_ref.at[qi & (tk // tq - 1)], tk, flash)
    for r in range(NSA_REP):
        sl = slice(r * tq, (r + 1) * tq)
        out = comb_ref[:, sl] + gate(r, 1) * o_sel[:, sl]
        o_ref[pl.ds(q0, tq), r * LANES:(r + 1) * LANES] = out.T.astype(BF16)


def _nsa(slabs, kvc, kx, cx_cmp, ovt, qx_nsa, misc, batch, seq):
    tq = _pick(seq, (256, 128))
    tk = _pick(seq, (512, 256, 128))
    assert tq & (tq - 1) == 0 and tk % tq == 0 and seq >= WINDOW + tq
    m = batch * seq
    mcols = NSA_REP * tq
    nc = kvc.shape[2]
    kv_spec = lambda base: pl.BlockSpec((1, seq, LANES), lambda b, g: (base + g, b, 0))
    const = lambda shape: pl.BlockSpec(shape, lambda b, g: (0,) * len(shape), pipeline_mode=pl.Buffered(1))
    return pl.pallas_call(
        functools.partial(_nsa_kernel, tq=tq, tk=tk),
        grid=(batch, NSA_KV_HEADS),
        in_specs=[pl.BlockSpec((NSA_REP, seq, LANES), lambda b, g: (SLAB_NQ // NSA_REP + g, b, 0)),
                  pl.BlockSpec((1, 1, nc, LANES), lambda b, g: (g, b, 0, 0)),
                  pl.BlockSpec((1, 1, nc, LANES), lambda b, g: (2 + g, b, 0, 0)),
                  kv_spec(SLAB_NKS), kv_spec(SLAB_NVS), kv_spec(SLAB_NKW), kv_spec(SLAB_NVW),
                  const((seq, LANES)), const((nc, LANES)), const((LANES, nc)),
                  pl.BlockSpec((1, NSA_REP, LANES), lambda b, g: (g, 0, 0)),
                  pl.BlockSpec((seq, LANES), lambda b, g: (b, 0))],
        out_specs=pl.BlockSpec((seq, NSA_REP * LANES), lambda b, g: (b, g)),
        out_shape=jax.ShapeDtypeStruct((m, NSA_HEADS * HEAD_DIM), BF16),
        scratch_shapes=[pltpu.VMEM((seq, 2 * LANES), BF16), pltpu.VMEM((WINDOW + seq, 2 * LANES), BF16),
                        pltpu.VMEM((LANES, seq), BF16), pltpu.VMEM((LANES, WINDOW + seq), BF16),
                        pltpu.VMEM((LANES, nc), BF16), pltpu.VMEM((LANES, tq), F32),
                        pltpu.VMEM((tk // tq, tk, mcols), F32), pltpu.VMEM((WINDOW + tq, mcols), F32),
                        pltpu.VMEM((LANES, mcols), F32)] + _flash_scratch(tk, mcols),
        compiler_params=_params(("parallel", "parallel")),
        name="nsa_attn",
    )(slabs, kvc, kvc, slabs, slabs, slabs, slabs, kx, cx_cmp, ovt, qx_nsa, misc)


def _merge_kernel(h_ref, of_ref, od_ref, on_ref, wg0_ref, wg1_ref, wg2_ref, wf_ref, wd_ref, wn_ref, o_ref):
    h = h_ref[...]
    acc = jax.nn.sigmoid(_dot(h, wg0_ref[...])) * _dot(of_ref[...], wf_ref[...])
    acc = acc + jax.nn.sigmoid(_dot(h, wg1_ref[...])) * _dot(od_ref[...], wd_ref[...])
    acc = acc + jax.nn.sigmoid(_dot(h, wg2_ref[...])) * _dot(on_ref[...], wn_ref[...])
    o_ref[...] = acc.astype(BF16)


def _merge(h, o_fox, o_diff, o_nsa, w_gate, wb_fox, wb_diff, wb_nsa, l):
    m, d = h.shape
    tm = _pick(m, (1024, 512, 256))
    tn = _pick(d, (256, 128))
    nj = d // tn
    row = lambda width: pl.BlockSpec((tm, width), lambda i, j: (i, 0))
    gate = lambda t: pl.BlockSpec((None, d, tn), lambda i, j: (l, 0, t * nj + j))
    col = lambda k: pl.BlockSpec((None, k, tn), lambda i, j: (l, 0, j))
    return pl.pallas_call(
        _merge_kernel,
        grid=(m // tm, nj),
        in_specs=[row(d), row(o_fox.shape[1]), row(o_diff.shape[1]), row(o_nsa.shape[1]),
                  gate(0), gate(1), gate(2),
                  col(wb_fox.shape[1]), col(wb_diff.shape[1]), col(wb_nsa.shape[1])],
        out_specs=pl.BlockSpec((tm, tn), lambda i, j: (i, j)),
        out_shape=jax.ShapeDtypeStruct((m, d), BF16),
        compiler_params=_params(("parallel", "arbitrary")),
        name="gate_merge",
    )(h, o_fox, o_diff, o_nsa, w_gate, w_gate, w_gate, wb_fox, wb_diff, wb_nsa)


def _wout_kernel(a_ref, w_ref, x_ref, gp_ref, gn_ref, xo_ref, ho_ref):
    y = _dot(a_ref[...], w_ref[...])
    x_new = x_ref[...] + _rms(y, gp_ref[...])
    xo_ref[...] = x_new
    ho_ref[...] = _rms(x_new, gn_ref[...]).astype(BF16)


def _wout(a, w, x, g_post, g_next, l, name):
    m, d = x.shape
    k = a.shape[1]
    tm = _pick(m, (512, 256, 128)) if k * d * 2 <= 8 * 1024 * 1024 else _pick(m, (256, 128))
    row = lambda width: pl.BlockSpec((tm, width), lambda i: (i, 0))
    vec = lambda: pl.BlockSpec((1, d), lambda i: (0, 0))
    return pl.pallas_call(
        _wout_kernel,
        grid=(m // tm,),
        in_specs=[row(k), pl.BlockSpec((None, k, d), lambda i: (l, 0, 0), pipeline_mode=pl.Buffered(1)),
                  row(d), vec(), vec()],
        out_specs=[row(d), row(d)],
        out_shape=[jax.ShapeDtypeStruct((m, d), F32), jax.ShapeDtypeStruct((m, d), BF16)],
        compiler_params=_params(("parallel",)),
        name=name,
    )(a, w, x, g_post, g_next)


def _ffn_up_kernel(h_ref, wg_ref, wu_ref, o_ref, wgb_ref, wub_ref):
    @pl.when(pl.program_id(1) == 0)
    def _():
        wgb_ref[...] = wg_ref[...].astype(BF16)
        wub_ref[...] = wu_ref[...].astype(BF16)

    h = h_ref[...]
    gate = _dot(h, wgb_ref[...])
    o_ref[...] = (gate * jax.nn.sigmoid(gate) * _dot(h, wub_ref[...])).astype(BF16)


def _ffn_up(h, w_up, l):
    m, d = h.shape
    dff = w_up.shape[2] // 2
    tm = _pick(m, (1024, 512, 256))
    tn = _pick(dff, (512, 256, 128))
    nj = dff // tn
    return pl.pallas_call(
        _ffn_up_kernel,
        grid=(nj, m // tm),
        in_specs=[pl.BlockSpec((tm, d), lambda j, i: (i, 0)),
                  pl.BlockSpec((None, d, tn), lambda j, i: (l, 0, j)),
                  pl.BlockSpec((None, d, tn), lambda j, i: (l, 0, nj + j))],
        out_specs=pl.BlockSpec((tm, tn), lambda j, i: (i, j)),
        out_shape=jax.ShapeDtypeStruct((m, dff), BF16),
        scratch_shapes=[pltpu.VMEM((d, tn), BF16), pltpu.VMEM((d, tn), BF16)],
        compiler_params=_params(("parallel", "arbitrary")),
        name="ffn_up",
    )(h, w_up, w_up)


def _pos_columns(pos):
    lane = jnp.arange(LANES)[None, :]
    hi = (lane >= AUG_HI) & (lane < AUG_HI + AUG_TERMS)
    lo = (lane >= AUG_LO) & (lane < AUG_LO + AUG_TERMS)
    return jnp.where(hi, (pos // LANES)[:, None], jnp.where(lo, (pos % LANES)[:, None], 0))


def _key_aug_table(seq):
    j = jnp.arange(seq)
    lane = jnp.arange(LANES)[None, :]
    onehot = (lane == (j // SEL_BLOCK)[:, None]) & (lane < SEL_BLOCK)
    return (_pos_columns(j) + onehot.astype(jnp.int32)).astype(BF16)


def _cmp_aug_table(nrow):
    return _pos_columns(jnp.arange(nrow) * CMP_STRIDE + CMP_BLOCK - 1).astype(BF16)


def _overlap_table_t(nrow, seq):
    start = jnp.arange(nrow)[None, :] * CMP_STRIDE
    blk = jnp.arange(LANES)[:, None]
    sel = blk * SEL_BLOCK
    ov = ((start < sel + SEL_BLOCK) & (start + CMP_BLOCK - 1 >= sel) & (blk < seq // SEL_BLOCK)
          & (jnp.arange(nrow)[None, :] < (seq - CMP_BLOCK) // CMP_STRIDE + 1))
    return ov.astype(BF16)


def _query_aug_rows(n_heads):
    slopes = 2.0 ** (-8.0 * jnp.arange(1, n_heads + 1, dtype=F32) / n_heads)
    terms = [t.astype(F32) for t in _split3(jnp.float32(LOG2E))]
    lane = jnp.arange(LANES)[None, :]
    out = jnp.where(lane == AUG_PAD, 1.0, jnp.zeros((n_heads, LANES), F32))
    for i, t in enumerate(terms):
        out = jnp.where(lane == AUG_HI + i, slopes[:, None] * t * LANES, out)
        out = jnp.where(lane == AUG_LO + i, slopes[:, None] * t, out)
    return out


def kernel(x, w_in, fox_forget_bias, diff_lambda, diff_subln, nsa_cmp_pos, nsa_cmp_w1, nsa_cmp_w2,
           w_branch_fox, w_branch_diff, w_branch_nsa, w_gate, w_out, norm_gains, w_ffn_up, w_ffn_down):
    batch, seq, d = x.shape
    depth = w_in.shape[0]
    m = batch * seq
    fw, dw, nw, kvw = FOX_HEADS * HEAD_DIM, DIFF_HEADS * HEAD_DIM, NSA_HEADS * HEAD_DIM, NSA_KV_HEADS * HEAD_DIM
    ff0 = 3 * fw
    dq0 = ff0 + FOX_HEADS
    ng0 = dq0 + 3 * dw + nw + 6 * kvw
    n_main = ng0 - FOX_HEADS
    assert n_main == N_SLABS * LANES and w_in.shape[2] == ng0 + 3 * NSA_HEADS

    w_in_b = w_in.astype(BF16)
    w_main = jnp.concatenate([w_in_b[:, :, :ff0], w_in_b[:, :, dq0:ng0]], axis=2)
    w_misc = jnp.concatenate([w_in_b[:, :, ff0:dq0], w_in_b[:, :, ng0:],
                              jnp.zeros((depth, d, LANES - FOX_HEADS - 3 * NSA_HEADS), BF16)], axis=2)
    colscale = jnp.ones((n_main,), F32)
    colscale = colscale.at[SLAB_FQ * LANES:SLAB_FK * LANES].set(HEAD_DIM ** -0.5 * LOG2E)
    colscale = colscale.at[SLAB_DQ * LANES:SLAB_DK * LANES].set(DIFF_HALF ** -0.5 * LOG2E)
    colscale = colscale.at[SLAB_NQ * LANES:SLAB_NKC * LANES].set(HEAD_DIM ** -0.5 * LOG2E)
    colscale = colscale[None, :]
    fbias = jnp.pad(fox_forget_bias.astype(F32), ((0, 0), (0, LANES - FOX_HEADS)))[:, None, :]
    half = CMP_STRIDE * HEAD_DIM
    pos2 = nsa_cmp_pos.astype(F32).reshape(depth, 2, 2, half)
    w1 = nsa_cmp_w1.astype(BF16)
    w2 = nsa_cmp_w2.astype(BF16)
    wbf, wbd, wbn = w_branch_fox.astype(BF16), w_branch_diff.astype(BF16), w_branch_nsa.astype(BF16)
    wg, wo = w_gate.astype(BF16), w_out.astype(BF16)
    wup, wdn = w_ffn_up.astype(F32), w_ffn_down.astype(BF16)
    gains = norm_gains.astype(F32)

    nrow = seq // CMP_STRIDE
    kx = _key_aug_table(seq)
    cx_cmp = _cmp_aug_table(nrow)
    ovt = _overlap_table_t(nrow, seq)
    qx_diff = _query_aug_rows(DIFF_HEADS)[:, None, :]
    qx_nsa = _query_aug_rows(NSA_HEADS).reshape(NSA_KV_HEADS, NSA_REP, LANES)

    xf = x.reshape(m, d).astype(F32)
    h = _norm(xf, gains[0, 0][None, :])
    for l in range(depth):
        lam_init = 0.8 - 0.6 * math.exp(-0.3 * l)
        slabs, misc = _inproj(h, w_main, colscale, w_misc, l)
        cx_fox = _logf(misc, fbias[l], batch, seq)
        o_fox = _fox(slabs, cx_fox, batch, seq)
        o_diff = _diff(slabs, kx, qx_diff, diff_lambda[l].astype(F32), diff_subln[l].astype(F32)[None, :],
                       batch, seq, lam_init)
        ykv = slabs[SLAB_NKC:SLAB_NKC + 4].reshape(4, batch, nrow, half)
        kvc = _compress(ykv, pos2, w1, w2, batch, l)
        o_nsa = _nsa(slabs, kvc, kx, cx_cmp, ovt, qx_nsa, misc, batch, seq)
        merged = _merge(h, o_fox, o_diff, o_nsa, wg, wbf, wbd, wbn, l)
        xf, h2 = _wout(merged, wo, xf, gains[l, 1][None, :], gains[l, 2][None, :], l, "out_proj")
        act = _ffn_up(h2, wup, l)
        g_next = gains[min(l + 1, depth - 1), 0][None, :]
        xf, h = _wout(act, wdn, xf, gains[l, 3][None, :], g_next, l, "ffn_down")
    return xf.reshape(batch, seq, d).astype(x.dtype)
```

```python
import functools
import math

import jax
import jax.numpy as jnp
from jax import lax
from jax.experimental import pallas as pl
from jax.experimental.pallas import tpu as pltpu

F32 = jnp.float32
BF16 = jnp.bfloat16

HEAD_DIM = 128
FOX_HEADS = 4
DIFF_HEADS = 4
DIFF_HALF = HEAD_DIM // 2
NSA_HEADS = 8
NSA_KV_HEADS = 2
NSA_REP = NSA_HEADS // NSA_KV_HEADS
CMP_BLOCK = 32
CMP_STRIDE = 16
CMP_HIDDEN = 256
SEL_BLOCK = 64
SEL_TOPK = 16
WINDOW = 512
N_BRANCHES = 3
EPS = 1e-6
NEG_INF = -1e30
FORCE_SCORE = 1e4
SEL_MASK = -32768.0
LOG2E = math.log2(math.e)
LANES = 128

SLAB_FQ, SLAB_FK, SLAB_FV = 0, 4, 8
SLAB_DQ, SLAB_DK, SLAB_DV = 12, 16, 20
SLAB_NQ = 24
SLAB_NKC, SLAB_NVC, SLAB_NKS, SLAB_NVS, SLAB_NKW, SLAB_NVW = 32, 34, 36, 38, 40, 42
N_SLABS = 44
MISC_FF = 0
MISC_NG = 4
AUG_HI = 64
AUG_LO = 67
AUG_TERMS = 3
AUG_PAD = 70

VMEM_LIMIT = 56 * 1024 * 1024


def _pick(n, prefs):
    for p in prefs:
        if p <= n and n % p == 0:
            return p
    return n


def _params(sem):
    return pltpu.CompilerParams(dimension_semantics=sem, vmem_limit_bytes=VMEM_LIMIT)


def _rms(y, g):
    return y * lax.rsqrt(jnp.mean(y * y, axis=-1, keepdims=True) + EPS) * g


def _dot(a, b):
    return jnp.dot(a, b, preferred_element_type=F32)


def _dot_nt(a, b):
    return lax.dot_general(a, b, (((1,), (1,)), ((), ())), preferred_element_type=F32)


def _split3(x):
    hi = x.astype(BF16)
    r = x - hi.astype(F32)
    mid = r.astype(BF16)
    lo = (r - mid.astype(F32)).astype(BF16)
    return hi, mid, lo


def _norm_kernel(x_ref, g_ref, h_ref):
    h_ref[...] = _rms(x_ref[...], g_ref[...]).astype(BF16)


def _norm(x, g):
    m, d = x.shape
    tm = _pick(m, (512, 256, 128))
    return pl.pallas_call(
        _norm_kernel,
        grid=(m // tm,),
        in_specs=[pl.BlockSpec((tm, d), lambda i: (i, 0)), pl.BlockSpec((1, d), lambda i: (0, 0))],
        out_specs=pl.BlockSpec((tm, d), lambda i: (i, 0)),
        out_shape=jax.ShapeDtypeStruct((m, d), BF16),
        compiler_params=_params(("parallel",)),
        name="norm_in",
    )(x, g)


def _inproj_kernel(h_ref, w_ref, cs_ref, wm_ref, o_ref, misc_ref):
    h = h_ref[...]
    acc = _dot(h, w_ref[...]) * cs_ref[...]
    for s in range(o_ref.shape[0]):
        o_ref[s] = acc[:, s * LANES:(s + 1) * LANES].astype(BF16)

    @pl.when(pl.program_id(1) == 0)
    def _():
        misc_ref[...] = _dot(h, wm_ref[...])


def _inproj(h, w_main, colscale, w_misc, l):
    m, d = h.shape
    n = w_main.shape[2]
    tm = _pick(m, (1024, 512, 256))
    tn = _pick(n, (11 * LANES, 4 * LANES))
    return pl.pallas_call(
        _inproj_kernel,
        grid=(m // tm, n // tn),
        in_specs=[pl.BlockSpec((tm, d), lambda i, j: (i, 0)),
                  pl.BlockSpec((None, d, tn), lambda i, j: (l, 0, j)),
                  pl.BlockSpec((1, tn), lambda i, j: (0, j)),
                  pl.BlockSpec((None, d, LANES), lambda i, j: (l, 0, 0))],
        out_specs=[pl.BlockSpec((tn // LANES, tm, LANES), lambda i, j: (j, i, 0)),
                   pl.BlockSpec((tm, LANES), lambda i, j: (i, 0))],
        out_shape=[jax.ShapeDtypeStruct((n // LANES, m, LANES), BF16),
                   jax.ShapeDtypeStruct((m, LANES), F32)],
        compiler_params=_params(("parallel", "arbitrary")),
        name="inproj",
    )(h, w_main, colscale, w_misc)


def _logf_kernel(misc_ref, bias_ref, o_ref, *, tc):
    s = misc_ref.shape[0]
    row = lax.broadcasted_iota(jnp.int32, (tc, tc), 0)
    col = lax.broadcasted_iota(jnp.int32, (tc, tc), 1)
    tri = jnp.where(col <= row, 1.0, 0.0).astype(BF16)
    lane = lax.broadcasted_iota(jnp.int32, (tc, LANES), 1)

    def chunk(c, carry):
        r0 = pl.multiple_of(c * tc, tc)
        z = misc_ref[pl.ds(r0, tc), :] + bias_ref[...]
        lf = jnp.minimum(z, 0.0) - jnp.log1p(jnp.exp(-jnp.abs(z)))
        hi, mid, lo = _split3(lf)
        cum = _dot(tri, hi) + _dot(tri, mid) + _dot(tri, lo) + carry
        for hd in range(FOX_HEADS):
            c2 = jnp.broadcast_to(cum[:, hd:hd + 1], (tc, LANES)) * LOG2E
            c_hi = c2.astype(BF16).astype(F32)
            c_mid = (c2 - c_hi).astype(BF16).astype(F32)
            aug = jnp.where(lane == 0, c_hi, jnp.where(lane == 1, c_mid,
                                                      jnp.where(lane == 2, c2 - c_hi - c_mid, 0.0)))
            o_ref[hd, pl.ds(r0, tc), :] = aug.astype(BF16)
        return cum[tc - 1:tc, :]

    lax.fori_loop(0, s // tc, chunk, jnp.zeros((1, LANES), F32))


def _logf(misc, bias_row, batch, seq):
    tc = _pick(seq, (256, 128))
    return pl.pallas_call(
        functools.partial(_logf_kernel, tc=tc),
        grid=(batch,),
        in_specs=[pl.BlockSpec((seq, LANES), lambda b: (b, 0)), pl.BlockSpec((1, LANES), lambda b: (0, 0))],
        out_specs=pl.BlockSpec((FOX_HEADS, seq, LANES), lambda b: (0, b, 0)),
        out_shape=jax.ShapeDtypeStruct((FOX_HEADS, batch * seq, LANES), BF16),
        compiler_params=_params(("parallel",)),
        name="fox_logf",
    )(misc, bias_row)


def _flash_scratch(tk, mcols):
    return [pltpu.VMEM((tk, mcols), F32), pltpu.VMEM((1, mcols), F32), pltpu.VMEM((1, mcols), F32),
            pltpu.VMEM((LANES, mcols), F32)]


def _causal_bias(tk, mcols, tq, offset):
    rk = lax.broadcasted_iota(jnp.int32, (tk, mcols), 0)
    rq = lax.broadcasted_iota(jnp.int32, (tk, mcols), 1) & (tq - 1)
    return jnp.where(rk <= rq + offset, 0.0, NEG_INF)


def _flash_tiles(qa, ka_ref, vt_ref, n_full, diag_bias_ref, tk, scratch, next_qa=None):
    s_ref, m_ref, l_ref, acc_ref = scratch

    def logits(j, queries=qa):
        k0 = pl.multiple_of(j * tk, tk)
        return _dot_nt(ka_ref[pl.ds(k0, tk), :], queries)

    def step(s, j):
        k0 = pl.multiple_of(j * tk, tk)
        m_prev = m_ref[...]
        m_new = jnp.maximum(m_prev, jnp.max(s, axis=0, keepdims=True))
        alpha = jnp.exp2(m_prev - m_new)
        p = jnp.exp2(s - m_new)
        l_ref[...] = alpha * l_ref[...] + jnp.sum(p, axis=0, keepdims=True)
        acc_ref[...] = alpha * acc_ref[...] + _dot(vt_ref[:, pl.ds(k0, tk)], p.astype(BF16))
        m_ref[...] = m_new

    m_ref[...] = jnp.full(m_ref.shape, NEG_INF, F32)
    l_ref[...] = jnp.zeros(l_ref.shape, F32)
    acc_ref[...] = jnp.zeros(acc_ref.shape, F32)
    if next_qa is None:
        s_ref[...] = logits(0)

    def full(j, c):
        s_next = logits(j + 1)
        step(s_ref[...], j)
        s_ref[...] = s_next
        return c

    lax.fori_loop(0, n_full, full, 0)
    if next_qa is None:
        step(s_ref[...] + diag_bias_ref[...], n_full)
    else:
        s_next = logits(0, next_qa)
        step(s_ref[...] + diag_bias_ref[...], n_full)
        s_ref[...] = s_next
    return acc_ref[...] * (1.0 / l_ref[...])


def _stage_kv(ka_ref, vt_ref, k_ref, kx_ref, v_ref, chunk, pad=0):
    def body(c, carry):
        r0 = pl.multiple_of(c * chunk, chunk)
        ka_ref[pl.ds(pad + r0, chunk), 0:LANES] = k_ref[0, pl.ds(r0, chunk), :]
        ka_ref[pl.ds(pad + r0, chunk), LANES:2 * LANES] = kx_ref[pl.ds(r0, chunk), :]
        vt_ref[:, pl.ds(pad + r0, chunk)] = v_ref[0, pl.ds(r0, chunk), :].astype(F32).T.astype(BF16)
        return carry

    lax.fori_loop(0, v_ref.shape[1] // chunk, body, 0)


def _fox_kernel(q_ref, k_ref, v_ref, cx_ref, o_ref, ka_ref, vt_ref, tri_ref, *flash, tq, tk):
    _stage_kv(ka_ref, vt_ref, k_ref, cx_ref.at[0], v_ref, tk)
    for o in range(tk // tq):
        tri_ref[o] = _causal_bias(tk, tq, tq, o * tq)
    lane = lax.broadcasted_iota(jnp.int32, (tq, LANES), 1)
    qx = jnp.where(lane < 3, -1.0, 0.0).astype(BF16)

    nq = q_ref.shape[1] // tq

    def queries(qi):
        q0 = pl.multiple_of(qi * tq, tq)
        return jnp.concatenate([q_ref[0, pl.ds(q0, tq), :], qx], axis=1)

    flash[0][...] = _dot_nt(ka_ref[0:tk, :], queries(0))

    def qstep(qi, c):
        o = _flash_tiles(queries(qi), ka_ref, vt_ref, (qi * tq) // tk, tri_ref.at[qi & (tk // tq - 1)], tk, flash,
                         next_qa=queries(jnp.minimum(qi + 1, nq - 1)))
        o_ref[pl.ds(pl.multiple_of(qi * tq, tq), tq), :] = o.T.astype(BF16)
        return c

    lax.fori_loop(0, nq, qstep, 0)


def _attn_tiles(seq):
    tq = _pick(seq, (512, 256, 128))
    tk = 2 * tq if seq % (2 * tq) == 0 else tq
    return tq, tk


def _fox(slabs, cx, batch, seq):
    tq, tk = _attn_tiles(seq)
    m = batch * seq
    head = lambda base: pl.BlockSpec((1, seq, LANES), lambda b, h: (base + h, b, 0))
    return pl.pallas_call(
        functools.partial(_fox_kernel, tq=tq, tk=tk),
        grid=(batch, FOX_HEADS),
        in_specs=[head(SLAB_FQ), head(SLAB_FK), head(SLAB_FV), head(0)],
        out_specs=pl.BlockSpec((seq, LANES), lambda b, h: (b, h)),
        out_shape=jax.ShapeDtypeStruct((m, FOX_HEADS * HEAD_DIM), BF16),
        scratch_shapes=[pltpu.VMEM((seq, 2 * LANES), BF16), pltpu.VMEM((LANES, seq), BF16),
                        pltpu.VMEM((tk // tq, tk, tq), F32)] + _flash_scratch(tk, tq),
        compiler_params=_params(("parallel", "parallel")),
        name="fox_attn",
    )(slabs, slabs, slabs, cx)


def _diff_kernel(q_ref, k_ref, v_ref, kx_ref, qx_ref, lam_ref, sub_ref, o_ref, ka_ref, vt_ref, tri_ref, *flash,
                 t, tk, lam_init):
    _stage_kv(ka_ref, vt_ref, k_ref, kx_ref, v_ref, tk)
    for o in range(tk // t):
        tri_ref[o] = _causal_bias(tk, 2 * t, t, o * t)
    lane = lax.broadcasted_iota(jnp.int32, (t, LANES), 1)
    qx = jnp.broadcast_to(qx_ref[0], (t, LANES)).astype(BF16)
    lv = lam_ref[...]
    lam = (jnp.exp(jnp.sum(lv[0:1] * lv[1:2], axis=-1, keepdims=True))
           - jnp.exp(jnp.sum(lv[2:3] * lv[3:4], axis=-1, keepdims=True)) + lam_init)

    nq = q_ref.shape[1] // t

    def queries(qi):
        q = q_ref[0, pl.ds(pl.multiple_of(qi * t, t), t), :].astype(F32)
        return jnp.concatenate([
            jnp.concatenate([jnp.where(lane < DIFF_HALF, q, 0.0).astype(BF16), qx], axis=1),
            jnp.concatenate([jnp.where(lane >= DIFF_HALF, q, 0.0).astype(BF16), qx], axis=1)], axis=0)

    flash[0][...] = _dot_nt(ka_ref[0:tk, :], queries(0))

    def qstep(qi, c):
        o = _flash_tiles(queries(qi), ka_ref, vt_ref, (qi * t) // tk, tri_ref.at[qi & (tk // t - 1)], tk, flash,
                         next_qa=queries(jnp.minimum(qi + 1, nq - 1)))
        o = (o[:, :t] - lam * o[:, t:]).T
        o_ref[pl.ds(pl.multiple_of(qi * t, t), t), :] = (_rms(o, sub_ref[...]) * (1.0 - lam_init)).astype(BF16)
        return c

    lax.fori_loop(0, nq, qstep, 0)


def _diff(slabs, kx, qx_diff, lam_vec, subln, batch, seq, lam_init):
    t, tk = _attn_tiles(seq)
    m = batch * seq
    head = lambda base: pl.BlockSpec((1, seq, LANES), lambda b, h: (base + h, b, 0))
    return pl.pallas_call(
        functools.partial(_diff_kernel, t=t, tk=tk, lam_init=lam_init),
        grid=(batch, DIFF_HEADS),
        in_specs=[head(SLAB_DQ), head(SLAB_DK), head(SLAB_DV),
                  pl.BlockSpec((seq, LANES), lambda b, h: (0, 0)),
                  pl.BlockSpec((1, 1, LANES), lambda b, h: (h, 0, 0)),
                  pl.BlockSpec((4, DIFF_HALF), lambda b, h: (0, 0)),
                  pl.BlockSpec((1, LANES), lambda b, h: (0, 0))],
        out_specs=pl.BlockSpec((seq, LANES), lambda b, h: (b, h)),
        out_shape=jax.ShapeDtypeStruct((m, DIFF_HEADS * HEAD_DIM), BF16),
        scratch_shapes=[pltpu.VMEM((seq, 2 * LANES), BF16), pltpu.VMEM((LANES, seq), BF16),
                        pltpu.VMEM((tk // t, tk, 2 * t), F32)] + _flash_scratch(tk, 2 * t),
        compiler_params=_params(("parallel", "parallel")),
        name="diff_attn",
    )(slabs, slabs, slabs, kx, qx_diff, lam_vec, subln)


def _compress_kernel(y_ref, pos_ref, w1_ref, w2_ref, o_ref):
    half = y_ref.shape[3]
    y = y_ref[0, 0].astype(F32)
    top = (y + pos_ref[0, 0:1, :]).astype(BF16)
    bot = (y + pos_ref[0, 1:2, :]).astype(BF16)
    a = _dot(top, w1_ref[0, 0:half, :])
    b = _dot(bot, w1_ref[0, half:2 * half, :])
    nrow = a.shape[0]
    hid = a + pltpu.roll(b, nrow - 1, 0)
    hid = hid * jax.nn.sigmoid(hid)
    o_ref[0, 0] = _dot(hid.astype(BF16), w2_ref[0]).astype(BF16)


def _compress(ykv, pos2, w1, w2, batch, l):
    nrow, half = ykv.shape[2], ykv.shape[3]
    return pl.pallas_call(
        _compress_kernel,
        grid=(4, batch),
        in_specs=[pl.BlockSpec((1, 1, nrow, half), lambda s, b: (s, b, 0, 0)),
                  pl.BlockSpec((None, 1, 2, half), lambda s, b: (l, s // 2, 0, 0)),
                  pl.BlockSpec((None, 1, 2 * half, CMP_HIDDEN), lambda s, b: (l, s // 2, 0, 0)),
                  pl.BlockSpec((None, 1, CMP_HIDDEN, HEAD_DIM), lambda s, b: (l, s // 2, 0, 0))],
        out_specs=pl.BlockSpec((1, 1, nrow, HEAD_DIM), lambda s, b: (s, b, 0, 0)),
        out_shape=jax.ShapeDtypeStruct((4, batch, nrow, HEAD_DIM), BF16),
        compiler_params=_params(("parallel", "parallel")),
        name="nsa_compress",
    )(ykv, pos2, w1, w2)


def _topk_bias(cand, topk):
    nsel, tq = cand.shape
    sub = 8
    blocks = [cand[b * sub:(b + 1) * sub] for b in range(nsel // sub)]
    jidx = lax.broadcasted_iota(jnp.int32, (sub, tq), 0)
    ranks = [jnp.zeros((sub, tq), F32) for _ in blocks]
    for k in range(nsel):
        rk = cand[k:k + 1, :]
        for b, cb in enumerate(blocks):
            if b * sub > k:
                beats = rk >= cb
            elif b * sub + sub - 1 < k:
                beats = rk > cb
            else:
                beats = (rk > cb) | ((rk == cb) & (jidx + b * sub > k))
            ranks[b] = ranks[b] + jnp.where(beats, 1.0, 0.0)
    rank = jnp.concatenate(ranks, axis=0)
    return jnp.where(rank < topk, 0.0, SEL_MASK)


def _nsa_kernel(q_ref, kc_ref, vc_ref, ks_ref, vs_ref, kw_ref, vw_ref, kx_ref, cx_ref, ovt_ref, qx_ref,
                misc_ref, o_ref, ksa_ref, kwa_ref, vst_ref, vwt_ref, vct_ref, gt_ref, dbias_ref, wbias_ref,
                comb_ref, *flash, tq, tk):
    g = pl.program_id(1)
    mcols = NSA_REP * tq
    nc = kc_ref.shape[2]
    seq = kx_ref.shape[0]
    nsel = seq // SEL_BLOCK
    span = WINDOW + tq

    _stage_kv(ksa_ref, vst_ref, ks_ref, kx_ref, vs_ref, tk)
    _stage_kv(kwa_ref, vwt_ref, kw_ref, kx_ref, vw_ref, tk, pad=WINDOW)
    pad_lane = lax.broadcasted_iota(jnp.int32, (WINDOW, 2 * LANES), 1)
    kwa_ref[0:WINDOW, :] = jnp.where(pad_lane == LANES + AUG_PAD, SEL_MASK, 0.0).astype(BF16)
    vwt_ref[:, 0:WINDOW] = jnp.zeros((LANES, WINDOW), BF16)
    vct_ref[...] = vc_ref[0, 0].astype(F32).T.astype(BF16)
    for o in range(tk // tq):
        dbias_ref[o] = _causal_bias(tk, mcols, tq, o * tq)
    rk = lax.broadcasted_iota(jnp.int32, (span, mcols), 0)
    rq = lax.broadcasted_iota(jnp.int32, (span, mcols), 1) & (tq - 1)
    wbias_ref[...] = jnp.where((rk > rq) & (rk <= rq + WINDOW), 0.0, NEG_INF)
    qx = jnp.concatenate([jnp.broadcast_to(qx_ref[0, r:r + 1, :], (tq, LANES)) for r in range(NSA_REP)], axis=0)
    kca = jnp.concatenate([kc_ref[0, 0], cx_ref[...]], axis=1)

    def qstep(qi, carry):
        _nsa_query_tile(qi, g, qx, kca, q_ref, ovt_ref, misc_ref, o_ref, ksa_ref, kwa_ref, vst_ref, vwt_ref,
                        vct_ref, gt_ref, dbias_ref, wbias_ref, comb_ref, flash, tq=tq, tk=tk, nc=nc, nsel=nsel)
        return carry

    lax.fori_loop(0, seq // tq, qstep, 0)


def _nsa_query_tile(qi, g, qx, kca, q_ref, ovt_ref, misc_ref, o_ref, ksa_ref, kwa_ref, vst_ref, vwt_ref, vct_ref,
                    gt_ref, dbias_ref, wbias_ref, comb_ref, flash, *, tq, tk, nc, nsel):
    mcols = NSA_REP * tq
    span = WINDOW + tq
    q0 = pl.multiple_of(qi * tq, tq)
    q4 = q_ref[:, pl.ds(q0, tq), :].reshape(mcols, LANES)
    qa = jnp.concatenate([q4, qx.astype(BF16)], axis=1)
    col = lax.broadcasted_iota(jnp.int32, (1, mcols), 1)
    col_pos = q0 + (col & (tq - 1))

    sc = _dot_nt(kca, qa)
    cend = lax.broadcasted_iota(jnp.int32, (nc, mcols), 0) * CMP_STRIDE + (CMP_BLOCK - 1)
    sc = jnp.where(cend <= col_pos, sc, NEG_INF)
    e = jnp.exp2(sc - jnp.max(sc, axis=0, keepdims=True))
    inv = jnp.where(col_pos >= CMP_BLOCK - 1, 1.0 / jnp.sum(e, axis=0, keepdims=True), 0.0)
    p = e * inv
    o_cmp = _dot(vct_ref[...], p.astype(BF16))

    psum = p[:, 0:tq]
    for r in range(1, NSA_REP):
        psum = psum + p[:, r * tq:(r + 1) * tq]
    p_hi = psum.astype(BF16)
    p_lo = (psum - p_hi.astype(F32)).astype(BF16)
    imp = _dot(ovt_ref[...], p_hi) + _dot(ovt_ref[...], p_lo)
    qpos = q0 + lax.broadcasted_iota(jnp.int32, (LANES, tq), 1)
    blk = lax.broadcasted_iota(jnp.int32, (LANES, tq), 0)
    cur = jnp.right_shift(qpos, SEL_BLOCK.bit_length() - 1)
    forced = (blk == 0) | (blk == cur) | (blk == cur - 1)
    imp = jnp.where(blk <= cur, jnp.where(forced, FORCE_SCORE, imp), -1.0)
    bias_t = _topk_bias(imp[0:nsel], min(SEL_TOPK, nsel))
    if nsel < LANES:
        bias_t = jnp.concatenate([bias_t, jnp.zeros((LANES - nsel, tq), F32)], axis=0)
    selbias = jnp.concatenate([bias_t.T] * NSA_REP, axis=0)
    lane4 = lax.broadcasted_iota(jnp.int32, (mcols, LANES), 1)
    qa_sel = jnp.concatenate([q4, jnp.where(lane4 < SEL_BLOCK, selbias, qx).astype(BF16)], axis=1)

    sw = _dot_nt(kwa_ref[pl.ds(q0, span), :], qa) + wbias_ref[...]
    ew = jnp.exp2(sw - jnp.max(sw, axis=0, keepdims=True))
    o_win = _dot(vwt_ref[:, pl.ds(q0, span)], ew.astype(BF16)) * (1.0 / jnp.sum(ew, axis=0, keepdims=True))

    gt_ref[...] = jax.nn.sigmoid(misc_ref[pl.ds(q0, tq), :]).T
    gate = lambda r, i: gt_ref[pl.ds(MISC_NG + 3 * (NSA_REP * g + r) + i, 1), :]
    for r in range(NSA_REP):
        sl = slice(r * tq, (r + 1) * tq)
        comb_ref[:, sl] = gate(r, 0) * o_cmp[:, sl] + gate(r, 2) * o_win[:, sl]

    o_sel = _flash_tiles(qa_sel, ksa_ref, vst_ref, q0 // tk, dbias_ref.at[qi & (tk // tq - 1)], tk, flash)
    for r in range(NSA_REP):
        sl = slice(r * tq, (r + 1) * tq)
        out = comb_ref[:, sl] + gate(r, 1) * o_sel[:, sl]
        o_ref[pl.ds(q0, tq), r * LANES:(r + 1) * LANES] = out.T.astype(BF16)


def _nsa(slabs, kvc, kx, cx_cmp, ovt, qx_nsa, misc, batch, seq):
    tq = _pick(seq, (256, 128))
    tk = _pick(seq, (512, 256, 128))
    assert tq & (tq - 1) == 0 and tk % tq == 0 and seq >= WINDOW + tq
    m = batch * seq
    mcols = NSA_REP * tq
    nc = kvc.shape[2]
    kv_spec = lambda base: pl.BlockSpec((1, seq, LANES), lambda b, g: (base + g, b, 0))
    const = lambda shape: pl.BlockSpec(shape, lambda b, g: (0,) * len(shape), pipeline_mode=pl.Buffered(1))
    return pl.pallas_call(
        functools.partial(_nsa_kernel, tq=tq, tk=tk),
        grid=(batch, NSA_KV_HEADS),
        in_specs=[pl.BlockSpec((NSA_REP, seq, LANES), lambda b, g: (SLAB_NQ // NSA_REP + g, b, 0)),
                  pl.BlockSpec((1, 1, nc, LANES), lambda b, g: (g, b, 0, 0)),
                  pl.BlockSpec((1, 1, nc, LANES), lambda b, g: (2 + g, b, 0, 0)),
                  kv_spec(SLAB_NKS), kv_spec(SLAB_NVS), kv_spec(SLAB_NKW), kv_spec(SLAB_NVW),
                  const((seq, LANES)), const((nc, LANES)), const((LANES, nc)),
                  pl.BlockSpec((1, NSA_REP, LANES), lambda b, g: (g, 0, 0)),
                  pl.BlockSpec((seq, LANES), lambda b, g: (b, 0))],
        out_specs=pl.BlockSpec((seq, NSA_REP * LANES), lambda b, g: (b, g)),
        out_shape=jax.ShapeDtypeStruct((m, NSA_HEADS * HEAD_DIM), BF16),
        scratch_shapes=[pltpu.VMEM((seq, 2 * LANES), BF16), pltpu.VMEM((WINDOW + seq, 2 * LANES), BF16),
                        pltpu.VMEM((LANES, seq), BF16), pltpu.VMEM((LANES, WINDOW + seq), BF16),
                        pltpu.VMEM((LANES, nc), BF16), pltpu.VMEM((LANES, tq), F32),
                        pltpu.VMEM((tk // tq, tk, mcols), F32), pltpu.VMEM((WINDOW + tq, mcols), F32),
                        pltpu.VMEM((LANES, mcols), F32)] + _flash_scratch(tk, mcols),
        compiler_params=_params(("parallel", "parallel")),
        name="nsa_attn",
    )(slabs, kvc, kvc, slabs, slabs, slabs, slabs, kx, cx_cmp, ovt, qx_nsa, misc)


def _merge_kernel(h_ref, of_ref, od_ref, on_ref, wg0_ref, wg1_ref, wg2_ref, wf_ref, wd_ref, wn_ref, o_ref):
    h = h_ref[...]
    acc = jax.nn.sigmoid(_dot(h, wg0_ref[...])) * _dot(of_ref[...], wf_ref[...])
    acc = acc + jax.nn.sigmoid(_dot(h, wg1_ref[...])) * _dot(od_ref[...], wd_ref[...])
    acc = acc + jax.nn.sigmoid(_dot(h, wg2_ref[...])) * _dot(on_ref[...], wn_ref[...])
    o_ref[...] = acc.astype(BF16)


def _merge(h, o_fox, o_diff, o_nsa, w_gate, wb_fox, wb_diff, wb_nsa, l):
    m, d = h.shape
    tm = _pick(m, (1024, 512, 256))
    tn = _pick(d, (256, 128))
    nj = d // tn
    row = lambda width: pl.BlockSpec((tm, width), lambda i, j: (i, 0))
    gate = lambda t: pl.BlockSpec((None, d, tn), lambda i, j: (l, 0, t * nj + j))
    col = lambda k: pl.BlockSpec((None, k, tn), lambda i, j: (l, 0, j))
    return pl.pallas_call(
        _merge_kernel,
        grid=(m // tm, nj),
        in_specs=[row(d), row(o_fox.shape[1]), row(o_diff.shape[1]), row(o_nsa.shape[1]),
                  gate(0), gate(1), gate(2),
                  col(wb_fox.shape[1]), col(wb_diff.shape[1]), col(wb_nsa.shape[1])],
        out_specs=pl.BlockSpec((tm, tn), lambda i, j: (i, j)),
        out_shape=jax.ShapeDtypeStruct((m, d), BF16),
        compiler_params=_params(("parallel", "arbitrary")),
        name="gate_merge",
    )(h, o_fox, o_diff, o_nsa, w_gate, w_gate, w_gate, wb_fox, wb_diff, wb_nsa)


def _wout_kernel(a_ref, w_ref, x_ref, gp_ref, gn_ref, xo_ref, ho_ref):
    y = _dot(a_ref[...], w_ref[...])
    x_new = x_ref[...] + _rms(y, gp_ref[...])
    xo_ref[...] = x_new
    ho_ref[...] = _rms(x_new, gn_ref[...]).astype(BF16)


def _wout(a, w, x, g_post, g_next, l, name):
    m, d = x.shape
    k = a.shape[1]
    tm = _pick(m, (512, 256, 128)) if k * d * 2 <= 8 * 1024 * 1024 else _pick(m, (256, 128))
    row = lambda width: pl.BlockSpec((tm, width), lambda i: (i, 0))
    vec = lambda: pl.BlockSpec((1, d), lambda i: (0, 0))
    return pl.pallas_call(
        _wout_kernel,
        grid=(m // tm,),
        in_specs=[row(k), pl.BlockSpec((None, k, d), lambda i: (l, 0, 0), pipeline_mode=pl.Buffered(1)),
                  row(d), vec(), vec()],
        out_specs=[row(d), row(d)],
        out_shape=[jax.ShapeDtypeStruct((m, d), F32), jax.ShapeDtypeStruct((m, d), BF16)],
        compiler_params=_params(("parallel",)),
        name=name,
    )(a, w, x, g_post, g_next)


def _ffn_up_kernel(h_ref, wg_ref, wu_ref, o_ref, wgb_ref, wub_ref):
    @pl.when(pl.program_id(1) == 0)
    def _():
        wgb_ref[...] = wg_ref[...].astype(BF16)
        wub_ref[...] = wu_ref[...].astype(BF16)

    h = h_ref[...]
    gate = _dot(h, wgb_ref[...])
    o_ref[...] = (gate * jax.nn.sigmoid(gate) * _dot(h, wub_ref[...])).astype(BF16)


def _ffn_up(h, w_up, l):
    m, d = h.shape
    dff = w_up.shape[2] // 2
    tm = _pick(m, (1024, 512, 256))
    tn = _pick(dff, (512, 256, 128))
    nj = dff // tn
    return pl.pallas_call(
        _ffn_up_kernel,
        grid=(nj, m // tm),
        in_specs=[pl.BlockSpec((tm, d), lambda j, i: (i, 0)),
                  pl.BlockSpec((None, d, tn), lambda j, i: (l, 0, j)),
                  pl.BlockSpec((None, d, tn), lambda j, i: (l, 0, nj + j))],
        out_specs=pl.BlockSpec((tm, tn), lambda j, i: (i, j)),
        out_shape=jax.ShapeDtypeStruct((m, dff), BF16),
        scratch_shapes=[pltpu.VMEM((d, tn), BF16), pltpu.VMEM((d, tn), BF16)],
        compiler_params=_params(("parallel", "arbitrary")),
        name="ffn_up",
    )(h, w_up, w_up)


def _pos_columns(pos):
    lane = jnp.arange(LANES)[None, :]
    hi = (lane >= AUG_HI) & (lane < AUG_HI + AUG_TERMS)
    lo = (lane >= AUG_LO) & (lane < AUG_LO + AUG_TERMS)
    return jnp.where(hi, (pos // LANES)[:, None], jnp.where(lo, (pos % LANES)[:, None], 0))


def _key_aug_table(seq):
    j = jnp.arange(seq)
    lane = jnp.arange(LANES)[None, :]
    onehot = (lane == (j // SEL_BLOCK)[:, None]) & (lane < SEL_BLOCK)
    return (_pos_columns(j) + onehot.astype(jnp.int32)).astype(BF16)


def _cmp_aug_table(nrow):
    return _pos_columns(jnp.arange(nrow) * CMP_STRIDE + CMP_BLOCK - 1).astype(BF16)


def _overlap_table_t(nrow, seq):
    start = jnp.arange(nrow)[None, :] * CMP_STRIDE
    blk = jnp.arange(LANES)[:, None]
    sel = blk * SEL_BLOCK
    ov = ((start < sel + SEL_BLOCK) & (start + CMP_BLOCK - 1 >= sel) & (blk < seq // SEL_BLOCK)
          & (jnp.arange(nrow)[None, :] < (seq - CMP_BLOCK) // CMP_STRIDE + 1))
    return ov.astype(BF16)


def _query_aug_rows(n_heads):
    slopes = 2.0 ** (-8.0 * jnp.arange(1, n_heads + 1, dtype=F32) / n_heads)
    terms = [t.astype(F32) for t in _split3(jnp.float32(LOG2E))]
    lane = jnp.arange(LANES)[None, :]
    out = jnp.where(lane == AUG_PAD, 1.0, jnp.zeros((n_heads, LANES), F32))
    for i, t in enumerate(terms):
        out = jnp.where(lane == AUG_HI + i, slopes[:, None] * t * LANES, out)
        out = jnp.where(lane == AUG_LO + i, slopes[:, None] * t, out)
    return out


def kernel(x, w_in, fox_forget_bias, diff_lambda, diff_subln, nsa_cmp_pos, nsa_cmp_w1, nsa_cmp_w2,
           w_branch_fox, w_branch_diff, w_branch_nsa, w_gate, w_out, norm_gains, w_ffn_up, w_ffn_down):
    batch, seq, d = x.shape
    depth = w_in.shape[0]
    m = batch * seq
    fw, dw, nw, kvw = FOX_HEADS * HEAD_DIM, DIFF_HEADS * HEAD_DIM, NSA_HEADS * HEAD_DIM, NSA_KV_HEADS * HEAD_DIM
    ff0 = 3 * fw
    dq0 = ff0 + FOX_HEADS
    ng0 = dq0 + 3 * dw + nw + 6 * kvw
    n_main = ng0 - FOX_HEADS
    assert n_main == N_SLABS * LANES and w_in.shape[2] == ng0 + 3 * NSA_HEADS

    w_main = jnp.concatenate([w_in[:, :, :ff0], w_in[:, :, dq0:ng0]], axis=2).astype(BF16)
    w_misc = jnp.concatenate([w_in[:, :, ff0:dq0], w_in[:, :, ng0:],
                              jnp.zeros((depth, d, LANES - FOX_HEADS - 3 * NSA_HEADS), F32)], axis=2).astype(BF16)
    colscale = jnp.ones((n_main,), F32)
    colscale = colscale.at[SLAB_FQ * LANES:SLAB_FK * LANES].set(HEAD_DIM ** -0.5 * LOG2E)
    colscale = colscale.at[SLAB_DQ * LANES:SLAB_DK * LANES].set(DIFF_HALF ** -0.5 * LOG2E)
    colscale = colscale.at[SLAB_NQ * LANES:SLAB_NKC * LANES].set(HEAD_DIM ** -0.5 * LOG2E)
    colscale = colscale[None, :]
    fbias = jnp.pad(fox_forget_bias.astype(F32), ((0, 0), (0, LANES - FOX_HEADS)))[:, None, :]
    half = CMP_STRIDE * HEAD_DIM
    pos2 = nsa_cmp_pos.astype(F32).reshape(depth, 2, 2, half)
    w1 = nsa_cmp_w1.astype(BF16)
    w2 = nsa_cmp_w2.astype(BF16)
    wbf, wbd, wbn = w_branch_fox.astype(BF16), w_branch_diff.astype(BF16), w_branch_nsa.astype(BF16)
    wg, wo = w_gate.astype(BF16), w_out.astype(BF16)
    wup, wdn = w_ffn_up.astype(F32), w_ffn_down.astype(BF16)
    gains = norm_gains.astype(F32)

    nrow = seq // CMP_STRIDE
    kx = _key_aug_table(seq)
    cx_cmp = _cmp_aug_table(nrow)
    ovt = _overlap_table_t(nrow, seq)
    qx_diff = _query_aug_rows(DIFF_HEADS)[:, None, :]
    qx_nsa = _query_aug_rows(NSA_HEADS).reshape(NSA_KV_HEADS, NSA_REP, LANES)

    xf = x.reshape(m, d).astype(F32)
    h = _norm(xf, gains[0, 0][None, :])
    for l in range(depth):
        lam_init = 0.8 - 0.6 * math.exp(-0.3 * l)
        slabs, misc = _inproj(h, w_main, colscale, w_misc, l)
        cx_fox = _logf(misc, fbias[l], batch, seq)
        o_fox = _fox(slabs, cx_fox, batch, seq)
        o_diff = _diff(slabs, kx, qx_diff, diff_lambda[l].astype(F32), diff_subln[l].astype(F32)[None, :],
                       batch, seq, lam_init)
        ykv = slabs[SLAB_NKC:SLAB_NKC + 4].reshape(4, batch, nrow, half)
        kvc = _compress(ykv, pos2, w1, w2, batch, l)
        o_nsa = _nsa(slabs, kvc, kx, cx_cmp, ovt, qx_nsa, misc, batch, seq)
        merged = _merge(h, o_fox, o_diff, o_nsa, wg, wbf, wbd, wbn, l)
        xf, h2 = _wout(merged, wo, xf, gains[l, 1][None, :], gains[l, 2][None, :], l, "out_proj")
        act = _ffn_up(h2, wup, l)
        g_next = gains[min(l + 1, depth - 1), 0][None, :]
        xf, h = _wout(act, wdn, xf, gains[l, 3][None, :], g_next, l, "ffn_down")
    return xf.reshape(batch, seq, d).astype(x.dtype)
```

```python
import functools
import math

import jax
import jax.numpy as jnp
from jax import lax
from jax.experimental import pallas as pl
from jax.experimental.pallas import tpu as pltpu

F32 = jnp.float32
BF16 = jnp.bfloat16

HEAD_DIM = 128
FOX_HEADS = 4
DIFF_HEADS = 4
DIFF_HALF = HEAD_DIM // 2
NSA_HEADS = 8
NSA_KV_HEADS = 2
NSA_REP = NSA_HEADS // NSA_KV_HEADS
CMP_BLOCK = 32
CMP_STRIDE = 16
CMP_HIDDEN = 256
SEL_BLOCK = 64
SEL_TOPK = 16
WINDOW = 512
N_BRANCHES = 3
EPS = 1e-6
NEG_INF = -1e30
FORCE_SCORE = 1e4
SEL_MASK = -32768.0
LOG2E = math.log2(math.e)
LANES = 128

SLAB_FQ, SLAB_FK, SLAB_FV = 0, 4, 8
SLAB_DQ, SLAB_DK, SLAB_DV = 12, 16, 20
SLAB_NQ = 24
SLAB_NKC, SLAB_NVC, SLAB_NKS, SLAB_NVS, SLAB_NKW, SLAB_NVW = 32, 34, 36, 38, 40, 42
N_SLABS = 44
MISC_FF = 0
MISC_NG = 4
AUG_HI = 64
AUG_LO = 67
AUG_TERMS = 3
AUG_PAD = 70

VMEM_LIMIT = 56 * 1024 * 1024


def _pick(n, prefs):
    for p in prefs:
        if p <= n and n % p == 0:
            return p
    return n


def _params(sem):
    return pltpu.CompilerParams(dimension_semantics=sem, vmem_limit_bytes=VMEM_LIMIT)


def _rms(y, g):
    return y * lax.rsqrt(jnp.mean(y * y, axis=-1, keepdims=True) + EPS) * g


def _dot(a, b):
    return jnp.dot(a, b, preferred_element_type=F32)


def _dot_nt(a, b):
    return lax.dot_general(a, b, (((1,), (1,)), ((), ())), preferred_element_type=F32)


def _split3(x):
    hi = x.astype(BF16)
    r = x - hi.astype(F32)
    mid = r.astype(BF16)
    lo = (r - mid.astype(F32)).astype(BF16)
    return hi, mid, lo


def _norm_kernel(x_ref, g_ref, h_ref):
    h_ref[...] = _rms(x_ref[...], g_ref[...]).astype(BF16)


def _norm(x, g):
    m, d = x.shape
    tm = _pick(m, (512, 256, 128))
    return pl.pallas_call(
        _norm_kernel,
        grid=(m // tm,),
        in_specs=[pl.BlockSpec((tm, d), lambda i: (i, 0)), pl.BlockSpec((1, d), lambda i: (0, 0))],
        out_specs=pl.BlockSpec((tm, d), lambda i: (i, 0)),
        out_shape=jax.ShapeDtypeStruct((m, d), BF16),
        compiler_params=_params(("parallel",)),
        name="norm_in",
    )(x, g)


def _inproj_kernel(h_ref, w_ref, cs_ref, wm_ref, o_ref, misc_ref):
    h = h_ref[...]
    acc = _dot(h, w_ref[...]) * cs_ref[...]
    for s in range(o_ref.shape[0]):
        o_ref[s] = acc[:, s * LANES:(s + 1) * LANES].astype(BF16)

    @pl.when(pl.program_id(1) == 0)
    def _():
        misc_ref[...] = _dot(h, wm_ref[...])


def _inproj(h, w_main, colscale, w_misc, l):
    m, d = h.shape
    n = w_main.shape[2]
    tm = _pick(m, (1024, 512, 256))
    tn = _pick(n, (11 * LANES, 4 * LANES))
    return pl.pallas_call(
        _inproj_kernel,
        grid=(m // tm, n // tn),
        in_specs=[pl.BlockSpec((tm, d), lambda i, j: (i, 0)),
                  pl.BlockSpec((None, d, tn), lambda i, j: (l, 0, j)),
                  pl.BlockSpec((1, tn), lambda i, j: (0, j)),
                  pl.BlockSpec((None, d, LANES), lambda i, j: (l, 0, 0))],
        out_specs=[pl.BlockSpec((tn // LANES, tm, LANES), lambda i, j: (j, i, 0)),
                   pl.BlockSpec((tm, LANES), lambda i, j: (i, 0))],
        out_shape=[jax.ShapeDtypeStruct((n // LANES, m, LANES), BF16),
                   jax.ShapeDtypeStruct((m, LANES), F32)],
        compiler_params=_params(("parallel", "arbitrary")),
        name="inproj",
    )(h, w_main, colscale, w_misc)


def _logf_kernel(misc_ref, bias_ref, o_ref, *, tc):
    s = misc_ref.shape[0]
    row = lax.broadcasted_iota(jnp.int32, (tc, tc), 0)
    col = lax.broadcasted_iota(jnp.int32, (tc, tc), 1)
    tri = jnp.where(col <= row, 1.0, 0.0).astype(BF16)
    lane = lax.broadcasted_iota(jnp.int32, (tc, LANES), 1)

    def chunk(c, carry):
        r0 = pl.multiple_of(c * tc, tc)
        z = misc_ref[pl.ds(r0, tc), :] + bias_ref[...]
        lf = jnp.minimum(z, 0.0) - jnp.log1p(jnp.exp(-jnp.abs(z)))
        hi, mid, lo = _split3(lf)
        cum = _dot(tri, hi) + _dot(tri, mid) + _dot(tri, lo) + carry
        for hd in range(FOX_HEADS):
            c2 = jnp.broadcast_to(cum[:, hd:hd + 1], (tc, LANES)) * LOG2E
            c_hi = c2.astype(BF16).astype(F32)
            c_mid = (c2 - c_hi).astype(BF16).astype(F32)
            aug = jnp.where(lane == 0, c_hi, jnp.where(lane == 1, c_mid,
                                                      jnp.where(lane == 2, c2 - c_hi - c_mid, 0.0)))
            o_ref[hd, pl.ds(r0, tc), :] = aug.astype(BF16)
        return cum[tc - 1:tc, :]

    lax.fori_loop(0, s // tc, chunk, jnp.zeros((1, LANES), F32))


def _logf(misc, bias_row, batch, seq):
    tc = _pick(seq, (256, 128))
    return pl.pallas_call(
        functools.partial(_logf_kernel, tc=tc),
        grid=(batch,),
        in_specs=[pl.BlockSpec((seq, LANES), lambda b: (b, 0)), pl.BlockSpec((1, LANES), lambda b: (0, 0))],
        out_specs=pl.BlockSpec((FOX_HEADS, seq, LANES), lambda b: (0, b, 0)),
        out_shape=jax.ShapeDtypeStruct((FOX_HEADS, batch * seq, LANES), BF16),
        compiler_params=_params(("parallel",)),
        name="fox_logf",
    )(misc, bias_row)


def _flash_scratch(tk, mcols):
    return [pltpu.VMEM((tk, mcols), F32), pltpu.VMEM((1, mcols), F32), pltpu.VMEM((1, mcols), F32),
            pltpu.VMEM((LANES, mcols), F32)]


def _causal_bias(tk, mcols, tq, offset):
    rk = lax.broadcasted_iota(jnp.int32, (tk, mcols), 0)
    rq = lax.broadcasted_iota(jnp.int32, (tk, mcols), 1) & (tq - 1)
    return jnp.where(rk <= rq + offset, 0.0, NEG_INF)


def _flash_tiles(qa, ka_ref, vt_ref, n_full, diag_bias_ref, tk, scratch, next_qa=None):
    s_ref, m_ref, l_ref, acc_ref = scratch

    def logits(j, queries=qa):
        k0 = pl.multiple_of(j * tk, tk)
        return _dot_nt(ka_ref[pl.ds(k0, tk), :], queries)

    def step(s, j):
        k0 = pl.multiple_of(j * tk, tk)
        m_prev = m_ref[...]
        m_new = jnp.maximum(m_prev, jnp.max(s, axis=0, keepdims=True))
        alpha = jnp.exp2(m_prev - m_new)
        p = jnp.exp2(s - m_new)
        l_ref[...] = alpha * l_ref[...] + jnp.sum(p, axis=0, keepdims=True)
        acc_ref[...] = alpha * acc_ref[...] + _dot(vt_ref[:, pl.ds(k0, tk)], p.astype(BF16))
        m_ref[...] = m_new

    m_ref[...] = jnp.full(m_ref.shape, NEG_INF, F32)
    l_ref[...] = jnp.zeros(l_ref.shape, F32)
    acc_ref[...] = jnp.zeros(acc_ref.shape, F32)
    if next_qa is None:
        s_ref[...] = logits(0)

    def full(j, c):
        s_next = logits(j + 1)
        step(s_ref[...], j)
        s_ref[...] = s_next
        return c

    lax.fori_loop(0, n_full, full, 0)
    if next_qa is None:
        step(s_ref[...] + diag_bias_ref[...], n_full)
    else:
        s_next = logits(0, next_qa)
        step(s_ref[...] + diag_bias_ref[...], n_full)
        s_ref[...] = s_next
    return acc_ref[...] * (1.0 / l_ref[...])


def _stage_kv(ka_ref, vt_ref, k_ref, kx_ref, v_ref, chunk, pad=0):
    def body(c, carry):
        r0 = pl.multiple_of(c * chunk, chunk)
        ka_ref[pl.ds(pad + r0, chunk), 0:LANES] = k_ref[0, pl.ds(r0, chunk), :]
        ka_ref[pl.ds(pad + r0, chunk), LANES:2 * LANES] = kx_ref[pl.ds(r0, chunk), :]
        vt_ref[:, pl.ds(pad + r0, chunk)] = v_ref[0, pl.ds(r0, chunk), :].astype(F32).T.astype(BF16)
        return carry

    lax.fori_loop(0, v_ref.shape[1] // chunk, body, 0)


def _fox_kernel(q_ref, k_ref, v_ref, cx_ref, o_ref, ka_ref, vt_ref, tri_ref, *flash, tq, tk):
    _stage_kv(ka_ref, vt_ref, k_ref, cx_ref.at[0], v_ref, tk)
    for o in range(tk // tq):
        tri_ref[o] = _causal_bias(tk, tq, tq, o * tq)
    lane = lax.broadcasted_iota(jnp.int32, (tq, LANES), 1)
    qx = jnp.where(lane < 3, -1.0, 0.0).astype(BF16)

    nq = q_ref.shape[1] // tq

    def queries(qi):
        q0 = pl.multiple_of(qi * tq, tq)
        return jnp.concatenate([q_ref[0, pl.ds(q0, tq), :], qx], axis=1)

    flash[0][...] = _dot_nt(ka_ref[0:tk, :], queries(0))

    def qstep(qi, c):
        o = _flash_tiles(queries(qi), ka_ref, vt_ref, (qi * tq) // tk, tri_ref.at[qi & (tk // tq - 1)], tk, flash,
                         next_qa=queries(jnp.minimum(qi + 1, nq - 1)))
        o_ref[pl.ds(pl.multiple_of(qi * tq, tq), tq), :] = o.T.astype(BF16)
        return c

    lax.fori_loop(0, nq, qstep, 0)


def _attn_tiles(seq):
    tq = _pick(seq, (512, 256, 128))
    tk = 2 * tq if seq % (2 * tq) == 0 else tq
    return tq, tk


def _fox(slabs, cx, batch, seq):
    tk = _attn_tiles(seq)[1]
    tq = tk
    m = batch * seq
    head = lambda base: pl.BlockSpec((1, seq, LANES), lambda b, h: (base + h, b, 0))
    return pl.pallas_call(
        functools.partial(_fox_kernel, tq=tq, tk=tk),
        grid=(batch, FOX_HEADS),
        in_specs=[head(SLAB_FQ), head(SLAB_FK), head(SLAB_FV), head(0)],
        out_specs=pl.BlockSpec((seq, LANES), lambda b, h: (b, h)),
        out_shape=jax.ShapeDtypeStruct((m, FOX_HEADS * HEAD_DIM), BF16),
        scratch_shapes=[pltpu.VMEM((seq, 2 * LANES), BF16), pltpu.VMEM((LANES, seq), BF16),
                        pltpu.VMEM((tk // tq, tk, tq), F32)] + _flash_scratch(tk, tq),
        compiler_params=_params(("parallel", "parallel")),
        name="fox_attn",
    )(slabs, slabs, slabs, cx)


def _diff_kernel(q_ref, k_ref, v_ref, kx_ref, qx_ref, lam_ref, sub_ref, o_ref, ka_ref, vt_ref, tri_ref, *flash,
                 t, tk, lam_init):
    _stage_kv(ka_ref, vt_ref, k_ref, kx_ref, v_ref, tk)
    for o in range(tk // t):
        tri_ref[o] = _causal_bias(tk, 2 * t, t, o * t)
    lane = lax.broadcasted_iota(jnp.int32, (t, LANES), 1)
    qx = jnp.broadcast_to(qx_ref[0], (t, LANES)).astype(BF16)
    lv = lam_ref[...]
    lam = (jnp.exp(jnp.sum(lv[0:1] * lv[1:2], axis=-1, keepdims=True))
           - jnp.exp(jnp.sum(lv[2:3] * lv[3:4], axis=-1, keepdims=True)) + lam_init)

    nq = q_ref.shape[1] // t

    def queries(qi):
        q = q_ref[0, pl.ds(pl.multiple_of(qi * t, t), t), :].astype(F32)
        return jnp.concatenate([
            jnp.concatenate([jnp.where(lane < DIFF_HALF, q, 0.0).astype(BF16), qx], axis=1),
            jnp.concatenate([jnp.where(lane >= DIFF_HALF, q, 0.0).astype(BF16), qx], axis=1)], axis=0)

    flash[0][...] = _dot_nt(ka_ref[0:tk, :], queries(0))

    def qstep(qi, c):
        o = _flash_tiles(queries(qi), ka_ref, vt_ref, (qi * t) // tk, tri_ref.at[qi & (tk // t - 1)], tk, flash,
                         next_qa=queries(jnp.minimum(qi + 1, nq - 1)))
        o = (o[:, :t] - lam * o[:, t:]).T
        o_ref[pl.ds(pl.multiple_of(qi * t, t), t), :] = (_rms(o, sub_ref[...]) * (1.0 - lam_init)).astype(BF16)
        return c

    lax.fori_loop(0, nq, qstep, 0)


def _diff(slabs, kx, qx_diff, lam_vec, subln, batch, seq, lam_init):
    t, tk = _attn_tiles(seq)
    m = batch * seq
    head = lambda base: pl.BlockSpec((1, seq, LANES), lambda b, h: (base + h, b, 0))
    return pl.pallas_call(
        functools.partial(_diff_kernel, t=t, tk=tk, lam_init=lam_init),
        grid=(batch, DIFF_HEADS),
        in_specs=[head(SLAB_DQ), head(SLAB_DK), head(SLAB_DV),
                  pl.BlockSpec((seq, LANES), lambda b, h: (0, 0)),
                  pl.BlockSpec((1, 1, LANES), lambda b, h: (h, 0, 0)),
                  pl.BlockSpec((4, DIFF_HALF), lambda b, h: (0, 0)),
                  pl.BlockSpec((1, LANES), lambda b, h: (0, 0))],
        out_specs=pl.BlockSpec((seq, LANES), lambda b, h: (b, h)),
        out_shape=jax.ShapeDtypeStruct((m, DIFF_HEADS * HEAD_DIM), BF16),
        scratch_shapes=[pltpu.VMEM((seq, 2 * LANES), BF16), pltpu.VMEM((LANES, seq), BF16),
                        pltpu.VMEM((tk // t, tk, 2 * t), F32)] + _flash_scratch(tk, 2 * t),
        compiler_params=_params(("parallel", "parallel")),
        name="diff_attn",
    )(slabs, slabs, slabs, kx, qx_diff, lam_vec, subln)


def _compress_kernel(y_ref, pos_ref, w1_ref, w2_ref, o_ref):
    half = y_ref.shape[3]
    y = y_ref[0, 0].astype(F32)
    top = (y + pos_ref[0, 0:1, :]).astype(BF16)
    bot = (y + pos_ref[0, 1:2, :]).astype(BF16)
    a = _dot(top, w1_ref[0, 0:half, :])
    b = _dot(bot, w1_ref[0, half:2 * half, :])
    nrow = a.shape[0]
    hid = a + pltpu.roll(b, nrow - 1, 0)
    hid = hid * jax.nn.sigmoid(hid)
    o_ref[0, 0] = _dot(hid.astype(BF16), w2_ref[0]).astype(BF16)


def _compress(ykv, pos2, w1, w2, batch, l):
    nrow, half = ykv.shape[2], ykv.shape[3]
    return pl.pallas_call(
        _compress_kernel,
        grid=(4, batch),
        in_specs=[pl.BlockSpec((1, 1, nrow, half), lambda s, b: (s, b, 0, 0)),
                  pl.BlockSpec((None, 1, 2, half), lambda s, b: (l, s // 2, 0, 0)),
                  pl.BlockSpec((None, 1, 2 * half, CMP_HIDDEN), lambda s, b: (l, s // 2, 0, 0)),
                  pl.BlockSpec((None, 1, CMP_HIDDEN, HEAD_DIM), lambda s, b: (l, s // 2, 0, 0))],
        out_specs=pl.BlockSpec((1, 1, nrow, HEAD_DIM), lambda s, b: (s, b, 0, 0)),
        out_shape=jax.ShapeDtypeStruct((4, batch, nrow, HEAD_DIM), BF16),
        compiler_params=_params(("parallel", "parallel")),
        name="nsa_compress",
    )(ykv, pos2, w1, w2)


def _topk_bias(cand, topk):
    nsel, tq = cand.shape
    sub = 8
    blocks = [cand[b * sub:(b + 1) * sub] for b in range(nsel // sub)]
    jidx = lax.broadcasted_iota(jnp.int32, (sub, tq), 0)
    ranks = [jnp.zeros((sub, tq), F32) for _ in blocks]
    for k in range(nsel):
        rk = cand[k:k + 1, :]
        for b, cb in enumerate(blocks):
            if b * sub > k:
                beats = rk >= cb
            elif b * sub + sub - 1 < k:
                beats = rk > cb
            else:
                beats = (rk > cb) | ((rk == cb) & (jidx + b * sub > k))
            ranks[b] = ranks[b] + jnp.where(beats, 1.0, 0.0)
    rank = jnp.concatenate(ranks, axis=0)
    return jnp.where(rank < topk, 0.0, SEL_MASK)


def _nsa_kernel(q_ref, kc_ref, vc_ref, ks_ref, vs_ref, kw_ref, vw_ref, kx_ref, cx_ref, ovt_ref, qx_ref,
                misc_ref, o_ref, ksa_ref, kwa_ref, vst_ref, vwt_ref, vct_ref, gt_ref, dbias_ref, wbias_ref,
                comb_ref, *flash, tq, tk):
    g = pl.program_id(1)
    qi = pl.program_id(2)
    mcols = NSA_REP * tq
    nc = kc_ref.shape[2]
    seq = kx_ref.shape[0]
    nsel = seq // SEL_BLOCK
    span = WINDOW + tq

    @pl.when(qi == 0)
    def _():
        _stage_kv(ksa_ref, vst_ref, ks_ref, kx_ref, vs_ref, tk)
        _stage_kv(kwa_ref, vwt_ref, kw_ref, kx_ref, vw_ref, tk, pad=WINDOW)
        pad_lane = lax.broadcasted_iota(jnp.int32, (WINDOW, 2 * LANES), 1)
        kwa_ref[0:WINDOW, :] = jnp.where(pad_lane == LANES + AUG_PAD, SEL_MASK, 0.0).astype(BF16)
        vwt_ref[:, 0:WINDOW] = jnp.zeros((LANES, WINDOW), BF16)
        vct_ref[...] = vc_ref[0, 0].astype(F32).T.astype(BF16)
        for o in range(tk // tq):
            dbias_ref[o] = _causal_bias(tk, mcols, tq, o * tq)
        rk = lax.broadcasted_iota(jnp.int32, (span, mcols), 0)
        rq = lax.broadcasted_iota(jnp.int32, (span, mcols), 1) & (tq - 1)
        wbias_ref[...] = jnp.where((rk > rq) & (rk <= rq + WINDOW), 0.0, NEG_INF)

    qx = jnp.concatenate([jnp.broadcast_to(qx_ref[0, r:r + 1, :], (tq, LANES)) for r in range(NSA_REP)], axis=0)
    kca = jnp.concatenate([kc_ref[0, 0], cx_ref[...]], axis=1)
    _nsa_query_tile(qi, g, qx, kca, q_ref, ovt_ref, misc_ref, o_ref, ksa_ref, kwa_ref, vst_ref, vwt_ref,
                    vct_ref, gt_ref, dbias_ref, wbias_ref, comb_ref, flash, tq=tq, tk=tk, nc=nc, nsel=nsel)


def _nsa_query_tile(qi, g, qx, kca, q_ref, ovt_ref, misc_ref, o_ref, ksa_ref, kwa_ref, vst_ref, vwt_ref, vct_ref,
                    gt_ref, dbias_ref, wbias_ref, comb_ref, flash, *, tq, tk, nc, nsel):
    mcols = NSA_REP * tq
    span = WINDOW + tq
    q0 = pl.multiple_of(qi * tq, tq)
    q4 = q_ref[...].reshape(mcols, LANES)
    qa = jnp.concatenate([q4, qx.astype(BF16)], axis=1)
    col = lax.broadcasted_iota(jnp.int32, (1, mcols), 1)
    col_pos = q0 + (col & (tq - 1))

    sc = _dot_nt(kca, qa)
    cend = lax.broadcasted_iota(jnp.int32, (nc, mcols), 0) * CMP_STRIDE + (CMP_BLOCK - 1)
    sc = jnp.where(cend <= col_pos, sc, NEG_INF)
    e = jnp.exp2(sc - jnp.max(sc, axis=0, keepdims=True))
    inv = jnp.where(col_pos >= CMP_BLOCK - 1, 1.0 / jnp.sum(e, axis=0, keepdims=True), 0.0)
    p = e * inv
    o_cmp = _dot(vct_ref[...], p.astype(BF16))

    psum = p[:, 0:tq]
    for r in range(1, NSA_REP):
        psum = psum + p[:, r * tq:(r + 1) * tq]
    p_hi = psum.astype(BF16)
    p_lo = (psum - p_hi.astype(F32)).astype(BF16)
    imp = _dot(ovt_ref[...], p_hi) + _dot(ovt_ref[...], p_lo)
    qpos = q0 + lax.broadcasted_iota(jnp.int32, (LANES, tq), 1)
    blk = lax.broadcasted_iota(jnp.int32, (LANES, tq), 0)
    cur = jnp.right_shift(qpos, SEL_BLOCK.bit_length() - 1)
    forced = (blk == 0) | (blk == cur) | (blk == cur - 1)
    imp = jnp.where(blk <= cur, jnp.where(forced, FORCE_SCORE, imp), -1.0)
    bias_t = _topk_bias(imp[0:nsel], min(SEL_TOPK, nsel))
    if nsel < LANES:
        bias_t = jnp.concatenate([bias_t, jnp.zeros((LANES - nsel, tq), F32)], axis=0)
    selbias = jnp.concatenate([bias_t.T] * NSA_REP, axis=0)
    lane4 = lax.broadcasted_iota(jnp.int32, (mcols, LANES), 1)
    qa_sel = jnp.concatenate([q4, jnp.where(lane4 < SEL_BLOCK, selbias, qx).astype(BF16)], axis=1)

    sw = _dot_nt(kwa_ref[pl.ds(q0, span), :], qa) + wbias_ref[...]
    ew = jnp.exp2(sw - jnp.max(sw, axis=0, keepdims=True))
    o_win = _dot(vwt_ref[:, pl.ds(q0, span)], ew.astype(BF16)) * (1.0 / jnp.sum(ew, axis=0, keepdims=True))

    gt_ref[...] = jax.nn.sigmoid(misc_ref[...]).T
    gate = lambda r, i: gt_ref[pl.ds(MISC_NG + 3 * (NSA_REP * g + r) + i, 1), :]
    for r in range(NSA_REP):
        sl = slice(r * tq, (r + 1) * tq)
        comb_ref[:, sl] = gate(r, 0) * o_cmp[:, sl] + gate(r, 2) * o_win[:, sl]

    o_sel = _flash_tiles(qa_sel, ksa_ref, vst_ref, q0 // tk, dbias_ref.at[qi & (tk // tq - 1)], tk, flash)
    for r in range(NSA_REP):
        sl = slice(r * tq, (r + 1) * tq)
        out = comb_ref[:, sl] + gate(r, 1) * o_sel[:, sl]
        o_ref[:, r * LANES:(r + 1) * LANES] = out.T.astype(BF16)


def _nsa(slabs, kvc, kx, cx_cmp, ovt, qx_nsa, misc, batch, seq):
    tq = _pick(seq, (512, 256, 128))
    tk = _pick(seq, (512, 256, 128))
    assert tq & (tq - 1) == 0 and tk % tq == 0 and seq >= WINDOW + tq
    nq = seq // tq
    m = batch * seq
    mcols = NSA_REP * tq
    nc = kvc.shape[2]
    kv_spec = lambda base: pl.BlockSpec((1, seq, LANES), lambda b, g, i: (base + g, b, 0))
    const = lambda shape: pl.BlockSpec(shape, lambda b, g, i: (0,) * len(shape), pipeline_mode=pl.Buffered(1))
    return pl.pallas_call(
        functools.partial(_nsa_kernel, tq=tq, tk=tk),
        grid=(batch, NSA_KV_HEADS, nq),
        in_specs=[pl.BlockSpec((NSA_REP, tq, LANES), lambda b, g, i: (SLAB_NQ // NSA_REP + g, b * nq + i, 0)),
                  pl.BlockSpec((1, 1, nc, LANES), lambda b, g, i: (g, b, 0, 0)),
                  pl.BlockSpec((1, 1, nc, LANES), lambda b, g, i: (2 + g, b, 0, 0)),
                  kv_spec(SLAB_NKS), kv_spec(SLAB_NVS), kv_spec(SLAB_NKW), kv_spec(SLAB_NVW),
                  const((seq, LANES)), const((nc, LANES)), const((LANES, nc)),
                  pl.BlockSpec((1, NSA_REP, LANES), lambda b, g, i: (g, 0, 0)),
                  pl.BlockSpec((tq, LANES), lambda b, g, i: (b * nq + i, 0))],
        out_specs=pl.BlockSpec((tq, NSA_REP * LANES), lambda b, g, i: (b * nq + i, g)),
        out_shape=jax.ShapeDtypeStruct((m, NSA_HEADS * HEAD_DIM), BF16),
        scratch_shapes=[pltpu.VMEM((seq, 2 * LANES), BF16), pltpu.VMEM((WINDOW + seq, 2 * LANES), BF16),
                        pltpu.VMEM((LANES, seq), BF16), pltpu.VMEM((LANES, WINDOW + seq), BF16),
                        pltpu.VMEM((LANES, nc), BF16), pltpu.VMEM((LANES, tq), F32),
                        pltpu.VMEM((tk // tq, tk, mcols), F32), pltpu.VMEM((WINDOW + tq, mcols), F32),
                        pltpu.VMEM((LANES, mcols), F32)] + _flash_scratch(tk, mcols),
        compiler_params=_params(("parallel", "parallel", "arbitrary")),
        name="nsa_attn",
    )(slabs, kvc, kvc, slabs, slabs, slabs, slabs, kx, cx_cmp, ovt, qx_nsa, misc)


def _merge_kernel(h_ref, of_ref, od_ref, on_ref, wg0_ref, wg1_ref, wg2_ref, wf_ref, wd_ref, wn_ref, o_ref):
    h = h_ref[...]
    acc = jax.nn.sigmoid(_dot(h, wg0_ref[...])) * _dot(of_ref[...], wf_ref[...])
    acc = acc + jax.nn.sigmoid(_dot(h, wg1_ref[...])) * _dot(od_ref[...], wd_ref[...])
    acc = acc + jax.nn.sigmoid(_dot(h, wg2_ref[...])) * _dot(on_ref[...], wn_ref[...])
    o_ref[...] = acc.astype(BF16)


def _merge(h, o_fox, o_diff, o_nsa, w_gate, wb_fox, wb_diff, wb_nsa, l):
    m, d = h.shape
    tm = _pick(m, (1024, 512, 256))
    tn = _pick(d, (256, 128))
    nj = d // tn
    row = lambda width: pl.BlockSpec((tm, width), lambda i, j: (i, 0))
    gate = lambda t: pl.BlockSpec((None, d, tn), lambda i, j: (l, 0, t * nj + j))
    col = lambda k: pl.BlockSpec((None, k, tn), lambda i, j: (l, 0, j))
    return pl.pallas_call(
        _merge_kernel,
        grid=(m // tm, nj),
        in_specs=[row(d), row(o_fox.shape[1]), row(o_diff.shape[1]), row(o_nsa.shape[1]),
                  gate(0), gate(1), gate(2),
                  col(wb_fox.shape[1]), col(wb_diff.shape[1]), col(wb_nsa.shape[1])],
        out_specs=pl.BlockSpec((tm, tn), lambda i, j: (i, j)),
        out_shape=jax.ShapeDtypeStruct((m, d), BF16),
        compiler_params=_params(("parallel", "arbitrary")),
        name="gate_merge",
    )(h, o_fox, o_diff, o_nsa, w_gate, w_gate, w_gate, wb_fox, wb_diff, wb_nsa)


def _wout_kernel(a_ref, w_ref, x_ref, gp_ref, gn_ref, xo_ref, ho_ref):
    y = _dot(a_ref[...], w_ref[...])
    x_new = x_ref[...] + _rms(y, gp_ref[...])
    xo_ref[...] = x_new
    ho_ref[...] = _rms(x_new, gn_ref[...]).astype(BF16)


def _wout(a, w, x, g_post, g_next, l, name):
    m, d = x.shape
    k = a.shape[1]
    tm = _pick(m, (512, 256, 128)) if k * d * 2 <= 8 * 1024 * 1024 else _pick(m, (256, 128))
    row = lambda width: pl.BlockSpec((tm, width), lambda i: (i, 0))
    vec = lambda: pl.BlockSpec((1, d), lambda i: (0, 0))
    return pl.pallas_call(
        _wout_kernel,
        grid=(m // tm,),
        in_specs=[row(k), pl.BlockSpec((None, k, d), lambda i: (l, 0, 0), pipeline_mode=pl.Buffered(1)),
                  row(d), vec(), vec()],
        out_specs=[row(d), row(d)],
        out_shape=[jax.ShapeDtypeStruct((m, d), F32), jax.ShapeDtypeStruct((m, d), BF16)],
        compiler_params=_params(("parallel",)),
        name=name,
    )(a, w, x, g_post, g_next)


def _ffn_up_kernel(h_ref, wg_ref, wu_ref, o_ref, wgb_ref, wub_ref):
    @pl.when(pl.program_id(1) == 0)
    def _():
        wgb_ref[...] = wg_ref[...].astype(BF16)
        wub_ref[...] = wu_ref[...].astype(BF16)

    h = h_ref[...]
    gate = _dot(h, wgb_ref[...])
    o_ref[...] = (gate * jax.nn.sigmoid(gate) * _dot(h, wub_ref[...])).astype(BF16)


def _ffn_up(h, w_up, l):
    m, d = h.shape
    dff = w_up.shape[2] // 2
    tm = _pick(m, (1024, 512, 256))
    tn = _pick(dff, (512, 256, 128))
    nj = dff // tn
    return pl.pallas_call(
        _ffn_up_kernel,
        grid=(nj, m // tm),
        in_specs=[pl.BlockSpec((tm, d), lambda j, i: (i, 0)),
                  pl.BlockSpec((None, d, tn), lambda j, i: (l, 0, j)),
                  pl.BlockSpec((None, d, tn), lambda j, i: (l, 0, nj + j))],
        out_specs=pl.BlockSpec((tm, tn), lambda j, i: (i, j)),
        out_shape=jax.ShapeDtypeStruct((m, dff), BF16),
        scratch_shapes=[pltpu.VMEM((d, tn), BF16), pltpu.VMEM((d, tn), BF16)],
        compiler_params=_params(("parallel", "arbitrary")),
        name="ffn_up",
    )(h, w_up, w_up)


def _pos_columns(pos):
    lane = jnp.arange(LANES)[None, :]
    hi = (lane >= AUG_HI) & (lane < AUG_HI + AUG_TERMS)
    lo = (lane >= AUG_LO) & (lane < AUG_LO + AUG_TERMS)
    return jnp.where(hi, (pos // LANES)[:, None], jnp.where(lo, (pos % LANES)[:, None], 0))


def _key_aug_table(seq):
    j = jnp.arange(seq)
    lane = jnp.arange(LANES)[None, :]
    onehot = (lane == (j // SEL_BLOCK)[:, None]) & (lane < SEL_BLOCK)
    return (_pos_columns(j) + onehot.astype(jnp.int32)).astype(BF16)


def _cmp_aug_table(nrow):
    return _pos_columns(jnp.arange(nrow) * CMP_STRIDE + CMP_BLOCK - 1).astype(BF16)


def _overlap_table_t(nrow, seq):
    start = jnp.arange(nrow)[None, :] * CMP_STRIDE
    blk = jnp.arange(LANES)[:, None]
    sel = blk * SEL_BLOCK
    ov = ((start < sel + SEL_BLOCK) & (start + CMP_BLOCK - 1 >= sel) & (blk < seq // SEL_BLOCK)
          & (jnp.arange(nrow)[None, :] < (seq - CMP_BLOCK) // CMP_STRIDE + 1))
    return ov.astype(BF16)


def _query_aug_rows(n_heads):
    slopes = 2.0 ** (-8.0 * jnp.arange(1, n_heads + 1, dtype=F32) / n_heads)
    terms = [t.astype(F32) for t in _split3(jnp.float32(LOG2E))]
    lane = jnp.arange(LANES)[None, :]
    out = jnp.where(lane == AUG_PAD, 1.0, jnp.zeros((n_heads, LANES), F32))
    for i, t in enumerate(terms):
        out = jnp.where(lane == AUG_HI + i, slopes[:, None] * t * LANES, out)
        out = jnp.where(lane == AUG_LO + i, slopes[:, None] * t, out)
    return out


def kernel(x, w_in, fox_forget_bias, diff_lambda, diff_subln, nsa_cmp_pos, nsa_cmp_w1, nsa_cmp_w2,
           w_branch_fox, w_branch_diff, w_branch_nsa, w_gate, w_out, norm_gains, w_ffn_up, w_ffn_down):
    batch, seq, d = x.shape
    depth = w_in.shape[0]
    m = batch * seq
    fw, dw, nw, kvw = FOX_HEADS * HEAD_DIM, DIFF_HEADS * HEAD_DIM, NSA_HEADS * HEAD_DIM, NSA_KV_HEADS * HEAD_DIM
    ff0 = 3 * fw
    dq0 = ff0 + FOX_HEADS
    ng0 = dq0 + 3 * dw + nw + 6 * kvw
    n_main = ng0 - FOX_HEADS
    assert n_main == N_SLABS * LANES and w_in.shape[2] == ng0 + 3 * NSA_HEADS

    w_main = jnp.concatenate([w_in[:, :, :ff0], w_in[:, :, dq0:ng0]], axis=2).astype(BF16)
    w_misc = jnp.concatenate([w_in[:, :, ff0:dq0], w_in[:, :, ng0:],
                              jnp.zeros((depth, d, LANES - FOX_HEADS - 3 * NSA_HEADS), F32)], axis=2).astype(BF16)
    colscale = jnp.ones((n_main,), F32)
    colscale = colscale.at[SLAB_FQ * LANES:SLAB_FK * LANES].set(HEAD_DIM ** -0.5 * LOG2E)
    colscale = colscale.at[SLAB_DQ * LANES:SLAB_DK * LANES].set(DIFF_HALF ** -0.5 * LOG2E)
    colscale = colscale.at[SLAB_NQ * LANES:SLAB_NKC * LANES].set(HEAD_DIM ** -0.5 * LOG2E)
    colscale = colscale[None, :]
    fbias = jnp.pad(fox_forget_bias.astype(F32), ((0, 0), (0, LANES - FOX_HEADS)))[:, None, :]
    half = CMP_STRIDE * HEAD_DIM
    pos2 = nsa_cmp_pos.astype(F32).reshape(depth, 2, 2, half)
    w1 = nsa_cmp_w1.astype(BF16)
    w2 = nsa_cmp_w2.astype(BF16)
    wbf, wbd, wbn = w_branch_fox.astype(BF16), w_branch_diff.astype(BF16), w_branch_nsa.astype(BF16)
    wg, wo = w_gate.astype(BF16), w_out.astype(BF16)
    wup, wdn = w_ffn_up.astype(F32), w_ffn_down.astype(BF16)
    gains = norm_gains.astype(F32)

    nrow = seq // CMP_STRIDE
    kx = _key_aug_table(seq)
    cx_cmp = _cmp_aug_table(nrow)
    ovt = _overlap_table_t(nrow, seq)
    qx_diff = _query_aug_rows(DIFF_HEADS)[:, None, :]
    qx_nsa = _query_aug_rows(NSA_HEADS).reshape(NSA_KV_HEADS, NSA_REP, LANES)

    xf = x.reshape(m, d).astype(F32)
    h = _norm(xf, gains[0, 0][None, :])
    for l in range(depth):
        lam_init = 0.8 - 0.6 * math.exp(-0.3 * l)
        slabs, misc = _inproj(h, w_main, colscale, w_misc, l)
        cx_fox = _logf(misc, fbias[l], batch, seq)
        o_fox = _fox(slabs, cx_fox, batch, seq)
        o_diff = _diff(slabs, kx, qx_diff, diff_lambda[l].astype(F32), diff_subln[l].astype(F32)[None, :],
                       batch, seq, lam_init)
        ykv = slabs[SLAB_NKC:SLAB_NKC + 4].reshape(4, batch, nrow, half)
        kvc = _compress(ykv, pos2, w1, w2, batch, l)
        o_nsa = _nsa(slabs, kvc, kx, cx_cmp, ovt, qx_nsa, misc, batch, seq)
        merged = _merge(h, o_fox, o_diff, o_nsa, wg, wbf, wbd, wbn, l)
        xf, h2 = _wout(merged, wo, xf, gains[l, 1][None, :], gains[l, 2][None, :], l, "out_proj")
        act = _ffn_up(h2, wup, l)
        g_next = gains[min(l + 1, depth - 1), 0][None, :]
        xf, h = _wout(act, wdn, xf, gains[l, 3][None, :], g_next, l, "ffn_down")
    return xf.reshape(batch, seq, d).astype(x.dtype)
```

```python
import functools
import math

import jax
import jax.numpy as jnp
from jax import lax
from jax.experimental import pallas as pl
from jax.experimental.pallas import tpu as pltpu

F32 = jnp.float32
BF16 = jnp.bfloat16

HEAD_DIM = 128
FOX_HEADS = 4
DIFF_HEADS = 4
DIFF_HALF = HEAD_DIM // 2
NSA_HEADS = 8
NSA_KV_HEADS = 2
NSA_REP = NSA_HEADS // NSA_KV_HEADS
CMP_BLOCK = 32
CMP_STRIDE = 16
CMP_HIDDEN = 256
SEL_BLOCK = 64
SEL_TOPK = 16
WINDOW = 512
N_BRANCHES = 3
EPS = 1e-6
NEG_INF = -1e30
FORCE_SCORE = 1e4
SEL_MASK = -32768.0
LOG2E = math.log2(math.e)
LANES = 128

SLAB_FQ, SLAB_FK, SLAB_FV = 0, 4, 8
SLAB_DQ, SLAB_DK, SLAB_DV = 12, 16, 20
SLAB_NQ = 24
SLAB_NKC, SLAB_NVC, SLAB_NKS, SLAB_NVS, SLAB_NKW, SLAB_NVW = 32, 34, 36, 38, 40, 42
N_SLABS = 44
MISC_FF = 0
MISC_NG = 4
AUG_HI = 64
AUG_LO = 67
AUG_TERMS = 3
AUG_PAD = 70

VMEM_LIMIT = 56 * 1024 * 1024


def _pick(n, prefs):
    for p in prefs:
        if p <= n and n % p == 0:
            return p
    return n


def _params(sem):
    return pltpu.CompilerParams(dimension_semantics=sem, vmem_limit_bytes=VMEM_LIMIT)


def _rms(y, g):
    return y * lax.rsqrt(jnp.mean(y * y, axis=-1, keepdims=True) + EPS) * g


def _dot(a, b):
    return jnp.dot(a, b, preferred_element_type=F32)


def _dot_nt(a, b):
    return lax.dot_general(a, b, (((1,), (1,)), ((), ())), preferred_element_type=F32)


def _split3(x):
    hi = x.astype(BF16)
    r = x - hi.astype(F32)
    mid = r.astype(BF16)
    lo = (r - mid.astype(F32)).astype(BF16)
    return hi, mid, lo


def _norm_kernel(x_ref, g_ref, h_ref):
    h_ref[...] = _rms(x_ref[...], g_ref[...]).astype(BF16)


def _norm(x, g):
    m, d = x.shape
    tm = _pick(m, (512, 256, 128))
    return pl.pallas_call(
        _norm_kernel,
        grid=(m // tm,),
        in_specs=[pl.BlockSpec((tm, d), lambda i: (i, 0)), pl.BlockSpec((1, d), lambda i: (0, 0))],
        out_specs=pl.BlockSpec((tm, d), lambda i: (i, 0)),
        out_shape=jax.ShapeDtypeStruct((m, d), BF16),
        compiler_params=_params(("parallel",)),
        name="norm_in",
    )(x, g)


def _inproj_kernel(h_ref, w_ref, cs_ref, wm_ref, o_ref, misc_ref):
    h = h_ref[...]
    acc = _dot(h, w_ref[...]) * cs_ref[...]
    for s in range(o_ref.shape[0]):
        o_ref[s] = acc[:, s * LANES:(s + 1) * LANES].astype(BF16)

    @pl.when(pl.program_id(1) == 0)
    def _():
        misc_ref[...] = _dot(h, wm_ref[...])


def _inproj(h, w_main, colscale, w_misc, l):
    m, d = h.shape
    n = w_main.shape[2]
    tm = _pick(m, (1024, 512, 256))
    tn = _pick(n, (11 * LANES, 4 * LANES))
    return pl.pallas_call(
        _inproj_kernel,
        grid=(m // tm, n // tn),
        in_specs=[pl.BlockSpec((tm, d), lambda i, j: (i, 0)),
                  pl.BlockSpec((None, d, tn), lambda i, j: (l, 0, j)),
                  pl.BlockSpec((1, tn), lambda i, j: (0, j)),
                  pl.BlockSpec((None, d, LANES), lambda i, j: (l, 0, 0))],
        out_specs=[pl.BlockSpec((tn // LANES, tm, LANES), lambda i, j: (j, i, 0)),
                   pl.BlockSpec((tm, LANES), lambda i, j: (i, 0))],
        out_shape=[jax.ShapeDtypeStruct((n // LANES, m, LANES), BF16),
                   jax.ShapeDtypeStruct((m, LANES), F32)],
        compiler_params=_params(("parallel", "arbitrary")),
        name="inproj",
    )(h, w_main, colscale, w_misc)


def _logf_kernel(misc_ref, bias_ref, o_ref, *, tc):
    s = misc_ref.shape[0]
    row = lax.broadcasted_iota(jnp.int32, (tc, tc), 0)
    col = lax.broadcasted_iota(jnp.int32, (tc, tc), 1)
    tri = jnp.where(col <= row, 1.0, 0.0).astype(BF16)
    lane = lax.broadcasted_iota(jnp.int32, (tc, LANES), 1)

    def chunk(c, carry):
        r0 = pl.multiple_of(c * tc, tc)
        z = misc_ref[pl.ds(r0, tc), :] + bias_ref[...]
        lf = jnp.minimum(z, 0.0) - jnp.log1p(jnp.exp(-jnp.abs(z)))
        hi, mid, lo = _split3(lf)
        cum = _dot(tri, hi) + _dot(tri, mid) + _dot(tri, lo) + carry
        for hd in range(FOX_HEADS):
            c2 = jnp.broadcast_to(cum[:, hd:hd + 1], (tc, LANES)) * LOG2E
            c_hi = c2.astype(BF16).astype(F32)
            c_mid = (c2 - c_hi).astype(BF16).astype(F32)
            aug = jnp.where(lane == 0, c_hi, jnp.where(lane == 1, c_mid,
                                                      jnp.where(lane == 2, c2 - c_hi - c_mid, 0.0)))
            o_ref[hd, pl.ds(r0, tc), :] = aug.astype(BF16)
        return cum[tc - 1:tc, :]

    lax.fori_loop(0, s // tc, chunk, jnp.zeros((1, LANES), F32))


def _logf(misc, bias_row, batch, seq):
    tc = _pick(seq, (256, 128))
    return pl.pallas_call(
        functools.partial(_logf_kernel, tc=tc),
        grid=(batch,),
        in_specs=[pl.BlockSpec((seq, LANES), lambda b: (b, 0)), pl.BlockSpec((1, LANES), lambda b: (0, 0))],
        out_specs=pl.BlockSpec((FOX_HEADS, seq, LANES), lambda b: (0, b, 0)),
        out_shape=jax.ShapeDtypeStruct((FOX_HEADS, batch * seq, LANES), BF16),
        compiler_params=_params(("parallel",)),
        name="fox_logf",
    )(misc, bias_row)


def _flash_scratch(tk, mcols):
    return [pltpu.VMEM((tk, mcols), F32), pltpu.VMEM((1, mcols), F32), pltpu.VMEM((1, mcols), F32),
            pltpu.VMEM((LANES, mcols), F32)]


def _causal_bias(tk, mcols, tq, offset):
    rk = lax.broadcasted_iota(jnp.int32, (tk, mcols), 0)
    rq = lax.broadcasted_iota(jnp.int32, (tk, mcols), 1) & (tq - 1)
    return jnp.where(rk <= rq + offset, 0.0, NEG_INF)


def _flash_tiles(qa, ka_ref, vt_ref, n_full, diag_bias_ref, tk, scratch, next_qa=None):
    s_ref, m_ref, l_ref, acc_ref = scratch

    def logits(j, queries=qa):
        k0 = pl.multiple_of(j * tk, tk)
        return _dot_nt(ka_ref[pl.ds(k0, tk), :], queries)

    def step(s, j):
        k0 = pl.multiple_of(j * tk, tk)
        m_prev = m_ref[...]
        m_new = jnp.maximum(m_prev, jnp.max(s, axis=0, keepdims=True))
        alpha = jnp.exp2(m_prev - m_new)
        p = jnp.exp2(s - m_new)
        l_ref[...] = alpha * l_ref[...] + jnp.sum(p, axis=0, keepdims=True)
        acc_ref[...] = alpha * acc_ref[...] + _dot(vt_ref[:, pl.ds(k0, tk)], p.astype(BF16))
        m_ref[...] = m_new

    m_ref[...] = jnp.full(m_ref.shape, NEG_INF, F32)
    l_ref[...] = jnp.zeros(l_ref.shape, F32)
    acc_ref[...] = jnp.zeros(acc_ref.shape, F32)
    if next_qa is None:
        s_ref[...] = logits(0)

    def full(j, c):
        s_next = logits(j + 1)
        step(s_ref[...], j)
        s_ref[...] = s_next
        return c

    lax.fori_loop(0, n_full, full, 0)
    if next_qa is None:
        step(s_ref[...] + diag_bias_ref[...], n_full)
    else:
        s_next = logits(0, next_qa)
        step(s_ref[...] + diag_bias_ref[...], n_full)
        s_ref[...] = s_next
    return acc_ref[...] * (1.0 / l_ref[...])


def _stage_kv(ka_ref, vt_ref, k_ref, kx_ref, v_ref, chunk, pad=0):
    def body(c, carry):
        r0 = pl.multiple_of(c * chunk, chunk)
        ka_ref[pl.ds(pad + r0, chunk), 0:LANES] = k_ref[0, pl.ds(r0, chunk), :]
        ka_ref[pl.ds(pad + r0, chunk), LANES:2 * LANES] = kx_ref[pl.ds(r0, chunk), :]
        vt_ref[:, pl.ds(pad + r0, chunk)] = v_ref[0, pl.ds(r0, chunk), :].astype(F32).T.astype(BF16)
        return carry

    lax.fori_loop(0, v_ref.shape[1] // chunk, body, 0)


def _fox_kernel(q_ref, k_ref, v_ref, cx_ref, o_ref, ka_ref, vt_ref, tri_ref, *flash, tq, tk):
    _stage_kv(ka_ref, vt_ref, k_ref, cx_ref.at[0], v_ref, tk)
    for o in range(tk // tq):
        tri_ref[o] = _causal_bias(tk, tq, tq, o * tq)
    lane = lax.broadcasted_iota(jnp.int32, (tq, LANES), 1)
    qx = jnp.where(lane < 3, -1.0, 0.0).astype(BF16)

    nq = q_ref.shape[1] // tq

    def queries(qi):
        q0 = pl.multiple_of(qi * tq, tq)
        return jnp.concatenate([q_ref[0, pl.ds(q0, tq), :], qx], axis=1)

    flash[0][...] = _dot_nt(ka_ref[0:tk, :], queries(0))

    def qstep(qi, c):
        o = _flash_tiles(queries(qi), ka_ref, vt_ref, (qi * tq) // tk, tri_ref.at[qi & (tk // tq - 1)], tk, flash,
                         next_qa=queries(jnp.minimum(qi + 1, nq - 1)))
        o_ref[pl.ds(pl.multiple_of(qi * tq, tq), tq), :] = o.T.astype(BF16)
        return c

    lax.fori_loop(0, nq, qstep, 0)


def _attn_tiles(seq):
    tq = _pick(seq, (512, 256, 128))
    tk = 2 * tq if seq % (2 * tq) == 0 else tq
    return tq, tk


def _fox(slabs, cx, batch, seq):
    tk = _attn_tiles(seq)[1]
    tq = tk
    m = batch * seq
    head = lambda base: pl.BlockSpec((1, seq, LANES), lambda b, h: (base + h, b, 0))
    return pl.pallas_call(
        functools.partial(_fox_kernel, tq=tq, tk=tk),
        grid=(batch, FOX_HEADS),
        in_specs=[head(SLAB_FQ), head(SLAB_FK), head(SLAB_FV), head(0)],
        out_specs=pl.BlockSpec((seq, LANES), lambda b, h: (b, h)),
        out_shape=jax.ShapeDtypeStruct((m, FOX_HEADS * HEAD_DIM), BF16),
        scratch_shapes=[pltpu.VMEM((seq, 2 * LANES), BF16), pltpu.VMEM((LANES, seq), BF16),
                        pltpu.VMEM((tk // tq, tk, tq), F32)] + _flash_scratch(tk, tq),
        compiler_params=_params(("parallel", "parallel")),
        name="fox_attn",
    )(slabs, slabs, slabs, cx)


def _diff_kernel(q_ref, k_ref, v_ref, kx_ref, qx_ref, lam_ref, sub_ref, o_ref, ka_ref, vt_ref, tri_ref, *flash,
                 t, tk, lam_init):
    _stage_kv(ka_ref, vt_ref, k_ref, kx_ref, v_ref, tk)
    for o in range(tk // t):
        tri_ref[o] = _causal_bias(tk, 2 * t, t, o * t)
    lane = lax.broadcasted_iota(jnp.int32, (t, LANES), 1)
    qx = jnp.broadcast_to(qx_ref[0], (t, LANES)).astype(BF16)
    lv = lam_ref[...]
    lam = (jnp.exp(jnp.sum(lv[0:1] * lv[1:2], axis=-1, keepdims=True))
           - jnp.exp(jnp.sum(lv[2:3] * lv[3:4], axis=-1, keepdims=True)) + lam_init)

    nq = q_ref.shape[1] // t

    def queries(qi):
        q = q_ref[0, pl.ds(pl.multiple_of(qi * t, t), t), :].astype(F32)
        return jnp.concatenate([
            jnp.concatenate([jnp.where(lane < DIFF_HALF, q, 0.0).astype(BF16), qx], axis=1),
            jnp.concatenate([jnp.where(lane >= DIFF_HALF, q, 0.0).astype(BF16), qx], axis=1)], axis=0)

    flash[0][...] = _dot_nt(ka_ref[0:tk, :], queries(0))

    def qstep(qi, c):
        o = _flash_tiles(queries(qi), ka_ref, vt_ref, (qi * t) // tk, tri_ref.at[qi & (tk // t - 1)], tk, flash,
                         next_qa=queries(jnp.minimum(qi + 1, nq - 1)))
        o = (o[:, :t] - lam * o[:, t:]).T
        o_ref[pl.ds(pl.multiple_of(qi * t, t), t), :] = (_rms(o, sub_ref[...]) * (1.0 - lam_init)).astype(BF16)
        return c

    lax.fori_loop(0, nq, qstep, 0)


def _diff(slabs, kx, qx_diff, lam_vec, subln, batch, seq, lam_init):
    tk = _attn_tiles(seq)[1]
    t = tk
    m = batch * seq
    head = lambda base: pl.BlockSpec((1, seq, LANES), lambda b, h: (base + h, b, 0))
    return pl.pallas_call(
        functools.partial(_diff_kernel, t=t, tk=tk, lam_init=lam_init),
        grid=(batch, DIFF_HEADS),
        in_specs=[head(SLAB_DQ), head(SLAB_DK), head(SLAB_DV),
                  pl.BlockSpec((seq, LANES), lambda b, h: (0, 0)),
                  pl.BlockSpec((1, 1, LANES), lambda b, h: (h, 0, 0)),
                  pl.BlockSpec((4, DIFF_HALF), lambda b, h: (0, 0)),
                  pl.BlockSpec((1, LANES), lambda b, h: (0, 0))],
        out_specs=pl.BlockSpec((seq, LANES), lambda b, h: (b, h)),
        out_shape=jax.ShapeDtypeStruct((m, DIFF_HEADS * HEAD_DIM), BF16),
        scratch_shapes=[pltpu.VMEM((seq, 2 * LANES), BF16), pltpu.VMEM((LANES, seq), BF16),
                        pltpu.VMEM((tk // t, tk, 2 * t), F32)] + _flash_scratch(tk, 2 * t),
        compiler_params=_params(("parallel", "parallel")),
        name="diff_attn",
    )(slabs, slabs, slabs, kx, qx_diff, lam_vec, subln)


def _compress_kernel(y_ref, pos_ref, w1_ref, w2_ref, o_ref):
    half = y_ref.shape[3]
    y = y_ref[0, 0].astype(F32)
    top = (y + pos_ref[0, 0:1, :]).astype(BF16)
    bot = (y + pos_ref[0, 1:2, :]).astype(BF16)
    a = _dot(top, w1_ref[0, 0:half, :])
    b = _dot(bot, w1_ref[0, half:2 * half, :])
    nrow = a.shape[0]
    hid = a + pltpu.roll(b, nrow - 1, 0)
    hid = hid * jax.nn.sigmoid(hid)
    o_ref[0, 0] = _dot(hid.astype(BF16), w2_ref[0]).astype(BF16)


def _compress(ykv, pos2, w1, w2, batch, l):
    nrow, half = ykv.shape[2], ykv.shape[3]
    return pl.pallas_call(
        _compress_kernel,
        grid=(4, batch),
        in_specs=[pl.BlockSpec((1, 1, nrow, half), lambda s, b: (s, b, 0, 0)),
                  pl.BlockSpec((None, 1, 2, half), lambda s, b: (l, s // 2, 0, 0)),
                  pl.BlockSpec((None, 1, 2 * half, CMP_HIDDEN), lambda s, b: (l, s // 2, 0, 0)),
                  pl.BlockSpec((None, 1, CMP_HIDDEN, HEAD_DIM), lambda s, b: (l, s // 2, 0, 0))],
        out_specs=pl.BlockSpec((1, 1, nrow, HEAD_DIM), lambda s, b: (s, b, 0, 0)),
        out_shape=jax.ShapeDtypeStruct((4, batch, nrow, HEAD_DIM), BF16),
        compiler_params=_params(("parallel", "parallel")),
        name="nsa_compress",
    )(ykv, pos2, w1, w2)


def _topk_bias(cand, topk):
    nsel, tq = cand.shape
    sub = 8
    blocks = [cand[b * sub:(b + 1) * sub] for b in range(nsel // sub)]
    jidx = lax.broadcasted_iota(jnp.int32, (sub, tq), 0)
    ranks = [jnp.zeros((sub, tq), F32) for _ in blocks]
    for k in range(nsel):
        rk = cand[k:k + 1, :]
        for b, cb in enumerate(blocks):
            if b * sub > k:
                beats = rk >= cb
            elif b * sub + sub - 1 < k:
                beats = rk > cb
            else:
                beats = (rk > cb) | ((rk == cb) & (jidx + b * sub > k))
            ranks[b] = ranks[b] + jnp.where(beats, 1.0, 0.0)
    rank = jnp.concatenate(ranks, axis=0)
    return jnp.where(rank < topk, 0.0, SEL_MASK)


def _nsa_kernel(q_ref, kc_ref, vc_ref, ks_ref, vs_ref, kw_ref, vw_ref, kx_ref, cx_ref, ovt_ref, qx_ref,
                misc_ref, o_ref, ksa_ref, kwa_ref, vst_ref, vwt_ref, vct_ref, gt_ref, dbias_ref, wbias_ref,
                comb_ref, *flash, tq, tk):
    g = pl.program_id(1)
    qi = pl.program_id(2)
    mcols = NSA_REP * tq
    nc = kc_ref.shape[2]
    seq = kx_ref.shape[0]
    nsel = seq // SEL_BLOCK
    span = WINDOW + tq

    @pl.when(qi == 0)
    def _():
        _stage_kv(ksa_ref, vst_ref, ks_ref, kx_ref, vs_ref, tk)
        _stage_kv(kwa_ref, vwt_ref, kw_ref, kx_ref, vw_ref, tk, pad=WINDOW)
        pad_lane = lax.broadcasted_iota(jnp.int32, (WINDOW, 2 * LANES), 1)
        kwa_ref[0:WINDOW, :] = jnp.where(pad_lane == LANES + AUG_PAD, SEL_MASK, 0.0).astype(BF16)
        vwt_ref[:, 0:WINDOW] = jnp.zeros((LANES, WINDOW), BF16)
        vct_ref[...] = vc_ref[0, 0].astype(F32).T.astype(BF16)
        for o in range(tk // tq):
            dbias_ref[o] = _causal_bias(tk, mcols, tq, o * tq)
        rk = lax.broadcasted_iota(jnp.int32, (span, mcols), 0)
        rq = lax.broadcasted_iota(jnp.int32, (span, mcols), 1) & (tq - 1)
        wbias_ref[...] = jnp.where((rk > rq) & (rk <= rq + WINDOW), 0.0, NEG_INF)

    qx = jnp.concatenate([jnp.broadcast_to(qx_ref[0, r:r + 1, :], (tq, LANES)) for r in range(NSA_REP)], axis=0)
    kca = jnp.concatenate([kc_ref[0, 0], cx_ref[...]], axis=1)
    _nsa_query_tile(qi, g, qx, kca, q_ref, ovt_ref, misc_ref, o_ref, ksa_ref, kwa_ref, vst_ref, vwt_ref,
                    vct_ref, gt_ref, dbias_ref, wbias_ref, comb_ref, flash, tq=tq, tk=tk, nc=nc, nsel=nsel)


def _nsa_query_tile(qi, g, qx, kca, q_ref, ovt_ref, misc_ref, o_ref, ksa_ref, kwa_ref, vst_ref, vwt_ref, vct_ref,
                    gt_ref, dbias_ref, wbias_ref, comb_ref, flash, *, tq, tk, nc, nsel):
    mcols = NSA_REP * tq
    span = WINDOW + tq
    q0 = pl.multiple_of(qi * tq, tq)
    q4 = q_ref[...].reshape(mcols, LANES)
    qa = jnp.concatenate([q4, qx.astype(BF16)], axis=1)
    col = lax.broadcasted_iota(jnp.int32, (1, mcols), 1)
    col_pos = q0 + (col & (tq - 1))

    sc = _dot_nt(kca, qa)
    cend = lax.broadcasted_iota(jnp.int32, (nc, mcols), 0) * CMP_STRIDE + (CMP_BLOCK - 1)
    sc = jnp.where(cend <= col_pos, sc, NEG_INF)
    e = jnp.exp2(sc - jnp.max(sc, axis=0, keepdims=True))
    inv = jnp.where(col_pos >= CMP_BLOCK - 1, 1.0 / jnp.sum(e, axis=0, keepdims=True), 0.0)
    p = e * inv
    o_cmp = _dot(vct_ref[...], p.astype(BF16))

    psum = p[:, 0:tq]
    for r in range(1, NSA_REP):
        psum = psum + p[:, r * tq:(r + 1) * tq]
    p_hi = psum.astype(BF16)
    p_lo = (psum - p_hi.astype(F32)).astype(BF16)
    imp = _dot(ovt_ref[...], p_hi) + _dot(ovt_ref[...], p_lo)
    qpos = q0 + lax.broadcasted_iota(jnp.int32, (LANES, tq), 1)
    blk = lax.broadcasted_iota(jnp.int32, (LANES, tq), 0)
    cur = jnp.right_shift(qpos, SEL_BLOCK.bit_length() - 1)
    forced = (blk == 0) | (blk == cur) | (blk == cur - 1)
    imp = jnp.where(blk <= cur, jnp.where(forced, FORCE_SCORE, imp), -1.0)
    bias_t = _topk_bias(imp[0:nsel], min(SEL_TOPK, nsel))
    if nsel < LANES:
        bias_t = jnp.concatenate([bias_t, jnp.zeros((LANES - nsel, tq), F32)], axis=0)
    selbias = jnp.concatenate([bias_t.T] * NSA_REP, axis=0)
    lane4 = lax.broadcasted_iota(jnp.int32, (mcols, LANES), 1)
    qa_sel = jnp.concatenate([q4, jnp.where(lane4 < SEL_BLOCK, selbias, qx).astype(BF16)], axis=1)

    sw = _dot_nt(kwa_ref[pl.ds(q0, span), :], qa) + wbias_ref[...]
    ew = jnp.exp2(sw - jnp.max(sw, axis=0, keepdims=True))
    o_win = _dot(vwt_ref[:, pl.ds(q0, span)], ew.astype(BF16)) * (1.0 / jnp.sum(ew, axis=0, keepdims=True))

    gt_ref[...] = jax.nn.sigmoid(misc_ref[...]).T
    gate = lambda r, i: gt_ref[pl.ds(MISC_NG + 3 * (NSA_REP * g + r) + i, 1), :]
    for r in range(NSA_REP):
        sl = slice(r * tq, (r + 1) * tq)
        comb_ref[:, sl] = gate(r, 0) * o_cmp[:, sl] + gate(r, 2) * o_win[:, sl]

    o_sel = _flash_tiles(qa_sel, ksa_ref, vst_ref, q0 // tk, dbias_ref.at[qi & (tk // tq - 1)], tk, flash)
    for r in range(NSA_REP):
        sl = slice(r * tq, (r + 1) * tq)
        out = comb_ref[:, sl] + gate(r, 1) * o_sel[:, sl]
        o_ref[:, r * LANES:(r + 1) * LANES] = out.T.astype(BF16)


def _nsa(slabs, kvc, kx, cx_cmp, ovt, qx_nsa, misc, batch, seq):
    tq = _pick(seq, (512, 256, 128))
    tk = _pick(seq, (512, 256, 128))
    assert tq & (tq - 1) == 0 and tk % tq == 0 and seq >= WINDOW + tq
    nq = seq // tq
    m = batch * seq
    mcols = NSA_REP * tq
    nc = kvc.shape[2]
    kv_spec = lambda base: pl.BlockSpec((1, seq, LANES), lambda b, g, i: (base + g, b, 0))
    const = lambda shape: pl.BlockSpec(shape, lambda b, g, i: (0,) * len(shape), pipeline_mode=pl.Buffered(1))
    return pl.pallas_call(
        functools.partial(_nsa_kernel, tq=tq, tk=tk),
        grid=(batch, NSA_KV_HEADS, nq),
        in_specs=[pl.BlockSpec((NSA_REP, tq, LANES), lambda b, g, i: (SLAB_NQ // NSA_REP + g, b * nq + i, 0)),
                  pl.BlockSpec((1, 1, nc, LANES), lambda b, g, i: (g, b, 0, 0)),
                  pl.BlockSpec((1, 1, nc, LANES), lambda b, g, i: (2 + g, b, 0, 0)),
                  kv_spec(SLAB_NKS), kv_spec(SLAB_NVS), kv_spec(SLAB_NKW), kv_spec(SLAB_NVW),
                  const((seq, LANES)), const((nc, LANES)), const((LANES, nc)),
                  pl.BlockSpec((1, NSA_REP, LANES), lambda b, g, i: (g, 0, 0)),
                  pl.BlockSpec((tq, LANES), lambda b, g, i: (b * nq + i, 0))],
        out_specs=pl.BlockSpec((tq, NSA_REP * LANES), lambda b, g, i: (b * nq + i, g)),
        out_shape=jax.ShapeDtypeStruct((m, NSA_HEADS * HEAD_DIM), BF16),
        scratch_shapes=[pltpu.VMEM((seq, 2 * LANES), BF16), pltpu.VMEM((WINDOW + seq, 2 * LANES), BF16),
                        pltpu.VMEM((LANES, seq), BF16), pltpu.VMEM((LANES, WINDOW + seq), BF16),
                        pltpu.VMEM((LANES, nc), BF16), pltpu.VMEM((LANES, tq), F32),
                        pltpu.VMEM((tk // tq, tk, mcols), F32), pltpu.VMEM((WINDOW + tq, mcols), F32),
                        pltpu.VMEM((LANES, mcols), F32)] + _flash_scratch(tk, mcols),
        compiler_params=_params(("parallel", "parallel", "arbitrary")),
        name="nsa_attn",
    )(slabs, kvc, kvc, slabs, slabs, slabs, slabs, kx, cx_cmp, ovt, qx_nsa, misc)


def _merge_kernel(h_ref, of_ref, od_ref, on_ref, wg0_ref, wg1_ref, wg2_ref, wf_ref, wd_ref, wn_ref, o_ref):
    h = h_ref[...]
    acc = jax.nn.sigmoid(_dot(h, wg0_ref[...])) * _dot(of_ref[...], wf_ref[...])
    acc = acc + jax.nn.sigmoid(_dot(h, wg1_ref[...])) * _dot(od_ref[...], wd_ref[...])
    acc = acc + jax.nn.sigmoid(_dot(h, wg2_ref[...])) * _dot(on_ref[...], wn_ref[...])
    o_ref[...] = acc.astype(BF16)


def _merge(h, o_fox, o_diff, o_nsa, w_gate, wb_fox, wb_diff, wb_nsa, l):
    m, d = h.shape
    tm = _pick(m, (1024, 512, 256))
    tn = _pick(d, (256, 128))
    nj = d // tn
    row = lambda width: pl.BlockSpec((tm, width), lambda i, j: (i, 0))
    gate = lambda t: pl.BlockSpec((None, d, tn), lambda i, j: (l, 0, t * nj + j))
    col = lambda k: pl.BlockSpec((None, k, tn), lambda i, j: (l, 0, j))
    return pl.pallas_call(
        _merge_kernel,
        grid=(m // tm, nj),
        in_specs=[row(d), row(o_fox.shape[1]), row(o_diff.shape[1]), row(o_nsa.shape[1]),
                  gate(0), gate(1), gate(2),
                  col(wb_fox.shape[1]), col(wb_diff.shape[1]), col(wb_nsa.shape[1])],
        out_specs=pl.BlockSpec((tm, tn), lambda i, j: (i, j)),
        out_shape=jax.ShapeDtypeStruct((m, d), BF16),
        compiler_params=_params(("parallel", "arbitrary")),
        name="gate_merge",
    )(h, o_fox, o_diff, o_nsa, w_gate, w_gate, w_gate, wb_fox, wb_diff, wb_nsa)


def _wout_kernel(a_ref, w_ref, x_ref, gp_ref, gn_ref, xo_ref, ho_ref):
    y = _dot(a_ref[...], w_ref[...])
    x_new = x_ref[...] + _rms(y, gp_ref[...])
    xo_ref[...] = x_new
    ho_ref[...] = _rms(x_new, gn_ref[...]).astype(BF16)


def _wout(a, w, x, g_post, g_next, l, name):
    m, d = x.shape
    k = a.shape[1]
    tm = _pick(m, (512, 256, 128)) if k * d * 2 <= 8 * 1024 * 1024 else _pick(m, (256, 128))
    row = lambda width: pl.BlockSpec((tm, width), lambda i: (i, 0))
    vec = lambda: pl.BlockSpec((1, d), lambda i: (0, 0))
    return pl.pallas_call(
        _wout_kernel,
        grid=(m // tm,),
        in_specs=[row(k), pl.BlockSpec((None, k, d), lambda i: (l, 0, 0), pipeline_mode=pl.Buffered(1)),
                  row(d), vec(), vec()],
        out_specs=[row(d), row(d)],
        out_shape=[jax.ShapeDtypeStruct((m, d), F32), jax.ShapeDtypeStruct((m, d), BF16)],
        compiler_params=_params(("parallel",)),
        name=name,
    )(a, w, x, g_post, g_next)


def _ffn_up_kernel(h_ref, wg_ref, wu_ref, o_ref, wgb_ref, wub_ref):
    @pl.when(pl.program_id(1) == 0)
    def _():
        wgb_ref[...] = wg_ref[...].astype(BF16)
        wub_ref[...] = wu_ref[...].astype(BF16)

    h = h_ref[...]
    gate = _dot(h, wgb_ref[...])
    o_ref[...] = (gate * jax.nn.sigmoid(gate) * _dot(h, wub_ref[...])).astype(BF16)


def _ffn_up(h, w_up, l):
    m, d = h.shape
    dff = w_up.shape[2] // 2
    tm = _pick(m, (1024, 512, 256))
    tn = _pick(dff, (512, 256, 128))
    nj = dff // tn
    return pl.pallas_call(
        _ffn_up_kernel,
        grid=(nj, m // tm),
        in_specs=[pl.BlockSpec((tm, d), lambda j, i: (i, 0)),
                  pl.BlockSpec((None, d, tn), lambda j, i: (l, 0, j)),
                  pl.BlockSpec((None, d, tn), lambda j, i: (l, 0, nj + j))],
        out_specs=pl.BlockSpec((tm, tn), lambda j, i: (i, j)),
        out_shape=jax.ShapeDtypeStruct((m, dff), BF16),
        scratch_shapes=[pltpu.VMEM((d, tn), BF16), pltpu.VMEM((d, tn), BF16)],
        compiler_params=_params(("parallel", "arbitrary")),
        name="ffn_up",
    )(h, w_up, w_up)


def _pos_columns(pos):
    lane = jnp.arange(LANES)[None, :]
    hi = (lane >= AUG_HI) & (lane < AUG_HI + AUG_TERMS)
    lo = (lane >= AUG_LO) & (lane < AUG_LO + AUG_TERMS)
    return jnp.where(hi, (pos // LANES)[:, None], jnp.where(lo, (pos % LANES)[:, None], 0))


def _key_aug_table(seq):
    j = jnp.arange(seq)
    lane = jnp.arange(LANES)[None, :]
    onehot = (lane == (j // SEL_BLOCK)[:, None]) & (lane < SEL_BLOCK)
    return (_pos_columns(j) + onehot.astype(jnp.int32)).astype(BF16)


def _cmp_aug_table(nrow):
    return _pos_columns(jnp.arange(nrow) * CMP_STRIDE + CMP_BLOCK - 1).astype(BF16)


def _overlap_table_t(nrow, seq):
    start = jnp.arange(nrow)[None, :] * CMP_STRIDE
    blk = jnp.arange(LANES)[:, None]
    sel = blk * SEL_BLOCK
    ov = ((start < sel + SEL_BLOCK) & (start + CMP_BLOCK - 1 >= sel) & (blk < seq // SEL_BLOCK)
          & (jnp.arange(nrow)[None, :] < (seq - CMP_BLOCK) // CMP_STRIDE + 1))
    return ov.astype(BF16)


def _query_aug_rows(n_heads):
    slopes = 2.0 ** (-8.0 * jnp.arange(1, n_heads + 1, dtype=F32) / n_heads)
    terms = [t.astype(F32) for t in _split3(jnp.float32(LOG2E))]
    lane = jnp.arange(LANES)[None, :]
    out = jnp.where(lane == AUG_PAD, 1.0, jnp.zeros((n_heads, LANES), F32))
    for i, t in enumerate(terms):
        out = jnp.where(lane == AUG_HI + i, slopes[:, None] * t * LANES, out)
        out = jnp.where(lane == AUG_LO + i, slopes[:, None] * t, out)
    return out


def kernel(x, w_in, fox_forget_bias, diff_lambda, diff_subln, nsa_cmp_pos, nsa_cmp_w1, nsa_cmp_w2,
           w_branch_fox, w_branch_diff, w_branch_nsa, w_gate, w_out, norm_gains, w_ffn_up, w_ffn_down):
    batch, seq, d = x.shape
    depth = w_in.shape[0]
    m = batch * seq
    fw, dw, nw, kvw = FOX_HEADS * HEAD_DIM, DIFF_HEADS * HEAD_DIM, NSA_HEADS * HEAD_DIM, NSA_KV_HEADS * HEAD_DIM
    ff0 = 3 * fw
    dq0 = ff0 + FOX_HEADS
    ng0 = dq0 + 3 * dw + nw + 6 * kvw
    n_main = ng0 - FOX_HEADS
    assert n_main == N_SLABS * LANES and w_in.shape[2] == ng0 + 3 * NSA_HEADS

    w_main = jnp.concatenate([w_in[:, :, :ff0], w_in[:, :, dq0:ng0]], axis=2).astype(BF16)
    w_misc = jnp.concatenate([w_in[:, :, ff0:dq0], w_in[:, :, ng0:],
                              jnp.zeros((depth, d, LANES - FOX_HEADS - 3 * NSA_HEADS), F32)], axis=2).astype(BF16)
    colscale = jnp.ones((n_main,), F32)
    colscale = colscale.at[SLAB_FQ * LANES:SLAB_FK * LANES].set(HEAD_DIM ** -0.5 * LOG2E)
    colscale = colscale.at[SLAB_DQ * LANES:SLAB_DK * LANES].set(DIFF_HALF ** -0.5 * LOG2E)
    colscale = colscale.at[SLAB_NQ * LANES:SLAB_NKC * LANES].set(HEAD_DIM ** -0.5 * LOG2E)
    colscale = colscale[None, :]
    fbias = jnp.pad(fox_forget_bias.astype(F32), ((0, 0), (0, LANES - FOX_HEADS)))[:, None, :]
    half = CMP_STRIDE * HEAD_DIM
    pos2 = nsa_cmp_pos.astype(F32).reshape(depth, 2, 2, half)
    w1 = nsa_cmp_w1.astype(BF16)
    w2 = nsa_cmp_w2.astype(BF16)
    wbf, wbd, wbn = w_branch_fox.astype(BF16), w_branch_diff.astype(BF16), w_branch_nsa.astype(BF16)
    wg, wo = w_gate.astype(BF16), w_out.astype(BF16)
    wup, wdn = w_ffn_up.astype(F32), w_ffn_down.astype(BF16)
    gains = norm_gains.astype(F32)

    nrow = seq // CMP_STRIDE
    kx = _key_aug_table(seq)
    cx_cmp = _cmp_aug_table(nrow)
    ovt = _overlap_table_t(nrow, seq)
    qx_diff = _query_aug_rows(DIFF_HEADS)[:, None, :]
    qx_nsa = _query_aug_rows(NSA_HEADS).reshape(NSA_KV_HEADS, NSA_REP, LANES)

    xf = x.reshape(m, d).astype(F32)
    h = _norm(xf, gains[0, 0][None, :])
    for l in range(depth):
        lam_init = 0.8 - 0.6 * math.exp(-0.3 * l)
        slabs, misc = _inproj(h, w_main, colscale, w_misc, l)
        cx_fox = _logf(misc, fbias[l], batch, seq)
        o_fox = _fox(slabs, cx_fox, batch, seq)
        o_diff = _diff(slabs, kx, qx_diff, diff_lambda[l].astype(F32), diff_subln[l].astype(F32)[None, :],
                       batch, seq, lam_init)
        ykv = slabs[SLAB_NKC:SLAB_NKC + 4].reshape(4, batch, nrow, half)
        kvc = _compress(ykv, pos2, w1, w2, batch, l)
        o_nsa = _nsa(slabs, kvc, kx, cx_cmp, ovt, qx_nsa, misc, batch, seq)
        merged = _merge(h, o_fox, o_diff, o_nsa, wg, wbf, wbd, wbn, l)
        xf, h2 = _wout(merged, wo, xf, gains[l, 1][None, :], gains[l, 2][None, :], l, "out_proj")
        act = _ffn_up(h2, wup, l)
        g_next = gains[min(l + 1, depth - 1), 0][None, :]
        xf, h = _wout(act, wdn, xf, gains[l, 3][None, :], g_next, l, "ffn_down")
    return xf.reshape(batch, seq, d).astype(x.dtype)
```

```python
import functools
import math

import jax
import jax.numpy as jnp
from jax import lax
from jax.experimental import pallas as pl
from jax.experimental.pallas import tpu as pltpu

F32 = jnp.float32
BF16 = jnp.bfloat16

HEAD_DIM = 128
FOX_HEADS = 4
DIFF_HEADS = 4
DIFF_HALF = HEAD_DIM // 2
NSA_HEADS = 8
NSA_KV_HEADS = 2
NSA_REP = NSA_HEADS // NSA_KV_HEADS
CMP_BLOCK = 32
CMP_STRIDE = 16
CMP_HIDDEN = 256
SEL_BLOCK = 64
SEL_TOPK = 16
WINDOW = 512
N_BRANCHES = 3
EPS = 1e-6
NEG_INF = -1e30
FORCE_SCORE = 1e4
SEL_MASK = -32768.0
LOG2E = math.log2(math.e)
LANES = 128

SLAB_FQ, SLAB_FK, SLAB_FV = 0, 4, 8
SLAB_DQ, SLAB_DK, SLAB_DV = 12, 16, 20
SLAB_NQ = 24
SLAB_NKC, SLAB_NVC, SLAB_NKS, SLAB_NVS, SLAB_NKW, SLAB_NVW = 32, 34, 36, 38, 40, 42
N_SLABS = 44
MISC_FF = 0
MISC_NG = 4
AUG_HI = 64
AUG_LO = 67
AUG_TERMS = 3
AUG_PAD = 70

VMEM_LIMIT = 56 * 1024 * 1024


def _pick(n, prefs):
    for p in prefs:
        if p <= n and n % p == 0:
            return p
    return n


def _params(sem):
    return pltpu.CompilerParams(dimension_semantics=sem, vmem_limit_bytes=VMEM_LIMIT)


def _rms(y, g):
    return y * lax.rsqrt(jnp.mean(y * y, axis=-1, keepdims=True) + EPS) * g


def _dot(a, b):
    return jnp.dot(a, b, preferred_element_type=F32)


def _dot_nt(a, b):
    return lax.dot_general(a, b, (((1,), (1,)), ((), ())), preferred_element_type=F32)


def _split3(x):
    hi = x.astype(BF16)
    r = x - hi.astype(F32)
    mid = r.astype(BF16)
    lo = (r - mid.astype(F32)).astype(BF16)
    return hi, mid, lo


def _norm_kernel(x_ref, g_ref, h_ref):
    h_ref[...] = _rms(x_ref[...], g_ref[...]).astype(BF16)


def _norm(x, g):
    m, d = x.shape
    tm = _pick(m, (512, 256, 128))
    return pl.pallas_call(
        _norm_kernel,
        grid=(m // tm,),
        in_specs=[pl.BlockSpec((tm, d), lambda i: (i, 0)), pl.BlockSpec((1, d), lambda i: (0, 0))],
        out_specs=pl.BlockSpec((tm, d), lambda i: (i, 0)),
        out_shape=jax.ShapeDtypeStruct((m, d), BF16),
        compiler_params=_params(("parallel",)),
        name="norm_in",
    )(x, g)


def _inproj_kernel(h_ref, w_ref, cs_ref, wm_ref, o_ref, misc_ref):
    h = h_ref[...]
    acc = _dot(h, w_ref[...]) * cs_ref[...]
    for s in range(o_ref.shape[0]):
        o_ref[s] = acc[:, s * LANES:(s + 1) * LANES].astype(BF16)

    @pl.when(pl.program_id(1) == 0)
    def _():
        misc_ref[...] = _dot(h, wm_ref[...])


def _inproj(h, w_main, colscale, w_misc, l):
    m, d = h.shape
    n = w_main.shape[2]
    tm = _pick(m, (1024, 512, 256))
    tn = _pick(n, (11 * LANES, 4 * LANES))
    return pl.pallas_call(
        _inproj_kernel,
        grid=(m // tm, n // tn),
        in_specs=[pl.BlockSpec((tm, d), lambda i, j: (i, 0)),
                  pl.BlockSpec((None, d, tn), lambda i, j: (l, 0, j)),
                  pl.BlockSpec((1, tn), lambda i, j: (0, j)),
                  pl.BlockSpec((None, d, LANES), lambda i, j: (l, 0, 0))],
        out_specs=[pl.BlockSpec((tn // LANES, tm, LANES), lambda i, j: (j, i, 0)),
                   pl.BlockSpec((tm, LANES), lambda i, j: (i, 0))],
        out_shape=[jax.ShapeDtypeStruct((n // LANES, m, LANES), BF16),
                   jax.ShapeDtypeStruct((m, LANES), F32)],
        compiler_params=_params(("parallel", "arbitrary")),
        name="inproj",
    )(h, w_main, colscale, w_misc)


def _logf_kernel(misc_ref, bias_ref, o_ref, *, tc):
    s = misc_ref.shape[0]
    row = lax.broadcasted_iota(jnp.int32, (tc, tc), 0)
    col = lax.broadcasted_iota(jnp.int32, (tc, tc), 1)
    tri = jnp.where(col <= row, 1.0, 0.0).astype(BF16)
    lane = lax.broadcasted_iota(jnp.int32, (tc, LANES), 1)

    def chunk(c, carry):
        r0 = pl.multiple_of(c * tc, tc)
        z = misc_ref[pl.ds(r0, tc), :] + bias_ref[...]
        lf = jnp.minimum(z, 0.0) - jnp.log1p(jnp.exp(-jnp.abs(z)))
        hi, mid, lo = _split3(lf)
        cum = _dot(tri, hi) + _dot(tri, mid) + _dot(tri, lo) + carry
        for hd in range(FOX_HEADS):
            c2 = jnp.broadcast_to(cum[:, hd:hd + 1], (tc, LANES)) * LOG2E
            c_hi = c2.astype(BF16).astype(F32)
            c_mid = (c2 - c_hi).astype(BF16).astype(F32)
            aug = jnp.where(lane == 0, c_hi, jnp.where(lane == 1, c_mid,
                                                      jnp.where(lane == 2, c2 - c_hi - c_mid, 0.0)))
            o_ref[hd, pl.ds(r0, tc), :] = aug.astype(BF16)
        return cum[tc - 1:tc, :]

    lax.fori_loop(0, s // tc, chunk, jnp.zeros((1, LANES), F32))


def _logf(misc, bias_row, batch, seq):
    tc = _pick(seq, (256, 128))
    return pl.pallas_call(
        functools.partial(_logf_kernel, tc=tc),
        grid=(batch,),
        in_specs=[pl.BlockSpec((seq, LANES), lambda b: (b, 0)), pl.BlockSpec((1, LANES), lambda b: (0, 0))],
        out_specs=pl.BlockSpec((FOX_HEADS, seq, LANES), lambda b: (0, b, 0)),
        out_shape=jax.ShapeDtypeStruct((FOX_HEADS, batch * seq, LANES), BF16),
        compiler_params=_params(("parallel",)),
        name="fox_logf",
    )(misc, bias_row)


def _flash_scratch(tk, mcols):
    return [pltpu.VMEM((tk, mcols), F32), pltpu.VMEM((1, mcols), F32), pltpu.VMEM((1, mcols), F32),
            pltpu.VMEM((LANES, mcols), F32)]


def _causal_bias(tk, mcols, tq, offset):
    rk = lax.broadcasted_iota(jnp.int32, (tk, mcols), 0)
    rq = lax.broadcasted_iota(jnp.int32, (tk, mcols), 1) & (tq - 1)
    return jnp.where(rk <= rq + offset, 0.0, NEG_INF)


def _flash_tiles(qa, ka_ref, vt_ref, n_full, diag_bias_ref, tk, scratch, next_qa=None):
    s_ref, m_ref, l_ref, acc_ref = scratch

    def logits(j, queries=qa):
        k0 = pl.multiple_of(j * tk, tk)
        return _dot_nt(ka_ref[pl.ds(k0, tk), :], queries)

    def step(s, j):
        k0 = pl.multiple_of(j * tk, tk)
        m_prev = m_ref[...]
        m_new = jnp.maximum(m_prev, jnp.max(s, axis=0, keepdims=True))
        alpha = jnp.exp2(m_prev - m_new)
        p = jnp.exp2(s - m_new)
        l_ref[...] = alpha * l_ref[...] + jnp.sum(p, axis=0, keepdims=True)
        acc_ref[...] = alpha * acc_ref[...] + _dot(vt_ref[:, pl.ds(k0, tk)], p.astype(BF16))
        m_ref[...] = m_new

    m_ref[...] = jnp.full(m_ref.shape, NEG_INF, F32)
    l_ref[...] = jnp.zeros(l_ref.shape, F32)
    acc_ref[...] = jnp.zeros(acc_ref.shape, F32)
    if next_qa is None:
        s_ref[...] = logits(0)

    def full(j, c):
        s_next = logits(j + 1)
        step(s_ref[...], j)
        s_ref[...] = s_next
        return c

    lax.fori_loop(0, n_full, full, 0)
    if next_qa is None:
        step(s_ref[...] + diag_bias_ref[...], n_full)
    else:
        s_next = logits(0, next_qa)
        step(s_ref[...] + diag_bias_ref[...], n_full)
        s_ref[...] = s_next
    return acc_ref[...] * (1.0 / l_ref[...])


def _stage_kv(ka_ref, vt_ref, k_ref, kx_ref, v_ref, chunk, pad=0):
    def body(c, carry):
        r0 = pl.multiple_of(c * chunk, chunk)
        ka_ref[pl.ds(pad + r0, chunk), 0:LANES] = k_ref[0, pl.ds(r0, chunk), :]
        ka_ref[pl.ds(pad + r0, chunk), LANES:2 * LANES] = kx_ref[pl.ds(r0, chunk), :]
        vt_ref[:, pl.ds(pad + r0, chunk)] = v_ref[0, pl.ds(r0, chunk), :].astype(F32).T.astype(BF16)
        return carry

    lax.fori_loop(0, v_ref.shape[1] // chunk, body, 0)


def _fox_kernel(q_ref, k_ref, v_ref, cx_ref, o_ref, ka_ref, vt_ref, tri_ref, *flash, tq, tk):
    _stage_kv(ka_ref, vt_ref, k_ref, cx_ref.at[0], v_ref, tk)
    for o in range(tk // tq):
        tri_ref[o] = _causal_bias(tk, tq, tq, o * tq)
    lane = lax.broadcasted_iota(jnp.int32, (tq, LANES), 1)
    qx = jnp.where(lane < 3, -1.0, 0.0).astype(BF16)

    nq = q_ref.shape[1] // tq

    def queries(qi):
        q0 = pl.multiple_of(qi * tq, tq)
        return jnp.concatenate([q_ref[0, pl.ds(q0, tq), :], qx], axis=1)

    flash[0][...] = _dot_nt(ka_ref[0:tk, :], queries(0))

    def qstep(qi, c):
        o = _flash_tiles(queries(qi), ka_ref, vt_ref, (qi * tq) // tk, tri_ref.at[qi & (tk // tq - 1)], tk, flash,
                         next_qa=queries(jnp.minimum(qi + 1, nq - 1)))
        o_ref[pl.ds(pl.multiple_of(qi * tq, tq), tq), :] = o.T.astype(BF16)
        return c

    lax.fori_loop(0, nq, qstep, 0)


def _attn_tiles(seq):
    tq = _pick(seq, (512, 256, 128))
    tk = 2 * tq if seq % (2 * tq) == 0 else tq
    return tq, tk


def _fox(slabs, cx, batch, seq):
    tk = _attn_tiles(seq)[1]
    tq = tk
    m = batch * seq
    head = lambda base: pl.BlockSpec((1, seq, LANES), lambda b, h: (base + h, b, 0))
    return pl.pallas_call(
        functools.partial(_fox_kernel, tq=tq, tk=tk),
        grid=(batch, FOX_HEADS),
        in_specs=[head(SLAB_FQ), head(SLAB_FK), head(SLAB_FV), head(0)],
        out_specs=pl.BlockSpec((seq, LANES), lambda b, h: (b, h)),
        out_shape=jax.ShapeDtypeStruct((m, FOX_HEADS * HEAD_DIM), BF16),
        scratch_shapes=[pltpu.VMEM((seq, 2 * LANES), BF16), pltpu.VMEM((LANES, seq), BF16),
                        pltpu.VMEM((tk // tq, tk, tq), F32)] + _flash_scratch(tk, tq),
        compiler_params=_params(("parallel", "parallel")),
        name="fox_attn",
    )(slabs, slabs, slabs, cx)


def _diff_kernel(q_ref, k_ref, v_ref, kx_ref, qx_ref, lam_ref, sub_ref, o_ref, ka_ref, vt_ref, tri_ref, *flash,
                 t, tk, lam_init):
    _stage_kv(ka_ref, vt_ref, k_ref, kx_ref, v_ref, tk)
    for o in range(tk // t):
        tri_ref[o] = _causal_bias(tk, 2 * t, t, o * t)
    lane = lax.broadcasted_iota(jnp.int32, (t, LANES), 1)
    qx = jnp.broadcast_to(qx_ref[0], (t, LANES)).astype(BF16)
    lv = lam_ref[...]
    lam = (jnp.exp(jnp.sum(lv[0:1] * lv[1:2], axis=-1, keepdims=True))
           - jnp.exp(jnp.sum(lv[2:3] * lv[3:4], axis=-1, keepdims=True)) + lam_init)

    nq = q_ref.shape[1] // t

    def queries(qi):
        q = q_ref[0, pl.ds(pl.multiple_of(qi * t, t), t), :].astype(F32)
        return jnp.concatenate([
            jnp.concatenate([jnp.where(lane < DIFF_HALF, q, 0.0).astype(BF16), qx], axis=1),
            jnp.concatenate([jnp.where(lane >= DIFF_HALF, q, 0.0).astype(BF16), qx], axis=1)], axis=0)

    flash[0][...] = _dot_nt(ka_ref[0:tk, :], queries(0))

    def qstep(qi, c):
        o = _flash_tiles(queries(qi), ka_ref, vt_ref, (qi * t) // tk, tri_ref.at[qi & (tk // t - 1)], tk, flash,
                         next_qa=queries(jnp.minimum(qi + 1, nq - 1)))
        o = (o[:, :t] - lam * o[:, t:]).T
        o_ref[pl.ds(pl.multiple_of(qi * t, t), t), :] = (_rms(o, sub_ref[...]) * (1.0 - lam_init)).astype(BF16)
        return c

    lax.fori_loop(0, nq, qstep, 0)


def _diff(slabs, kx, qx_diff, lam_vec, subln, batch, seq, lam_init):
    tk = _attn_tiles(seq)[1]
    t = tk
    m = batch * seq
    head = lambda base: pl.BlockSpec((1, seq, LANES), lambda b, h: (base + h, b, 0))
    return pl.pallas_call(
        functools.partial(_diff_kernel, t=t, tk=tk, lam_init=lam_init),
        grid=(batch, DIFF_HEADS),
        in_specs=[head(SLAB_DQ), head(SLAB_DK), head(SLAB_DV),
                  pl.BlockSpec((seq, LANES), lambda b, h: (0, 0)),
                  pl.BlockSpec((1, 1, LANES), lambda b, h: (h, 0, 0)),
                  pl.BlockSpec((4, DIFF_HALF), lambda b, h: (0, 0)),
                  pl.BlockSpec((1, LANES), lambda b, h: (0, 0))],
        out_specs=pl.BlockSpec((seq, LANES), lambda b, h: (b, h)),
        out_shape=jax.ShapeDtypeStruct((m, DIFF_HEADS * HEAD_DIM), BF16),
        scratch_shapes=[pltpu.VMEM((seq, 2 * LANES), BF16), pltpu.VMEM((LANES, seq), BF16),
                        pltpu.VMEM((tk // t, tk, 2 * t), F32)] + _flash_scratch(tk, 2 * t),
        compiler_params=_params(("parallel", "parallel")),
        name="diff_attn",
    )(slabs, slabs, slabs, kx, qx_diff, lam_vec, subln)


def _compress_kernel(y_ref, pos_ref, w1_ref, w2_ref, o_ref):
    half = y_ref.shape[3]
    y = y_ref[0, 0].astype(F32)
    top = (y + pos_ref[0, 0:1, :]).astype(BF16)
    bot = (y + pos_ref[0, 1:2, :]).astype(BF16)
    a = _dot(top, w1_ref[0, 0:half, :])
    b = _dot(bot, w1_ref[0, half:2 * half, :])
    nrow = a.shape[0]
    hid = a + pltpu.roll(b, nrow - 1, 0)
    hid = hid * jax.nn.sigmoid(hid)
    o_ref[0, 0] = _dot(hid.astype(BF16), w2_ref[0]).astype(BF16)


def _compress(ykv, pos2, w1, w2, batch, l):
    nrow, half = ykv.shape[2], ykv.shape[3]
    return pl.pallas_call(
        _compress_kernel,
        grid=(4, batch),
        in_specs=[pl.BlockSpec((1, 1, nrow, half), lambda s, b: (s, b, 0, 0)),
                  pl.BlockSpec((None, 1, 2, half), lambda s, b: (l, s // 2, 0, 0)),
                  pl.BlockSpec((None, 1, 2 * half, CMP_HIDDEN), lambda s, b: (l, s // 2, 0, 0)),
                  pl.BlockSpec((None, 1, CMP_HIDDEN, HEAD_DIM), lambda s, b: (l, s // 2, 0, 0))],
        out_specs=pl.BlockSpec((1, 1, nrow, HEAD_DIM), lambda s, b: (s, b, 0, 0)),
        out_shape=jax.ShapeDtypeStruct((4, batch, nrow, HEAD_DIM), BF16),
        compiler_params=_params(("parallel", "parallel")),
        name="nsa_compress",
    )(ykv, pos2, w1, w2)


def _topk_bias(cand, topk):
    nsel, tq = cand.shape
    sub = 8
    blocks = [cand[b * sub:(b + 1) * sub] for b in range(nsel // sub)]
    jidx = lax.broadcasted_iota(jnp.int32, (sub, tq), 0)
    ranks = [jnp.zeros((sub, tq), F32) for _ in blocks]
    for k in range(nsel):
        rk = cand[k:k + 1, :]
        for b, cb in enumerate(blocks):
            if b * sub > k:
                beats = rk >= cb
            elif b * sub + sub - 1 < k:
                beats = rk > cb
            else:
                beats = (rk > cb) | ((rk == cb) & (jidx + b * sub > k))
            ranks[b] = ranks[b] + jnp.where(beats, 1.0, 0.0)
    rank = jnp.concatenate(ranks, axis=0)
    return jnp.where(rank < topk, 0.0, SEL_MASK)


def _nsa_kernel(q_ref, kc_ref, vc_ref, ks_ref, vs_ref, kw_ref, vw_ref, kx_ref, cx_ref, ovt_ref, qx_ref,
                misc_ref, o_ref, ksa_ref, kwa_ref, vst_ref, vwt_ref, vct_ref, gt_ref, dbias_ref, wbias_ref,
                comb_ref, *flash, tq, tk):
    g = pl.program_id(1)
    qi = pl.program_id(2)
    mcols = NSA_REP * tq
    nc = kc_ref.shape[2]
    seq = kx_ref.shape[0]
    nsel = seq // SEL_BLOCK
    span = WINDOW + tq

    @pl.when(qi == 0)
    def _():
        _stage_kv(ksa_ref, vst_ref, ks_ref, kx_ref, vs_ref, tk)
        _stage_kv(kwa_ref, vwt_ref, kw_ref, kx_ref, vw_ref, tk, pad=WINDOW)
        pad_lane = lax.broadcasted_iota(jnp.int32, (WINDOW, 2 * LANES), 1)
        kwa_ref[0:WINDOW, :] = jnp.where(pad_lane == LANES + AUG_PAD, SEL_MASK, 0.0).astype(BF16)
        vwt_ref[:, 0:WINDOW] = jnp.zeros((LANES, WINDOW), BF16)
        vct_ref[...] = vc_ref[0, 0].astype(F32).T.astype(BF16)
        for o in range(tk // tq):
            dbias_ref[o] = _causal_bias(tk, mcols, tq, o * tq)
        rk = lax.broadcasted_iota(jnp.int32, (span, mcols), 0)
        rq = lax.broadcasted_iota(jnp.int32, (span, mcols), 1) & (tq - 1)
        wbias_ref[...] = jnp.where((rk > rq) & (rk <= rq + WINDOW), 0.0, NEG_INF)

    qx = jnp.concatenate([jnp.broadcast_to(qx_ref[0, r:r + 1, :], (tq, LANES)) for r in range(NSA_REP)], axis=0)
    kca = jnp.concatenate([kc_ref[0, 0], cx_ref[...]], axis=1)
    _nsa_query_tile(qi, g, qx, kca, q_ref, ovt_ref, misc_ref, o_ref, ksa_ref, kwa_ref, vst_ref, vwt_ref,
                    vct_ref, gt_ref, dbias_ref, wbias_ref, comb_ref, flash, tq=tq, tk=tk, nc=nc, nsel=nsel)


def _nsa_query_tile(qi, g, qx, kca, q_ref, ovt_ref, misc_ref, o_ref, ksa_ref, kwa_ref, vst_ref, vwt_ref, vct_ref,
                    gt_ref, dbias_ref, wbias_ref, comb_ref, flash, *, tq, tk, nc, nsel):
    mcols = NSA_REP * tq
    span = WINDOW + tq
    q0 = pl.multiple_of(qi * tq, tq)
    q4 = q_ref[...].reshape(mcols, LANES)
    qa = jnp.concatenate([q4, qx.astype(BF16)], axis=1)
    col = lax.broadcasted_iota(jnp.int32, (1, mcols), 1)
    col_pos = q0 + (col & (tq - 1))

    sc = _dot_nt(kca, qa)
    cend = lax.broadcasted_iota(jnp.int32, (nc, mcols), 0) * CMP_STRIDE + (CMP_BLOCK - 1)
    sc = jnp.where(cend <= col_pos, sc, NEG_INF)
    e = jnp.exp2(sc - jnp.max(sc, axis=0, keepdims=True))
    inv = jnp.where(col_pos >= CMP_BLOCK - 1, 1.0 / jnp.sum(e, axis=0, keepdims=True), 0.0)
    p = e * inv
    o_cmp = _dot(vct_ref[...], p.astype(BF16))

    psum = p[:, 0:tq]
    for r in range(1, NSA_REP):
        psum = psum + p[:, r * tq:(r + 1) * tq]
    p_hi = psum.astype(BF16)
    p_lo = (psum - p_hi.astype(F32)).astype(BF16)
    imp = _dot(ovt_ref[...], p_hi) + _dot(ovt_ref[...], p_lo)
    qpos = q0 + lax.broadcasted_iota(jnp.int32, (LANES, tq), 1)
    blk = lax.broadcasted_iota(jnp.int32, (LANES, tq), 0)
    cur = jnp.right_shift(qpos, SEL_BLOCK.bit_length() - 1)
    forced = (blk == 0) | (blk == cur) | (blk == cur - 1)
    imp = jnp.where(blk <= cur, jnp.where(forced, FORCE_SCORE, imp), -1.0)
    bias_t = _topk_bias(imp[0:nsel], min(SEL_TOPK, nsel))
    if nsel < LANES:
        bias_t = jnp.concatenate([bias_t, jnp.zeros((LANES - nsel, tq), F32)], axis=0)
    selbias = jnp.concatenate([bias_t.T] * NSA_REP, axis=0)
    lane4 = lax.broadcasted_iota(jnp.int32, (mcols, LANES), 1)
    qa_sel = jnp.concatenate([q4, jnp.where(lane4 < SEL_BLOCK, selbias, qx).astype(BF16)], axis=1)

    sw = _dot_nt(kwa_ref[pl.ds(q0, span), :], qa) + wbias_ref[...]
    ew = jnp.exp2(sw - jnp.max(sw, axis=0, keepdims=True))
    o_win = _dot(vwt_ref[:, pl.ds(q0, span)], ew.astype(BF16)) * (1.0 / jnp.sum(ew, axis=0, keepdims=True))

    gt_ref[...] = jax.nn.sigmoid(misc_ref[...]).T
    gate = lambda r, i: gt_ref[pl.ds(MISC_NG + 3 * (NSA_REP * g + r) + i, 1), :]
    for r in range(NSA_REP):
        sl = slice(r * tq, (r + 1) * tq)
        comb_ref[:, sl] = gate(r, 0) * o_cmp[:, sl] + gate(r, 2) * o_win[:, sl]

    o_sel = _flash_tiles(qa_sel, ksa_ref, vst_ref, q0 // tk, dbias_ref.at[qi & (tk // tq - 1)], tk, flash)
    for r in range(NSA_REP):
        sl = slice(r * tq, (r + 1) * tq)
        out = comb_ref[:, sl] + gate(r, 1) * o_sel[:, sl]
        o_ref[:, r * LANES:(r + 1) * LANES] = out.T.astype(BF16)


def _nsa(slabs, kvc, kx, cx_cmp, ovt, qx_nsa, misc, batch, seq):
    tq = _pick(seq, (512, 256, 128))
    tk = _pick(seq, (512, 256, 128))
    assert tq & (tq - 1) == 0 and tk % tq == 0 and seq >= WINDOW + tq
    nq = seq // tq
    m = batch * seq
    mcols = NSA_REP * tq
    nc = kvc.shape[2]
    kv_spec = lambda base: pl.BlockSpec((1, seq, LANES), lambda b, g, i: (base + g, b, 0))
    const = lambda shape: pl.BlockSpec(shape, lambda b, g, i: (0,) * len(shape), pipeline_mode=pl.Buffered(1))
    return pl.pallas_call(
        functools.partial(_nsa_kernel, tq=tq, tk=tk),
        grid=(batch, NSA_KV_HEADS, nq),
        in_specs=[pl.BlockSpec((NSA_REP, tq, LANES), lambda b, g, i: (SLAB_NQ // NSA_REP + g, b * nq + i, 0)),
                  pl.BlockSpec((1, 1, nc, LANES), lambda b, g, i: (g, b, 0, 0)),
                  pl.BlockSpec((1, 1, nc, LANES), lambda b, g, i: (2 + g, b, 0, 0)),
                  kv_spec(SLAB_NKS), kv_spec(SLAB_NVS), kv_spec(SLAB_NKW), kv_spec(SLAB_NVW),
                  const((seq, LANES)), const((nc, LANES)), const((LANES, nc)),
                  pl.BlockSpec((1, NSA_REP, LANES), lambda b, g, i: (g, 0, 0)),
                  pl.BlockSpec((tq, LANES), lambda b, g, i: (b * nq + i, 0))],
        out_specs=pl.BlockSpec((tq, NSA_REP * LANES), lambda b, g, i: (b * nq + i, g)),
        out_shape=jax.ShapeDtypeStruct((m, NSA_HEADS * HEAD_DIM), BF16),
        scratch_shapes=[pltpu.VMEM((seq, 2 * LANES), BF16), pltpu.VMEM((WINDOW + seq, 2 * LANES), BF16),
                        pltpu.VMEM((LANES, seq), BF16), pltpu.VMEM((LANES, WINDOW + seq), BF16),
                        pltpu.VMEM((LANES, nc), BF16), pltpu.VMEM((LANES, tq), F32),
                        pltpu.VMEM((tk // tq, tk, mcols), F32), pltpu.VMEM((WINDOW + tq, mcols), F32),
                        pltpu.VMEM((LANES, mcols), F32)] + _flash_scratch(tk, mcols),
        compiler_params=_params(("parallel", "parallel", "arbitrary")),
        name="nsa_attn",
    )(slabs, kvc, kvc, slabs, slabs, slabs, slabs, kx, cx_cmp, ovt, qx_nsa, misc)


def _merge_kernel(h_ref, of_ref, od_ref, on_ref, wg0_ref, wg1_ref, wg2_ref, wf_ref, wd_ref, wn_ref, o_ref):
    h = h_ref[...]
    acc = jax.nn.sigmoid(_dot(h, wg0_ref[...])) * _dot(of_ref[...], wf_ref[...])
    acc = acc + jax.nn.sigmoid(_dot(h, wg1_ref[...])) * _dot(od_ref[...], wd_ref[...])
    acc = acc + jax.nn.sigmoid(_dot(h, wg2_ref[...])) * _dot(on_ref[...], wn_ref[...])
    o_ref[...] = acc.astype(BF16)


def _merge(h, o_fox, o_diff, o_nsa, w_gate, wb_fox, wb_diff, wb_nsa, l):
    m, d = h.shape
    tm = _pick(m, (1024, 512, 256))
    tn = _pick(d, (512, 256, 128))
    nj = d // tn
    row = lambda width: pl.BlockSpec((tm, width), lambda i, j: (i, 0))
    gate = lambda t: pl.BlockSpec((None, d, tn), lambda i, j: (l, 0, t * nj + j))
    col = lambda k: pl.BlockSpec((None, k, tn), lambda i, j: (l, 0, j))
    return pl.pallas_call(
        _merge_kernel,
        grid=(m // tm, nj),
        in_specs=[row(d), row(o_fox.shape[1]), row(o_diff.shape[1]), row(o_nsa.shape[1]),
                  gate(0), gate(1), gate(2),
                  col(wb_fox.shape[1]), col(wb_diff.shape[1]), col(wb_nsa.shape[1])],
        out_specs=pl.BlockSpec((tm, tn), lambda i, j: (i, j)),
        out_shape=jax.ShapeDtypeStruct((m, d), BF16),
        compiler_params=_params(("parallel", "arbitrary")),
        name="gate_merge",
    )(h, o_fox, o_diff, o_nsa, w_gate, w_gate, w_gate, wb_fox, wb_diff, wb_nsa)


def _wout_kernel(a_ref, w_ref, x_ref, gp_ref, gn_ref, xo_ref, ho_ref):
    sub = min(a_ref.shape[0], 256)
    for s in range(a_ref.shape[0] // sub):
        rows = slice(s * sub, (s + 1) * sub)
        y = _dot(a_ref[rows, :], w_ref[...])
        x_new = x_ref[rows, :] + _rms(y, gp_ref[...])
        xo_ref[rows, :] = x_new
        ho_ref[rows, :] = _rms(x_new, gn_ref[...]).astype(BF16)


def _wout(a, w, x, g_post, g_next, l, name):
    m, d = x.shape
    k = a.shape[1]
    tm = _pick(m, (512, 256, 128)) if k * d * 2 <= 8 * 1024 * 1024 else _pick(m, (256, 128))
    row = lambda width: pl.BlockSpec((tm, width), lambda i: (i, 0))
    vec = lambda: pl.BlockSpec((1, d), lambda i: (0, 0))
    return pl.pallas_call(
        _wout_kernel,
        grid=(m // tm,),
        in_specs=[row(k), pl.BlockSpec((None, k, d), lambda i: (l, 0, 0), pipeline_mode=pl.Buffered(1)),
                  row(d), vec(), vec()],
        out_specs=[row(d), row(d)],
        out_shape=[jax.ShapeDtypeStruct((m, d), F32), jax.ShapeDtypeStruct((m, d), BF16)],
        compiler_params=_params(("parallel",)),
        name=name,
    )(a, w, x, g_post, g_next)


def _ffn_up_kernel(h_ref, wg_ref, wu_ref, o_ref, wgb_ref, wub_ref):
    @pl.when(pl.program_id(1) == 0)
    def _():
        wgb_ref[...] = wg_ref[...].astype(BF16)
        wub_ref[...] = wu_ref[...].astype(BF16)

    h = h_ref[...]
    gate = _dot(h, wgb_ref[...])
    o_ref[...] = (gate * jax.nn.sigmoid(gate) * _dot(h, wub_ref[...])).astype(BF16)


def _ffn_up(h, w_up, l):
    m, d = h.shape
    dff = w_up.shape[2] // 2
    tm = _pick(m, (1024, 512, 256))
    tn = _pick(dff, (512, 256, 128))
    nj = dff // tn
    return pl.pallas_call(
        _ffn_up_kernel,
        grid=(nj, m // tm),
        in_specs=[pl.BlockSpec((tm, d), lambda j, i: (i, 0)),
                  pl.BlockSpec((None, d, tn), lambda j, i: (l, 0, j)),
                  pl.BlockSpec((None, d, tn), lambda j, i: (l, 0, nj + j))],
        out_specs=pl.BlockSpec((tm, tn), lambda j, i: (i, j)),
        out_shape=jax.ShapeDtypeStruct((m, dff), BF16),
        scratch_shapes=[pltpu.VMEM((d, tn), BF16), pltpu.VMEM((d, tn), BF16)],
        compiler_params=_params(("parallel", "arbitrary")),
        name="ffn_up",
    )(h, w_up, w_up)


def _pos_columns(pos):
    lane = jnp.arange(LANES)[None, :]
    hi = (lane >= AUG_HI) & (lane < AUG_HI + AUG_TERMS)
    lo = (lane >= AUG_LO) & (lane < AUG_LO + AUG_TERMS)
    return jnp.where(hi, (pos // LANES)[:, None], jnp.where(lo, (pos % LANES)[:, None], 0))


def _key_aug_table(seq):
    j = jnp.arange(seq)
    lane = jnp.arange(LANES)[None, :]
    onehot = (lane == (j // SEL_BLOCK)[:, None]) & (lane < SEL_BLOCK)
    return (_pos_columns(j) + onehot.astype(jnp.int32)).astype(BF16)


def _cmp_aug_table(nrow):
    return _pos_columns(jnp.arange(nrow) * CMP_STRIDE + CMP_BLOCK - 1).astype(BF16)


def _overlap_table_t(nrow, seq):
    start = jnp.arange(nrow)[None, :] * CMP_STRIDE
    blk = jnp.arange(LANES)[:, None]
    sel = blk * SEL_BLOCK
    ov = ((start < sel + SEL_BLOCK) & (start + CMP_BLOCK - 1 >= sel) & (blk < seq // SEL_BLOCK)
          & (jnp.arange(nrow)[None, :] < (seq - CMP_BLOCK) // CMP_STRIDE + 1))
    return ov.astype(BF16)


def _query_aug_rows(n_heads):
    slopes = 2.0 ** (-8.0 * jnp.arange(1, n_heads + 1, dtype=F32) / n_heads)
    terms = [t.astype(F32) for t in _split3(jnp.float32(LOG2E))]
    lane = jnp.arange(LANES)[None, :]
    out = jnp.where(lane == AUG_PAD, 1.0, jnp.zeros((n_heads, LANES), F32))
    for i, t in enumerate(terms):
        out = jnp.where(lane == AUG_HI + i, slopes[:, None] * t * LANES, out)
        out = jnp.where(lane == AUG_LO + i, slopes[:, None] * t, out)
    return out


def kernel(x, w_in, fox_forget_bias, diff_lambda, diff_subln, nsa_cmp_pos, nsa_cmp_w1, nsa_cmp_w2,
           w_branch_fox, w_branch_diff, w_branch_nsa, w_gate, w_out, norm_gains, w_ffn_up, w_ffn_down):
    batch, seq, d = x.shape
    depth = w_in.shape[0]
    m = batch * seq
    fw, dw, nw, kvw = FOX_HEADS * HEAD_DIM, DIFF_HEADS * HEAD_DIM, NSA_HEADS * HEAD_DIM, NSA_KV_HEADS * HEAD_DIM
    ff0 = 3 * fw
    dq0 = ff0 + FOX_HEADS
    ng0 = dq0 + 3 * dw + nw + 6 * kvw
    n_main = ng0 - FOX_HEADS
    assert n_main == N_SLABS * LANES and w_in.shape[2] == ng0 + 3 * NSA_HEADS

    w_main = jnp.concatenate([w_in[:, :, :ff0], w_in[:, :, dq0:ng0]], axis=2).astype(BF16)
    w_misc = jnp.concatenate([w_in[:, :, ff0:dq0], w_in[:, :, ng0:],
                              jnp.zeros((depth, d, LANES - FOX_HEADS - 3 * NSA_HEADS), F32)], axis=2).astype(BF16)
    colscale = jnp.ones((n_main,), F32)
    colscale = colscale.at[SLAB_FQ * LANES:SLAB_FK * LANES].set(HEAD_DIM ** -0.5 * LOG2E)
    colscale = colscale.at[SLAB_DQ * LANES:SLAB_DK * LANES].set(DIFF_HALF ** -0.5 * LOG2E)
    colscale = colscale.at[SLAB_NQ * LANES:SLAB_NKC * LANES].set(HEAD_DIM ** -0.5 * LOG2E)
    colscale = colscale[None, :]
    fbias = jnp.pad(fox_forget_bias.astype(F32), ((0, 0), (0, LANES - FOX_HEADS)))[:, None, :]
    half = CMP_STRIDE * HEAD_DIM
    pos2 = nsa_cmp_pos.astype(F32).reshape(depth, 2, 2, half)
    w1 = nsa_cmp_w1.astype(BF16)
    w2 = nsa_cmp_w2.astype(BF16)
    wbf, wbd, wbn = w_branch_fox.astype(BF16), w_branch_diff.astype(BF16), w_branch_nsa.astype(BF16)
    wg, wo = w_gate.astype(BF16), w_out.astype(BF16)
    wup, wdn = w_ffn_up.astype(F32), w_ffn_down.astype(BF16)
    gains = norm_gains.astype(F32)

    nrow = seq // CMP_STRIDE
    kx = _key_aug_table(seq)
    cx_cmp = _cmp_aug_table(nrow)
    ovt = _overlap_table_t(nrow, seq)
    qx_diff = _query_aug_rows(DIFF_HEADS)[:, None, :]
    qx_nsa = _query_aug_rows(NSA_HEADS).reshape(NSA_KV_HEADS, NSA_REP, LANES)

    xf = x.reshape(m, d).astype(F32)
    h = _norm(xf, gains[0, 0][None, :])
    for l in range(depth):
        lam_init = 0.8 - 0.6 * math.exp(-0.3 * l)
        slabs, misc = _inproj(h, w_main, colscale, w_misc, l)
        cx_fox = _logf(misc, fbias[l], batch, seq)
        o_fox = _fox(slabs, cx_fox, batch, seq)
        o_diff = _diff(slabs, kx, qx_diff, diff_lambda[l].astype(F32), diff_subln[l].astype(F32)[None, :],
                       batch, seq, lam_init)
        ykv = slabs[SLAB_NKC:SLAB_NKC + 4].reshape(4, batch, nrow, half)
        kvc = _compress(ykv, pos2, w1, w2, batch, l)
        o_nsa = _nsa(slabs, kvc, kx, cx_cmp, ovt, qx_nsa, misc, batch, seq)
        merged = _merge(h, o_fox, o_diff, o_nsa, wg, wbf, wbd, wbn, l)
        xf, h2 = _wout(merged, wo, xf, gains[l, 1][None, :], gains[l, 2][None, :], l, "out_proj")
        act = _ffn_up(h2, wup, l)
        g_next = gains[min(l + 1, depth - 1), 0][None, :]
        xf, h = _wout(act, wdn, xf, gains[l, 3][None, :], g_next, l, "ffn_down")
    return xf.reshape(batch, seq, d).astype(x.dtype)
```

```python
import functools
import math

import jax
import jax.numpy as jnp
from jax import lax
from jax.experimental import pallas as pl
from jax.experimental.pallas import tpu as pltpu

F32 = jnp.float32
BF16 = jnp.bfloat16

HEAD_DIM = 128
FOX_HEADS = 4
DIFF_HEADS = 4
DIFF_HALF = HEAD_DIM // 2
NSA_HEADS = 8
NSA_KV_HEADS = 2
NSA_REP = NSA_HEADS // NSA_KV_HEADS
CMP_BLOCK = 32
CMP_STRIDE = 16
CMP_HIDDEN = 256
SEL_BLOCK = 64
SEL_TOPK = 16
WINDOW = 512
N_BRANCHES = 3
EPS = 1e-6
NEG_INF = -1e30
FORCE_SCORE = 1e4
SEL_MASK = NEG_INF
LOG2E = math.log2(math.e)
LANES = 128

SLAB_FQ, SLAB_FK, SLAB_FV = 0, 4, 8
SLAB_DQ, SLAB_DK, SLAB_DV = 12, 16, 20
SLAB_NQ = 24
SLAB_NKC, SLAB_NVC, SLAB_NKS, SLAB_NVS, SLAB_NKW, SLAB_NVW = 32, 34, 36, 38, 40, 42
N_SLABS = 44
MISC_FF = 0
MISC_NG = 4
AUG_HI = 64
AUG_LO = 67
AUG_TERMS = 3
AUG_PAD = 70

VMEM_LIMIT = 56 * 1024 * 1024


def _pick(n, prefs):
    for p in prefs:
        if p <= n and n % p == 0:
            return p
    return n


def _params(sem):
    return pltpu.CompilerParams(dimension_semantics=sem, vmem_limit_bytes=VMEM_LIMIT)


def _rms(y, g):
    return y * lax.rsqrt(jnp.mean(y * y, axis=-1, keepdims=True) + EPS) * g


def _dot(a, b):
    return jnp.dot(a, b, preferred_element_type=F32)


def _dot_nt(a, b):
    return lax.dot_general(a, b, (((1,), (1,)), ((), ())), preferred_element_type=F32)


def _split3(x):
    hi = x.astype(BF16)
    r = x - hi.astype(F32)
    mid = r.astype(BF16)
    lo = (r - mid.astype(F32)).astype(BF16)
    return hi, mid, lo


def _norm_kernel(x_ref, g_ref, h_ref):
    h_ref[...] = _rms(x_ref[...], g_ref[...]).astype(BF16)


def _norm(x, g):
    m, d = x.shape
    tm = _pick(m, (512, 256, 128))
    return pl.pallas_call(
        _norm_kernel,
        grid=(m // tm,),
        in_specs=[pl.BlockSpec((tm, d), lambda i: (i, 0)), pl.BlockSpec((1, d), lambda i: (0, 0))],
        out_specs=pl.BlockSpec((tm, d), lambda i: (i, 0)),
        out_shape=jax.ShapeDtypeStruct((m, d), BF16),
        compiler_params=_params(("parallel",)),
        name="norm_in",
    )(x, g)


def _inproj_kernel(h_ref, w_ref, cs_ref, wm_ref, o_ref, misc_ref):
    h = h_ref[...]
    acc = _dot(h, w_ref[...]) * cs_ref[...]
    for s in range(o_ref.shape[0]):
        o_ref[s] = acc[:, s * LANES:(s + 1) * LANES].astype(BF16)

    @pl.when(pl.program_id(1) == 0)
    def _():
        misc_ref[...] = _dot(h, wm_ref[...])


def _inproj(h, w_main, colscale, w_misc, l):
    m, d = h.shape
    n = w_main.shape[2]
    tm = _pick(m, (1024, 512, 256))
    tn = _pick(n, (11 * LANES, 4 * LANES))
    return pl.pallas_call(
        _inproj_kernel,
        grid=(m // tm, n // tn),
        in_specs=[pl.BlockSpec((tm, d), lambda i, j: (i, 0)),
                  pl.BlockSpec((None, d, tn), lambda i, j: (l, 0, j)),
                  pl.BlockSpec((1, tn), lambda i, j: (0, j)),
                  pl.BlockSpec((None, d, LANES), lambda i, j: (l, 0, 0))],
        out_specs=[pl.BlockSpec((tn // LANES, tm, LANES), lambda i, j: (j, i, 0)),
                   pl.BlockSpec((tm, LANES), lambda i, j: (i, 0))],
        out_shape=[jax.ShapeDtypeStruct((n // LANES, m, LANES), BF16),
                   jax.ShapeDtypeStruct((m, LANES), F32)],
        compiler_params=_params(("parallel", "arbitrary")),
        name="inproj",
    )(h, w_main, colscale, w_misc)


def _logf_kernel(misc_ref, bias_ref, o_ref, *, tc):
    s = misc_ref.shape[0]
    row = lax.broadcasted_iota(jnp.int32, (tc, tc), 0)
    col = lax.broadcasted_iota(jnp.int32, (tc, tc), 1)
    tri = jnp.where(col <= row, 1.0, 0.0).astype(BF16)
    lane = lax.broadcasted_iota(jnp.int32, (tc, LANES), 1)

    def chunk(c, carry):
        r0 = pl.multiple_of(c * tc, tc)
        z = misc_ref[pl.ds(r0, tc), :] + bias_ref[...]
        lf = jnp.minimum(z, 0.0) - jnp.log1p(jnp.exp(-jnp.abs(z)))
        hi, mid, lo = _split3(lf)
        cum = _dot(tri, hi) + _dot(tri, mid) + _dot(tri, lo) + carry
        for hd in range(FOX_HEADS):
            c2 = jnp.broadcast_to(cum[:, hd:hd + 1], (tc, LANES)) * LOG2E
            c_hi = c2.astype(BF16).astype(F32)
            c_mid = (c2 - c_hi).astype(BF16).astype(F32)
            aug = jnp.where(lane == 0, c_hi, jnp.where(lane == 1, c_mid,
                                                      jnp.where(lane == 2, c2 - c_hi - c_mid, 0.0)))
            o_ref[hd, pl.ds(r0, tc), :] = aug.astype(BF16)
        return cum[tc - 1:tc, :]

    lax.fori_loop(0, s // tc, chunk, jnp.zeros((1, LANES), F32))


def _logf(misc, bias_row, batch, seq):
    tc = _pick(seq, (256, 128))
    return pl.pallas_call(
        functools.partial(_logf_kernel, tc=tc),
        grid=(batch,),
        in_specs=[pl.BlockSpec((seq, LANES), lambda b: (b, 0)), pl.BlockSpec((1, LANES), lambda b: (0, 0))],
        out_specs=pl.BlockSpec((FOX_HEADS, seq, LANES), lambda b: (0, b, 0)),
        out_shape=jax.ShapeDtypeStruct((FOX_HEADS, batch * seq, LANES), BF16),
        compiler_params=_params(("parallel",)),
        name="fox_logf",
    )(misc, bias_row)


def _flash_scratch(tk, mcols):
    return [pltpu.VMEM((tk, mcols), F32), pltpu.VMEM((1, mcols), F32), pltpu.VMEM((1, mcols), F32),
            pltpu.VMEM((LANES, mcols), F32)]


def _causal_bias(tk, mcols, tq, offset):
    rk = lax.broadcasted_iota(jnp.int32, (tk, mcols), 0)
    rq = lax.broadcasted_iota(jnp.int32, (tk, mcols), 1) & (tq - 1)
    return jnp.where(rk <= rq + offset, 0.0, NEG_INF)


def _flash_tiles(qa, ka_ref, vt_ref, n_full, diag_bias_ref, tk, scratch, next_qa=None):
    s_ref, m_ref, l_ref, acc_ref = scratch

    def logits(j, queries=qa):
        k0 = pl.multiple_of(j * tk, tk)
        return _dot_nt(ka_ref[pl.ds(k0, tk), :], queries)

    def step(s, j):
        k0 = pl.multiple_of(j * tk, tk)
        m_prev = m_ref[...]
        m_new = jnp.maximum(m_prev, jnp.max(s, axis=0, keepdims=True))
        alpha = jnp.exp2(m_prev - m_new)
        p = jnp.exp2(s - m_new)
        l_ref[...] = alpha * l_ref[...] + jnp.sum(p, axis=0, keepdims=True)
        acc_ref[...] = alpha * acc_ref[...] + _dot(vt_ref[:, pl.ds(k0, tk)], p.astype(BF16))
        m_ref[...] = m_new

    m_ref[...] = jnp.full(m_ref.shape, NEG_INF, F32)
    l_ref[...] = jnp.zeros(l_ref.shape, F32)
    acc_ref[...] = jnp.zeros(acc_ref.shape, F32)
    if next_qa is None:
        s_ref[...] = logits(0)

    def full(j, c):
        s_next = logits(j + 1)
        step(s_ref[...], j)
        s_ref[...] = s_next
        return c

    lax.fori_loop(0, n_full, full, 0)
    if next_qa is None:
        step(s_ref[...] + diag_bias_ref[...], n_full)
    else:
        s_next = logits(0, next_qa)
        step(s_ref[...] + diag_bias_ref[...], n_full)
        s_ref[...] = s_next
    return acc_ref[...] * (1.0 / l_ref[...])


def _stage_kv(ka_ref, vt_ref, k_ref, kx_ref, v_ref, chunk, pad=0):
    def body(c, carry):
        r0 = pl.multiple_of(c * chunk, chunk)
        ka_ref[pl.ds(pad + r0, chunk), 0:LANES] = k_ref[0, pl.ds(r0, chunk), :]
        ka_ref[pl.ds(pad + r0, chunk), LANES:2 * LANES] = kx_ref[pl.ds(r0, chunk), :]
        vt_ref[:, pl.ds(pad + r0, chunk)] = v_ref[0, pl.ds(r0, chunk), :].astype(F32).T.astype(BF16)
        return carry

    lax.fori_loop(0, v_ref.shape[1] // chunk, body, 0)


def _fox_kernel(q_ref, k_ref, v_ref, cx_ref, o_ref, ka_ref, vt_ref, tri_ref, *flash, tq, tk):
    _stage_kv(ka_ref, vt_ref, k_ref, cx_ref.at[0], v_ref, tk)
    for o in range(tk // tq):
        tri_ref[o] = _causal_bias(tk, tq, tq, o * tq)
    lane = lax.broadcasted_iota(jnp.int32, (tq, LANES), 1)
    qx = jnp.where(lane < 3, -1.0, 0.0).astype(BF16)

    nq = q_ref.shape[1] // tq

    def queries(qi):
        q0 = pl.multiple_of(qi * tq, tq)
        return jnp.concatenate([q_ref[0, pl.ds(q0, tq), :], qx], axis=1)

    flash[0][...] = _dot_nt(ka_ref[0:tk, :], queries(0))

    def qstep(qi, c):
        o = _flash_tiles(queries(qi), ka_ref, vt_ref, (qi * tq) // tk, tri_ref.at[qi & (tk // tq - 1)], tk, flash,
                         next_qa=queries(jnp.minimum(qi + 1, nq - 1)))
        o_ref[pl.ds(pl.multiple_of(qi * tq, tq), tq), :] = o.T.astype(BF16)
        return c

    lax.fori_loop(0, nq, qstep, 0)


def _attn_tiles(seq):
    tq = _pick(seq, (512, 256, 128))
    tk = 2 * tq if seq % (2 * tq) == 0 else tq
    return tq, tk


def _fox(slabs, cx, batch, seq):
    tk = _attn_tiles(seq)[1]
    tq = tk
    m = batch * seq
    head = lambda base: pl.BlockSpec((1, seq, LANES), lambda b, h: (base + h, b, 0))
    return pl.pallas_call(
        functools.partial(_fox_kernel, tq=tq, tk=tk),
        grid=(batch, FOX_HEADS),
        in_specs=[head(SLAB_FQ), head(SLAB_FK), head(SLAB_FV), head(0)],
        out_specs=pl.BlockSpec((seq, LANES), lambda b, h: (b, h)),
        out_shape=jax.ShapeDtypeStruct((m, FOX_HEADS * HEAD_DIM), BF16),
        scratch_shapes=[pltpu.VMEM((seq, 2 * LANES), BF16), pltpu.VMEM((LANES, seq), BF16),
                        pltpu.VMEM((tk // tq, tk, tq), F32)] + _flash_scratch(tk, tq),
        compiler_params=_params(("parallel", "parallel")),
        name="fox_attn",
    )(slabs, slabs, slabs, cx)


def _diff_kernel(q_ref, k_ref, v_ref, kx_ref, qx_ref, lam_ref, sub_ref, o_ref, ka_ref, vt_ref, tri_ref, *flash,
                 t, tk, lam_init):
    _stage_kv(ka_ref, vt_ref, k_ref, kx_ref, v_ref, tk)
    for o in range(tk // t):
        tri_ref[o] = _causal_bias(tk, 2 * t, t, o * t)
    lane = lax.broadcasted_iota(jnp.int32, (t, LANES), 1)
    qx = jnp.broadcast_to(qx_ref[0], (t, LANES)).astype(BF16)
    lv = lam_ref[...]
    lam = (jnp.exp(jnp.sum(lv[0:1] * lv[1:2], axis=-1, keepdims=True))
           - jnp.exp(jnp.sum(lv[2:3] * lv[3:4], axis=-1, keepdims=True)) + lam_init)

    nq = q_ref.shape[1] // t

    def queries(qi):
        q = q_ref[0, pl.ds(pl.multiple_of(qi * t, t), t), :].astype(F32)
        return jnp.concatenate([
            jnp.concatenate([jnp.where(lane < DIFF_HALF, q, 0.0).astype(BF16), qx], axis=1),
            jnp.concatenate([jnp.where(lane >= DIFF_HALF, q, 0.0).astype(BF16), qx], axis=1)], axis=0)

    flash[0][...] = _dot_nt(ka_ref[0:tk, :], queries(0))

    def qstep(qi, c):
        o = _flash_tiles(queries(qi), ka_ref, vt_ref, (qi * t) // tk, tri_ref.at[qi & (tk // t - 1)], tk, flash,
                         next_qa=queries(jnp.minimum(qi + 1, nq - 1)))
        o = (o[:, :t] - lam * o[:, t:]).T
        o_ref[pl.ds(pl.multiple_of(qi * t, t), t), :] = (_rms(o, sub_ref[...]) * (1.0 - lam_init)).astype(BF16)
        return c

    lax.fori_loop(0, nq, qstep, 0)


def _diff(slabs, kx, qx_diff, lam_vec, subln, batch, seq, lam_init):
    tk = _attn_tiles(seq)[1]
    t = tk
    m = batch * seq
    head = lambda base: pl.BlockSpec((1, seq, LANES), lambda b, h: (base + h, b, 0))
    return pl.pallas_call(
        functools.partial(_diff_kernel, t=t, tk=tk, lam_init=lam_init),
        grid=(batch, DIFF_HEADS),
        in_specs=[head(SLAB_DQ), head(SLAB_DK), head(SLAB_DV),
                  pl.BlockSpec((seq, LANES), lambda b, h: (0, 0)),
                  pl.BlockSpec((1, 1, LANES), lambda b, h: (h, 0, 0)),
                  pl.BlockSpec((4, DIFF_HALF), lambda b, h: (0, 0)),
                  pl.BlockSpec((1, LANES), lambda b, h: (0, 0))],
        out_specs=pl.BlockSpec((seq, LANES), lambda b, h: (b, h)),
        out_shape=jax.ShapeDtypeStruct((m, DIFF_HEADS * HEAD_DIM), BF16),
        scratch_shapes=[pltpu.VMEM((seq, 2 * LANES), BF16), pltpu.VMEM((LANES, seq), BF16),
                        pltpu.VMEM((tk // t, tk, 2 * t), F32)] + _flash_scratch(tk, 2 * t),
        compiler_params=_params(("parallel", "parallel")),
        name="diff_attn",
    )(slabs, slabs, slabs, kx, qx_diff, lam_vec, subln)


def _compress_kernel(y_ref, pos_ref, w1_ref, w2_ref, o_ref):
    half = y_ref.shape[3]
    y = y_ref[0, 0].astype(F32)
    top = (y + pos_ref[0, 0:1, :]).astype(BF16)
    bot = (y + pos_ref[0, 1:2, :]).astype(BF16)
    a = _dot(top, w1_ref[0, 0:half, :])
    b = _dot(bot, w1_ref[0, half:2 * half, :])
    nrow = a.shape[0]
    hid = a + pltpu.roll(b, nrow - 1, 0)
    hid = hid * jax.nn.sigmoid(hid)
    o_ref[0, 0] = _dot(hid.astype(BF16), w2_ref[0]).astype(BF16)


def _compress(ykv, pos2, w1, w2, batch, l):
    nrow, half = ykv.shape[2], ykv.shape[3]
    return pl.pallas_call(
        _compress_kernel,
        grid=(4, batch),
        in_specs=[pl.BlockSpec((1, 1, nrow, half), lambda s, b: (s, b, 0, 0)),
                  pl.BlockSpec((None, 1, 2, half), lambda s, b: (l, s // 2, 0, 0)),
                  pl.BlockSpec((None, 1, 2 * half, CMP_HIDDEN), lambda s, b: (l, s // 2, 0, 0)),
                  pl.BlockSpec((None, 1, CMP_HIDDEN, HEAD_DIM), lambda s, b: (l, s // 2, 0, 0))],
        out_specs=pl.BlockSpec((1, 1, nrow, HEAD_DIM), lambda s, b: (s, b, 0, 0)),
        out_shape=jax.ShapeDtypeStruct((4, batch, nrow, HEAD_DIM), BF16),
        compiler_params=_params(("parallel", "parallel")),
        name="nsa_compress",
    )(ykv, pos2, w1, w2)


def _topk_bias(cand, topk):
    nsel, tq = cand.shape
    sub = 8
    blocks = [cand[b * sub:(b + 1) * sub] for b in range(nsel // sub)]
    jidx = lax.broadcasted_iota(jnp.int32, (sub, tq), 0)
    ranks = [jnp.zeros((sub, tq), F32) for _ in blocks]
    for k in range(nsel):
        rk = cand[k:k + 1, :]
        for b, cb in enumerate(blocks):
            if b * sub > k:
                beats = rk >= cb
            elif b * sub + sub - 1 < k:
                beats = rk > cb
            else:
                beats = (rk > cb) | ((rk == cb) & (jidx + b * sub > k))
            ranks[b] = ranks[b] + jnp.where(beats, 1.0, 0.0)
    rank = jnp.concatenate(ranks, axis=0)
    return jnp.where(rank < topk, 0.0, SEL_MASK)


def _nsa_kernel(q_ref, kc_ref, vc_ref, ks_ref, vs_ref, kw_ref, vw_ref, kx_ref, cx_ref, ovt_ref, qx_ref,
                misc_ref, o_ref, ksa_ref, kwa_ref, vst_ref, vwt_ref, vct_ref, gt_ref, dbias_ref, wbias_ref,
                comb_ref, *flash, tq, tk):
    g = pl.program_id(1)
    qi = pl.program_id(2)
    mcols = NSA_REP * tq
    nc = kc_ref.shape[2]
    seq = kx_ref.shape[0]
    nsel = seq // SEL_BLOCK
    span = WINDOW + tq

    @pl.when(qi == 0)
    def _():
        _stage_kv(ksa_ref, vst_ref, ks_ref, kx_ref, vs_ref, tk)
        _stage_kv(kwa_ref, vwt_ref, kw_ref, kx_ref, vw_ref, tk, pad=WINDOW)
        pad_lane = lax.broadcasted_iota(jnp.int32, (WINDOW, 2 * LANES), 1)
        kwa_ref[0:WINDOW, :] = jnp.where(pad_lane == LANES + AUG_PAD, SEL_MASK, 0.0).astype(BF16)
        vwt_ref[:, 0:WINDOW] = jnp.zeros((LANES, WINDOW), BF16)
        vct_ref[...] = vc_ref[0, 0].astype(F32).T.astype(BF16)
        for o in range(tk // tq):
            dbias_ref[o] = _causal_bias(tk, mcols, tq, o * tq)
        rk = lax.broadcasted_iota(jnp.int32, (span, mcols), 0)
        rq = lax.broadcasted_iota(jnp.int32, (span, mcols), 1) & (tq - 1)
        wbias_ref[...] = jnp.where((rk > rq) & (rk <= rq + WINDOW), 0.0, NEG_INF)

    qx = jnp.concatenate([jnp.broadcast_to(qx_ref[0, r:r + 1, :], (tq, LANES)) for r in range(NSA_REP)], axis=0)
    kca = jnp.concatenate([kc_ref[0, 0], cx_ref[...]], axis=1)
    _nsa_query_tile(qi, g, qx, kca, q_ref, ovt_ref, misc_ref, o_ref, ksa_ref, kwa_ref, vst_ref, vwt_ref,
                    vct_ref, gt_ref, dbias_ref, wbias_ref, comb_ref, flash, tq=tq, tk=tk, nc=nc, nsel=nsel)


def _nsa_query_tile(qi, g, qx, kca, q_ref, ovt_ref, misc_ref, o_ref, ksa_ref, kwa_ref, vst_ref, vwt_ref, vct_ref,
                    gt_ref, dbias_ref, wbias_ref, comb_ref, flash, *, tq, tk, nc, nsel):
    mcols = NSA_REP * tq
    span = WINDOW + tq
    q0 = pl.multiple_of(qi * tq, tq)
    q4 = q_ref[...].reshape(mcols, LANES)
    qa = jnp.concatenate([q4, qx.astype(BF16)], axis=1)
    col = lax.broadcasted_iota(jnp.int32, (1, mcols), 1)
    col_pos = q0 + (col & (tq - 1))

    sc = _dot_nt(kca, qa)
    cend = lax.broadcasted_iota(jnp.int32, (nc, mcols), 0) * CMP_STRIDE + (CMP_BLOCK - 1)
    sc = jnp.where(cend <= col_pos, sc, NEG_INF)
    e = jnp.exp2(sc - jnp.max(sc, axis=0, keepdims=True))
    inv = jnp.where(col_pos >= CMP_BLOCK - 1, 1.0 / jnp.sum(e, axis=0, keepdims=True), 0.0)
    p = e * inv
    o_cmp = _dot(vct_ref[...], p.astype(BF16))

    psum = p[:, 0:tq]
    for r in range(1, NSA_REP):
        psum = psum + p[:, r * tq:(r + 1) * tq]
    p_hi = psum.astype(BF16)
    p_lo = (psum - p_hi.astype(F32)).astype(BF16)
    imp = _dot(ovt_ref[...], p_hi) + _dot(ovt_ref[...], p_lo)
    qpos = q0 + lax.broadcasted_iota(jnp.int32, (LANES, tq), 1)
    blk = lax.broadcasted_iota(jnp.int32, (LANES, tq), 0)
    cur = jnp.right_shift(qpos, SEL_BLOCK.bit_length() - 1)
    forced = (blk == 0) | (blk == cur) | (blk == cur - 1)
    imp = jnp.where(blk <= cur, jnp.where(forced, FORCE_SCORE, imp), -1.0)
    bias_t = _topk_bias(imp[0:nsel], min(SEL_TOPK, nsel))
    if nsel < LANES:
        bias_t = jnp.concatenate([bias_t, jnp.zeros((LANES - nsel, tq), F32)], axis=0)
    selbias = jnp.concatenate([bias_t.T] * NSA_REP, axis=0)
    lane4 = lax.broadcasted_iota(jnp.int32, (mcols, LANES), 1)
    qa_sel = jnp.concatenate([q4, jnp.where(lane4 < SEL_BLOCK, selbias, qx).astype(BF16)], axis=1)

    sw = _dot_nt(kwa_ref[pl.ds(q0, span), :], qa) + wbias_ref[...]
    ew = jnp.exp2(sw - jnp.max(sw, axis=0, keepdims=True))
    o_win = _dot(vwt_ref[:, pl.ds(q0, span)], ew.astype(BF16)) * (1.0 / jnp.sum(ew, axis=0, keepdims=True))

    gt_ref[...] = jax.nn.sigmoid(misc_ref[...]).T
    gate = lambda r, i: gt_ref[pl.ds(MISC_NG + 3 * (NSA_REP * g + r) + i, 1), :]
    for r in range(NSA_REP):
        sl = slice(r * tq, (r + 1) * tq)
        comb_ref[:, sl] = gate(r, 0) * o_cmp[:, sl] + gate(r, 2) * o_win[:, sl]

    o_sel = _flash_tiles(qa_sel, ksa_ref, vst_ref, q0 // tk, dbias_ref.at[qi & (tk // tq - 1)], tk, flash)
    for r in range(NSA_REP):
        sl = slice(r * tq, (r + 1) * tq)
        out = comb_ref[:, sl] + gate(r, 1) * o_sel[:, sl]
        o_ref[:, r * LANES:(r + 1) * LANES] = out.T.astype(BF16)


def _nsa(slabs, kvc, kx, cx_cmp, ovt, qx_nsa, misc, batch, seq):
    tq = _pick(seq, (512, 256, 128))
    tk = _pick(seq, (512, 256, 128))
    assert tq & (tq - 1) == 0 and tk % tq == 0 and seq >= WINDOW + tq
    nq = seq // tq
    m = batch * seq
    mcols = NSA_REP * tq
    nc = kvc.shape[2]
    kv_spec = lambda base: pl.BlockSpec((1, seq, LANES), lambda b, g, i: (base + g, b, 0))
    const = lambda shape: pl.BlockSpec(shape, lambda b, g, i: (0,) * len(shape), pipeline_mode=pl.Buffered(1))
    return pl.pallas_call(
        functools.partial(_nsa_kernel, tq=tq, tk=tk),
        grid=(batch, NSA_KV_HEADS, nq),
        in_specs=[pl.BlockSpec((NSA_REP, tq, LANES), lambda b, g, i: (SLAB_NQ // NSA_REP + g, b * nq + i, 0)),
                  pl.BlockSpec((1, 1, nc, LANES), lambda b, g, i: (g, b, 0, 0)),
                  pl.BlockSpec((1, 1, nc, LANES), lambda b, g, i: (2 + g, b, 0, 0)),
                  kv_spec(SLAB_NKS), kv_spec(SLAB_NVS), kv_spec(SLAB_NKW), kv_spec(SLAB_NVW),
                  const((seq, LANES)), const((nc, LANES)), const((LANES, nc)),
                  pl.BlockSpec((1, NSA_REP, LANES), lambda b, g, i: (g, 0, 0)),
                  pl.BlockSpec((tq, LANES), lambda b, g, i: (b * nq + i, 0))],
        out_specs=pl.BlockSpec((tq, NSA_REP * LANES), lambda b, g, i: (b * nq + i, g)),
        out_shape=jax.ShapeDtypeStruct((m, NSA_HEADS * HEAD_DIM), BF16),
        scratch_shapes=[pltpu.VMEM((seq, 2 * LANES), BF16), pltpu.VMEM((WINDOW + seq, 2 * LANES), BF16),
                        pltpu.VMEM((LANES, seq), BF16), pltpu.VMEM((LANES, WINDOW + seq), BF16),
                        pltpu.VMEM((LANES, nc), BF16), pltpu.VMEM((LANES, tq), F32),
                        pltpu.VMEM((tk // tq, tk, mcols), F32), pltpu.VMEM((WINDOW + tq, mcols), F32),
                        pltpu.VMEM((LANES, mcols), F32)] + _flash_scratch(tk, mcols),
        compiler_params=_params(("parallel", "parallel", "arbitrary")),
        name="nsa_attn",
    )(slabs, kvc, kvc, slabs, slabs, slabs, slabs, kx, cx_cmp, ovt, qx_nsa, misc)


def _merge_kernel(h_ref, of_ref, od_ref, on_ref, wg0_ref, wg1_ref, wg2_ref, wf_ref, wd_ref, wn_ref, o_ref):
    h = h_ref[...]
    acc = jax.nn.sigmoid(_dot(h, wg0_ref[...])) * _dot(of_ref[...], wf_ref[...])
    acc = acc + jax.nn.sigmoid(_dot(h, wg1_ref[...])) * _dot(od_ref[...], wd_ref[...])
    acc = acc + jax.nn.sigmoid(_dot(h, wg2_ref[...])) * _dot(on_ref[...], wn_ref[...])
    o_ref[...] = acc.astype(BF16)


def _merge(h, o_fox, o_diff, o_nsa, w_gate, wb_fox, wb_diff, wb_nsa, l):
    m, d = h.shape
    tm = _pick(m, (1024, 512, 256))
    tn = _pick(d, (512, 256, 128))
    nj = d // tn
    row = lambda width: pl.BlockSpec((tm, width), lambda i, j: (i, 0))
    gate = lambda t: pl.BlockSpec((None, d, tn), lambda i, j: (l, 0, t * nj + j))
    col = lambda k: pl.BlockSpec((None, k, tn), lambda i, j: (l, 0, j))
    return pl.pallas_call(
        _merge_kernel,
        grid=(m // tm, nj),
        in_specs=[row(d), row(o_fox.shape[1]), row(o_diff.shape[1]), row(o_nsa.shape[1]),
                  gate(0), gate(1), gate(2),
                  col(wb_fox.shape[1]), col(wb_diff.shape[1]), col(wb_nsa.shape[1])],
        out_specs=pl.BlockSpec((tm, tn), lambda i, j: (i, j)),
        out_shape=jax.ShapeDtypeStruct((m, d), BF16),
        compiler_params=_params(("parallel", "arbitrary")),
        name="gate_merge",
    )(h, o_fox, o_diff, o_nsa, w_gate, w_gate, w_gate, wb_fox, wb_diff, wb_nsa)


def _wout_kernel(a_ref, w_ref, x_ref, gp_ref, gn_ref, xo_ref, ho_ref):
    sub = min(a_ref.shape[0], 256)
    for s in range(a_ref.shape[0] // sub):
        rows = slice(s * sub, (s + 1) * sub)
        y = _dot(a_ref[rows, :], w_ref[...])
        x_new = x_ref[rows, :] + _rms(y, gp_ref[...])
        xo_ref[rows, :] = x_new
        ho_ref[rows, :] = _rms(x_new, gn_ref[...]).astype(BF16)


def _wout(a, w, x, g_post, g_next, l, name):
    m, d = x.shape
    k = a.shape[1]
    tm = _pick(m, (512, 256, 128)) if k * d * 2 <= 8 * 1024 * 1024 else _pick(m, (256, 128))
    row = lambda width: pl.BlockSpec((tm, width), lambda i: (i, 0))
    vec = lambda: pl.BlockSpec((1, d), lambda i: (0, 0))
    return pl.pallas_call(
        _wout_kernel,
        grid=(m // tm,),
        in_specs=[row(k), pl.BlockSpec((None, k, d), lambda i: (l, 0, 0), pipeline_mode=pl.Buffered(1)),
                  row(d), vec(), vec()],
        out_specs=[row(d), row(d)],
        out_shape=[jax.ShapeDtypeStruct((m, d), F32), jax.ShapeDtypeStruct((m, d), BF16)],
        compiler_params=_params(("parallel",)),
        name=name,
    )(a, w, x, g_post, g_next)


def _ffn_up_kernel(h_ref, wg_ref, wu_ref, o_ref, wgb_ref, wub_ref):
    @pl.when(pl.program_id(1) == 0)
    def _():
        wgb_ref[...] = wg_ref[...].astype(BF16)
        wub_ref[...] = wu_ref[...].astype(BF16)

    h = h_ref[...]
    gate = _dot(h, wgb_ref[...])
    o_ref[...] = (gate * jax.nn.sigmoid(gate) * _dot(h, wub_ref[...])).astype(BF16)


def _ffn_up(h, w_up, l):
    m, d = h.shape
    dff = w_up.shape[2] // 2
    tm = _pick(m, (1024, 512, 256))
    tn = _pick(dff, (512, 256, 128))
    nj = dff // tn
    return pl.pallas_call(
        _ffn_up_kernel,
        grid=(nj, m // tm),
        in_specs=[pl.BlockSpec((tm, d), lambda j, i: (i, 0)),
                  pl.BlockSpec((None, d, tn), lambda j, i: (l, 0, j)),
                  pl.BlockSpec((None, d, tn), lambda j, i: (l, 0, nj + j))],
        out_specs=pl.BlockSpec((tm, tn), lambda j, i: (i, j)),
        out_shape=jax.ShapeDtypeStruct((m, dff), BF16),
        scratch_shapes=[pltpu.VMEM((d, tn), BF16), pltpu.VMEM((d, tn), BF16)],
        compiler_params=_params(("parallel", "arbitrary")),
        name="ffn_up",
    )(h, w_up, w_up)


def _pos_columns(pos):
    lane = jnp.arange(LANES)[None, :]
    hi = (lane >= AUG_HI) & (lane < AUG_HI + AUG_TERMS)
    lo = (lane >= AUG_LO) & (lane < AUG_LO + AUG_TERMS)
    return jnp.where(hi, (pos // LANES)[:, None], jnp.where(lo, (pos % LANES)[:, None], 0))


def _key_aug_table(seq):
    j = jnp.arange(seq)
    lane = jnp.arange(LANES)[None, :]
    onehot = (lane == (j // SEL_BLOCK)[:, None]) & (lane < SEL_BLOCK)
    return (_pos_columns(j) + onehot.astype(jnp.int32)).astype(BF16)


def _cmp_aug_table(nrow):
    return _pos_columns(jnp.arange(nrow) * CMP_STRIDE + CMP_BLOCK - 1).astype(BF16)


def _overlap_table_t(nrow, seq):
    start = jnp.arange(nrow)[None, :] * CMP_STRIDE
    blk = jnp.arange(LANES)[:, None]
    sel = blk * SEL_BLOCK
    ov = ((start < sel + SEL_BLOCK) & (start + CMP_BLOCK - 1 >= sel) & (blk < seq // SEL_BLOCK)
          & (jnp.arange(nrow)[None, :] < (seq - CMP_BLOCK) // CMP_STRIDE + 1))
    return ov.astype(BF16)


def _query_aug_rows(n_heads):
    slopes = 2.0 ** (-8.0 * jnp.arange(1, n_heads + 1, dtype=F32) / n_heads)
    terms = [t.astype(F32) for t in _split3(jnp.float32(LOG2E))]
    lane = jnp.arange(LANES)[None, :]
    out = jnp.where(lane == AUG_PAD, 1.0, jnp.zeros((n_heads, LANES), F32))
    for i, t in enumerate(terms):
        out = jnp.where(lane == AUG_HI + i, slopes[:, None] * t * LANES, out)
        out = jnp.where(lane == AUG_LO + i, slopes[:, None] * t, out)
    return out


def kernel(x, w_in, fox_forget_bias, diff_lambda, diff_subln, nsa_cmp_pos, nsa_cmp_w1, nsa_cmp_w2,
           w_branch_fox, w_branch_diff, w_branch_nsa, w_gate, w_out, norm_gains, w_ffn_up, w_ffn_down):
    batch, seq, d = x.shape
    depth = w_in.shape[0]
    m = batch * seq
    fw, dw, nw, kvw = FOX_HEADS * HEAD_DIM, DIFF_HEADS * HEAD_DIM, NSA_HEADS * HEAD_DIM, NSA_KV_HEADS * HEAD_DIM
    ff0 = 3 * fw
    dq0 = ff0 + FOX_HEADS
    ng0 = dq0 + 3 * dw + nw + 6 * kvw
    n_main = ng0 - FOX_HEADS
    assert n_main == N_SLABS * LANES and w_in.shape[2] == ng0 + 3 * NSA_HEADS

    w_main = jnp.concatenate([w_in[:, :, :ff0], w_in[:, :, dq0:ng0]], axis=2).astype(BF16)
    w_misc = jnp.concatenate([w_in[:, :, ff0:dq0], w_in[:, :, ng0:],
                              jnp.zeros((depth, d, LANES - FOX_HEADS - 3 * NSA_HEADS), F32)], axis=2).astype(BF16)
    colscale = jnp.ones((n_main,), F32)
    colscale = colscale.at[SLAB_FQ * LANES:SLAB_FK * LANES].set(HEAD_DIM ** -0.5 * LOG2E)
    colscale = colscale.at[SLAB_DQ * LANES:SLAB_DK * LANES].set(DIFF_HALF ** -0.5 * LOG2E)
    colscale = colscale.at[SLAB_NQ * LANES:SLAB_NKC * LANES].set(HEAD_DIM ** -0.5 * LOG2E)
    colscale = colscale[None, :]
    fbias = jnp.pad(fox_forget_bias.astype(F32), ((0, 0), (0, LANES - FOX_HEADS)))[:, None, :]
    half = CMP_STRIDE * HEAD_DIM
    pos2 = nsa_cmp_pos.astype(F32).reshape(depth, 2, 2, half)
    w1 = nsa_cmp_w1.astype(BF16)
    w2 = nsa_cmp_w2.astype(BF16)
    wbf, wbd, wbn = w_branch_fox.astype(BF16), w_branch_diff.astype(BF16), w_branch_nsa.astype(BF16)
    wg, wo = w_gate.astype(BF16), w_out.astype(BF16)
    wup, wdn = w_ffn_up.astype(F32), w_ffn_down.astype(BF16)
    gains = norm_gains.astype(F32)

    nrow = seq // CMP_STRIDE
    kx = _key_aug_table(seq)
    cx_cmp = _cmp_aug_table(nrow)
    ovt = _overlap_table_t(nrow, seq)
    qx_diff = _query_aug_rows(DIFF_HEADS)[:, None, :]
    qx_nsa = _query_aug_rows(NSA_HEADS).reshape(NSA_KV_HEADS, NSA_REP, LANES)

    xf = x.reshape(m, d).astype(F32)
    h = _norm(xf, gains[0, 0][None, :])
    for l in range(depth):
        lam_init = 0.8 - 0.6 * math.exp(-0.3 * l)
        slabs, misc = _inproj(h, w_main, colscale, w_misc, l)
        cx_fox = _logf(misc, fbias[l], batch, seq)
        o_fox = _fox(slabs, cx_fox, batch, seq)
        o_diff = _diff(slabs, kx, qx_diff, diff_lambda[l].astype(F32), diff_subln[l].astype(F32)[None, :],
                       batch, seq, lam_init)
        ykv = slabs[SLAB_NKC:SLAB_NKC + 4].reshape(4, batch, nrow, half)
        kvc = _compress(ykv, pos2, w1, w2, batch, l)
        o_nsa = _nsa(slabs, kvc, kx, cx_cmp, ovt, qx_nsa, misc, batch, seq)
        merged = _merge(h, o_fox, o_diff, o_nsa, wg, wbf, wbd, wbn, l)
        xf, h2 = _wout(merged, wo, xf, gains[l, 1][None, :], gains[l, 2][None, :], l, "out_proj")
        act = _ffn_up(h2, wup, l)
        g_next = gains[min(l + 1, depth - 1), 0][None, :]
        xf, h = _wout(act, wdn, xf, gains[l, 3][None, :], g_next, l, "ffn_down")
    return xf.reshape(batch, seq, d).astype(x.dtype)
```

```python
import functools
import math

import jax
import jax.numpy as jnp
from jax import lax
from jax.experimental import pallas as pl
from jax.experimental.pallas import tpu as pltpu

F32 = jnp.float32
BF16 = jnp.bfloat16

HEAD_DIM = 128
FOX_HEADS = 4
DIFF_HEADS = 4
DIFF_HALF = HEAD_DIM // 2
NSA_HEADS = 8
NSA_KV_HEADS = 2
NSA_REP = NSA_HEADS // NSA_KV_HEADS
CMP_BLOCK = 32
CMP_STRIDE = 16
CMP_HIDDEN = 256
SEL_BLOCK = 64
SEL_TOPK = 16
WINDOW = 512
N_BRANCHES = 3
EPS = 1e-6
NEG_INF = -1e30
FORCE_SCORE = 1e4
SEL_MASK = NEG_INF
LOG2E = math.log2(math.e)
LANES = 128

SLAB_FQ, SLAB_FK, SLAB_FV = 0, 4, 8
SLAB_DQ, SLAB_DK, SLAB_DV = 12, 16, 20
SLAB_NQ = 24
SLAB_NKC, SLAB_NVC, SLAB_NKS, SLAB_NVS, SLAB_NKW, SLAB_NVW = 32, 34, 36, 38, 40, 42
N_SLABS = 44
MISC_FF = 0
MISC_NG = 4
AUG_HI = 64
AUG_LO = 67
AUG_TERMS = 3
AUG_PAD = 70

VMEM_LIMIT = 56 * 1024 * 1024


def _pick(n, prefs):
    for p in prefs:
        if p <= n and n % p == 0:
            return p
    return n


def _params(sem):
    return pltpu.CompilerParams(dimension_semantics=sem, vmem_limit_bytes=VMEM_LIMIT)


def _rms(y, g):
    return y * lax.rsqrt(jnp.mean(y * y, axis=-1, keepdims=True) + EPS) * g


def _dot(a, b):
    return jnp.dot(a, b, preferred_element_type=F32)


def _dot_nt(a, b):
    return lax.dot_general(a, b, (((1,), (1,)), ((), ())), preferred_element_type=F32)


def _split3(x):
    hi = x.astype(BF16)
    r = x - hi.astype(F32)
    mid = r.astype(BF16)
    lo = (r - mid.astype(F32)).astype(BF16)
    return hi, mid, lo


def _norm_kernel(x_ref, g_ref, h_ref):
    h_ref[...] = _rms(x_ref[...], g_ref[...]).astype(BF16)


def _norm(x, g):
    m, d = x.shape
    tm = _pick(m, (512, 256, 128))
    return pl.pallas_call(
        _norm_kernel,
        grid=(m // tm,),
        in_specs=[pl.BlockSpec((tm, d), lambda i: (i, 0)), pl.BlockSpec((1, d), lambda i: (0, 0))],
        out_specs=pl.BlockSpec((tm, d), lambda i: (i, 0)),
        out_shape=jax.ShapeDtypeStruct((m, d), BF16),
        compiler_params=_params(("parallel",)),
        name="norm_in",
    )(x, g)


def _inproj_kernel(h_ref, w_ref, cs_ref, wm_ref, o_ref, misc_ref):
    h = h_ref[...]
    acc = _dot(h, w_ref[...]) * cs_ref[...]
    for s in range(o_ref.shape[0]):
        o_ref[s] = acc[:, s * LANES:(s + 1) * LANES].astype(BF16)

    @pl.when(pl.program_id(1) == 0)
    def _():
        misc_ref[...] = _dot(h, wm_ref[...])


def _inproj(h, w_main, colscale, w_misc, l):
    m, d = h.shape
    n = w_main.shape[2]
    tm = _pick(m, (1024, 512, 256))
    tn = _pick(n, (11 * LANES, 4 * LANES))
    return pl.pallas_call(
        _inproj_kernel,
        grid=(m // tm, n // tn),
        in_specs=[pl.BlockSpec((tm, d), lambda i, j: (i, 0)),
                  pl.BlockSpec((None, d, tn), lambda i, j: (l, 0, j)),
                  pl.BlockSpec((1, tn), lambda i, j: (0, j)),
                  pl.BlockSpec((None, d, LANES), lambda i, j: (l, 0, 0))],
        out_specs=[pl.BlockSpec((tn // LANES, tm, LANES), lambda i, j: (j, i, 0)),
                   pl.BlockSpec((tm, LANES), lambda i, j: (i, 0))],
        out_shape=[jax.ShapeDtypeStruct((n // LANES, m, LANES), BF16),
                   jax.ShapeDtypeStruct((m, LANES), F32)],
        compiler_params=_params(("parallel", "arbitrary")),
        name="inproj",
    )(h, w_main, colscale, w_misc)


def _logf_kernel(misc_ref, bias_ref, o_ref, *, tc):
    s = misc_ref.shape[0]
    row = lax.broadcasted_iota(jnp.int32, (tc, tc), 0)
    col = lax.broadcasted_iota(jnp.int32, (tc, tc), 1)
    tri = jnp.where(col <= row, 1.0, 0.0).astype(BF16)
    lane = lax.broadcasted_iota(jnp.int32, (tc, LANES), 1)

    def chunk(c, carry):
        r0 = pl.multiple_of(c * tc, tc)
        z = misc_ref[pl.ds(r0, tc), :] + bias_ref[...]
        lf = jnp.minimum(z, 0.0) - jnp.log1p(jnp.exp(-jnp.abs(z)))
        hi, mid, lo = _split3(lf)
        cum = _dot(tri, hi) + _dot(tri, mid) + _dot(tri, lo) + carry
        for hd in range(FOX_HEADS):
            c2 = jnp.broadcast_to(cum[:, hd:hd + 1], (tc, LANES)) * LOG2E
            c_hi = c2.astype(BF16).astype(F32)
            c_mid = (c2 - c_hi).astype(BF16).astype(F32)
            aug = jnp.where(lane == 0, c_hi, jnp.where(lane == 1, c_mid,
                                                      jnp.where(lane == 2, c2 - c_hi - c_mid, 0.0)))
            o_ref[hd, pl.ds(r0, tc), :] = aug.astype(BF16)
        return cum[tc - 1:tc, :]

    lax.fori_loop(0, s // tc, chunk, jnp.zeros((1, LANES), F32))


def _logf(misc, bias_row, batch, seq):
    tc = _pick(seq, (256, 128))
    return pl.pallas_call(
        functools.partial(_logf_kernel, tc=tc),
        grid=(batch,),
        in_specs=[pl.BlockSpec((seq, LANES), lambda b: (b, 0)), pl.BlockSpec((1, LANES), lambda b: (0, 0))],
        out_specs=pl.BlockSpec((FOX_HEADS, seq, LANES), lambda b: (0, b, 0)),
        out_shape=jax.ShapeDtypeStruct((FOX_HEADS, batch * seq, LANES), BF16),
        compiler_params=_params(("parallel",)),
        name="fox_logf",
    )(misc, bias_row)


def _flash_scratch(tk, mcols):
    return [pltpu.VMEM((tk, mcols), F32), pltpu.VMEM((1, mcols), F32), pltpu.VMEM((1, mcols), F32),
            pltpu.VMEM((LANES, mcols), F32)]


def _causal_bias(tk, mcols, tq, offset):
    rk = lax.broadcasted_iota(jnp.int32, (tk, mcols), 0)
    rq = lax.broadcasted_iota(jnp.int32, (tk, mcols), 1) & (tq - 1)
    return jnp.where(rk <= rq + offset, 0.0, NEG_INF)


def _flash_tiles(qa, ka_ref, vt_ref, n_full, diag_bias_ref, tk, scratch, next_qa=None, late_cols=None):
    s_ref, m_ref, l_ref, acc_ref = scratch

    def logits(j, queries=qa):
        k0 = pl.multiple_of(j * tk, tk)
        return _dot_nt(ka_ref[pl.ds(k0, tk), :], queries)

    def step(s, k0, rows=tk, cols=slice(None)):
        m_prev = m_ref[:, cols]
        m_new = jnp.maximum(m_prev, jnp.max(s, axis=0, keepdims=True))
        alpha = jnp.exp2(m_prev - m_new)
        p = jnp.exp2(s - m_new)
        l_ref[:, cols] = alpha * l_ref[:, cols] + jnp.sum(p, axis=0, keepdims=True)
        acc_ref[:, cols] = alpha * acc_ref[:, cols] + _dot(vt_ref[:, pl.ds(k0, rows)], p.astype(BF16))
        m_ref[:, cols] = m_new

    def diag_step():
        k0 = pl.multiple_of(n_full * tk, tk)
        if late_cols is None:
            step(s_ref[...] + diag_bias_ref[...], k0)
            return
        half = tk // 2
        step(s_ref[0:half, :] + diag_bias_ref[0:half, :], k0, rows=half)
        for c0, c1 in late_cols:
            step(s_ref[half:tk, c0:c1] + diag_bias_ref[half:tk, c0:c1], k0 + half, rows=half, cols=slice(c0, c1))

    m_ref[...] = jnp.full(m_ref.shape, NEG_INF, F32)
    l_ref[...] = jnp.zeros(l_ref.shape, F32)
    acc_ref[...] = jnp.zeros(acc_ref.shape, F32)
    if next_qa is None:
        s_ref[...] = logits(0)

    def full(j, c):
        s_next = logits(j + 1)
        step(s_ref[...], pl.multiple_of(j * tk, tk))
        s_ref[...] = s_next
        return c

    lax.fori_loop(0, n_full, full, 0)
    if next_qa is None:
        diag_step()
    else:
        s_next = logits(0, next_qa)
        diag_step()
        s_ref[...] = s_next
    return acc_ref[...] * (1.0 / l_ref[...])


def _stage_kv(ka_ref, vt_ref, k_ref, kx_ref, v_ref, chunk, pad=0):
    def body(c, carry):
        r0 = pl.multiple_of(c * chunk, chunk)
        ka_ref[pl.ds(pad + r0, chunk), 0:LANES] = k_ref[0, pl.ds(r0, chunk), :]
        ka_ref[pl.ds(pad + r0, chunk), LANES:2 * LANES] = kx_ref[pl.ds(r0, chunk), :]
        vt_ref[:, pl.ds(pad + r0, chunk)] = v_ref[0, pl.ds(r0, chunk), :].astype(F32).T.astype(BF16)
        return carry

    lax.fori_loop(0, v_ref.shape[1] // chunk, body, 0)


def _fox_kernel(q_ref, k_ref, v_ref, cx_ref, o_ref, ka_ref, vt_ref, tri_ref, *flash, tq, tk):
    _stage_kv(ka_ref, vt_ref, k_ref, cx_ref.at[0], v_ref, tk)
    for o in range(tk // tq):
        tri_ref[o] = _causal_bias(tk, tq, tq, o * tq)
    lane = lax.broadcasted_iota(jnp.int32, (tq, LANES), 1)
    qx = jnp.where(lane < 3, -1.0, 0.0).astype(BF16)

    nq = q_ref.shape[1] // tq

    def queries(qi):
        q0 = pl.multiple_of(qi * tq, tq)
        return jnp.concatenate([q_ref[0, pl.ds(q0, tq), :], qx], axis=1)

    flash[0][...] = _dot_nt(ka_ref[0:tk, :], queries(0))

    def qstep(qi, c):
        o = _flash_tiles(queries(qi), ka_ref, vt_ref, (qi * tq) // tk, tri_ref.at[qi & (tk // tq - 1)], tk, flash,
                         next_qa=queries(jnp.minimum(qi + 1, nq - 1)),
                         late_cols=((tq // 2, tq),) if tq == tk else None)
        o_ref[pl.ds(pl.multiple_of(qi * tq, tq), tq), :] = o.T.astype(BF16)
        return c

    lax.fori_loop(0, nq, qstep, 0)


def _attn_tiles(seq):
    tq = _pick(seq, (512, 256, 128))
    tk = 2 * tq if seq % (2 * tq) == 0 else tq
    return tq, tk


def _fox(slabs, cx, batch, seq):
    tk = _attn_tiles(seq)[1]
    tq = tk
    m = batch * seq
    head = lambda base: pl.BlockSpec((1, seq, LANES), lambda b, h: (base + h, b, 0))
    return pl.pallas_call(
        functools.partial(_fox_kernel, tq=tq, tk=tk),
        grid=(batch, FOX_HEADS),
        in_specs=[head(SLAB_FQ), head(SLAB_FK), head(SLAB_FV), head(0)],
        out_specs=pl.BlockSpec((seq, LANES), lambda b, h: (b, h)),
        out_shape=jax.ShapeDtypeStruct((m, FOX_HEADS * HEAD_DIM), BF16),
        scratch_shapes=[pltpu.VMEM((seq, 2 * LANES), BF16), pltpu.VMEM((LANES, seq), BF16),
                        pltpu.VMEM((tk // tq, tk, tq), F32)] + _flash_scratch(tk, tq),
        compiler_params=_params(("parallel", "parallel")),
        name="fox_attn",
    )(slabs, slabs, slabs, cx)


def _diff_kernel(q_ref, k_ref, v_ref, kx_ref, qx_ref, lam_ref, sub_ref, o_ref, ka_ref, vt_ref, tri_ref, *flash,
                 t, tk, lam_init):
    _stage_kv(ka_ref, vt_ref, k_ref, kx_ref, v_ref, tk)
    for o in range(tk // t):
        tri_ref[o] = _causal_bias(tk, 2 * t, t, o * t)
    lane = lax.broadcasted_iota(jnp.int32, (t, LANES), 1)
    qx = jnp.broadcast_to(qx_ref[0], (t, LANES)).astype(BF16)
    lv = lam_ref[...]
    lam = (jnp.exp(jnp.sum(lv[0:1] * lv[1:2], axis=-1, keepdims=True))
           - jnp.exp(jnp.sum(lv[2:3] * lv[3:4], axis=-1, keepdims=True)) + lam_init)

    nq = q_ref.shape[1] // t

    def queries(qi):
        q = q_ref[0, pl.ds(pl.multiple_of(qi * t, t), t), :].astype(F32)
        return jnp.concatenate([
            jnp.concatenate([jnp.where(lane < DIFF_HALF, q, 0.0).astype(BF16), qx], axis=1),
            jnp.concatenate([jnp.where(lane >= DIFF_HALF, q, 0.0).astype(BF16), qx], axis=1)], axis=0)

    flash[0][...] = _dot_nt(ka_ref[0:tk, :], queries(0))

    def qstep(qi, c):
        o = _flash_tiles(queries(qi), ka_ref, vt_ref, (qi * t) // tk, tri_ref.at[qi & (tk // t - 1)], tk, flash,
                         next_qa=queries(jnp.minimum(qi + 1, nq - 1)),
                         late_cols=((t // 2, t), (t + t // 2, 2 * t)) if t == tk else None)
        o = (o[:, :t] - lam * o[:, t:]).T
        o_ref[pl.ds(pl.multiple_of(qi * t, t), t), :] = (_rms(o, sub_ref[...]) * (1.0 - lam_init)).astype(BF16)
        return c

    lax.fori_loop(0, nq, qstep, 0)


def _diff(slabs, kx, qx_diff, lam_vec, subln, batch, seq, lam_init):
    tk = _attn_tiles(seq)[1]
    t = tk
    m = batch * seq
    head = lambda base: pl.BlockSpec((1, seq, LANES), lambda b, h: (base + h, b, 0))
    return pl.pallas_call(
        functools.partial(_diff_kernel, t=t, tk=tk, lam_init=lam_init),
        grid=(batch, DIFF_HEADS),
        in_specs=[head(SLAB_DQ), head(SLAB_DK), head(SLAB_DV),
                  pl.BlockSpec((seq, LANES), lambda b, h: (0, 0)),
                  pl.BlockSpec((1, 1, LANES), lambda b, h: (h, 0, 0)),
                  pl.BlockSpec((4, DIFF_HALF), lambda b, h: (0, 0)),
                  pl.BlockSpec((1, LANES), lambda b, h: (0, 0))],
        out_specs=pl.BlockSpec((seq, LANES), lambda b, h: (b, h)),
        out_shape=jax.ShapeDtypeStruct((m, DIFF_HEADS * HEAD_DIM), BF16),
        scratch_shapes=[pltpu.VMEM((seq, 2 * LANES), BF16), pltpu.VMEM((LANES, seq), BF16),
                        pltpu.VMEM((tk // t, tk, 2 * t), F32)] + _flash_scratch(tk, 2 * t),
        compiler_params=_params(("parallel", "parallel")),
        name="diff_attn",
    )(slabs, slabs, slabs, kx, qx_diff, lam_vec, subln)


def _compress_kernel(y_ref, pos_ref, w1_ref, w2_ref, o_ref):
    half = y_ref.shape[3]
    y = y_ref[0, 0].astype(F32)
    top = (y + pos_ref[0, 0:1, :]).astype(BF16)
    bot = (y + pos_ref[0, 1:2, :]).astype(BF16)
    a = _dot(top, w1_ref[0, 0:half, :])
    b = _dot(bot, w1_ref[0, half:2 * half, :])
    nrow = a.shape[0]
    hid = a + pltpu.roll(b, nrow - 1, 0)
    hid = hid * jax.nn.sigmoid(hid)
    o_ref[0, 0] = _dot(hid.astype(BF16), w2_ref[0]).astype(BF16)


def _compress(ykv, pos2, w1, w2, batch, l):
    nrow, half = ykv.shape[2], ykv.shape[3]
    return pl.pallas_call(
        _compress_kernel,
        grid=(4, batch),
        in_specs=[pl.BlockSpec((1, 1, nrow, half), lambda s, b: (s, b, 0, 0)),
                  pl.BlockSpec((None, 1, 2, half), lambda s, b: (l, s // 2, 0, 0)),
                  pl.BlockSpec((None, 1, 2 * half, CMP_HIDDEN), lambda s, b: (l, s // 2, 0, 0)),
                  pl.BlockSpec((None, 1, CMP_HIDDEN, HEAD_DIM), lambda s, b: (l, s // 2, 0, 0))],
        out_specs=pl.BlockSpec((1, 1, nrow, HEAD_DIM), lambda s, b: (s, b, 0, 0)),
        out_shape=jax.ShapeDtypeStruct((4, batch, nrow, HEAD_DIM), BF16),
        compiler_params=_params(("parallel", "parallel")),
        name="nsa_compress",
    )(ykv, pos2, w1, w2)


def _topk_bias(cand, topk):
    nsel, tq = cand.shape
    sub = 8
    blocks = [cand[b * sub:(b + 1) * sub] for b in range(nsel // sub)]
    jidx = lax.broadcasted_iota(jnp.int32, (sub, tq), 0)
    ranks = [jnp.zeros((sub, tq), F32) for _ in blocks]
    for k in range(nsel):
        rk = cand[k:k + 1, :]
        for b, cb in enumerate(blocks):
            if b * sub > k:
                beats = rk >= cb
            elif b * sub + sub - 1 < k:
                beats = rk > cb
            else:
                beats = (rk > cb) | ((rk == cb) & (jidx + b * sub > k))
            ranks[b] = ranks[b] + jnp.where(beats, 1.0, 0.0)
    rank = jnp.concatenate(ranks, axis=0)
    return jnp.where(rank < topk, 0.0, SEL_MASK)


def _nsa_kernel(q_ref, kc_ref, vc_ref, ks_ref, vs_ref, kw_ref, vw_ref, kx_ref, cx_ref, ovt_ref, qx_ref,
                misc_ref, o_ref, ksa_ref, kwa_ref, vst_ref, vwt_ref, vct_ref, gt_ref, dbias_ref, wbias_ref,
                comb_ref, *flash, tq, tk):
    g = pl.program_id(1)
    qi = pl.program_id(2)
    mcols = NSA_REP * tq
    nc = kc_ref.shape[2]
    seq = kx_ref.shape[0]
    nsel = seq // SEL_BLOCK
    span = WINDOW + tq

    @pl.when(qi == 0)
    def _():
        _stage_kv(ksa_ref, vst_ref, ks_ref, kx_ref, vs_ref, tk)
        _stage_kv(kwa_ref, vwt_ref, kw_ref, kx_ref, vw_ref, tk, pad=WINDOW)
        pad_lane = lax.broadcasted_iota(jnp.int32, (WINDOW, 2 * LANES), 1)
        kwa_ref[0:WINDOW, :] = jnp.where(pad_lane == LANES + AUG_PAD, SEL_MASK, 0.0).astype(BF16)
        vwt_ref[:, 0:WINDOW] = jnp.zeros((LANES, WINDOW), BF16)
        vct_ref[...] = vc_ref[0, 0].astype(F32).T.astype(BF16)
        for o in range(tk // tq):
            dbias_ref[o] = _causal_bias(tk, mcols, tq, o * tq)
        rk = lax.broadcasted_iota(jnp.int32, (span, mcols), 0)
        rq = lax.broadcasted_iota(jnp.int32, (span, mcols), 1) & (tq - 1)
        wbias_ref[...] = jnp.where((rk > rq) & (rk <= rq + WINDOW), 0.0, NEG_INF)

    qx = jnp.concatenate([jnp.broadcast_to(qx_ref[0, r:r + 1, :], (tq, LANES)) for r in range(NSA_REP)], axis=0)
    kca = jnp.concatenate([kc_ref[0, 0], cx_ref[...]], axis=1)
    _nsa_query_tile(qi, g, qx, kca, q_ref, ovt_ref, misc_ref, o_ref, ksa_ref, kwa_ref, vst_ref, vwt_ref,
                    vct_ref, gt_ref, dbias_ref, wbias_ref, comb_ref, flash, tq=tq, tk=tk, nc=nc, nsel=nsel)


def _nsa_query_tile(qi, g, qx, kca, q_ref, ovt_ref, misc_ref, o_ref, ksa_ref, kwa_ref, vst_ref, vwt_ref, vct_ref,
                    gt_ref, dbias_ref, wbias_ref, comb_ref, flash, *, tq, tk, nc, nsel):
    mcols = NSA_REP * tq
    span = WINDOW + tq
    q0 = pl.multiple_of(qi * tq, tq)
    q4 = q_ref[...].reshape(mcols, LANES)
    qa = jnp.concatenate([q4, qx.astype(BF16)], axis=1)
    col = lax.broadcasted_iota(jnp.int32, (1, mcols), 1)
    col_pos = q0 + (col & (tq - 1))

    sc = _dot_nt(kca, qa)
    cend = lax.broadcasted_iota(jnp.int32, (nc, mcols), 0) * CMP_STRIDE + (CMP_BLOCK - 1)
    sc = jnp.where(cend <= col_pos, sc, NEG_INF)
    e = jnp.exp2(sc - jnp.max(sc, axis=0, keepdims=True))
    inv = jnp.where(col_pos >= CMP_BLOCK - 1, 1.0 / jnp.sum(e, axis=0, keepdims=True), 0.0)
    p = e * inv
    o_cmp = _dot(vct_ref[...], p.astype(BF16))

    psum = p[:, 0:tq]
    for r in range(1, NSA_REP):
        psum = psum + p[:, r * tq:(r + 1) * tq]
    p_hi = psum.astype(BF16)
    p_lo = (psum - p_hi.astype(F32)).astype(BF16)
    imp = _dot(ovt_ref[...], p_hi) + _dot(ovt_ref[...], p_lo)
    qpos = q0 + lax.broadcasted_iota(jnp.int32, (LANES, tq), 1)
    blk = lax.broadcasted_iota(jnp.int32, (LANES, tq), 0)
    cur = jnp.right_shift(qpos, SEL_BLOCK.bit_length() - 1)
    forced = (blk == 0) | (blk == cur) | (blk == cur - 1)
    imp = jnp.where(blk <= cur, jnp.where(forced, FORCE_SCORE, imp), -1.0)
    bias_t = _topk_bias(imp[0:nsel], min(SEL_TOPK, nsel))
    if nsel < LANES:
        bias_t = jnp.concatenate([bias_t, jnp.zeros((LANES - nsel, tq), F32)], axis=0)
    selbias = jnp.concatenate([bias_t.T] * NSA_REP, axis=0)
    lane4 = lax.broadcasted_iota(jnp.int32, (mcols, LANES), 1)
    qa_sel = jnp.concatenate([q4, jnp.where(lane4 < SEL_BLOCK, selbias, qx).astype(BF16)], axis=1)

    sw = _dot_nt(kwa_ref[pl.ds(q0, span), :], qa) + wbias_ref[...]
    ew = jnp.exp2(sw - jnp.max(sw, axis=0, keepdims=True))
    o_win = _dot(vwt_ref[:, pl.ds(q0, span)], ew.astype(BF16)) * (1.0 / jnp.sum(ew, axis=0, keepdims=True))

    gt_ref[...] = jax.nn.sigmoid(misc_ref[...]).T
    gate = lambda r, i: gt_ref[pl.ds(MISC_NG + 3 * (NSA_REP * g + r) + i, 1), :]
    for r in range(NSA_REP):
        sl = slice(r * tq, (r + 1) * tq)
        comb_ref[:, sl] = gate(r, 0) * o_cmp[:, sl] + gate(r, 2) * o_win[:, sl]

    o_sel = _flash_tiles(qa_sel, ksa_ref, vst_ref, q0 // tk, dbias_ref.at[qi & (tk // tq - 1)], tk, flash)
    for r in range(NSA_REP):
        sl = slice(r * tq, (r + 1) * tq)
        out = comb_ref[:, sl] + gate(r, 1) * o_sel[:, sl]
        o_ref[:, r * LANES:(r + 1) * LANES] = out.T.astype(BF16)


def _nsa(slabs, kvc, kx, cx_cmp, ovt, qx_nsa, misc, batch, seq):
    tq = _pick(seq, (512, 256, 128))
    tk = _pick(seq, (512, 256, 128))
    assert tq & (tq - 1) == 0 and tk % tq == 0 and seq >= WINDOW + tq
    nq = seq // tq
    m = batch * seq
    mcols = NSA_REP * tq
    nc = kvc.shape[2]
    kv_spec = lambda base: pl.BlockSpec((1, seq, LANES), lambda b, g, i: (base + g, b, 0))
    const = lambda shape: pl.BlockSpec(shape, lambda b, g, i: (0,) * len(shape), pipeline_mode=pl.Buffered(1))
    return pl.pallas_call(
        functools.partial(_nsa_kernel, tq=tq, tk=tk),
        grid=(batch, NSA_KV_HEADS, nq),
        in_specs=[pl.BlockSpec((NSA_REP, tq, LANES), lambda b, g, i: (SLAB_NQ // NSA_REP + g, b * nq + i, 0)),
                  pl.BlockSpec((1, 1, nc, LANES), lambda b, g, i: (g, b, 0, 0)),
                  pl.BlockSpec((1, 1, nc, LANES), lambda b, g, i: (2 + g, b, 0, 0)),
                  kv_spec(SLAB_NKS), kv_spec(SLAB_NVS), kv_spec(SLAB_NKW), kv_spec(SLAB_NVW),
                  const((seq, LANES)), const((nc, LANES)), const((LANES, nc)),
                  pl.BlockSpec((1, NSA_REP, LANES), lambda b, g, i: (g, 0, 0)),
                  pl.BlockSpec((tq, LANES), lambda b, g, i: (b * nq + i, 0))],
        out_specs=pl.BlockSpec((tq, NSA_REP * LANES), lambda b, g, i: (b * nq + i, g)),
        out_shape=jax.ShapeDtypeStruct((m, NSA_HEADS * HEAD_DIM), BF16),
        scratch_shapes=[pltpu.VMEM((seq, 2 * LANES), BF16), pltpu.VMEM((WINDOW + seq, 2 * LANES), BF16),
                        pltpu.VMEM((LANES, seq), BF16), pltpu.VMEM((LANES, WINDOW + seq), BF16),
                        pltpu.VMEM((LANES, nc), BF16), pltpu.VMEM((LANES, tq), F32),
                        pltpu.VMEM((tk // tq, tk, mcols), F32), pltpu.VMEM((WINDOW + tq, mcols), F32),
                        pltpu.VMEM((LANES, mcols), F32)] + _flash_scratch(tk, mcols),
        compiler_params=_params(("parallel", "parallel", "arbitrary")),
        name="nsa_attn",
    )(slabs, kvc, kvc, slabs, slabs, slabs, slabs, kx, cx_cmp, ovt, qx_nsa, misc)


def _merge_kernel(h_ref, of_ref, od_ref, on_ref, wg0_ref, wg1_ref, wg2_ref, wf_ref, wd_ref, wn_ref, o_ref):
    h = h_ref[...]
    acc = jax.nn.sigmoid(_dot(h, wg0_ref[...])) * _dot(of_ref[...], wf_ref[...])
    acc = acc + jax.nn.sigmoid(_dot(h, wg1_ref[...])) * _dot(od_ref[...], wd_ref[...])
    acc = acc + jax.nn.sigmoid(_dot(h, wg2_ref[...])) * _dot(on_ref[...], wn_ref[...])
    o_ref[...] = acc.astype(BF16)


def _merge(h, o_fox, o_diff, o_nsa, w_gate, wb_fox, wb_diff, wb_nsa, l):
    m, d = h.shape
    tm = _pick(m, (1024, 512, 256))
    tn = _pick(d, (512, 256, 128))
    nj = d // tn
    row = lambda width: pl.BlockSpec((tm, width), lambda i, j: (i, 0))
    gate = lambda t: pl.BlockSpec((None, d, tn), lambda i, j: (l, 0, t * nj + j))
    col = lambda k: pl.BlockSpec((None, k, tn), lambda i, j: (l, 0, j))
    return pl.pallas_call(
        _merge_kernel,
        grid=(m // tm, nj),
        in_specs=[row(d), row(o_fox.shape[1]), row(o_diff.shape[1]), row(o_nsa.shape[1]),
                  gate(0), gate(1), gate(2),
                  col(wb_fox.shape[1]), col(wb_diff.shape[1]), col(wb_nsa.shape[1])],
        out_specs=pl.BlockSpec((tm, tn), lambda i, j: (i, j)),
        out_shape=jax.ShapeDtypeStruct((m, d), BF16),
        compiler_params=_params(("parallel", "arbitrary")),
        name="gate_merge",
    )(h, o_fox, o_diff, o_nsa, w_gate, w_gate, w_gate, wb_fox, wb_diff, wb_nsa)


def _wout_kernel(a_ref, w_ref, x_ref, gp_ref, gn_ref, xo_ref, ho_ref):
    sub = min(a_ref.shape[0], 256)
    for s in range(a_ref.shape[0] // sub):
        rows = slice(s * sub, (s + 1) * sub)
        y = _dot(a_ref[rows, :], w_ref[...])
        x_new = x_ref[rows, :] + _rms(y, gp_ref[...])
        xo_ref[rows, :] = x_new
        ho_ref[rows, :] = _rms(x_new, gn_ref[...]).astype(BF16)


def _wout(a, w, x, g_post, g_next, l, name):
    m, d = x.shape
    k = a.shape[1]
    tm = _pick(m, (512, 256, 128)) if k * d * 2 <= 8 * 1024 * 1024 else _pick(m, (256, 128))
    row = lambda width: pl.BlockSpec((tm, width), lambda i: (i, 0))
    vec = lambda: pl.BlockSpec((1, d), lambda i: (0, 0))
    return pl.pallas_call(
        _wout_kernel,
        grid=(m // tm,),
        in_specs=[row(k), pl.BlockSpec((None, k, d), lambda i: (l, 0, 0), pipeline_mode=pl.Buffered(1)),
                  row(d), vec(), vec()],
        out_specs=[row(d), row(d)],
        out_shape=[jax.ShapeDtypeStruct((m, d), F32), jax.ShapeDtypeStruct((m, d), BF16)],
        compiler_params=_params(("parallel",)),
        name=name,
    )(a, w, x, g_post, g_next)


def _ffn_up_kernel(h_ref, wg_ref, wu_ref, o_ref, wgb_ref, wub_ref):
    @pl.when(pl.program_id(1) == 0)
    def _():
        wgb_ref[...] = wg_ref[...].astype(BF16)
        wub_ref[...] = wu_ref[...].astype(BF16)

    h = h_ref[...]
    gate = _dot(h, wgb_ref[...])
    o_ref[...] = (gate * jax.nn.sigmoid(gate) * _dot(h, wub_ref[...])).astype(BF16)


def _ffn_up(h, w_up, l):
    m, d = h.shape
    dff = w_up.shape[2] // 2
    tm = _pick(m, (1024, 512, 256))
    tn = _pick(dff, (512, 256, 128))
    nj = dff // tn
    return pl.pallas_call(
        _ffn_up_kernel,
        grid=(nj, m // tm),
        in_specs=[pl.BlockSpec((tm, d), lambda j, i: (i, 0)),
                  pl.BlockSpec((None, d, tn), lambda j, i: (l, 0, j)),
                  pl.BlockSpec((None, d, tn), lambda j, i: (l, 0, nj + j))],
        out_specs=pl.BlockSpec((tm, tn), lambda j, i: (i, j)),
        out_shape=jax.ShapeDtypeStruct((m, dff), BF16),
        scratch_shapes=[pltpu.VMEM((d, tn), BF16), pltpu.VMEM((d, tn), BF16)],
        compiler_params=_params(("parallel", "arbitrary")),
        name="ffn_up",
    )(h, w_up, w_up)


def _pos_columns(pos):
    lane = jnp.arange(LANES)[None, :]
    hi = (lane >= AUG_HI) & (lane < AUG_HI + AUG_TERMS)
    lo = (lane >= AUG_LO) & (lane < AUG_LO + AUG_TERMS)
    return jnp.where(hi, (pos // LANES)[:, None], jnp.where(lo, (pos % LANES)[:, None], 0))


def _key_aug_table(seq):
    j = jnp.arange(seq)
    lane = jnp.arange(LANES)[None, :]
    onehot = (lane == (j // SEL_BLOCK)[:, None]) & (lane < SEL_BLOCK)
    return (_pos_columns(j) + onehot.astype(jnp.int32)).astype(BF16)


def _cmp_aug_table(nrow):
    return _pos_columns(jnp.arange(nrow) * CMP_STRIDE + CMP_BLOCK - 1).astype(BF16)


def _overlap_table_t(nrow, seq):
    start = jnp.arange(nrow)[None, :] * CMP_STRIDE
    blk = jnp.arange(LANES)[:, None]
    sel = blk * SEL_BLOCK
    ov = ((start < sel + SEL_BLOCK) & (start + CMP_BLOCK - 1 >= sel) & (blk < seq // SEL_BLOCK)
          & (jnp.arange(nrow)[None, :] < (seq - CMP_BLOCK) // CMP_STRIDE + 1))
    return ov.astype(BF16)


def _query_aug_rows(n_heads):
    slopes = 2.0 ** (-8.0 * jnp.arange(1, n_heads + 1, dtype=F32) / n_heads)
    terms = [t.astype(F32) for t in _split3(jnp.float32(LOG2E))]
    lane = jnp.arange(LANES)[None, :]
    out = jnp.where(lane == AUG_PAD, 1.0, jnp.zeros((n_heads, LANES), F32))
    for i, t in enumerate(terms):
        out = jnp.where(lane == AUG_HI + i, slopes[:, None] * t * LANES, out)
        out = jnp.where(lane == AUG_LO + i, slopes[:, None] * t, out)
    return out


def kernel(x, w_in, fox_forget_bias, diff_lambda, diff_subln, nsa_cmp_pos, nsa_cmp_w1, nsa_cmp_w2,
           w_branch_fox, w_branch_diff, w_branch_nsa, w_gate, w_out, norm_gains, w_ffn_up, w_ffn_down):
    batch, seq, d = x.shape
    depth = w_in.shape[0]
    m = batch * seq
    fw, dw, nw, kvw = FOX_HEADS * HEAD_DIM, DIFF_HEADS * HEAD_DIM, NSA_HEADS * HEAD_DIM, NSA_KV_HEADS * HEAD_DIM
    ff0 = 3 * fw
    dq0 = ff0 + FOX_HEADS
    ng0 = dq0 + 3 * dw + nw + 6 * kvw
    n_main = ng0 - FOX_HEADS
    assert n_main == N_SLABS * LANES and w_in.shape[2] == ng0 + 3 * NSA_HEADS

    w_main = jnp.concatenate([w_in[:, :, :ff0], w_in[:, :, dq0:ng0]], axis=2).astype(BF16)
    w_misc = jnp.concatenate([w_in[:, :, ff0:dq0], w_in[:, :, ng0:],
                              jnp.zeros((depth, d, LANES - FOX_HEADS - 3 * NSA_HEADS), F32)], axis=2).astype(BF16)
    colscale = jnp.ones((n_main,), F32)
    colscale = colscale.at[SLAB_FQ * LANES:SLAB_FK * LANES].set(HEAD_DIM ** -0.5 * LOG2E)
    colscale = colscale.at[SLAB_DQ * LANES:SLAB_DK * LANES].set(DIFF_HALF ** -0.5 * LOG2E)
    colscale = colscale.at[SLAB_NQ * LANES:SLAB_NKC * LANES].set(HEAD_DIM ** -0.5 * LOG2E)
    colscale = colscale[None, :]
    fbias = jnp.pad(fox_forget_bias.astype(F32), ((0, 0), (0, LANES - FOX_HEADS)))[:, None, :]
    half = CMP_STRIDE * HEAD_DIM
    pos2 = nsa_cmp_pos.astype(F32).reshape(depth, 2, 2, half)
    w1 = nsa_cmp_w1.astype(BF16)
    w2 = nsa_cmp_w2.astype(BF16)
    wbf, wbd, wbn = w_branch_fox.astype(BF16), w_branch_diff.astype(BF16), w_branch_nsa.astype(BF16)
    wg, wo = w_gate.astype(BF16), w_out.astype(BF16)
    wup, wdn = w_ffn_up.astype(F32), w_ffn_down.astype(BF16)
    gains = norm_gains.astype(F32)

    nrow = seq // CMP_STRIDE
    kx = _key_aug_table(seq)
    cx_cmp = _cmp_aug_table(nrow)
    ovt = _overlap_table_t(nrow, seq)
    qx_diff = _query_aug_rows(DIFF_HEADS)[:, None, :]
    qx_nsa = _query_aug_rows(NSA_HEADS).reshape(NSA_KV_HEADS, NSA_REP, LANES)

    xf = x.reshape(m, d).astype(F32)
    h = _norm(xf, gains[0, 0][None, :])
    for l in range(depth):
        lam_init = 0.8 - 0.6 * math.exp(-0.3 * l)
        slabs, misc = _inproj(h, w_main, colscale, w_misc, l)
        cx_fox = _logf(misc, fbias[l], batch, seq)
        o_fox = _fox(slabs, cx_fox, batch, seq)
        o_diff = _diff(slabs, kx, qx_diff, diff_lambda[l].astype(F32), diff_subln[l].astype(F32)[None, :],
                       batch, seq, lam_init)
        ykv = slabs[SLAB_NKC:SLAB_NKC + 4].reshape(4, batch, nrow, half)
        kvc = _compress(ykv, pos2, w1, w2, batch, l)
        o_nsa = _nsa(slabs, kvc, kx, cx_cmp, ovt, qx_nsa, misc, batch, seq)
        merged = _merge(h, o_fox, o_diff, o_nsa, wg, wbf, wbd, wbn, l)
        xf, h2 = _wout(merged, wo, xf, gains[l, 1][None, :], gains[l, 2][None, :], l, "out_proj")
        act = _ffn_up(h2, wup, l)
        g_next = gains[min(l + 1, depth - 1), 0][None, :]
        xf, h = _wout(act, wdn, xf, gains[l, 3][None, :], g_next, l, "ffn_down")
    return xf.reshape(batch, seq, d).astype(x.dtype)
```

```python
import functools
import math

import jax
import jax.numpy as jnp
from jax import lax
from jax.experimental import pallas as pl
from jax.experimental.pallas import tpu as pltpu

F32 = jnp.float32
BF16 = jnp.bfloat16

HEAD_DIM = 128
FOX_HEADS = 4
DIFF_HEADS = 4
DIFF_HALF = HEAD_DIM // 2
NSA_HEADS = 8
NSA_KV_HEADS = 2
NSA_REP = NSA_HEADS // NSA_KV_HEADS
CMP_BLOCK = 32
CMP_STRIDE = 16
CMP_HIDDEN = 256
SEL_BLOCK = 64
SEL_TOPK = 16
WINDOW = 512
N_BRANCHES = 3
EPS = 1e-6
NEG_INF = -1e30
FORCE_SCORE = 1e4
SEL_MASK = NEG_INF
LOG2E = math.log2(math.e)
LANES = 128

SLAB_FQ, SLAB_FK, SLAB_FV = 0, 4, 8
SLAB_DQ, SLAB_DK, SLAB_DV = 12, 16, 20
SLAB_NQ = 24
SLAB_NKC, SLAB_NVC, SLAB_NKS, SLAB_NVS, SLAB_NKW, SLAB_NVW = 32, 34, 36, 38, 40, 42
N_SLABS = 44
MISC_FF = 0
MISC_NG = 4
AUG_HI = 64
AUG_LO = 67
AUG_TERMS = 3
AUG_PAD = 70

VMEM_LIMIT = 56 * 1024 * 1024


def _pick(n, prefs):
    for p in prefs:
        if p <= n and n % p == 0:
            return p
    return n


def _params(sem):
    return pltpu.CompilerParams(dimension_semantics=sem, vmem_limit_bytes=VMEM_LIMIT)


def _rms(y, g):
    return y * lax.rsqrt(jnp.mean(y * y, axis=-1, keepdims=True) + EPS) * g


def _dot(a, b):
    return jnp.dot(a, b, preferred_element_type=F32)


def _dot_nt(a, b):
    return lax.dot_general(a, b, (((1,), (1,)), ((), ())), preferred_element_type=F32)


def _split3(x):
    hi = x.astype(BF16)
    r = x - hi.astype(F32)
    mid = r.astype(BF16)
    lo = (r - mid.astype(F32)).astype(BF16)
    return hi, mid, lo


def _norm_kernel(x_ref, g_ref, h_ref):
    h_ref[...] = _rms(x_ref[...], g_ref[...]).astype(BF16)


def _norm(x, g):
    m, d = x.shape
    tm = _pick(m, (512, 256, 128))
    return pl.pallas_call(
        _norm_kernel,
        grid=(m // tm,),
        in_specs=[pl.BlockSpec((tm, d), lambda i: (i, 0)), pl.BlockSpec((1, d), lambda i: (0, 0))],
        out_specs=pl.BlockSpec((tm, d), lambda i: (i, 0)),
        out_shape=jax.ShapeDtypeStruct((m, d), BF16),
        compiler_params=_params(("parallel",)),
        name="norm_in",
    )(x, g)


def _inproj_kernel(h_ref, w_ref, cs_ref, wm_ref, o_ref, misc_ref):
    h = h_ref[...]
    acc = _dot(h, w_ref[...]) * cs_ref[...]
    for s in range(o_ref.shape[0]):
        o_ref[s] = acc[:, s * LANES:(s + 1) * LANES].astype(BF16)

    @pl.when(pl.program_id(1) == 0)
    def _():
        misc_ref[...] = _dot(h, wm_ref[...])


def _inproj(h, w_main, colscale, w_misc, l):
    m, d = h.shape
    n = w_main.shape[2]
    tm = _pick(m, (1024, 512, 256))
    tn = _pick(n, (11 * LANES, 4 * LANES))
    return pl.pallas_call(
        _inproj_kernel,
        grid=(m // tm, n // tn),
        in_specs=[pl.BlockSpec((tm, d), lambda i, j: (i, 0)),
                  pl.BlockSpec((None, d, tn), lambda i, j: (l, 0, j)),
                  pl.BlockSpec((1, tn), lambda i, j: (0, j)),
                  pl.BlockSpec((None, d, LANES), lambda i, j: (l, 0, 0))],
        out_specs=[pl.BlockSpec((tn // LANES, tm, LANES), lambda i, j: (j, i, 0)),
                   pl.BlockSpec((tm, LANES), lambda i, j: (i, 0))],
        out_shape=[jax.ShapeDtypeStruct((n // LANES, m, LANES), BF16),
                   jax.ShapeDtypeStruct((m, LANES), F32)],
        compiler_params=_params(("parallel", "arbitrary")),
        name="inproj",
    )(h, w_main, colscale, w_misc)


def _logf_kernel(misc_ref, bias_ref, o_ref, *, tc):
    s = misc_ref.shape[0]
    row = lax.broadcasted_iota(jnp.int32, (tc, tc), 0)
    col = lax.broadcasted_iota(jnp.int32, (tc, tc), 1)
    tri = jnp.where(col <= row, 1.0, 0.0).astype(BF16)
    lane = lax.broadcasted_iota(jnp.int32, (tc, LANES), 1)

    def chunk(c, carry):
        r0 = pl.multiple_of(c * tc, tc)
        z = misc_ref[pl.ds(r0, tc), :] + bias_ref[...]
        lf = jnp.minimum(z, 0.0) - jnp.log1p(jnp.exp(-jnp.abs(z)))
        hi, mid, lo = _split3(lf)
        cum = _dot(tri, hi) + _dot(tri, mid) + _dot(tri, lo) + carry
        for hd in range(FOX_HEADS):
            c2 = jnp.broadcast_to(cum[:, hd:hd + 1], (tc, LANES)) * LOG2E
            c_hi = c2.astype(BF16).astype(F32)
            c_mid = (c2 - c_hi).astype(BF16).astype(F32)
            aug = jnp.where(lane == 0, c_hi, jnp.where(lane == 1, c_mid,
                                                      jnp.where(lane == 2, c2 - c_hi - c_mid, 0.0)))
            o_ref[hd, pl.ds(r0, tc), :] = aug.astype(BF16)
        return cum[tc - 1:tc, :]

    lax.fori_loop(0, s // tc, chunk, jnp.zeros((1, LANES), F32))


def _logf(misc, bias_row, batch, seq):
    tc = _pick(seq, (256, 128))
    return pl.pallas_call(
        functools.partial(_logf_kernel, tc=tc),
        grid=(batch,),
        in_specs=[pl.BlockSpec((seq, LANES), lambda b: (b, 0)), pl.BlockSpec((1, LANES), lambda b: (0, 0))],
        out_specs=pl.BlockSpec((FOX_HEADS, seq, LANES), lambda b: (0, b, 0)),
        out_shape=jax.ShapeDtypeStruct((FOX_HEADS, batch * seq, LANES), BF16),
        compiler_params=_params(("parallel",)),
        name="fox_logf",
    )(misc, bias_row)


def _flash_scratch(tk, mcols):
    return [pltpu.VMEM((tk, mcols), F32), pltpu.VMEM((1, mcols), F32), pltpu.VMEM((1, mcols), F32),
            pltpu.VMEM((LANES, mcols), F32)]


def _causal_bias(tk, mcols, tq, offset):
    rk = lax.broadcasted_iota(jnp.int32, (tk, mcols), 0)
    rq = lax.broadcasted_iota(jnp.int32, (tk, mcols), 1) & (tq - 1)
    return jnp.where(rk <= rq + offset, 0.0, NEG_INF)


def _flash_tiles(qa, ka_ref, vt_ref, n_full, diag_bias_ref, tk, scratch, next_qa=None, late_cols=None):
    s_ref, m_ref, l_ref, acc_ref = scratch

    def logits(j, queries=qa):
        k0 = pl.multiple_of(j * tk, tk)
        return _dot_nt(ka_ref[pl.ds(k0, tk), :], queries)

    def step(s, k0, rows=tk, cols=slice(None)):
        m_prev = m_ref[:, cols]
        m_new = jnp.maximum(m_prev, jnp.max(s, axis=0, keepdims=True))
        alpha = jnp.exp2(m_prev - m_new)
        p = jnp.exp2(s - m_new)
        l_ref[:, cols] = alpha * l_ref[:, cols] + jnp.sum(p, axis=0, keepdims=True)
        acc_ref[:, cols] = alpha * acc_ref[:, cols] + _dot(vt_ref[:, pl.ds(k0, rows)], p.astype(BF16))
        m_ref[:, cols] = m_new

    def diag_step():
        k0 = pl.multiple_of(n_full * tk, tk)
        if late_cols is None:
            step(s_ref[...] + diag_bias_ref[...], k0)
            return
        half = tk // 2
        step(s_ref[0:half, :] + diag_bias_ref[0:half, :], k0, rows=half)
        for c0, c1 in late_cols:
            step(s_ref[half:tk, c0:c1] + diag_bias_ref[half:tk, c0:c1], k0 + half, rows=half, cols=slice(c0, c1))

    m_ref[...] = jnp.full(m_ref.shape, NEG_INF, F32)
    l_ref[...] = jnp.zeros(l_ref.shape, F32)
    acc_ref[...] = jnp.zeros(acc_ref.shape, F32)
    if next_qa is None:
        s_ref[...] = logits(0)

    def full(j, c):
        s_next = logits(j + 1)
        step(s_ref[...], pl.multiple_of(j * tk, tk))
        s_ref[...] = s_next
        return c

    lax.fori_loop(0, n_full, full, 0)
    if next_qa is None:
        diag_step()
    else:
        s_next = logits(0, next_qa)
        diag_step()
        s_ref[...] = s_next
    return acc_ref[...] * (1.0 / l_ref[...])


def _stage_kv(ka_ref, vt_ref, k_ref, kx_ref, v_ref, chunk, pad=0):
    def body(c, carry):
        r0 = pl.multiple_of(c * chunk, chunk)
        ka_ref[pl.ds(pad + r0, chunk), 0:LANES] = k_ref[0, pl.ds(r0, chunk), :]
        ka_ref[pl.ds(pad + r0, chunk), LANES:2 * LANES] = kx_ref[pl.ds(r0, chunk), :]
        vt_ref[:, pl.ds(pad + r0, chunk)] = v_ref[0, pl.ds(r0, chunk), :].astype(F32).T.astype(BF16)
        return carry

    lax.fori_loop(0, v_ref.shape[1] // chunk, body, 0)


def _fox_kernel(q_ref, k_ref, v_ref, cx_ref, o_ref, ka_ref, vt_ref, tri_ref, *flash, tq, tk):
    _stage_kv(ka_ref, vt_ref, k_ref, cx_ref.at[0], v_ref, tk)
    for o in range(tk // tq):
        tri_ref[o] = _causal_bias(tk, tq, tq, o * tq)
    lane = lax.broadcasted_iota(jnp.int32, (tq, LANES), 1)
    qx = jnp.where(lane < 3, -1.0, 0.0).astype(BF16)

    nq = q_ref.shape[1] // tq

    def queries(qi):
        q0 = pl.multiple_of(qi * tq, tq)
        return jnp.concatenate([q_ref[0, pl.ds(q0, tq), :], qx], axis=1)

    flash[0][...] = _dot_nt(ka_ref[0:tk, :], queries(0))

    def qstep(qi, c):
        o = _flash_tiles(queries(qi), ka_ref, vt_ref, (qi * tq) // tk, tri_ref.at[qi & (tk // tq - 1)], tk, flash,
                         next_qa=queries(jnp.minimum(qi + 1, nq - 1)),
                         late_cols=((tq // 2, tq),) if tq == tk else None)
        o_ref[pl.ds(pl.multiple_of(qi * tq, tq), tq), :] = o.T.astype(BF16)
        return c

    lax.fori_loop(0, nq, qstep, 0)


def _attn_tiles(seq):
    tq = _pick(seq, (512, 256, 128))
    tk = 2 * tq if seq % (2 * tq) == 0 else tq
    return tq, tk


def _fox(slabs, cx, batch, seq):
    tk = _attn_tiles(seq)[1]
    tq = tk
    m = batch * seq
    head = lambda base: pl.BlockSpec((1, seq, LANES), lambda b, h: (base + h, b, 0))
    return pl.pallas_call(
        functools.partial(_fox_kernel, tq=tq, tk=tk),
        grid=(batch, FOX_HEADS),
        in_specs=[head(SLAB_FQ), head(SLAB_FK), head(SLAB_FV), head(0)],
        out_specs=pl.BlockSpec((seq, LANES), lambda b, h: (b, h)),
        out_shape=jax.ShapeDtypeStruct((m, FOX_HEADS * HEAD_DIM), BF16),
        scratch_shapes=[pltpu.VMEM((seq, 2 * LANES), BF16), pltpu.VMEM((LANES, seq), BF16),
                        pltpu.VMEM((tk // tq, tk, tq), F32)] + _flash_scratch(tk, tq),
        compiler_params=_params(("parallel", "parallel")),
        name="fox_attn",
    )(slabs, slabs, slabs, cx)


def _diff_kernel(q_ref, k_ref, v_ref, kx_ref, qx_ref, lam_ref, sub_ref, o_ref, ka_ref, vt_ref, tri_ref, *flash,
                 t, tk, lam_init):
    _stage_kv(ka_ref, vt_ref, k_ref, kx_ref, v_ref, tk)
    for o in range(tk // t):
        tri_ref[o] = _causal_bias(tk, 2 * t, t, o * t)
    lane = lax.broadcasted_iota(jnp.int32, (t, LANES), 1)
    qx = jnp.broadcast_to(qx_ref[0], (t, LANES)).astype(BF16)
    lv = lam_ref[...]
    lam = (jnp.exp(jnp.sum(lv[0:1] * lv[1:2], axis=-1, keepdims=True))
           - jnp.exp(jnp.sum(lv[2:3] * lv[3:4], axis=-1, keepdims=True)) + lam_init)

    nq = q_ref.shape[1] // t

    def queries(qi):
        q = q_ref[0, pl.ds(pl.multiple_of(qi * t, t), t), :].astype(F32)
        return jnp.concatenate([
            jnp.concatenate([jnp.where(lane < DIFF_HALF, q, 0.0).astype(BF16), qx], axis=1),
            jnp.concatenate([jnp.where(lane >= DIFF_HALF, q, 0.0).astype(BF16), qx], axis=1)], axis=0)

    flash[0][...] = _dot_nt(ka_ref[0:tk, :], queries(0))

    def qstep(qi, c):
        o = _flash_tiles(queries(qi), ka_ref, vt_ref, (qi * t) // tk, tri_ref.at[qi & (tk // t - 1)], tk, flash,
                         next_qa=queries(jnp.minimum(qi + 1, nq - 1)),
                         late_cols=((t // 2, t), (t + t // 2, 2 * t)) if t == tk else None)
        o = (o[:, :t] - lam * o[:, t:]).T
        o_ref[pl.ds(pl.multiple_of(qi * t, t), t), :] = (_rms(o, sub_ref[...]) * (1.0 - lam_init)).astype(BF16)
        return c

    lax.fori_loop(0, nq, qstep, 0)


def _diff(slabs, kx, qx_diff, lam_vec, subln, batch, seq, lam_init):
    tk = _attn_tiles(seq)[1]
    t = tk
    m = batch * seq
    head = lambda base: pl.BlockSpec((1, seq, LANES), lambda b, h: (base + h, b, 0))
    return pl.pallas_call(
        functools.partial(_diff_kernel, t=t, tk=tk, lam_init=lam_init),
        grid=(batch, DIFF_HEADS),
        in_specs=[head(SLAB_DQ), head(SLAB_DK), head(SLAB_DV),
                  pl.BlockSpec((seq, LANES), lambda b, h: (0, 0)),
                  pl.BlockSpec((1, 1, LANES), lambda b, h: (h, 0, 0)),
                  pl.BlockSpec((4, DIFF_HALF), lambda b, h: (0, 0)),
                  pl.BlockSpec((1, LANES), lambda b, h: (0, 0))],
        out_specs=pl.BlockSpec((seq, LANES), lambda b, h: (b, h)),
        out_shape=jax.ShapeDtypeStruct((m, DIFF_HEADS * HEAD_DIM), BF16),
        scratch_shapes=[pltpu.VMEM((seq, 2 * LANES), BF16), pltpu.VMEM((LANES, seq), BF16),
                        pltpu.VMEM((tk // t, tk, 2 * t), F32)] + _flash_scratch(tk, 2 * t),
        compiler_params=_params(("parallel", "parallel")),
        name="diff_attn",
    )(slabs, slabs, slabs, kx, qx_diff, lam_vec, subln)


def _compress_kernel(y_ref, pos_ref, w1_ref, w2_ref, o_ref):
    half = y_ref.shape[3]
    y = y_ref[0, 0].astype(F32)
    top = (y + pos_ref[0, 0:1, :]).astype(BF16)
    bot = (y + pos_ref[0, 1:2, :]).astype(BF16)
    a = _dot(top, w1_ref[0, 0:half, :])
    b = _dot(bot, w1_ref[0, half:2 * half, :])
    nrow = a.shape[0]
    hid = a + pltpu.roll(b, nrow - 1, 0)
    hid = hid * jax.nn.sigmoid(hid)
    o_ref[0, 0] = _dot(hid.astype(BF16), w2_ref[0]).astype(BF16)


def _compress(ykv, pos2, w1, w2, batch, l):
    nrow, half = ykv.shape[2], ykv.shape[3]
    return pl.pallas_call(
        _compress_kernel,
        grid=(4, batch),
        in_specs=[pl.BlockSpec((1, 1, nrow, half), lambda s, b: (s, b, 0, 0)),
                  pl.BlockSpec((None, 1, 2, half), lambda s, b: (l, s // 2, 0, 0)),
                  pl.BlockSpec((None, 1, 2 * half, CMP_HIDDEN), lambda s, b: (l, s // 2, 0, 0)),
                  pl.BlockSpec((None, 1, CMP_HIDDEN, HEAD_DIM), lambda s, b: (l, s // 2, 0, 0))],
        out_specs=pl.BlockSpec((1, 1, nrow, HEAD_DIM), lambda s, b: (s, b, 0, 0)),
        out_shape=jax.ShapeDtypeStruct((4, batch, nrow, HEAD_DIM), BF16),
        compiler_params=_params(("parallel", "parallel")),
        name="nsa_compress",
    )(ykv, pos2, w1, w2)


def _topk_bias(cand, topk):
    nsel, tq = cand.shape
    sub = 8
    blocks = [cand[b * sub:(b + 1) * sub] for b in range(nsel // sub)]
    jidx = lax.broadcasted_iota(jnp.int32, (sub, tq), 0)
    ranks = [jnp.zeros((sub, tq), F32) for _ in blocks]
    for k in range(nsel):
        rk = cand[k:k + 1, :]
        for b, cb in enumerate(blocks):
            if b * sub > k:
                beats = rk >= cb
            elif b * sub + sub - 1 < k:
                beats = rk > cb
            else:
                beats = (rk > cb) | ((rk == cb) & (jidx + b * sub > k))
            ranks[b] = ranks[b] + jnp.where(beats, 1.0, 0.0)
    rank = jnp.concatenate(ranks, axis=0)
    return jnp.where(rank < topk, 0.0, SEL_MASK)


def _nsa_kernel(q_ref, kc_ref, vc_ref, ks_ref, vs_ref, kw_ref, vw_ref, kx_ref, cx_ref, ovt_ref, qx_ref,
                misc_ref, o_ref, ksa_ref, kwa_ref, vst_ref, vwt_ref, vct_ref, gt_ref, dbias_ref, wbias_ref,
                comb_ref, *flash, tq, tk):
    g = pl.program_id(1)
    qi = pl.program_id(2)
    mcols = NSA_REP * tq
    nc = kc_ref.shape[2]
    seq = kx_ref.shape[0]
    nsel = seq // SEL_BLOCK
    span = WINDOW + tq

    @pl.when(qi == 0)
    def _():
        _stage_kv(ksa_ref, vst_ref, ks_ref, kx_ref, vs_ref, tk)
        _stage_kv(kwa_ref, vwt_ref, kw_ref, kx_ref, vw_ref, tk, pad=WINDOW)
        pad_lane = lax.broadcasted_iota(jnp.int32, (WINDOW, 2 * LANES), 1)
        kwa_ref[0:WINDOW, :] = jnp.where(pad_lane == LANES + AUG_PAD, SEL_MASK, 0.0).astype(BF16)
        vwt_ref[:, 0:WINDOW] = jnp.zeros((LANES, WINDOW), BF16)
        vct_ref[...] = vc_ref[0, 0].astype(F32).T.astype(BF16)
        for o in range(tk // tq):
            dbias_ref[o] = _causal_bias(tk, mcols, tq, o * tq)
        rk = lax.broadcasted_iota(jnp.int32, (span, mcols), 0)
        rq = lax.broadcasted_iota(jnp.int32, (span, mcols), 1) & (tq - 1)
        wbias_ref[...] = jnp.where((rk > rq) & (rk <= rq + WINDOW), 0.0, NEG_INF)

    qx = jnp.concatenate([jnp.broadcast_to(qx_ref[0, r:r + 1, :], (tq, LANES)) for r in range(NSA_REP)], axis=0)
    kca = jnp.concatenate([kc_ref[0, 0], cx_ref[...]], axis=1)
    _nsa_query_tile(qi, g, qx, kca, q_ref, ovt_ref, misc_ref, o_ref, ksa_ref, kwa_ref, vst_ref, vwt_ref,
                    vct_ref, gt_ref, dbias_ref, wbias_ref, comb_ref, flash, tq=tq, tk=tk, nc=nc, nsel=nsel)


def _nsa_query_tile(qi, g, qx, kca, q_ref, ovt_ref, misc_ref, o_ref, ksa_ref, kwa_ref, vst_ref, vwt_ref, vct_ref,
                    gt_ref, dbias_ref, wbias_ref, comb_ref, flash, *, tq, tk, nc, nsel):
    mcols = NSA_REP * tq
    span = WINDOW + tq
    q0 = pl.multiple_of(qi * tq, tq)
    q4 = q_ref[...].reshape(mcols, LANES)
    qa = jnp.concatenate([q4, qx.astype(BF16)], axis=1)
    col = lax.broadcasted_iota(jnp.int32, (1, mcols), 1)
    col_pos = q0 + (col & (tq - 1))

    sc = _dot_nt(kca, qa)
    cend = lax.broadcasted_iota(jnp.int32, (nc, mcols), 0) * CMP_STRIDE + (CMP_BLOCK - 1)
    sc = jnp.where(cend <= col_pos, sc, NEG_INF)
    e = jnp.exp2(sc - jnp.max(sc, axis=0, keepdims=True))
    inv = jnp.where(col_pos >= CMP_BLOCK - 1, 1.0 / jnp.sum(e, axis=0, keepdims=True), 0.0)
    p = e * inv
    o_cmp = _dot(vct_ref[...], p.astype(BF16))

    psum = p[:, 0:tq]
    for r in range(1, NSA_REP):
        psum = psum + p[:, r * tq:(r + 1) * tq]
    p_hi = psum.astype(BF16)
    p_lo = (psum - p_hi.astype(F32)).astype(BF16)
    imp = _dot(ovt_ref[...], p_hi) + _dot(ovt_ref[...], p_lo)
    qpos = q0 + lax.broadcasted_iota(jnp.int32, (LANES, tq), 1)
    blk = lax.broadcasted_iota(jnp.int32, (LANES, tq), 0)
    cur = jnp.right_shift(qpos, SEL_BLOCK.bit_length() - 1)
    forced = (blk == 0) | (blk == cur) | (blk == cur - 1)
    imp = jnp.where(blk <= cur, jnp.where(forced, FORCE_SCORE, imp), -1.0)
    bias_t = _topk_bias(imp[0:nsel], min(SEL_TOPK, nsel))
    if nsel < LANES:
        bias_t = jnp.concatenate([bias_t, jnp.zeros((LANES - nsel, tq), F32)], axis=0)
    selbias = jnp.concatenate([bias_t.T] * NSA_REP, axis=0)
    lane4 = lax.broadcasted_iota(jnp.int32, (mcols, LANES), 1)
    qa_sel = jnp.concatenate([q4, jnp.where(lane4 < SEL_BLOCK, selbias, qx).astype(BF16)], axis=1)

    sw = _dot_nt(kwa_ref[pl.ds(q0, span), :], qa) + wbias_ref[...]
    ew = jnp.exp2(sw - jnp.max(sw, axis=0, keepdims=True))
    o_win = _dot(vwt_ref[:, pl.ds(q0, span)], ew.astype(BF16)) * (1.0 / jnp.sum(ew, axis=0, keepdims=True))

    gt_ref[...] = jax.nn.sigmoid(misc_ref[...]).T
    gate = lambda r, i: gt_ref[pl.ds(MISC_NG + 3 * (NSA_REP * g + r) + i, 1), :]
    for r in range(NSA_REP):
        sl = slice(r * tq, (r + 1) * tq)
        comb_ref[:, sl] = gate(r, 0) * o_cmp[:, sl] + gate(r, 2) * o_win[:, sl]

    late = tuple((r * tq + tq // 2, (r + 1) * tq) for r in range(NSA_REP)) if tq == tk else None
    o_sel = _flash_tiles(qa_sel, ksa_ref, vst_ref, q0 // tk, dbias_ref.at[qi & (tk // tq - 1)], tk, flash,
                         late_cols=late)
    for r in range(NSA_REP):
        sl = slice(r * tq, (r + 1) * tq)
        out = comb_ref[:, sl] + gate(r, 1) * o_sel[:, sl]
        o_ref[:, r * LANES:(r + 1) * LANES] = out.T.astype(BF16)


def _nsa(slabs, kvc, kx, cx_cmp, ovt, qx_nsa, misc, batch, seq):
    tq = _pick(seq, (512, 256, 128))
    tk = _pick(seq, (512, 256, 128))
    assert tq & (tq - 1) == 0 and tk % tq == 0 and seq >= WINDOW + tq
    nq = seq // tq
    m = batch * seq
    mcols = NSA_REP * tq
    nc = kvc.shape[2]
    kv_spec = lambda base: pl.BlockSpec((1, seq, LANES), lambda b, g, i: (base + g, b, 0))
    const = lambda shape: pl.BlockSpec(shape, lambda b, g, i: (0,) * len(shape), pipeline_mode=pl.Buffered(1))
    return pl.pallas_call(
        functools.partial(_nsa_kernel, tq=tq, tk=tk),
        grid=(batch, NSA_KV_HEADS, nq),
        in_specs=[pl.BlockSpec((NSA_REP, tq, LANES), lambda b, g, i: (SLAB_NQ // NSA_REP + g, b * nq + i, 0)),
                  pl.BlockSpec((1, 1, nc, LANES), lambda b, g, i: (g, b, 0, 0)),
                  pl.BlockSpec((1, 1, nc, LANES), lambda b, g, i: (2 + g, b, 0, 0)),
                  kv_spec(SLAB_NKS), kv_spec(SLAB_NVS), kv_spec(SLAB_NKW), kv_spec(SLAB_NVW),
                  const((seq, LANES)), const((nc, LANES)), const((LANES, nc)),
                  pl.BlockSpec((1, NSA_REP, LANES), lambda b, g, i: (g, 0, 0)),
                  pl.BlockSpec((tq, LANES), lambda b, g, i: (b * nq + i, 0))],
        out_specs=pl.BlockSpec((tq, NSA_REP * LANES), lambda b, g, i: (b * nq + i, g)),
        out_shape=jax.ShapeDtypeStruct((m, NSA_HEADS * HEAD_DIM), BF16),
        scratch_shapes=[pltpu.VMEM((seq, 2 * LANES), BF16), pltpu.VMEM((WINDOW + seq, 2 * LANES), BF16),
                        pltpu.VMEM((LANES, seq), BF16), pltpu.VMEM((LANES, WINDOW + seq), BF16),
                        pltpu.VMEM((LANES, nc), BF16), pltpu.VMEM((LANES, tq), F32),
                        pltpu.VMEM((tk // tq, tk, mcols), F32), pltpu.VMEM((WINDOW + tq, mcols), F32),
                        pltpu.VMEM((LANES, mcols), F32)] + _flash_scratch(tk, mcols),
        compiler_params=_params(("parallel", "parallel", "arbitrary")),
        name="nsa_attn",
    )(slabs, kvc, kvc, slabs, slabs, slabs, slabs, kx, cx_cmp, ovt, qx_nsa, misc)


def _merge_kernel(h_ref, of_ref, od_ref, on_ref, wg0_ref, wg1_ref, wg2_ref, wf_ref, wd_ref, wn_ref, o_ref):
    h = h_ref[...]
    acc = jax.nn.sigmoid(_dot(h, wg0_ref[...])) * _dot(of_ref[...], wf_ref[...])
    acc = acc + jax.nn.sigmoid(_dot(h, wg1_ref[...])) * _dot(od_ref[...], wd_ref[...])
    acc = acc + jax.nn.sigmoid(_dot(h, wg2_ref[...])) * _dot(on_ref[...], wn_ref[...])
    o_ref[...] = acc.astype(BF16)


def _merge(h, o_fox, o_diff, o_nsa, w_gate, wb_fox, wb_diff, wb_nsa, l):
    m, d = h.shape
    tm = _pick(m, (1024, 512, 256))
    tn = _pick(d, (512, 256, 128))
    nj = d // tn
    row = lambda width: pl.BlockSpec((tm, width), lambda i, j: (i, 0))
    gate = lambda t: pl.BlockSpec((None, d, tn), lambda i, j: (l, 0, t * nj + j))
    col = lambda k: pl.BlockSpec((None, k, tn), lambda i, j: (l, 0, j))
    return pl.pallas_call(
        _merge_kernel,
        grid=(m // tm, nj),
        in_specs=[row(d), row(o_fox.shape[1]), row(o_diff.shape[1]), row(o_nsa.shape[1]),
                  gate(0), gate(1), gate(2),
                  col(wb_fox.shape[1]), col(wb_diff.shape[1]), col(wb_nsa.shape[1])],
        out_specs=pl.BlockSpec((tm, tn), lambda i, j: (i, j)),
        out_shape=jax.ShapeDtypeStruct((m, d), BF16),
        compiler_params=_params(("parallel", "arbitrary")),
        name="gate_merge",
    )(h, o_fox, o_diff, o_nsa, w_gate, w_gate, w_gate, wb_fox, wb_diff, wb_nsa)


def _wout_kernel(a_ref, w_ref, x_ref, gp_ref, gn_ref, xo_ref, ho_ref):
    sub = min(a_ref.shape[0], 256)
    for s in range(a_ref.shape[0] // sub):
        rows = slice(s * sub, (s + 1) * sub)
        y = _dot(a_ref[rows, :], w_ref[...])
        x_new = x_ref[rows, :] + _rms(y, gp_ref[...])
        xo_ref[rows, :] = x_new
        ho_ref[rows, :] = _rms(x_new, gn_ref[...]).astype(BF16)


def _wout(a, w, x, g_post, g_next, l, name):
    m, d = x.shape
    k = a.shape[1]
    tm = _pick(m, (512, 256, 128)) if k * d * 2 <= 8 * 1024 * 1024 else _pick(m, (256, 128))
    row = lambda width: pl.BlockSpec((tm, width), lambda i: (i, 0))
    vec = lambda: pl.BlockSpec((1, d), lambda i: (0, 0))
    return pl.pallas_call(
        _wout_kernel,
        grid=(m // tm,),
        in_specs=[row(k), pl.BlockSpec((None, k, d), lambda i: (l, 0, 0), pipeline_mode=pl.Buffered(1)),
                  row(d), vec(), vec()],
        out_specs=[row(d), row(d)],
        out_shape=[jax.ShapeDtypeStruct((m, d), F32), jax.ShapeDtypeStruct((m, d), BF16)],
        compiler_params=_params(("parallel",)),
        name=name,
    )(a, w, x, g_post, g_next)


def _ffn_up_kernel(h_ref, wg_ref, wu_ref, o_ref, wgb_ref, wub_ref):
    @pl.when(pl.program_id(1) == 0)
    def _():
        wgb_ref[...] = wg_ref[...].astype(BF16)
        wub_ref[...] = wu_ref[...].astype(BF16)

    h = h_ref[...]
    gate = _dot(h, wgb_ref[...])
    o_ref[...] = (gate * jax.nn.sigmoid(gate) * _dot(h, wub_ref[...])).astype(BF16)


def _ffn_up(h, w_up, l):
    m, d = h.shape
    dff = w_up.shape[2] // 2
    tm = _pick(m, (1024, 512, 256))
    tn = _pick(dff, (512, 256, 128))
    nj = dff // tn
    return pl.pallas_call(
        _ffn_up_kernel,
        grid=(nj, m // tm),
        in_specs=[pl.BlockSpec((tm, d), lambda j, i: (i, 0)),
                  pl.BlockSpec((None, d, tn), lambda j, i: (l, 0, j)),
                  pl.BlockSpec((None, d, tn), lambda j, i: (l, 0, nj + j))],
        out_specs=pl.BlockSpec((tm, tn), lambda j, i: (i, j)),
        out_shape=jax.ShapeDtypeStruct((m, dff), BF16),
        scratch_shapes=[pltpu.VMEM((d, tn), BF16), pltpu.VMEM((d, tn), BF16)],
        compiler_params=_params(("parallel", "arbitrary")),
        name="ffn_up",
    )(h, w_up, w_up)


def _pos_columns(pos):
    lane = jnp.arange(LANES)[None, :]
    hi = (lane >= AUG_HI) & (lane < AUG_HI + AUG_TERMS)
    lo = (lane >= AUG_LO) & (lane < AUG_LO + AUG_TERMS)
    return jnp.where(hi, (pos // LANES)[:, None], jnp.where(lo, (pos % LANES)[:, None], 0))


def _key_aug_table(seq):
    j = jnp.arange(seq)
    lane = jnp.arange(LANES)[None, :]
    onehot = (lane == (j // SEL_BLOCK)[:, None]) & (lane < SEL_BLOCK)
    return (_pos_columns(j) + onehot.astype(jnp.int32)).astype(BF16)


def _cmp_aug_table(nrow):
    return _pos_columns(jnp.arange(nrow) * CMP_STRIDE + CMP_BLOCK - 1).astype(BF16)


def _overlap_table_t(nrow, seq):
    start = jnp.arange(nrow)[None, :] * CMP_STRIDE
    blk = jnp.arange(LANES)[:, None]
    sel = blk * SEL_BLOCK
    ov = ((start < sel + SEL_BLOCK) & (start + CMP_BLOCK - 1 >= sel) & (blk < seq // SEL_BLOCK)
          & (jnp.arange(nrow)[None, :] < (seq - CMP_BLOCK) // CMP_STRIDE + 1))
    return ov.astype(BF16)


def _query_aug_rows(n_heads):
    slopes = 2.0 ** (-8.0 * jnp.arange(1, n_heads + 1, dtype=F32) / n_heads)
    terms = [t.astype(F32) for t in _split3(jnp.float32(LOG2E))]
    lane = jnp.arange(LANES)[None, :]
    out = jnp.where(lane == AUG_PAD, 1.0, jnp.zeros((n_heads, LANES), F32))
    for i, t in enumerate(terms):
        out = jnp.where(lane == AUG_HI + i, slopes[:, None] * t * LANES, out)
        out = jnp.where(lane == AUG_LO + i, slopes[:, None] * t, out)
    return out


def kernel(x, w_in, fox_forget_bias, diff_lambda, diff_subln, nsa_cmp_pos, nsa_cmp_w1, nsa_cmp_w2,
           w_branch_fox, w_branch_diff, w_branch_nsa, w_gate, w_out, norm_gains, w_ffn_up, w_ffn_down):
    batch, seq, d = x.shape
    depth = w_in.shape[0]
    m = batch * seq
    fw, dw, nw, kvw = FOX_HEADS * HEAD_DIM, DIFF_HEADS * HEAD_DIM, NSA_HEADS * HEAD_DIM, NSA_KV_HEADS * HEAD_DIM
    ff0 = 3 * fw
    dq0 = ff0 + FOX_HEADS
    ng0 = dq0 + 3 * dw + nw + 6 * kvw
    n_main = ng0 - FOX_HEADS
    assert n_main == N_SLABS * LANES and w_in.shape[2] == ng0 + 3 * NSA_HEADS

    w_main = jnp.concatenate([w_in[:, :, :ff0], w_in[:, :, dq0:ng0]], axis=2).astype(BF16)
    w_misc = jnp.concatenate([w_in[:, :, ff0:dq0], w_in[:, :, ng0:],
                              jnp.zeros((depth, d, LANES - FOX_HEADS - 3 * NSA_HEADS), F32)], axis=2).astype(BF16)
    colscale = jnp.ones((n_main,), F32)
    colscale = colscale.at[SLAB_FQ * LANES:SLAB_FK * LANES].set(HEAD_DIM ** -0.5 * LOG2E)
    colscale = colscale.at[SLAB_DQ * LANES:SLAB_DK * LANES].set(DIFF_HALF ** -0.5 * LOG2E)
    colscale = colscale.at[SLAB_NQ * LANES:SLAB_NKC * LANES].set(HEAD_DIM ** -0.5 * LOG2E)
    colscale = colscale[None, :]
    fbias = jnp.pad(fox_forget_bias.astype(F32), ((0, 0), (0, LANES - FOX_HEADS)))[:, None, :]
    half = CMP_STRIDE * HEAD_DIM
    pos2 = nsa_cmp_pos.astype(F32).reshape(depth, 2, 2, half)
    w1 = nsa_cmp_w1.astype(BF16)
    w2 = nsa_cmp_w2.astype(BF16)
    wbf, wbd, wbn = w_branch_fox.astype(BF16), w_branch_diff.astype(BF16), w_branch_nsa.astype(BF16)
    wg, wo = w_gate.astype(BF16), w_out.astype(BF16)
    wup, wdn = w_ffn_up.astype(F32), w_ffn_down.astype(BF16)
    gains = norm_gains.astype(F32)

    nrow = seq // CMP_STRIDE
    kx = _key_aug_table(seq)
    cx_cmp = _cmp_aug_table(nrow)
    ovt = _overlap_table_t(nrow, seq)
    qx_diff = _query_aug_rows(DIFF_HEADS)[:, None, :]
    qx_nsa = _query_aug_rows(NSA_HEADS).reshape(NSA_KV_HEADS, NSA_REP, LANES)

    xf = x.reshape(m, d).astype(F32)
    h = _norm(xf, gains[0, 0][None, :])
    for l in range(depth):
        lam_init = 0.8 - 0.6 * math.exp(-0.3 * l)
        slabs, misc = _inproj(h, w_main, colscale, w_misc, l)
        cx_fox = _logf(misc, fbias[l], batch, seq)
        o_fox = _fox(slabs, cx_fox, batch, seq)
        o_diff = _diff(slabs, kx, qx_diff, diff_lambda[l].astype(F32), diff_subln[l].astype(F32)[None, :],
                       batch, seq, lam_init)
        ykv = slabs[SLAB_NKC:SLAB_NKC + 4].reshape(4, batch, nrow, half)
        kvc = _compress(ykv, pos2, w1, w2, batch, l)
        o_nsa = _nsa(slabs, kvc, kx, cx_cmp, ovt, qx_nsa, misc, batch, seq)
        merged = _merge(h, o_fox, o_diff, o_nsa, wg, wbf, wbd, wbn, l)
        xf, h2 = _wout(merged, wo, xf, gains[l, 1][None, :], gains[l, 2][None, :], l, "out_proj")
        act = _ffn_up(h2, wup, l)
        g_next = gains[min(l + 1, depth - 1), 0][None, :]
        xf, h = _wout(act, wdn, xf, gains[l, 3][None, :], g_next, l, "ffn_down")
    return xf.reshape(batch, seq, d).astype(x.dtype)
```

```python
import functools
import math

import jax
import jax.numpy as jnp
from jax import lax
from jax.experimental import pallas as pl
from jax.experimental.pallas import tpu as pltpu

F32 = jnp.float32
BF16 = jnp.bfloat16

HEAD_DIM = 128
FOX_HEADS = 4
DIFF_HEADS = 4
DIFF_HALF = HEAD_DIM // 2
NSA_HEADS = 8
NSA_KV_HEADS = 2
NSA_REP = NSA_HEADS // NSA_KV_HEADS
CMP_BLOCK = 32
CMP_STRIDE = 16
CMP_HIDDEN = 256
SEL_BLOCK = 64
SEL_TOPK = 16
WINDOW = 512
N_BRANCHES = 3
EPS = 1e-6
NEG_INF = -1e30
FORCE_SCORE = 1e4
SEL_MASK = NEG_INF
LOG2E = math.log2(math.e)
LANES = 128

SLAB_FQ, SLAB_FK, SLAB_FV = 0, 4, 8
SLAB_DQ, SLAB_DK, SLAB_DV = 12, 16, 20
SLAB_NQ = 24
SLAB_NKC, SLAB_NVC, SLAB_NKS, SLAB_NVS, SLAB_NKW, SLAB_NVW = 32, 34, 36, 38, 40, 42
N_SLABS = 44
MISC_FF = 0
MISC_NG = 4
AUG_HI = 64
AUG_LO = 67
AUG_TERMS = 3
AUG_PAD = 70

VMEM_LIMIT = 56 * 1024 * 1024
MXU_ROWS = 256
SMALL_WEIGHT_BYTES = 8 * 1024 * 1024


def _pick(n, prefs):
    for p in prefs:
        if p <= n and n % p == 0:
            return p
    return n


def _params(sem):
    return pltpu.CompilerParams(dimension_semantics=sem, vmem_limit_bytes=VMEM_LIMIT)


def _rms(y, g):
    return y * lax.rsqrt(jnp.mean(y * y, axis=-1, keepdims=True) + EPS) * g


def _dot(a, b):
    return jnp.dot(a, b, preferred_element_type=F32)


def _dot_nt(a, b):
    return lax.dot_general(a, b, (((1,), (1,)), ((), ())), preferred_element_type=F32)


def _split3(x):
    hi = x.astype(BF16)
    r = x - hi.astype(F32)
    mid = r.astype(BF16)
    lo = (r - mid.astype(F32)).astype(BF16)
    return hi, mid, lo


def _norm_kernel(x_ref, g_ref, h_ref):
    h_ref[...] = _rms(x_ref[...], g_ref[...]).astype(BF16)


def _norm(x, g):
    m, d = x.shape
    tm = _pick(m, (512, 256, 128))
    return pl.pallas_call(
        _norm_kernel,
        grid=(m // tm,),
        in_specs=[pl.BlockSpec((tm, d), lambda i: (i, 0)), pl.BlockSpec((1, d), lambda i: (0, 0))],
        out_specs=pl.BlockSpec((tm, d), lambda i: (i, 0)),
        out_shape=jax.ShapeDtypeStruct((m, d), BF16),
        compiler_params=_params(("parallel",)),
        name="norm_in",
    )(x, g)


def _inproj_kernel(h_ref, w_ref, cs_ref, wm_ref, o_ref, misc_ref):
    h = h_ref[...]
    acc = _dot(h, w_ref[...]) * cs_ref[...]
    for s in range(o_ref.shape[0]):
        o_ref[s] = acc[:, s * LANES:(s + 1) * LANES].astype(BF16)

    @pl.when(pl.program_id(1) == 0)
    def _():
        misc_ref[...] = _dot(h, wm_ref[...])


def _inproj(h, w_main, colscale, w_misc, l):
    m, d = h.shape
    n = w_main.shape[2]
    tm = _pick(m, (1024, 512, 256))
    tn = _pick(n, (N_SLABS // 4 * LANES, 4 * LANES))
    return pl.pallas_call(
        _inproj_kernel,
        grid=(m // tm, n // tn),
        in_specs=[pl.BlockSpec((tm, d), lambda i, j: (i, 0)),
                  pl.BlockSpec((None, d, tn), lambda i, j: (l, 0, j)),
                  pl.BlockSpec((1, tn), lambda i, j: (0, j)),
                  pl.BlockSpec((None, d, LANES), lambda i, j: (l, 0, 0))],
        out_specs=[pl.BlockSpec((tn // LANES, tm, LANES), lambda i, j: (j, i, 0)),
                   pl.BlockSpec((tm, LANES), lambda i, j: (i, 0))],
        out_shape=[jax.ShapeDtypeStruct((n // LANES, m, LANES), BF16),
                   jax.ShapeDtypeStruct((m, LANES), F32)],
        compiler_params=_params(("parallel", "arbitrary")),
        name="inproj",
    )(h, w_main, colscale, w_misc)


def _logf_kernel(misc_ref, bias_ref, o_ref, *, tc):
    s = misc_ref.shape[0]
    row = lax.broadcasted_iota(jnp.int32, (tc, tc), 0)
    col = lax.broadcasted_iota(jnp.int32, (tc, tc), 1)
    tri = jnp.where(col <= row, 1.0, 0.0).astype(BF16)
    lane = lax.broadcasted_iota(jnp.int32, (tc, LANES), 1)

    def chunk(c, carry):
        r0 = pl.multiple_of(c * tc, tc)
        z = misc_ref[pl.ds(r0, tc), :] + bias_ref[...]
        lf = jnp.minimum(z, 0.0) - jnp.log1p(jnp.exp(-jnp.abs(z)))
        hi, mid, lo = _split3(lf)
        cum = _dot(tri, hi) + _dot(tri, mid) + _dot(tri, lo) + carry
        for hd in range(FOX_HEADS):
            c2 = jnp.broadcast_to(cum[:, hd:hd + 1], (tc, LANES)) * LOG2E
            c_hi = c2.astype(BF16).astype(F32)
            c_mid = (c2 - c_hi).astype(BF16).astype(F32)
            aug = jnp.where(lane == 0, c_hi, jnp.where(lane == 1, c_mid,
                                                      jnp.where(lane == 2, c2 - c_hi - c_mid, 0.0)))
            o_ref[hd, pl.ds(r0, tc), :] = aug.astype(BF16)
        return cum[tc - 1:tc, :]

    lax.fori_loop(0, s // tc, chunk, jnp.zeros((1, LANES), F32))


def _logf(misc, bias_row, batch, seq):
    tc = _pick(seq, (256, 128))
    return pl.pallas_call(
        functools.partial(_logf_kernel, tc=tc),
        grid=(batch,),
        in_specs=[pl.BlockSpec((seq, LANES), lambda b: (b, 0)), pl.BlockSpec((1, LANES), lambda b: (0, 0))],
        out_specs=pl.BlockSpec((FOX_HEADS, seq, LANES), lambda b: (0, b, 0)),
        out_shape=jax.ShapeDtypeStruct((FOX_HEADS, batch * seq, LANES), BF16),
        compiler_params=_params(("parallel",)),
        name="fox_logf",
    )(misc, bias_row)


def _flash_scratch(tk, mcols):
    return [pltpu.VMEM((tk, mcols), F32), pltpu.VMEM((1, mcols), F32), pltpu.VMEM((1, mcols), F32),
            pltpu.VMEM((LANES, mcols), F32)]


def _causal_bias(tk, mcols, tq, offset):
    rk = lax.broadcasted_iota(jnp.int32, (tk, mcols), 0)
    rq = lax.broadcasted_iota(jnp.int32, (tk, mcols), 1) & (tq - 1)
    return jnp.where(rk <= rq + offset, 0.0, NEG_INF)


def _flash_tiles(qa, ka_ref, vt_ref, n_full, diag_bias_ref, tk, scratch, next_qa=None, late_cols=None):
    s_ref, m_ref, l_ref, acc_ref = scratch

    def logits(j, queries=qa):
        k0 = pl.multiple_of(j * tk, tk)
        return _dot_nt(ka_ref[pl.ds(k0, tk), :], queries)

    def step(s, k0, rows=tk, cols=slice(None)):
        m_prev = m_ref[:, cols]
        m_new = jnp.maximum(m_prev, jnp.max(s, axis=0, keepdims=True))
        alpha = jnp.exp2(m_prev - m_new)
        p = jnp.exp2(s - m_new)
        l_ref[:, cols] = alpha * l_ref[:, cols] + jnp.sum(p, axis=0, keepdims=True)
        acc_ref[:, cols] = alpha * acc_ref[:, cols] + _dot(vt_ref[:, pl.ds(k0, rows)], p.astype(BF16))
        m_ref[:, cols] = m_new

    def diag_step():
        k0 = pl.multiple_of(n_full * tk, tk)
        if late_cols is None:
            step(s_ref[...] + diag_bias_ref[...], k0)
            return
        half = tk // 2
        step(s_ref[0:half, :] + diag_bias_ref[0:half, :], k0, rows=half)
        for c0, c1 in late_cols:
            step(s_ref[half:tk, c0:c1] + diag_bias_ref[half:tk, c0:c1], k0 + half, rows=half, cols=slice(c0, c1))

    m_ref[...] = jnp.full(m_ref.shape, NEG_INF, F32)
    l_ref[...] = jnp.zeros(l_ref.shape, F32)
    acc_ref[...] = jnp.zeros(acc_ref.shape, F32)
    if next_qa is None:
        s_ref[...] = logits(0)

    def full(j, c):
        s_next = logits(j + 1)
        step(s_ref[...], pl.multiple_of(j * tk, tk))
        s_ref[...] = s_next
        return c

    lax.fori_loop(0, n_full, full, 0)
    if next_qa is None:
        diag_step()
    else:
        s_next = logits(0, next_qa)
        diag_step()
        s_ref[...] = s_next
    return acc_ref[...] * (1.0 / l_ref[...])


def _stage_kv(ka_ref, vt_ref, k_ref, kx_ref, v_ref, chunk, pad=0):
    def body(c, carry):
        r0 = pl.multiple_of(c * chunk, chunk)
        ka_ref[pl.ds(pad + r0, chunk), 0:LANES] = k_ref[0, pl.ds(r0, chunk), :]
        ka_ref[pl.ds(pad + r0, chunk), LANES:2 * LANES] = kx_ref[pl.ds(r0, chunk), :]
        vt_ref[:, pl.ds(pad + r0, chunk)] = v_ref[0, pl.ds(r0, chunk), :].astype(F32).T.astype(BF16)
        return carry

    lax.fori_loop(0, v_ref.shape[1] // chunk, body, 0)


def _fox_kernel(q_ref, k_ref, v_ref, cx_ref, o_ref, ka_ref, vt_ref, tri_ref, *flash, tq, tk):
    _stage_kv(ka_ref, vt_ref, k_ref, cx_ref.at[0], v_ref, tk)
    for o in range(tk // tq):
        tri_ref[o] = _causal_bias(tk, tq, tq, o * tq)
    lane = lax.broadcasted_iota(jnp.int32, (tq, LANES), 1)
    qx = jnp.where(lane < 3, -1.0, 0.0).astype(BF16)

    nq = q_ref.shape[1] // tq

    def queries(qi):
        q0 = pl.multiple_of(qi * tq, tq)
        return jnp.concatenate([q_ref[0, pl.ds(q0, tq), :], qx], axis=1)

    flash[0][...] = _dot_nt(ka_ref[0:tk, :], queries(0))

    def qstep(qi, c):
        o = _flash_tiles(queries(qi), ka_ref, vt_ref, (qi * tq) // tk, tri_ref.at[qi & (tk // tq - 1)], tk, flash,
                         next_qa=queries(jnp.minimum(qi + 1, nq - 1)),
                         late_cols=((tq // 2, tq),) if tq == tk else None)
        o_ref[pl.ds(pl.multiple_of(qi * tq, tq), tq), :] = o.T.astype(BF16)
        return c

    lax.fori_loop(0, nq, qstep, 0)


def _attn_tiles(seq):
    tq = _pick(seq, (512, 256, 128))
    tk = 2 * tq if seq % (2 * tq) == 0 else tq
    return tq, tk


def _fox(slabs, cx, batch, seq):
    tk = _attn_tiles(seq)[1]
    tq = tk
    m = batch * seq
    head = lambda base: pl.BlockSpec((1, seq, LANES), lambda b, h: (base + h, b, 0))
    return pl.pallas_call(
        functools.partial(_fox_kernel, tq=tq, tk=tk),
        grid=(batch, FOX_HEADS),
        in_specs=[head(SLAB_FQ), head(SLAB_FK), head(SLAB_FV), head(0)],
        out_specs=pl.BlockSpec((seq, LANES), lambda b, h: (b, h)),
        out_shape=jax.ShapeDtypeStruct((m, FOX_HEADS * HEAD_DIM), BF16),
        scratch_shapes=[pltpu.VMEM((seq, 2 * LANES), BF16), pltpu.VMEM((LANES, seq), BF16),
                        pltpu.VMEM((tk // tq, tk, tq), F32)] + _flash_scratch(tk, tq),
        compiler_params=_params(("parallel", "parallel")),
        name="fox_attn",
    )(slabs, slabs, slabs, cx)


def _diff_kernel(q_ref, k_ref, v_ref, kx_ref, qx_ref, lam_ref, sub_ref, o_ref, ka_ref, vt_ref, tri_ref, *flash,
                 t, tk, lam_init):
    _stage_kv(ka_ref, vt_ref, k_ref, kx_ref, v_ref, tk)
    for o in range(tk // t):
        tri_ref[o] = _causal_bias(tk, 2 * t, t, o * t)
    lane = lax.broadcasted_iota(jnp.int32, (t, LANES), 1)
    qx = jnp.broadcast_to(qx_ref[0], (t, LANES)).astype(BF16)
    lv = lam_ref[...]
    lam = (jnp.exp(jnp.sum(lv[0:1] * lv[1:2], axis=-1, keepdims=True))
           - jnp.exp(jnp.sum(lv[2:3] * lv[3:4], axis=-1, keepdims=True)) + lam_init)

    nq = q_ref.shape[1] // t

    def queries(qi):
        q = q_ref[0, pl.ds(pl.multiple_of(qi * t, t), t), :].astype(F32)
        return jnp.concatenate([
            jnp.concatenate([jnp.where(lane < DIFF_HALF, q, 0.0).astype(BF16), qx], axis=1),
            jnp.concatenate([jnp.where(lane >= DIFF_HALF, q, 0.0).astype(BF16), qx], axis=1)], axis=0)

    flash[0][...] = _dot_nt(ka_ref[0:tk, :], queries(0))

    def qstep(qi, c):
        o = _flash_tiles(queries(qi), ka_ref, vt_ref, (qi * t) // tk, tri_ref.at[qi & (tk // t - 1)], tk, flash,
                         next_qa=queries(jnp.minimum(qi + 1, nq - 1)),
                         late_cols=((t // 2, t), (t + t // 2, 2 * t)) if t == tk else None)
        o = (o[:, :t] - lam * o[:, t:]).T
        o_ref[pl.ds(pl.multiple_of(qi * t, t), t), :] = (_rms(o, sub_ref[...]) * (1.0 - lam_init)).astype(BF16)
        return c

    lax.fori_loop(0, nq, qstep, 0)


def _diff(slabs, kx, qx_diff, lam_vec, subln, batch, seq, lam_init):
    tk = _attn_tiles(seq)[1]
    t = tk
    m = batch * seq
    head = lambda base: pl.BlockSpec((1, seq, LANES), lambda b, h: (base + h, b, 0))
    return pl.pallas_call(
        functools.partial(_diff_kernel, t=t, tk=tk, lam_init=lam_init),
        grid=(batch, DIFF_HEADS),
        in_specs=[head(SLAB_DQ), head(SLAB_DK), head(SLAB_DV),
                  pl.BlockSpec((seq, LANES), lambda b, h: (0, 0)),
                  pl.BlockSpec((1, 1, LANES), lambda b, h: (h, 0, 0)),
                  pl.BlockSpec((4, DIFF_HALF), lambda b, h: (0, 0)),
                  pl.BlockSpec((1, LANES), lambda b, h: (0, 0))],
        out_specs=pl.BlockSpec((seq, LANES), lambda b, h: (b, h)),
        out_shape=jax.ShapeDtypeStruct((m, DIFF_HEADS * HEAD_DIM), BF16),
        scratch_shapes=[pltpu.VMEM((seq, 2 * LANES), BF16), pltpu.VMEM((LANES, seq), BF16),
                        pltpu.VMEM((tk // t, tk, 2 * t), F32)] + _flash_scratch(tk, 2 * t),
        compiler_params=_params(("parallel", "parallel")),
        name="diff_attn",
    )(slabs, slabs, slabs, kx, qx_diff, lam_vec, subln)


def _compress_kernel(y_ref, pos_ref, w1_ref, w2_ref, o_ref):
    half = y_ref.shape[3]
    y = y_ref[0, 0].astype(F32)
    top = (y + pos_ref[0, 0:1, :]).astype(BF16)
    bot = (y + pos_ref[0, 1:2, :]).astype(BF16)
    a = _dot(top, w1_ref[0, 0:half, :])
    b = _dot(bot, w1_ref[0, half:2 * half, :])
    nrow = a.shape[0]
    hid = a + pltpu.roll(b, nrow - 1, 0)
    hid = hid * jax.nn.sigmoid(hid)
    o_ref[0, 0] = _dot(hid.astype(BF16), w2_ref[0]).astype(BF16)


def _compress(ykv, pos2, w1, w2, batch, l):
    nrow, half = ykv.shape[2], ykv.shape[3]
    return pl.pallas_call(
        _compress_kernel,
        grid=(4, batch),
        in_specs=[pl.BlockSpec((1, 1, nrow, half), lambda s, b: (s, b, 0, 0)),
                  pl.BlockSpec((None, 1, 2, half), lambda s, b: (l, s // 2, 0, 0)),
                  pl.BlockSpec((None, 1, 2 * half, CMP_HIDDEN), lambda s, b: (l, s // 2, 0, 0)),
                  pl.BlockSpec((None, 1, CMP_HIDDEN, HEAD_DIM), lambda s, b: (l, s // 2, 0, 0))],
        out_specs=pl.BlockSpec((1, 1, nrow, HEAD_DIM), lambda s, b: (s, b, 0, 0)),
        out_shape=jax.ShapeDtypeStruct((4, batch, nrow, HEAD_DIM), BF16),
        compiler_params=_params(("parallel", "parallel")),
        name="nsa_compress",
    )(ykv, pos2, w1, w2)


def _topk_bias(cand, topk):
    nsel, tq = cand.shape
    sub = 8
    blocks = [cand[b * sub:(b + 1) * sub] for b in range(nsel // sub)]
    jidx = lax.broadcasted_iota(jnp.int32, (sub, tq), 0)
    ranks = [jnp.zeros((sub, tq), F32) for _ in blocks]
    for k in range(nsel):
        rk = cand[k:k + 1, :]
        for b, cb in enumerate(blocks):
            if b * sub > k:
                beats = rk >= cb
            elif b * sub + sub - 1 < k:
                beats = rk > cb
            else:
                beats = (rk > cb) | ((rk == cb) & (jidx + b * sub > k))
            ranks[b] = ranks[b] + jnp.where(beats, 1.0, 0.0)
    rank = jnp.concatenate(ranks, axis=0)
    return jnp.where(rank < topk, 0.0, SEL_MASK)


def _nsa_kernel(q_ref, kc_ref, vc_ref, ks_ref, vs_ref, kw_ref, vw_ref, kx_ref, cx_ref, ovt_ref, qx_ref,
                misc_ref, o_ref, ksa_ref, kwa_ref, vst_ref, vwt_ref, vct_ref, gt_ref, dbias_ref, wbias_ref,
                comb_ref, *flash, tq, tk):
    g = pl.program_id(1)
    qi = pl.program_id(2)
    mcols = NSA_REP * tq
    nc = kc_ref.shape[2]
    seq = kx_ref.shape[0]
    nsel = seq // SEL_BLOCK
    span = WINDOW + tq

    @pl.when(qi == 0)
    def _():
        _stage_kv(ksa_ref, vst_ref, ks_ref, kx_ref, vs_ref, tk)
        _stage_kv(kwa_ref, vwt_ref, kw_ref, kx_ref, vw_ref, tk, pad=WINDOW)
        pad_lane = lax.broadcasted_iota(jnp.int32, (WINDOW, 2 * LANES), 1)
        kwa_ref[0:WINDOW, :] = jnp.where(pad_lane == LANES + AUG_PAD, SEL_MASK, 0.0).astype(BF16)
        vwt_ref[:, 0:WINDOW] = jnp.zeros((LANES, WINDOW), BF16)
        vct_ref[...] = vc_ref[0, 0].astype(F32).T.astype(BF16)
        for o in range(tk // tq):
            dbias_ref[o] = _causal_bias(tk, mcols, tq, o * tq)
        rk = lax.broadcasted_iota(jnp.int32, (span, mcols), 0)
        rq = lax.broadcasted_iota(jnp.int32, (span, mcols), 1) & (tq - 1)
        wbias_ref[...] = jnp.where((rk > rq) & (rk <= rq + WINDOW), 0.0, NEG_INF)

    qx = jnp.concatenate([jnp.broadcast_to(qx_ref[0, r:r + 1, :], (tq, LANES)) for r in range(NSA_REP)], axis=0)
    kca = jnp.concatenate([kc_ref[0, 0], cx_ref[...]], axis=1)
    _nsa_query_tile(qi, g, qx, kca, q_ref, ovt_ref, misc_ref, o_ref, ksa_ref, kwa_ref, vst_ref, vwt_ref,
                    vct_ref, gt_ref, dbias_ref, wbias_ref, comb_ref, flash, tq=tq, tk=tk, nc=nc, nsel=nsel)


def _nsa_query_tile(qi, g, qx, kca, q_ref, ovt_ref, misc_ref, o_ref, ksa_ref, kwa_ref, vst_ref, vwt_ref, vct_ref,
                    gt_ref, dbias_ref, wbias_ref, comb_ref, flash, *, tq, tk, nc, nsel):
    mcols = NSA_REP * tq
    span = WINDOW + tq
    q0 = pl.multiple_of(qi * tq, tq)
    q4 = q_ref[...].reshape(mcols, LANES)
    qa = jnp.concatenate([q4, qx.astype(BF16)], axis=1)
    col = lax.broadcasted_iota(jnp.int32, (1, mcols), 1)
    col_pos = q0 + (col & (tq - 1))

    sc = _dot_nt(kca, qa)
    cend = lax.broadcasted_iota(jnp.int32, (nc, mcols), 0) * CMP_STRIDE + (CMP_BLOCK - 1)
    sc = jnp.where(cend <= col_pos, sc, NEG_INF)
    e = jnp.exp2(sc - jnp.max(sc, axis=0, keepdims=True))
    inv = jnp.where(col_pos >= CMP_BLOCK - 1, 1.0 / jnp.sum(e, axis=0, keepdims=True), 0.0)
    p = e * inv
    o_cmp = _dot(vct_ref[...], p.astype(BF16))

    psum = p[:, 0:tq]
    for r in range(1, NSA_REP):
        psum = psum + p[:, r * tq:(r + 1) * tq]
    p_hi = psum.astype(BF16)
    p_lo = (psum - p_hi.astype(F32)).astype(BF16)
    imp = _dot(ovt_ref[...], p_hi) + _dot(ovt_ref[...], p_lo)
    qpos = q0 + lax.broadcasted_iota(jnp.int32, (LANES, tq), 1)
    blk = lax.broadcasted_iota(jnp.int32, (LANES, tq), 0)
    cur = jnp.right_shift(qpos, SEL_BLOCK.bit_length() - 1)
    forced = (blk == 0) | (blk == cur) | (blk == cur - 1)
    imp = jnp.where(blk <= cur, jnp.where(forced, FORCE_SCORE, imp), -1.0)
    bias_t = _topk_bias(imp[0:nsel], min(SEL_TOPK, nsel))
    if nsel < LANES:
        bias_t = jnp.concatenate([bias_t, jnp.zeros((LANES - nsel, tq), F32)], axis=0)
    selbias = jnp.concatenate([bias_t.T] * NSA_REP, axis=0)
    lane4 = lax.broadcasted_iota(jnp.int32, (mcols, LANES), 1)
    qa_sel = jnp.concatenate([q4, jnp.where(lane4 < SEL_BLOCK, selbias, qx).astype(BF16)], axis=1)

    sw = _dot_nt(kwa_ref[pl.ds(q0, span), :], qa) + wbias_ref[...]
    ew = jnp.exp2(sw - jnp.max(sw, axis=0, keepdims=True))
    o_win = _dot(vwt_ref[:, pl.ds(q0, span)], ew.astype(BF16)) * (1.0 / jnp.sum(ew, axis=0, keepdims=True))

    gt_ref[...] = jax.nn.sigmoid(misc_ref[...]).T
    gate = lambda r, i: gt_ref[pl.ds(MISC_NG + 3 * (NSA_REP * g + r) + i, 1), :]
    for r in range(NSA_REP):
        sl = slice(r * tq, (r + 1) * tq)
        comb_ref[:, sl] = gate(r, 0) * o_cmp[:, sl] + gate(r, 2) * o_win[:, sl]

    late = tuple((r * tq + tq // 2, (r + 1) * tq) for r in range(NSA_REP)) if tq == tk else None
    o_sel = _flash_tiles(qa_sel, ksa_ref, vst_ref, q0 // tk, dbias_ref.at[qi & (tk // tq - 1)], tk, flash,
                         late_cols=late)
    for r in range(NSA_REP):
        sl = slice(r * tq, (r + 1) * tq)
        out = comb_ref[:, sl] + gate(r, 1) * o_sel[:, sl]
        o_ref[:, r * LANES:(r + 1) * LANES] = out.T.astype(BF16)


def _nsa(slabs, kvc, kx, cx_cmp, ovt, qx_nsa, misc, batch, seq):
    tq = _pick(seq, (512, 256, 128))
    tk = _pick(seq, (512, 256, 128))
    assert tq & (tq - 1) == 0 and tk % tq == 0 and seq >= WINDOW + tq
    nq = seq // tq
    m = batch * seq
    mcols = NSA_REP * tq
    nc = kvc.shape[2]
    kv_spec = lambda base: pl.BlockSpec((1, seq, LANES), lambda b, g, i: (base + g, b, 0))
    const = lambda shape: pl.BlockSpec(shape, lambda b, g, i: (0,) * len(shape), pipeline_mode=pl.Buffered(1))
    return pl.pallas_call(
        functools.partial(_nsa_kernel, tq=tq, tk=tk),
        grid=(batch, NSA_KV_HEADS, nq),
        in_specs=[pl.BlockSpec((NSA_REP, tq, LANES), lambda b, g, i: (SLAB_NQ // NSA_REP + g, b * nq + i, 0)),
                  pl.BlockSpec((1, 1, nc, LANES), lambda b, g, i: (g, b, 0, 0)),
                  pl.BlockSpec((1, 1, nc, LANES), lambda b, g, i: (2 + g, b, 0, 0)),
                  kv_spec(SLAB_NKS), kv_spec(SLAB_NVS), kv_spec(SLAB_NKW), kv_spec(SLAB_NVW),
                  const((seq, LANES)), const((nc, LANES)), const((LANES, nc)),
                  pl.BlockSpec((1, NSA_REP, LANES), lambda b, g, i: (g, 0, 0)),
                  pl.BlockSpec((tq, LANES), lambda b, g, i: (b * nq + i, 0))],
        out_specs=pl.BlockSpec((tq, NSA_REP * LANES), lambda b, g, i: (b * nq + i, g)),
        out_shape=jax.ShapeDtypeStruct((m, NSA_HEADS * HEAD_DIM), BF16),
        scratch_shapes=[pltpu.VMEM((seq, 2 * LANES), BF16), pltpu.VMEM((WINDOW + seq, 2 * LANES), BF16),
                        pltpu.VMEM((LANES, seq), BF16), pltpu.VMEM((LANES, WINDOW + seq), BF16),
                        pltpu.VMEM((LANES, nc), BF16), pltpu.VMEM((LANES, tq), F32),
                        pltpu.VMEM((tk // tq, tk, mcols), F32), pltpu.VMEM((WINDOW + tq, mcols), F32),
                        pltpu.VMEM((LANES, mcols), F32)] + _flash_scratch(tk, mcols),
        compiler_params=_params(("parallel", "parallel", "arbitrary")),
        name="nsa_attn",
    )(slabs, kvc, kvc, slabs, slabs, slabs, slabs, kx, cx_cmp, ovt, qx_nsa, misc)


def _merge_kernel(h_ref, of_ref, od_ref, on_ref, wg0_ref, wg1_ref, wg2_ref, wf_ref, wd_ref, wn_ref, o_ref):
    h = h_ref[...]
    acc = jax.nn.sigmoid(_dot(h, wg0_ref[...])) * _dot(of_ref[...], wf_ref[...])
    acc = acc + jax.nn.sigmoid(_dot(h, wg1_ref[...])) * _dot(od_ref[...], wd_ref[...])
    acc = acc + jax.nn.sigmoid(_dot(h, wg2_ref[...])) * _dot(on_ref[...], wn_ref[...])
    o_ref[...] = acc.astype(BF16)


def _merge(h, o_fox, o_diff, o_nsa, w_gate, wb_fox, wb_diff, wb_nsa, l):
    m, d = h.shape
    tm = _pick(m, (1024, 512, 256))
    tn = _pick(d, (512, 256, 128))
    nj = d // tn
    row = lambda width: pl.BlockSpec((tm, width), lambda i, j: (i, 0))
    gate = lambda t: pl.BlockSpec((None, d, tn), lambda i, j: (l, 0, t * nj + j))
    col = lambda k: pl.BlockSpec((None, k, tn), lambda i, j: (l, 0, j))
    return pl.pallas_call(
        _merge_kernel,
        grid=(m // tm, nj),
        in_specs=[row(d), row(o_fox.shape[1]), row(o_diff.shape[1]), row(o_nsa.shape[1]),
                  gate(0), gate(1), gate(2),
                  col(wb_fox.shape[1]), col(wb_diff.shape[1]), col(wb_nsa.shape[1])],
        out_specs=pl.BlockSpec((tm, tn), lambda i, j: (i, j)),
        out_shape=jax.ShapeDtypeStruct((m, d), BF16),
        compiler_params=_params(("parallel", "arbitrary")),
        name="gate_merge",
    )(h, o_fox, o_diff, o_nsa, w_gate, w_gate, w_gate, wb_fox, wb_diff, wb_nsa)


def _wout_kernel(a_ref, w_ref, x_ref, gp_ref, gn_ref, xo_ref, ho_ref):
    sub = min(a_ref.shape[0], MXU_ROWS)
    for s in range(a_ref.shape[0] // sub):
        rows = slice(s * sub, (s + 1) * sub)
        y = _dot(a_ref[rows, :], w_ref[...])
        x_new = x_ref[rows, :] + _rms(y, gp_ref[...])
        xo_ref[rows, :] = x_new
        ho_ref[rows, :] = _rms(x_new, gn_ref[...]).astype(BF16)


def _wout(a, w, x, g_post, g_next, l, name):
    m, d = x.shape
    k = a.shape[1]
    tm = _pick(m, (512, 256, 128)) if k * d * 2 <= SMALL_WEIGHT_BYTES else _pick(m, (MXU_ROWS, 128))
    row = lambda width: pl.BlockSpec((tm, width), lambda i: (i, 0))
    vec = lambda: pl.BlockSpec((1, d), lambda i: (0, 0))
    return pl.pallas_call(
        _wout_kernel,
        grid=(m // tm,),
        in_specs=[row(k), pl.BlockSpec((None, k, d), lambda i: (l, 0, 0), pipeline_mode=pl.Buffered(1)),
                  row(d), vec(), vec()],
        out_specs=[row(d), row(d)],
        out_shape=[jax.ShapeDtypeStruct((m, d), F32), jax.ShapeDtypeStruct((m, d), BF16)],
        compiler_params=_params(("parallel",)),
        name=name,
    )(a, w, x, g_post, g_next)


def _ffn_up_kernel(h_ref, wg_ref, wu_ref, o_ref, wgb_ref, wub_ref):
    @pl.when(pl.program_id(1) == 0)
    def _():
        wgb_ref[...] = wg_ref[...].astype(BF16)
        wub_ref[...] = wu_ref[...].astype(BF16)

    h = h_ref[...]
    gate = _dot(h, wgb_ref[...])
    o_ref[...] = (gate * jax.nn.sigmoid(gate) * _dot(h, wub_ref[...])).astype(BF16)


def _ffn_up(h, w_up, l):
    m, d = h.shape
    dff = w_up.shape[2] // 2
    tm = _pick(m, (1024, 512, 256))
    tn = _pick(dff, (512, 256, 128))
    nj = dff // tn
    return pl.pallas_call(
        _ffn_up_kernel,
        grid=(nj, m // tm),
        in_specs=[pl.BlockSpec((tm, d), lambda j, i: (i, 0)),
                  pl.BlockSpec((None, d, tn), lambda j, i: (l, 0, j)),
                  pl.BlockSpec((None, d, tn), lambda j, i: (l, 0, nj + j))],
        out_specs=pl.BlockSpec((tm, tn), lambda j, i: (i, j)),
        out_shape=jax.ShapeDtypeStruct((m, dff), BF16),
        scratch_shapes=[pltpu.VMEM((d, tn), BF16), pltpu.VMEM((d, tn), BF16)],
        compiler_params=_params(("parallel", "arbitrary")),
        name="ffn_up",
    )(h, w_up, w_up)


def _pos_columns(pos):
    lane = jnp.arange(LANES)[None, :]
    hi = (lane >= AUG_HI) & (lane < AUG_HI + AUG_TERMS)
    lo = (lane >= AUG_LO) & (lane < AUG_LO + AUG_TERMS)
    return jnp.where(hi, (pos // LANES)[:, None], jnp.where(lo, (pos % LANES)[:, None], 0))


def _key_aug_table(seq):
    j = jnp.arange(seq)
    lane = jnp.arange(LANES)[None, :]
    onehot = (lane == (j // SEL_BLOCK)[:, None]) & (lane < SEL_BLOCK)
    return (_pos_columns(j) + onehot.astype(jnp.int32)).astype(BF16)


def _cmp_aug_table(nrow):
    return _pos_columns(jnp.arange(nrow) * CMP_STRIDE + CMP_BLOCK - 1).astype(BF16)


def _overlap_table_t(nrow, seq):
    start = jnp.arange(nrow)[None, :] * CMP_STRIDE
    blk = jnp.arange(LANES)[:, None]
    sel = blk * SEL_BLOCK
    ov = ((start < sel + SEL_BLOCK) & (start + CMP_BLOCK - 1 >= sel) & (blk < seq // SEL_BLOCK)
          & (jnp.arange(nrow)[None, :] < (seq - CMP_BLOCK) // CMP_STRIDE + 1))
    return ov.astype(BF16)


def _query_aug_rows(n_heads):
    slopes = 2.0 ** (-8.0 * jnp.arange(1, n_heads + 1, dtype=F32) / n_heads)
    terms = [t.astype(F32) for t in _split3(jnp.float32(LOG2E))]
    lane = jnp.arange(LANES)[None, :]
    out = jnp.where(lane == AUG_PAD, 1.0, jnp.zeros((n_heads, LANES), F32))
    for i, t in enumerate(terms):
        out = jnp.where(lane == AUG_HI + i, slopes[:, None] * t * LANES, out)
        out = jnp.where(lane == AUG_LO + i, slopes[:, None] * t, out)
    return out


def kernel(x, w_in, fox_forget_bias, diff_lambda, diff_subln, nsa_cmp_pos, nsa_cmp_w1, nsa_cmp_w2,
           w_branch_fox, w_branch_diff, w_branch_nsa, w_gate, w_out, norm_gains, w_ffn_up, w_ffn_down):
    batch, seq, d = x.shape
    depth = w_in.shape[0]
    m = batch * seq
    fw, dw, nw, kvw = FOX_HEADS * HEAD_DIM, DIFF_HEADS * HEAD_DIM, NSA_HEADS * HEAD_DIM, NSA_KV_HEADS * HEAD_DIM
    ff0 = 3 * fw
    dq0 = ff0 + FOX_HEADS
    ng0 = dq0 + 3 * dw + nw + 6 * kvw
    n_main = ng0 - FOX_HEADS
    assert n_main == N_SLABS * LANES and w_in.shape[2] == ng0 + 3 * NSA_HEADS

    w_main = jnp.concatenate([w_in[:, :, :ff0], w_in[:, :, dq0:ng0]], axis=2).astype(BF16)
    w_misc = jnp.concatenate([w_in[:, :, ff0:dq0], w_in[:, :, ng0:],
                              jnp.zeros((depth, d, LANES - FOX_HEADS - 3 * NSA_HEADS), F32)], axis=2).astype(BF16)
    colscale = jnp.ones((n_main,), F32)
    colscale = colscale.at[SLAB_FQ * LANES:SLAB_FK * LANES].set(HEAD_DIM ** -0.5 * LOG2E)
    colscale = colscale.at[SLAB_DQ * LANES:SLAB_DK * LANES].set(DIFF_HALF ** -0.5 * LOG2E)
    colscale = colscale.at[SLAB_NQ * LANES:SLAB_NKC * LANES].set(HEAD_DIM ** -0.5 * LOG2E)
    colscale = colscale[None, :]
    fbias = jnp.pad(fox_forget_bias.astype(F32), ((0, 0), (0, LANES - FOX_HEADS)))[:, None, :]
    half = CMP_STRIDE * HEAD_DIM
    pos2 = nsa_cmp_pos.astype(F32).reshape(depth, 2, 2, half)
    w1 = nsa_cmp_w1.astype(BF16)
    w2 = nsa_cmp_w2.astype(BF16)
    wbf, wbd, wbn = w_branch_fox.astype(BF16), w_branch_diff.astype(BF16), w_branch_nsa.astype(BF16)
    wg, wo = w_gate.astype(BF16), w_out.astype(BF16)
    wup, wdn = w_ffn_up.astype(F32), w_ffn_down.astype(BF16)
    gains = norm_gains.astype(F32)

    nrow = seq // CMP_STRIDE
    kx = _key_aug_table(seq)
    cx_cmp = _cmp_aug_table(nrow)
    ovt = _overlap_table_t(nrow, seq)
    qx_diff = _query_aug_rows(DIFF_HEADS)[:, None, :]
    qx_nsa = _query_aug_rows(NSA_HEADS).reshape(NSA_KV_HEADS, NSA_REP, LANES)

    xf = x.reshape(m, d).astype(F32)
    h = _norm(xf, gains[0, 0][None, :])
    for l in range(depth):
        lam_init = 0.8 - 0.6 * math.exp(-0.3 * l)
        slabs, misc = _inproj(h, w_main, colscale, w_misc, l)
        cx_fox = _logf(misc, fbias[l], batch, seq)
        o_fox = _fox(slabs, cx_fox, batch, seq)
        o_diff = _diff(slabs, kx, qx_diff, diff_lambda[l].astype(F32), diff_subln[l].astype(F32)[None, :],
                       batch, seq, lam_init)
        ykv = slabs[SLAB_NKC:SLAB_NKC + 4].reshape(4, batch, nrow, half)
        kvc = _compress(ykv, pos2, w1, w2, batch, l)
        o_nsa = _nsa(slabs, kvc, kx, cx_cmp, ovt, qx_nsa, misc, batch, seq)
        merged = _merge(h, o_fox, o_diff, o_nsa, wg, wbf, wbd, wbn, l)
        xf, h2 = _wout(merged, wo, xf, gains[l, 1][None, :], gains[l, 2][None, :], l, "out_proj")
        act = _ffn_up(h2, wup, l)
        g_next = gains[min(l + 1, depth - 1), 0][None, :]
        xf, h = _wout(act, wdn, xf, gains[l, 3][None, :], g_next, l, "ffn_down")
    return xf.reshape(batch, seq, d).astype(x.dtype)
```
